```python
import jax
import jax.numpy as jnp
from jax import lax
import numpy as np

D_MODEL = 1024
BATCH = 8
SEQ = 4096
DEPTH = 2

GRID_W = 64
CTX_LEN = 256

HEAD_DIM = 64
CONV_WIDTH = D_MODEL // 4
CONV_K = 3
RET_WIDTH = D_MODEL // 4
RET_HEADS = RET_WIDTH // HEAD_DIM
RET_CHUNK = 128
ATT_WIDTH = D_MODEL // 2
ATT_HEADS = ATT_WIDTH // HEAD_DIM
ATT_KV_HEADS = ATT_HEADS // 4
ATT_KV_WIDTH = ATT_KV_HEADS * HEAD_DIM
ATT_BLOCK = 128
MIX_WIDTH = CONV_WIDTH + RET_WIDTH + ATT_WIDTH
ROPE_BASE = 10000.0
N_EXPERTS = 32
TOP_K = 4
EXPERT_FF = D_MODEL
SWIGLU_ALPHA = 1.702
SWIGLU_LIMIT = 7.0
MOE_BLOCK = 512
N_MOD = 6
EPS = 1e-6
IN_SPLITS = (CONV_WIDTH, CONV_WIDTH, CONV_WIDTH, RET_WIDTH, RET_WIDTH, RET_WIDTH, RET_WIDTH, ATT_WIDTH, ATT_KV_WIDTH, ATT_KV_WIDTH)
IN_WIDTH = sum(IN_SPLITS)
SPLIT_POINTS = tuple(sum(IN_SPLITS[: i + 1]) for i in range(len(IN_SPLITS) - 1))

kernel_name = 'hybrid_conv_retention_gqa_moe_dit'


def _layer_norm(x, g, b):
    xf = x.astype(jnp.float32)
    mu = xf.mean(-1, keepdims=True)
    var = jnp.square(xf - mu).mean(-1, keepdims=True)
    return ((xf - mu) * lax.rsqrt(var + EPS) * g + b).astype(x.dtype)


def _rms_norm(x, g):
    xf = x.astype(jnp.float32)
    return (xf * lax.rsqrt(jnp.mean(xf * xf, -1, keepdims=True) + EPS) * g).astype(x.dtype)


def _heads(z, n_heads):
    b, l, w = z.shape
    return z.reshape(b, l, n_heads, w // n_heads).transpose(0, 2, 1, 3)


def _merge_heads(y):
    b, n, l, dh = y.shape
    return y.transpose(0, 2, 1, 3).reshape(b, l, n * dh)


def _flip(a):
    return jnp.flip(a, axis=2)


def _axial_rope(seq_len, head_dim):
    rows = seq_len // GRID_W
    axis_dim = head_dim // 2
    inv_freq = ROPE_BASE ** (-jnp.arange(0, axis_dim, 2, dtype=jnp.float32) / axis_dim)
    row = jnp.repeat(jnp.arange(rows, dtype=jnp.float32), GRID_W)
    col = jnp.tile(jnp.arange(GRID_W, dtype=jnp.float32), rows)
    ang = jnp.stack([row[:, None] * inv_freq, col[:, None] * inv_freq], axis=1)
    return jnp.cos(ang), jnp.sin(ang)


def _apply_rope(x, cos, sin):
    b, h, l, dh = x.shape
    xa = x.astype(jnp.float32).reshape(b, h, l, 2, 2, dh // 4)
    x1, x2 = xa[..., 0, :], xa[..., 1, :]
    out = jnp.stack([x1 * cos - x2 * sin, x1 * sin + x2 * cos], axis=-2)
    return out.reshape(b, h, l, dh).astype(x.dtype)


def _short_conv(u, w):
    l = u.shape[1]
    up = jnp.pad(u, ((0, 0), (1, 1), (0, 0)))
    return up[:, :l] * w[:, 0] + up[:, 1:l + 1] * w[:, 1] + up[:, 2:] * w[:, 2]


def _retention_scan(q, k, v, log_gamma, s0, include_diag):
    b, h, l, dk = q.shape
    dv = v.shape[-1]
    nc = l // RET_CHUNK
    qc = q.reshape(b, h, nc, RET_CHUNK, dk)
    kc = k.reshape(b, h, nc, RET_CHUNK, dk)
    vc = v.reshape(b, h, nc, RET_CHUNK, dv)
    pos = jnp.arange(RET_CHUNK, dtype=jnp.float32)
    diff = pos[:, None] - pos[None, :]
    keep = (diff >= 0) if include_diag else (diff > 0)
    decay = jnp.where(keep, jnp.exp(log_gamma[:, None, None] * jnp.maximum(diff, 0.0)), 0.0)
    scores = jnp.einsum('bhnid,bhnjd->bhnij', qc, kc) * decay[:, None]
    y_intra = jnp.einsum('bhnij,bhnjv->bhniv', scores, vc)
    zeta = jnp.exp(log_gamma[:, None] * (RET_CHUNK - 1.0 - pos))
    chunk_kv = jnp.einsum('bhnjd,hj,bhnjv->nbhdv', kc, zeta, vc)
    chunk_decay = jnp.exp(log_gamma * RET_CHUNK)[None, :, None, None]

    def step(state, kv_n):
        return chunk_decay * state + kv_n, state

    s_final, s_prev = lax.scan(step, s0, chunk_kv)
    xi = jnp.exp(log_gamma[:, None] * (pos + 1.0))
    y_cross = jnp.einsum('bhnid,hi,nbhdv->bhniv', qc, xi, s_prev)
    return (y_intra + y_cross).reshape(b, h, l, dv), s_final


def _retention_state(k, v, log_gamma):
    l = k.shape[2]
    w = jnp.exp(log_gamma[:, None] * (l - 1.0 - jnp.arange(l, dtype=jnp.float32)))
    return jnp.einsum('bhld,hl,bhlv->bhdv', k, w, v)


def _ret_output(y, gate_z, gn_g):
    mu = y.mean(-1, keepdims=True)
    var = jnp.square(y - mu).mean(-1, keepdims=True)
    yn = _merge_heads((y - mu) * lax.rsqrt(var + EPS)) * gn_g
    return yn.astype(gate_z.dtype) * jax.nn.silu(gate_z)


def _block_attention(q, k, v):
    b, hq, lq, dh = q.shape
    hkv = k.shape[1]
    g = hq // hkv
    nb = lq // ATT_BLOCK
    qb = q.reshape(b, hkv, g, nb, ATT_BLOCK, dh).transpose(3, 0, 1, 2, 4, 5)
    scale = dh ** -0.5

    def one_block(q_blk):
        s = jnp.einsum('bkgqd,bksd->bkgqs', q_blk, k, preferred_element_type=jnp.float32) * scale
        p = jax.nn.softmax(s, axis=-1)
        return jnp.einsum('bkgqs,bksd->bkgqd', p.astype(v.dtype), v)

    out = lax.map(one_block, qb)
    return out.transpose(1, 2, 3, 0, 4, 5).reshape(b, hq, lq, dh)


def _token_mixers(h_lat, h_ctx, w_in, conv_w, ret_decay_exp, ret_gn_g, q_norm_g, k_norm_g, w_out, with_ctx_out):
    f32 = jnp.float32
    z_lat = jnp.split(h_lat @ w_in, SPLIT_POINTS, axis=-1)
    z_ctx = jnp.split(h_ctx @ w_in, SPLIT_POINTS, axis=-1)
    cos, sin = _axial_rope(h_lat.shape[1], HEAD_DIM)
    log_gamma = jnp.log1p(-jnp.exp2(-ret_decay_exp.astype(f32)))

    def conv_group(z):
        return z[1] * _short_conv(z[2] * z[0], conv_w)

    def ret_q(z):
        return _heads(z[3], RET_HEADS)

    def ret_kv(z):
        return _heads(z[4], RET_HEADS) * HEAD_DIM ** -0.5, _heads(z[5], RET_HEADS)

    def att_q(z):
        return _rms_norm(_heads(z[7], ATT_HEADS), q_norm_g)

    def att_kv(z):
        return _rms_norm(_heads(z[8], ATT_KV_HEADS), k_norm_g), _heads(z[9], ATT_KV_HEADS)

    kc, vc = ret_kv(z_ctx)
    kc, vc = kc.astype(f32), vc.astype(f32)
    ql = _apply_rope(ret_q(z_lat), cos, sin).astype(f32)
    kl, vl = ret_kv(z_lat)
    kl, vl = _apply_rope(kl, cos, sin).astype(f32), vl.astype(f32)
    if with_ctx_out:
        qc = ret_q(z_ctx).astype(f32)
        zeros = jnp.zeros(kc.shape[:2] + (HEAD_DIM, HEAD_DIM), f32)
        yf_c, s_fwd = _retention_scan(qc, kc, vc, log_gamma[0], zeros, True)
        yb_c, s_bwd = _retention_scan(_flip(qc), _flip(kc), _flip(vc), log_gamma[1], zeros, False)
        ret_ctx = yf_c + _flip(yb_c)
    else:
        s_fwd = _retention_state(kc, vc, log_gamma[0])
        s_bwd = _retention_state(_flip(kc), _flip(vc), log_gamma[1])
    yf_l, _ = _retention_scan(ql, kl, vl, log_gamma[0], s_fwd, True)
    yb_l, _ = _retention_scan(_flip(ql), _flip(kl), _flip(vl), log_gamma[1], s_bwd, False)
    ret_lat = yf_l + _flip(yb_l)

    ka_c, va_c = att_kv(z_ctx)
    qa_l = _apply_rope(att_q(z_lat), cos, sin)
    ka_l, va_l = att_kv(z_lat)
    ka_l = _apply_rope(ka_l, cos, sin)
    att_lat = _block_attention(qa_l, jnp.concatenate([ka_c, ka_l], axis=2), jnp.concatenate([va_c, va_l], axis=2))

    y_lat = jnp.concatenate([conv_group(z_lat), _ret_output(ret_lat, z_lat[6], ret_gn_g), _merge_heads(att_lat)], axis=-1) @ w_out
    if with_ctx_out:
        att_ctx = _block_attention(att_q(z_ctx), ka_c, va_c)
        y_ctx = jnp.concatenate([conv_group(z_ctx), _ret_output(ret_ctx, z_ctx[6], ret_gn_g), _merge_heads(att_ctx)], axis=-1) @ w_out
    else:
        y_ctx = None
    return y_lat, y_ctx


def _moe(h, w_router, b_router, w_gate_up, b_gate_up, w_down, b_down):
    t, d = h.shape
    logits = (h @ w_router + b_router).astype(jnp.float32)
    top_v, top_e = lax.top_k(logits, TOP_K)
    top_w = jax.nn.softmax(top_v, axis=-1)
    n_assign = t * TOP_K
    flat_e = top_e.reshape(n_assign).astype(jnp.int32)
    flat_w = top_w.reshape(n_assign)
    flat_tok = jnp.arange(n_assign, dtype=jnp.int32) // TOP_K
    order = jnp.argsort(flat_e)
    e_sorted = flat_e[order]
    counts = jnp.bincount(flat_e, length=N_EXPERTS).astype(jnp.int32)
    padded = (counts + MOE_BLOCK - 1) // MOE_BLOCK * MOE_BLOCK
    grp_start = jnp.cumsum(counts) - counts
    pad_end = jnp.cumsum(padded)
    pad_start = pad_end - padded
    dest = pad_start[e_sorted] + jnp.arange(n_assign, dtype=jnp.int32) - grp_start[e_sorted]
    n_blocks = (n_assign + N_EXPERTS * (MOE_BLOCK - 1) + MOE_BLOCK - 1) // MOE_BLOCK
    n_rows = n_blocks * MOE_BLOCK
    row_tok = jnp.full((n_rows,), t, jnp.int32).at[dest].set(flat_tok[order])
    row_w = jnp.zeros((n_rows,), jnp.float32).at[dest].set(flat_w[order])
    block_e = jnp.minimum(jnp.searchsorted(pad_end, jnp.arange(n_blocks, dtype=jnp.int32) * MOE_BLOCK, side='right'), N_EXPERTS - 1)
    h_pad = jnp.concatenate([h, jnp.zeros((1, d), h.dtype)], axis=0)

    def step(acc, blk):
        tok, wt, e = blk
        xb = h_pad[tok]
        gu = xb @ w_gate_up[e] + b_gate_up[e]
        gate, up = jnp.split(gu, 2, axis=-1)
        gate = jnp.minimum(gate, SWIGLU_LIMIT)
        up = jnp.clip(up, -SWIGLU_LIMIT, SWIGLU_LIMIT)
        act = (up + 1.0) * (gate * jax.nn.sigmoid(SWIGLU_ALPHA * gate))
        yb = act @ w_down[e] + b_down[e]
        return acc.at[tok].add(yb.astype(jnp.float32) * wt[:, None]), None

    acc, _ = lax.scan(step, jnp.zeros((t + 1, d), jnp.float32),
                      (row_tok.reshape(n_blocks, MOE_BLOCK), row_w.reshape(n_blocks, MOE_BLOCK), block_e))
    return acc[:t].astype(h.dtype)


def setup_inputs(seed: int = 0) -> dict:
    key = jax.random.key(seed)
    ks = jax.random.split(key, 24)
    f32 = jnp.float32

    def nrm(k, shape, scale):
        return jax.random.normal(k, shape, f32) * scale

    beta = (8.0 * DEPTH) ** -0.25
    return {
        'x': nrm(ks[0], (BATCH, SEQ, D_MODEL), 1.0),
        'c': nrm(ks[1], (BATCH, D_MODEL), 1.0),
        'ctx': nrm(ks[2], (BATCH, CTX_LEN, D_MODEL), 1.0),
        'c_ctx': nrm(ks[3], (D_MODEL,), 1.0),
        'w_mod': nrm(ks[4], (DEPTH, D_MODEL, N_MOD * D_MODEL), 0.5 * D_MODEL ** -0.5),
        'b_mod': nrm(ks[5], (DEPTH, N_MOD * D_MODEL), 0.02),
        'w_in': nrm(ks[6], (DEPTH, D_MODEL, IN_WIDTH), D_MODEL ** -0.5),
        'conv_w': nrm(ks[7], (DEPTH, CONV_WIDTH, CONV_K), CONV_K ** -0.5),
        'ret_decay_exp': 5.0 + jnp.arange(RET_HEADS, dtype=f32) + 0.25 * jax.random.uniform(ks[8], (DEPTH, 2, RET_HEADS), f32),
        'ret_gn_g': 1.0 + nrm(ks[9], (DEPTH, RET_WIDTH), 0.02),
        'q_norm_g': 1.0 + nrm(ks[10], (DEPTH, HEAD_DIM), 0.02),
        'k_norm_g': 1.0 + nrm(ks[11], (DEPTH, HEAD_DIM), 0.02),
        'w_out': nrm(ks[12], (DEPTH, MIX_WIDTH, D_MODEL), beta * MIX_WIDTH ** -0.5),
        'ln_g': 1.0 + nrm(ks[13], (DEPTH, 2, D_MODEL), 0.02),
        'ln_b': nrm(ks[14], (DEPTH, 2, D_MODEL), 0.02),
        'w_router': nrm(ks[15], (DEPTH, D_MODEL, N_EXPERTS), D_MODEL ** -0.5),
        'b_router': nrm(ks[16], (DEPTH, N_EXPERTS), 0.01),
        'w_gate_up': nrm(ks[17], (DEPTH, N_EXPERTS, D_MODEL, 2 * EXPERT_FF), D_MODEL ** -0.5),
        'b_gate_up': nrm(ks[18], (DEPTH, N_EXPERTS, 2 * EXPERT_FF), 0.01),
        'w_down': nrm(ks[19], (DEPTH, N_EXPERTS, EXPERT_FF, D_MODEL), beta * EXPERT_FF ** -0.5),
        'b_down': nrm(ks[20], (DEPTH, N_EXPERTS, D_MODEL), 0.01),
    }


def reference(x, c, ctx, c_ctx, w_mod, b_mod, w_in, conv_w, ret_decay_exp, ret_gn_g, q_norm_g, k_norm_g, w_out, ln_g, ln_b, w_router, b_router, w_gate_up, b_gate_up, w_down, b_down):
    alpha = (2.0 * DEPTH) ** 0.25
    b, l, d = x.shape
    lc = ctx.shape[1]
    cond = jax.nn.silu(c)
    cond_ctx = jax.nn.silu(c_ctx)
    for layer in range(DEPTH):
        last = layer == DEPTH - 1
        mod = (cond @ w_mod[layer] + b_mod[layer])[:, None, :]
        mod_c = cond_ctx @ w_mod[layer] + b_mod[layer]
        sh1, sc1, g1, sh2, sc2, g2 = jnp.split(mod, N_MOD, axis=-1)
        sh1c, sc1c, g1c, sh2c, sc2c, g2c = jnp.split(mod_c, N_MOD, axis=-1)

        y_lat, y_ctx = _token_mixers(x * (1.0 + sc1) + sh1, ctx * (1.0 + sc1c) + sh1c,
                                     w_in[layer], conv_w[layer], ret_decay_exp[layer], ret_gn_g[layer],
                                     q_norm_g[layer], k_norm_g[layer], w_out[layer], not last)
        x = _layer_norm(alpha * x + g1 * y_lat, ln_g[layer, 0], ln_b[layer, 0])

        h_lat = (x * (1.0 + sc2) + sh2).reshape(b * l, d)
        if not last:
            ctx = _layer_norm(alpha * ctx + g1c * y_ctx, ln_g[layer, 0], ln_b[layer, 0])
            h_ctx = (ctx * (1.0 + sc2c) + sh2c).reshape(b * lc, d)
            f = _moe(jnp.concatenate([h_ctx, h_lat], axis=0), w_router[layer], b_router[layer],
                     w_gate_up[layer], b_gate_up[layer], w_down[layer], b_down[layer])
            f_ctx = f[: b * lc].reshape(b, lc, d)
            f_lat = f[b * lc:].reshape(b, l, d)
            ctx = _layer_norm(alpha * ctx + g2c * f_ctx, ln_g[layer, 1], ln_b[layer, 1])
        else:
            f_lat = _moe(h_lat, w_router[layer], b_router[layer], w_gate_up[layer], b_gate_up[layer],
                         w_down[layer], b_down[layer]).reshape(b, l, d)
        x = _layer_norm(alpha * x + g2 * f_lat, ln_g[layer, 1], ln_b[layer, 1])
    return x
```

```python
import functools

import jax
import jax.numpy as jnp
from jax import lax
from jax.experimental import pallas as pl
from jax.experimental.pallas import tpu as pltpu

F32 = jnp.float32
BF16 = jnp.bfloat16

HEAD_DIM = 64
GRID_W = 64
ROPE_BASE = 10000.0
N_EXPERTS = 32
TOP_K = 4
SWIGLU_ALPHA = 1.702
SWIGLU_LIMIT = 7.0
N_MOD = 6
EPS = 1e-6
RET_HEADS = 4
RET_CHUNK = 128
ATT_Q_PER_KV = 4
LANES = 128
ROUTER_PAD = 128
VMEM_LIMIT = 56 * 1024 * 1024


def _dot(a, b):
    return jnp.dot(a, b, preferred_element_type=F32)


def _dot_nt(a, b):
    return lax.dot_general(a, b, (((1,), (1,)), ((), ())), preferred_element_type=F32)


def _split_bf16(x):
    hi = x.astype(BF16)
    lo = (x - hi.astype(F32)).astype(BF16)
    return hi, lo


def _cparams(*sem):
    return pltpu.CompilerParams(dimension_semantics=sem, vmem_limit_bytes=VMEM_LIMIT)


def _mod_kernel(c_ref, w_ref, b_ref, o_ref):
    c = c_ref[...]
    a = c * jax.nn.sigmoid(c)
    a_hi, a_lo = _split_bf16(a)
    w_hi, w_lo = _split_bf16(w_ref[0])
    o_ref[0] = _dot(a_hi, w_hi) + _dot(a_lo, w_hi) + _dot(a_hi, w_lo) + b_ref[0]


def _modulation(c_all, w_mod, b_mod):
    depth, d, n = w_mod.shape
    r = c_all.shape[0]
    tn = 1536
    return pl.pallas_call(
        _mod_kernel,
        grid=(depth, n // tn),
        in_specs=[
            pl.BlockSpec((r, d), lambda l, j: (0, 0)),
            pl.BlockSpec((1, d, tn), lambda l, j: (l, 0, j)),
            pl.BlockSpec((1, 1, tn), lambda l, j: (l, 0, j)),
        ],
        out_specs=pl.BlockSpec((1, r, tn), lambda l, j: (l, 0, j)),
        out_shape=jax.ShapeDtypeStruct((depth, r, n), F32),
        compiler_params=_cparams("arbitrary", "arbitrary"),
        name="modulation",
    )(c_all, w_mod, b_mod.reshape(depth, 1, n))


def _inproj_kernel(x_ref, sc_ref, sh_ref, w_ref, cos_ref, sin_ref, gq_ref, gk_ref, ones_ref,
                   cu_ref, cb_ref, rq_ref, rk_ref, rv_ref, rg_ref, aq_ref, ak_ref, av_ref):
    h = (x_ref[0] * sc_ref[0] + sh_ref[0]).astype(BF16)
    z = _dot(h, w_ref[...])
    cos = cos_ref[...]
    sin = sin_ref[...]
    ones = ones_ref[...]
    lane = lax.broadcasted_iota(jnp.int32, cos.shape, 1)
    first_half = (lane & 31) < 16

    def rope(xs):
        nxt = pltpu.roll(xs, LANES - 16, axis=1)
        prv = pltpu.roll(xs, 16, axis=1)
        return xs * cos + jnp.where(first_half, nxt, prv) * sin

    def rms(xs, g):
        s_hi, s_lo = _split_bf16(xs * xs)
        ssq = _dot(s_hi, ones) + _dot(s_lo, ones)
        return xs * lax.rsqrt(ssq * (1.0 / HEAD_DIM) + EPS) * g

    cu_ref[0] = z[:, 512:768] * z[:, 0:256]
    cb_ref[0] = z[:, 256:512]
    for j in range(2):
        lo, hi = j * LANES, (j + 1) * LANES
        rq_ref[0, :, lo:hi] = rope(z[:, 768 + lo:768 + hi])
        rk_ref[0, :, lo:hi] = rope(z[:, 1024 + lo:1024 + hi] * (HEAD_DIM ** -0.5))
    rv_ref[0] = z[:, 1280:1536]
    rg_ref[0] = z[:, 1536:1792]
    gq = gq_ref[...]
    for j in range(4):
        lo, hi = j * LANES, (j + 1) * LANES
        q = rope(rms(z[:, 1792 + lo:1792 + hi], gq)) * (HEAD_DIM ** -0.5)
        aq_ref[0, :, lo:hi] = q.astype(BF16)
    ak_ref[0] = rope(rms(z[:, 2304:2432], gk_ref[...])).astype(BF16)
    av_ref[0] = z[:, 2432:2560].astype(BF16)


def _inproj(x, sc, sh, w_bf, cos, sin, gq, gk, ones_bd, tm):
    b, l, d = x.shape
    n = w_bf.shape[1]
    tok = lambda w: pl.BlockSpec((1, tm, w), lambda i, j: (i, j, 0))
    vec = pl.BlockSpec((1, 1, d), lambda i, j: (i, 0, 0))
    full = lambda s: pl.BlockSpec(s, lambda i, j: (0,) * len(s))
    widths = (256, 256, 256, 256, 256, 256, 512, 128, 128)
    dtypes = (F32,) * 6 + (BF16,) * 3
    return pl.pallas_call(
        _inproj_kernel,
        grid=(b, l // tm),
        in_specs=[tok(d), vec, vec, full((d, n)),
                  pl.BlockSpec((tm, LANES), lambda i, j: (j, 0)),
                  pl.BlockSpec((tm, LANES), lambda i, j: (j, 0)),
                  full((1, LANES)), full((1, LANES)), full((LANES, LANES))],
        out_specs=[tok(w) for w in widths],
        out_shape=[jax.ShapeDtypeStruct((b, l, w), dt) for w, dt in zip(widths, dtypes)],
        compiler_params=_cparams("arbitrary", "arbitrary"),
        name="inproj",
    )(x, sc, sh, w_bf, cos, sin, gq, gk, ones_bd)


def _attn_kernel(q_ref, k_ref, v_ref, o_ref):
    tq = q_ref.shape[1]
    q4 = q_ref[0]
    q = jnp.concatenate([q4[:, h * HEAD_DIM:(h + 1) * HEAD_DIM] for h in range(ATT_Q_PER_KV)], axis=0)
    s = _dot_nt(q, k_ref[0, 0])
    m = jnp.max(s, axis=-1, keepdims=True)
    p = jnp.exp(s - m)
    l = jnp.sum(p, axis=-1, keepdims=True)
    o = _dot(p.astype(BF16), v_ref[0, 0]) / l
    o_ref[0] = jnp.concatenate([o[h * tq:(h + 1) * tq] for h in range(ATT_Q_PER_KV)], axis=1)


def _attention(q, k, v, tq):
    b, lq, wq = q.shape
    hkv, lk = k.shape[1], k.shape[2]
    wg = ATT_Q_PER_KV * HEAD_DIM
    return pl.pallas_call(
        _attn_kernel,
        grid=(b, hkv, lq // tq),
        in_specs=[pl.BlockSpec((1, tq, wg), lambda i, g, j: (i, j, g)),
                  pl.BlockSpec((1, 1, lk, HEAD_DIM), lambda i, g, j: (i, g, 0, 0)),
                  pl.BlockSpec((1, 1, lk, HEAD_DIM), lambda i, g, j: (i, g, 0, 0))],
        out_specs=pl.BlockSpec((1, tq, wg), lambda i, g, j: (i, j, g)),
        out_shape=jax.ShapeDtypeStruct((b, lq, wq), F32),
        compiler_params=_cparams("arbitrary", "arbitrary", "arbitrary"),
        name="attention",
    )(q, k, v)


def _ret_bwd_kernel(k_ref, v_ref, s0_ref, lgb_ref, sb_ref, sfin_ref, s_scr):
    i = pl.program_id(1)
    c = k_ref.shape[1]
    w = k_ref.shape[2]

    @pl.when(i == 0)
    def _():
        s_scr[...] = s0_ref[0]

    sb_ref[0, 0] = s_scr[...]
    lgb = lgb_ref[...]
    pos = lax.broadcasted_iota(jnp.int32, (c, w), 0).astype(F32)
    kzt = (k_ref[0] * jnp.exp(lgb * pos)).T.astype(BF16)
    v = v_ref[0].astype(BF16)
    cdec = jnp.exp(lgb * float(c))
    for h in range(RET_HEADS):
        lo, hi = h * HEAD_DIM, (h + 1) * HEAD_DIM
        s_scr[h] = cdec[:, lo:hi] * s_scr[h] + _dot(kzt[lo:hi, :], v[:, lo:hi])

    @pl.when(i == pl.num_programs(1) - 1)
    def _():
        sfin_ref[0] = s_scr[...]


def _ret_fwd_kernel(lg_ref, q_ref, k_ref, v_ref, g_ref, sb_ref, s0_ref, lgf_ref, lgb_ref, gn_ref,
                    o_ref, sfin_ref, s_scr):
    i = pl.program_id(1)
    c = q_ref.shape[1]
    w = q_ref.shape[2]

    @pl.when(i == 0)
    def _():
        s_scr[...] = s0_ref[0]

    q = q_ref[0]
    k = k_ref[0]
    lgf = lgf_ref[...]
    lgb = lgb_ref[...]
    pos = lax.broadcasted_iota(jnp.int32, (c, w), 0).astype(F32)
    qf = (q * jnp.exp(lgf * (pos + 1.0))).astype(BF16)
    qb = (q * jnp.exp(lgb * (float(c) - pos))).astype(BF16)
    kzt = (k * jnp.exp(lgf * (float(c) - 1.0 - pos))).T.astype(BF16)
    qh = q.astype(BF16)
    kh = k.astype(BF16)
    vh = v_ref[0].astype(BF16)
    diff = (lax.broadcasted_iota(jnp.int32, (c, c), 0) - lax.broadcasted_iota(jnp.int32, (c, c), 1)).astype(F32)
    cdec = jnp.exp(lgf * float(c))
    outs = []
    for h in range(RET_HEADS):
        lo, hi = h * HEAD_DIM, (h + 1) * HEAD_DIM
        sc = _dot_nt(qh[:, lo:hi], kh[:, lo:hi])
        dec = jnp.where(diff >= 0.0, jnp.exp(lg_ref[0, h] * jnp.maximum(diff, 0.0)),
                        jnp.exp(lg_ref[1, h] * jnp.maximum(-diff, 0.0)))
        y = (_dot((sc * dec).astype(BF16), vh[:, lo:hi])
             + _dot(qf[:, lo:hi], s_scr[h].astype(BF16))
             + _dot(qb[:, lo:hi], sb_ref[0, 0, h].astype(BF16)))
        mu = jnp.mean(y, axis=-1, keepdims=True)
        yc = y - mu
        var = jnp.mean(yc * yc, axis=-1, keepdims=True)
        outs.append(yc * lax.rsqrt(var + EPS))
        s_scr[h] = cdec[:, lo:hi] * s_scr[h] + _dot(kzt[lo:hi, :], vh[:, lo:hi])
    gate = g_ref[0]
    o_ref[0] = jnp.concatenate(outs, axis=1) * gn_ref[...] * (gate * jax.nn.sigmoid(gate))

    @pl.when(i == pl.num_programs(1) - 1)
    def _():
        sfin_ref[0] = s_scr[...]


def _retention(q, k, v, gate, s0_f, s0_b, lg, lgf_row, lgb_row, gn_row):
    b, l, w = q.shape
    c = RET_CHUNK
    nc = l // c
    st = (1, RET_HEADS, HEAD_DIM, HEAD_DIM)
    st_spec = pl.BlockSpec(st, lambda i, j: (i, 0, 0, 0))
    row = pl.BlockSpec((1, w), lambda i, j: (0, 0))
    sb_all, sfin_b = pl.pallas_call(
        _ret_bwd_kernel,
        grid=(b, nc),
        in_specs=[pl.BlockSpec((1, c, w), lambda i, j: (i, nc - 1 - j, 0)),
                  pl.BlockSpec((1, c, w), lambda i, j: (i, nc - 1 - j, 0)),
                  st_spec, row],
        out_specs=[pl.BlockSpec((1,) + st, lambda i, j: (i, nc - 1 - j, 0, 0, 0)), st_spec],
        out_shape=[jax.ShapeDtypeStruct((b, nc) + st[1:], F32), jax.ShapeDtypeStruct((b,) + st[1:], F32)],
        scratch_shapes=[pltpu.VMEM(st[1:], F32)],
        compiler_params=_cparams("arbitrary", "arbitrary"),
        name="retention_right_states",
    )(k, v, s0_b, lgb_row)
    tok = pl.BlockSpec((1, c, w), lambda i, j: (i, j, 0))
    out, sfin_f = pl.pallas_call(
        _ret_fwd_kernel,
        grid=(b, nc),
        in_specs=[pl.BlockSpec(memory_space=pltpu.SMEM), tok, tok, tok, tok,
                  pl.BlockSpec((1,) + st, lambda i, j: (i, j, 0, 0, 0)), st_spec, row, row, row],
        out_specs=[tok, st_spec],
        out_shape=[jax.ShapeDtypeStruct((b, l, w), F32), jax.ShapeDtypeStruct((b,) + st[1:], F32)],
        scratch_shapes=[pltpu.VMEM(st[1:], F32)],
        compiler_params=_cparams("arbitrary", "arbitrary"),
        name="retention",
    )(lg, q, k, v, gate, sb_all, s0_f, lgf_row, lgb_row, gn_row)
    return out, sfin_f, sfin_b


def _layer_norm(x, g, b):
    mu = jnp.mean(x, axis=-1, keepdims=True)
    xc = x - mu
    var = jnp.mean(xc * xc, axis=-1, keepdims=True)
    return xc * lax.rsqrt(var + EPS) * g + b


def _outproj_kernel(alpha, cu_ref, cup_ref, cun_ref, cb_ref, cw_ref, ret_ref, att_ref, x_ref, g1_ref,
                    w_ref, lng_ref, lnb_ref, sc_ref, sh_ref, wrh_ref, wrl_ref, br_ref,
                    x1_ref, h2_ref, lg_ref):
    j = pl.program_id(1)
    t = cu_ref[0]
    tm = t.shape[0]
    prev_row = jnp.where(j == 0, 0.0, cup_ref[0][7:8, :])
    next_row = jnp.where(j == pl.num_programs(1) - 1, 0.0, cun_ref[0][0:1, :])
    row = lax.broadcasted_iota(jnp.int32, t.shape, 0)
    t_prev = jnp.where(row == 0, prev_row, pltpu.roll(t, 1, axis=0))
    t_next = jnp.where(row == tm - 1, next_row, pltpu.roll(t, tm - 1, axis=0))
    conv = cb_ref[0] * (t_prev * cw_ref[0] + t * cw_ref[1] + t_next * cw_ref[2])
    y = (_dot(conv.astype(BF16), w_ref[0:256, :])
         + _dot(ret_ref[0].astype(BF16), w_ref[256:512, :])
         + _dot(att_ref[0].astype(BF16), w_ref[512:1024, :]))
    x1 = _layer_norm(alpha * x_ref[0] + g1_ref[0] * y, lng_ref[...], lnb_ref[...])
    x1_ref[0] = x1
    h2 = x1 * sc_ref[0] + sh_ref[0]
    h2_ref[0] = h2.astype(BF16)
    h_hi, h_lo = _split_bf16(h2)
    wrh = wrh_ref[...]
    lg_ref[0] = _dot(h_hi, wrh) + _dot(h_lo, wrh) + _dot(h_hi, wrl_ref[...]) + br_ref[...]


def _outproj(alpha, cu, cb, cw, ret, att, x, g1, w_bf, lng, lnb, sc2, sh2, wrh, wrl, br, tm):
    b, l, d = x.shape
    tok = lambda w: pl.BlockSpec((1, tm, w), lambda i, j: (i, j, 0))
    vec = pl.BlockSpec((1, 1, d), lambda i, j: (i, 0, 0))
    full = lambda s: pl.BlockSpec(s, lambda i, j: (0,) * len(s))
    r8 = tm // 8
    nb8 = l // 8
    return pl.pallas_call(
        functools.partial(_outproj_kernel, alpha),
        grid=(b, l // tm),
        in_specs=[tok(256),
                  pl.BlockSpec((1, 8, 256), lambda i, j: (i, jnp.maximum(j * r8 - 1, 0), 0)),
                  pl.BlockSpec((1, 8, 256), lambda i, j: (i, jnp.minimum((j + 1) * r8, nb8 - 1), 0)),
                  tok(256), full((3, 1, 256)), tok(256), tok(512), tok(d), vec,
                  full((d, d)), full((1, d)), full((1, d)), vec, vec,
                  full((d, ROUTER_PAD)), full((d, ROUTER_PAD)), full((1, ROUTER_PAD))],
        out_specs=[tok(d), tok(d), tok(ROUTER_PAD)],
        out_shape=[jax.ShapeDtypeStruct((b, l, d), F32), jax.ShapeDtypeStruct((b, l, d), BF16),
                   jax.ShapeDtypeStruct((b, l, ROUTER_PAD), F32)],
        compiler_params=_cparams("arbitrary", "arbitrary"),
        name="outproj_ln_router",
    )(cu, cu, cu, cb, cw, ret, att, x, g1, w_bf, lng, lnb, sc2, sh2, wrh, wrl, br)


def _ffn_kernel(be_ref, nu_ref, x_ref, wgu_ref, bgu_ref, wd_ref, bd_ref, o_ref):
    j = pl.program_id(0)
    f = wd_ref.shape[1]

    @pl.when(j < nu_ref[0])
    def _():
        gu = _dot(x_ref[...], wgu_ref[0]) + bgu_ref[0]
        gate = jnp.minimum(gu[:, :f], SWIGLU_LIMIT)
        up = jnp.clip(gu[:, f:], -SWIGLU_LIMIT, SWIGLU_LIMIT)
        act = (up + 1.0) * (gate * jax.nn.sigmoid(SWIGLU_ALPHA * gate))
        o_ref[...] = _dot(act.astype(BF16), wd_ref[0]) + bd_ref[0]

    @pl.when(j >= nu_ref[0])
    def _():
        o_ref[...] = jnp.zeros_like(o_ref)


def _expert_ffn(block_e, n_used, xs, wgu_bf, bgu, wd_bf, bd, tm):
    n_rows, d = xs.shape
    ne, _, f2 = wgu_bf.shape
    f = f2 // 2
    grid_spec = pltpu.PrefetchScalarGridSpec(
        num_scalar_prefetch=2,
        grid=(n_rows // tm,),
        in_specs=[pl.BlockSpec((tm, d), lambda j, be, nu: (j, 0)),
                  pl.BlockSpec((1, d, f2), lambda j, be, nu: (be[j], 0, 0)),
                  pl.BlockSpec((1, 1, f2), lambda j, be, nu: (be[j], 0, 0)),
                  pl.BlockSpec((1, f, d), lambda j, be, nu: (be[j], 0, 0)),
                  pl.BlockSpec((1, 1, d), lambda j, be, nu: (be[j], 0, 0))],
        out_specs=pl.BlockSpec((tm, d), lambda j, be, nu: (j, 0)),
    )
    return pl.pallas_call(
        _ffn_kernel,
        grid_spec=grid_spec,
        out_shape=jax.ShapeDtypeStruct((n_rows, d), F32),
        compiler_params=_cparams("arbitrary"),
        name="expert_ffn",
    )(block_e, n_used, xs, wgu_bf, bgu.reshape(ne, 1, f2), wd_bf, bd.reshape(ne, 1, d))


def _combine_kernel(alpha, g_ref, w_ref, x_ref, g2_ref, lng_ref, lnb_ref, o_ref):
    w = w_ref[0]
    f = g_ref[0, 0] * w[:, 0:1]
    for kk in range(1, TOP_K):
        f = f + g_ref[kk, 0] * w[:, kk:kk + 1]
    o_ref[0] = _layer_norm(alpha * x_ref[0] + g2_ref[0] * f, lng_ref[...], lnb_ref[...])


def _combine(alpha, gathered, wts, x1, g2, lng, lnb, tm):
    b, l, d = x1.shape
    tok = pl.BlockSpec((1, tm, d), lambda i, j: (i, j, 0))
    full = pl.BlockSpec((1, d), lambda i, j: (0, 0))
    return pl.pallas_call(
        functools.partial(_combine_kernel, alpha),
        grid=(b, l // tm),
        in_specs=[pl.BlockSpec((TOP_K, 1, tm, d), lambda i, j: (0, i, j, 0)),
                  pl.BlockSpec((1, tm, TOP_K), lambda i, j: (i, j, 0)),
                  tok, pl.BlockSpec((1, 1, d), lambda i, j: (i, 0, 0)), full, full],
        out_specs=tok,
        out_shape=jax.ShapeDtypeStruct((b, l, d), F32),
        compiler_params=_cparams("arbitrary", "arbitrary"),
        name="combine_ln",
    )(gathered, wts, x1, g2, lng, lnb)


def _rope_tables(l):
    rows = l // GRID_W
    axis_dim = HEAD_DIM // 2
    inv_freq = ROPE_BASE ** (-jnp.arange(0, axis_dim, 2, dtype=F32) / axis_dim)
    row = jnp.repeat(jnp.arange(rows, dtype=F32), GRID_W)
    col = jnp.tile(jnp.arange(GRID_W, dtype=F32), rows)
    ang = jnp.stack([row[:, None] * inv_freq, col[:, None] * inv_freq], axis=1)
    cos, sin = jnp.cos(ang), jnp.sin(ang)
    cos64 = jnp.broadcast_to(cos[:, :, None, :], (l, 2, 2, HEAD_DIM // 4)).reshape(l, HEAD_DIM)
    sin64 = jnp.stack([-sin, sin], axis=2).reshape(l, HEAD_DIM)
    return jnp.tile(cos64, (1, LANES // HEAD_DIM)), jnp.tile(sin64, (1, LANES // HEAD_DIM))


def _route(logits, tm):
    t = logits.shape[0]
    n = t * TOP_K
    top_v, top_e = lax.top_k(logits, TOP_K)
    top_w = jax.nn.softmax(top_v, axis=-1)
    flat_e = top_e.reshape(n).astype(jnp.int32)
    oh = (flat_e[:, None] == jnp.arange(N_EXPERTS, dtype=jnp.int32)[None, :]).astype(jnp.int32)
    csum = jnp.cumsum(oh, axis=0)
    rank = jnp.sum((csum - oh) * oh, axis=1)
    counts = csum[-1]
    padded = (counts + tm - 1) // tm * tm
    pad_end = jnp.cumsum(padded)
    pad_start = pad_end - padded
    dest = pad_start[flat_e] + rank
    n_tiles = (n + N_EXPERTS * (tm - 1) + tm - 1) // tm
    block_e = jnp.minimum(jnp.searchsorted(pad_end, jnp.arange(n_tiles, dtype=jnp.int32) * tm, side='right'),
                          N_EXPERTS - 1).astype(jnp.int32)
    n_used = (pad_end[-1] // tm).astype(jnp.int32).reshape(1)
    row_tok = jnp.zeros((n_tiles * tm,), jnp.int32).at[dest].set(jnp.arange(n, dtype=jnp.int32) // TOP_K)
    return top_w, dest, row_tok, block_e, n_used


def kernel(x, c, ctx, c_ctx, w_mod, b_mod, w_in, conv_w, ret_decay_exp, ret_gn_g, q_norm_g, k_norm_g, w_out,
           ln_g, ln_b, w_router, b_router, w_gate_up, b_gate_up, w_down, b_down):
    depth = w_mod.shape[0]
    alpha = (2.0 * depth) ** 0.25
    b, l, d = x.shape
    lc = ctx.shape[1]
    tm_lat, tm_ctx, tm_moe, tq = 512, 256, 512, 128

    n_rows = (b + 1 + 7) // 8 * 8
    c_all = jnp.zeros((n_rows, d), F32).at[:b].set(c).at[b].set(c_ctx)
    mod = _modulation(c_all, w_mod, b_mod)

    cos_l, sin_l = _rope_tables(l)
    cos_c, sin_c = jnp.ones((lc, LANES), F32), jnp.zeros((lc, LANES), F32)
    lane_head = jnp.arange(LANES) // HEAD_DIM
    ones_bd = (lane_head[:, None] == lane_head[None, :]).astype(BF16)
    zero_state = jnp.zeros((b, RET_HEADS, HEAD_DIM, HEAD_DIM), F32)

    for layer in range(depth):
        last = layer == depth - 1
        m_lat = mod[layer, :b].reshape(b, 1, N_MOD, d)
        m_ctx = jnp.broadcast_to(mod[layer, b].reshape(1, 1, N_MOD, d), (b, 1, N_MOD, d))
        sh1, sc1, g1, sh2, sc2, g2 = (m_lat[:, :, i] for i in range(N_MOD))
        sh1c, sc1c, g1c, sh2c, sc2c, g2c = (m_ctx[:, :, i] for i in range(N_MOD))

        w_in_bf = w_in[layer].astype(BF16)
        w_out_bf = w_out[layer].astype(BF16)
        gq = jnp.tile(q_norm_g[layer], LANES // HEAD_DIM).reshape(1, LANES)
        gk = jnp.tile(k_norm_g[layer], LANES // HEAD_DIM).reshape(1, LANES)
        log_gamma = jnp.log1p(-jnp.exp2(-ret_decay_exp[layer].astype(F32)))
        lg_rows = jnp.repeat(log_gamma, HEAD_DIM, axis=1)
        lgf_row, lgb_row = lg_rows[0:1], lg_rows[1:2]
        gn_row = ret_gn_g[layer].reshape(1, -1)
        cw = conv_w[layer].T.reshape(3, 1, -1)
        wr_pad = jnp.zeros((d, ROUTER_PAD), F32).at[:, :N_EXPERTS].set(w_router[layer])
        wrh = wr_pad.astype(BF16)
        wrl = (wr_pad - wrh.astype(F32)).astype(BF16)
        br = jnp.zeros((1, ROUTER_PAD), F32).at[0, :N_EXPERTS].set(b_router[layer])
        lng1, lnb1 = ln_g[layer, 0].reshape(1, d), ln_b[layer, 0].reshape(1, d)
        lng2, lnb2 = ln_g[layer, 1].reshape(1, d), ln_b[layer, 1].reshape(1, d)

        zc = _inproj(ctx, 1.0 + sc1c, sh1c, w_in_bf, cos_c, sin_c, gq, gk, ones_bd, tm_ctx)
        zl = _inproj(x, 1.0 + sc1, sh1, w_in_bf, cos_l, sin_l, gq, gk, ones_bd, tm_lat)
        cu_c, cb_c, rq_c, rk_c, rv_c, rg_c, aq_c, ak_c, av_c = zc
        cu_l, cb_l, rq_l, rk_l, rv_l, rg_l, aq_l, ak_l, av_l = zl

        ret_c, s_fwd, s_bwd = _retention(rq_c, rk_c, rv_c, rg_c, zero_state, zero_state,
                                         log_gamma, lgf_row, lgb_row, gn_row)
        ret_l, _, _ = _retention(rq_l, rk_l, rv_l, rg_l, s_fwd, s_bwd, log_gamma, lgf_row, lgb_row, gn_row)

        def kv_heads(a):
            return a.reshape(b, a.shape[1], -1, HEAD_DIM).transpose(0, 2, 1, 3)

        k_all = kv_heads(jnp.concatenate([ak_c, ak_l], axis=1))
        v_all = kv_heads(jnp.concatenate([av_c, av_l], axis=1))
        att_l = _attention(aq_l, k_all, v_all, tq)

        x1, h2_l, logit_l = _outproj(alpha, cu_l, cb_l, cw, ret_l, att_l, x, g1, w_out_bf, lng1, lnb1,
                                     1.0 + sc2, sh2, wrh, wrl, br, tm_lat)
        if not last:
            att_c = _attention(aq_c, kv_heads(ak_c), kv_heads(av_c), tq)
            ctx1, h2_c, logit_c = _outproj(alpha, cu_c, cb_c, cw, ret_c, att_c, ctx, g1c, w_out_bf, lng1, lnb1,
                                           1.0 + sc2c, sh2c, wrh, wrl, br, tm_ctx)
            h2 = jnp.concatenate([h2_c.reshape(b * lc, d), h2_l.reshape(b * l, d)], axis=0)
            logits = jnp.concatenate([logit_c.reshape(b * lc, -1), logit_l.reshape(b * l, -1)], axis=0)
        else:
            h2 = h2_l.reshape(b * l, d)
            logits = logit_l.reshape(b * l, -1)

        top_w, dest, row_tok, block_e, n_used = _route(logits[:, :N_EXPERTS], tm_moe)
        xs = jnp.take(h2, row_tok, axis=0)
        ys = _expert_ffn(block_e, n_used, xs, w_gate_up[layer].astype(BF16), b_gate_up[layer],
                         w_down[layer].astype(BF16), b_down[layer], tm_moe)
        gathered = jnp.take(ys, dest.reshape(-1, TOP_K).T, axis=0)
        if not last:
            n_c = b * lc
            ctx = _combine(alpha, gathered[:, :n_c].reshape(TOP_K, b, lc, d), top_w[:n_c].reshape(b, lc, TOP_K),
                           ctx1, g2c, lng2, lnb2, tm_ctx)
            x = _combine(alpha, gathered[:, n_c:].reshape(TOP_K, b, l, d), top_w[n_c:].reshape(b, l, TOP_K),
                         x1, g2, lng2, lnb2, tm_lat)
        else:
            x = _combine(alpha, gathered.reshape(TOP_K, b, l, d), top_w.reshape(b, l, TOP_K),
                         x1, g2, lng2, lnb2, tm_lat)
    return x
```

```python
import functools

import jax
import jax.numpy as jnp
from jax import lax
from jax.experimental import pallas as pl
from jax.experimental.pallas import tpu as pltpu
from jax.experimental.pallas import tpu_sc as plsc

F32 = jnp.float32
BF16 = jnp.bfloat16

HEAD_DIM = 64
GRID_W = 64
ROPE_BASE = 10000.0
N_EXPERTS = 32
TOP_K = 4
SWIGLU_ALPHA = 1.702
SWIGLU_LIMIT = 7.0
N_MOD = 6
EPS = 1e-6
RET_HEADS = 4
RET_CHUNK = 128
ATT_Q_PER_KV = 4
LANES = 128
ROUTER_PAD = 128
VMEM_LIMIT = 56 * 1024 * 1024
SUBLANES = 8
SC_CORES = 2
SC_SUBCORES = 16
SC_WINDOW = 32


def _dot(a, b):
    return jnp.dot(a, b, preferred_element_type=F32)


def _dot_nt(a, b):
    return lax.dot_general(a, b, (((1,), (1,)), ((), ())), preferred_element_type=F32)


def _split_bf16(x):
    hi = x.astype(BF16)
    lo = (x - hi.astype(F32)).astype(BF16)
    return hi, lo


def _cparams(*sem):
    return pltpu.CompilerParams(dimension_semantics=sem, vmem_limit_bytes=VMEM_LIMIT)


def _mod_kernel(c_ref, w_ref, b_ref, o_ref):
    c = c_ref[...]
    a = c * jax.nn.sigmoid(c)
    a_hi, a_lo = _split_bf16(a)
    w_hi, w_lo = _split_bf16(w_ref[0])
    o_ref[0] = _dot(a_hi, w_hi) + _dot(a_lo, w_hi) + _dot(a_hi, w_lo) + b_ref[0]


def _modulation(c_all, w_mod, b_mod):
    depth, d, n = w_mod.shape
    r = c_all.shape[0]
    tn = 1536
    return pl.pallas_call(
        _mod_kernel,
        grid=(depth, n // tn),
        in_specs=[
            pl.BlockSpec((r, d), lambda l, j: (0, 0)),
            pl.BlockSpec((1, d, tn), lambda l, j: (l, 0, j)),
            pl.BlockSpec((1, 1, tn), lambda l, j: (l, 0, j)),
        ],
        out_specs=pl.BlockSpec((1, r, tn), lambda l, j: (l, 0, j)),
        out_shape=jax.ShapeDtypeStruct((depth, r, n), F32),
        compiler_params=_cparams("arbitrary", "arbitrary"),
        name="modulation",
    )(c_all, w_mod, b_mod.reshape(depth, 1, n))


def _inproj_kernel(x_ref, sc_ref, sh_ref, w_ref, cos_ref, sin_ref, gq_ref, gk_ref, ones_ref,
                   cu_ref, cb_ref, rq_ref, rk_ref, rv_ref, rg_ref, aq_ref, ak_ref, av_ref):
    h = (x_ref[0] * sc_ref[0] + sh_ref[0]).astype(BF16)
    z = _dot(h, w_ref[...])
    cos = cos_ref[...]
    sin = sin_ref[...]
    ones = ones_ref[...]
    lane = lax.broadcasted_iota(jnp.int32, cos.shape, 1)
    first_half = (lane & 31) < 16

    def rope(xs):
        nxt = pltpu.roll(xs, LANES - 16, axis=1)
        prv = pltpu.roll(xs, 16, axis=1)
        return xs * cos + jnp.where(first_half, nxt, prv) * sin

    def rms(xs, g):
        s_hi, s_lo = _split_bf16(xs * xs)
        ssq = _dot(s_hi, ones) + _dot(s_lo, ones)
        return xs * lax.rsqrt(ssq * (1.0 / HEAD_DIM) + EPS) * g

    cu_ref[0] = z[:, 512:768] * z[:, 0:256]
    cb_ref[0] = z[:, 256:512]
    for j in range(2):
        lo, hi = j * LANES, (j + 1) * LANES
        rq_ref[0, :, lo:hi] = rope(z[:, 768 + lo:768 + hi])
        rk_ref[0, :, lo:hi] = rope(z[:, 1024 + lo:1024 + hi] * (HEAD_DIM ** -0.5))
    rv_ref[0] = z[:, 1280:1536]
    rg_ref[0] = z[:, 1536:1792]
    gq = gq_ref[...]
    for j in range(4):
        lo, hi = j * LANES, (j + 1) * LANES
        q = rope(rms(z[:, 1792 + lo:1792 + hi], gq)) * (HEAD_DIM ** -0.5)
        aq_ref[0, :, lo:hi] = q.astype(BF16)
    ak_ref[0] = rope(rms(z[:, 2304:2432], gk_ref[...])).astype(BF16)
    av_ref[0] = z[:, 2432:2560].astype(BF16)


def _inproj(x, sc, sh, w_bf, cos, sin, gq, gk, ones_bd, tm):
    b, l, d = x.shape
    n = w_bf.shape[1]
    tok = lambda w: pl.BlockSpec((1, tm, w), lambda i, j: (i, j, 0))
    vec = pl.BlockSpec((1, 1, d), lambda i, j: (i, 0, 0))
    full = lambda s: pl.BlockSpec(s, lambda i, j: (0,) * len(s))
    widths = (256, 256, 256, 256, 256, 256, 512, 128, 128)
    dtypes = (F32,) * 6 + (BF16,) * 3
    return pl.pallas_call(
        _inproj_kernel,
        grid=(b, l // tm),
        in_specs=[tok(d), vec, vec, full((d, n)),
                  pl.BlockSpec((tm, LANES), lambda i, j: (j, 0)),
                  pl.BlockSpec((tm, LANES), lambda i, j: (j, 0)),
                  full((1, LANES)), full((1, LANES)), full((LANES, LANES))],
        out_specs=[tok(w) for w in widths],
        out_shape=[jax.ShapeDtypeStruct((b, l, w), dt) for w, dt in zip(widths, dtypes)],
        compiler_params=_cparams("arbitrary", "arbitrary"),
        name="inproj",
    )(x, sc, sh, w_bf, cos, sin, gq, gk, ones_bd)


def _attn_kernel(q_ref, k_ref, v_ref, o_ref):
    tq = q_ref.shape[1]
    q4 = q_ref[0]
    q = jnp.concatenate([q4[:, h * HEAD_DIM:(h + 1) * HEAD_DIM] for h in range(ATT_Q_PER_KV)], axis=0)
    s = _dot_nt(q, k_ref[0, 0])
    m = jnp.max(s, axis=-1, keepdims=True)
    p = jnp.exp(s - m)
    l = jnp.sum(p, axis=-1, keepdims=True)
    o = _dot(p.astype(BF16), v_ref[0, 0]) / l
    o_ref[0] = jnp.concatenate([o[h * tq:(h + 1) * tq] for h in range(ATT_Q_PER_KV)], axis=1)


def _attention(q, k, v, tq):
    b, lq, wq = q.shape
    hkv, lk = k.shape[1], k.shape[2]
    wg = ATT_Q_PER_KV * HEAD_DIM
    return pl.pallas_call(
        _attn_kernel,
        grid=(b, hkv, lq // tq),
        in_specs=[pl.BlockSpec((1, tq, wg), lambda i, g, j: (i, j, g)),
                  pl.BlockSpec((1, 1, lk, HEAD_DIM), lambda i, g, j: (i, g, 0, 0)),
                  pl.BlockSpec((1, 1, lk, HEAD_DIM), lambda i, g, j: (i, g, 0, 0))],
        out_specs=pl.BlockSpec((1, tq, wg), lambda i, g, j: (i, j, g)),
        out_shape=jax.ShapeDtypeStruct((b, lq, wq), F32),
        compiler_params=_cparams("arbitrary", "arbitrary", "arbitrary"),
        name="attention",
    )(q, k, v)


def _ret_bwd_kernel(k_ref, v_ref, s0_ref, lgb_ref, sb_ref, sfin_ref, s_scr):
    i = pl.program_id(1)
    c = k_ref.shape[1]
    w = k_ref.shape[2]

    @pl.when(i == 0)
    def _():
        s_scr[...] = s0_ref[0]

    sb_ref[0, 0] = s_scr[...]
    lgb = lgb_ref[...]
    pos = lax.broadcasted_iota(jnp.int32, (c, w), 0).astype(F32)
    kzt = (k_ref[0] * jnp.exp(lgb * pos)).T.astype(BF16)
    v = v_ref[0].astype(BF16)
    cdec = jnp.exp(lgb * float(c))
    for h in range(RET_HEADS):
        lo, hi = h * HEAD_DIM, (h + 1) * HEAD_DIM
        s_scr[h] = cdec[:, lo:hi] * s_scr[h] + _dot(kzt[lo:hi, :], v[:, lo:hi])

    @pl.when(i == pl.num_programs(1) - 1)
    def _():
        sfin_ref[0] = s_scr[...]


def _ret_fwd_kernel(lg_ref, q_ref, k_ref, v_ref, g_ref, sb_ref, s0_ref, lgf_ref, lgb_ref, gn_ref,
                    o_ref, sfin_ref, s_scr):
    i = pl.program_id(1)
    c = q_ref.shape[1]
    w = q_ref.shape[2]

    @pl.when(i == 0)
    def _():
        s_scr[...] = s0_ref[0]

    q = q_ref[0]
    k = k_ref[0]
    lgf = lgf_ref[...]
    lgb = lgb_ref[...]
    pos = lax.broadcasted_iota(jnp.int32, (c, w), 0).astype(F32)
    qf = (q * jnp.exp(lgf * (pos + 1.0))).astype(BF16)
    qb = (q * jnp.exp(lgb * (float(c) - pos))).astype(BF16)
    kzt = (k * jnp.exp(lgf * (float(c) - 1.0 - pos))).T.astype(BF16)
    qh = q.astype(BF16)
    kh = k.astype(BF16)
    vh = v_ref[0].astype(BF16)
    diff = (lax.broadcasted_iota(jnp.int32, (c, c), 0) - lax.broadcasted_iota(jnp.int32, (c, c), 1)).astype(F32)
    cdec = jnp.exp(lgf * float(c))
    outs = []
    for h in range(RET_HEADS):
        lo, hi = h * HEAD_DIM, (h + 1) * HEAD_DIM
        sc = _dot_nt(qh[:, lo:hi], kh[:, lo:hi])
        dec = jnp.where(diff >= 0.0, jnp.exp(lg_ref[0, h] * jnp.maximum(diff, 0.0)),
                        jnp.exp(lg_ref[1, h] * jnp.maximum(-diff, 0.0)))
        y = (_dot((sc * dec).astype(BF16), vh[:, lo:hi])
             + _dot(qf[:, lo:hi], s_scr[h].astype(BF16))
             + _dot(qb[:, lo:hi], sb_ref[0, 0, h].astype(BF16)))
        mu = jnp.mean(y, axis=-1, keepdims=True)
        yc = y - mu
        var = jnp.mean(yc * yc, axis=-1, keepdims=True)
        outs.append(yc * lax.rsqrt(var + EPS))
        s_scr[h] = cdec[:, lo:hi] * s_scr[h] + _dot(kzt[lo:hi, :], vh[:, lo:hi])
    gate = g_ref[0]
    o_ref[0] = jnp.concatenate(outs, axis=1) * gn_ref[...] * (gate * jax.nn.sigmoid(gate))

    @pl.when(i == pl.num_programs(1) - 1)
    def _():
        sfin_ref[0] = s_scr[...]


def _retention(q, k, v, gate, s0_f, s0_b, lg, lgf_row, lgb_row, gn_row):
    b, l, w = q.shape
    c = RET_CHUNK
    nc = l // c
    st = (1, RET_HEADS, HEAD_DIM, HEAD_DIM)
    st_spec = pl.BlockSpec(st, lambda i, j: (i, 0, 0, 0))
    row = pl.BlockSpec((1, w), lambda i, j: (0, 0))
    sb_all, sfin_b = pl.pallas_call(
        _ret_bwd_kernel,
        grid=(b, nc),
        in_specs=[pl.BlockSpec((1, c, w), lambda i, j: (i, nc - 1 - j, 0)),
                  pl.BlockSpec((1, c, w), lambda i, j: (i, nc - 1 - j, 0)),
                  st_spec, row],
        out_specs=[pl.BlockSpec((1,) + st, lambda i, j: (i, nc - 1 - j, 0, 0, 0)), st_spec],
        out_shape=[jax.ShapeDtypeStruct((b, nc) + st[1:], F32), jax.ShapeDtypeStruct((b,) + st[1:], F32)],
        scratch_shapes=[pltpu.VMEM(st[1:], F32)],
        compiler_params=_cparams("arbitrary", "arbitrary"),
        name="retention_right_states",
    )(k, v, s0_b, lgb_row)
    tok = pl.BlockSpec((1, c, w), lambda i, j: (i, j, 0))
    out, sfin_f = pl.pallas_call(
        _ret_fwd_kernel,
        grid=(b, nc),
        in_specs=[pl.BlockSpec(memory_space=pltpu.SMEM), tok, tok, tok, tok,
                  pl.BlockSpec((1,) + st, lambda i, j: (i, j, 0, 0, 0)), st_spec, row, row, row],
        out_specs=[tok, st_spec],
        out_shape=[jax.ShapeDtypeStruct((b, l, w), F32), jax.ShapeDtypeStruct((b,) + st[1:], F32)],
        scratch_shapes=[pltpu.VMEM(st[1:], F32)],
        compiler_params=_cparams("arbitrary", "arbitrary"),
        name="retention",
    )(lg, q, k, v, gate, sb_all, s0_f, lgf_row, lgb_row, gn_row)
    return out, sfin_f, sfin_b


def _layer_norm(x, g, b):
    mu = jnp.mean(x, axis=-1, keepdims=True)
    xc = x - mu
    var = jnp.mean(xc * xc, axis=-1, keepdims=True)
    return xc * lax.rsqrt(var + EPS) * g + b


def _store_rows_tiled(ref, x):
    tm = x.shape[0]
    for s in range(SUBLANES):
        ref[pl.ds(s, tm, stride=SUBLANES), :] = x[:, s * LANES:(s + 1) * LANES]


def _load_rows_tiled(ref, tm):
    return jnp.concatenate([ref[pl.ds(s, tm, stride=SUBLANES), :] for s in range(SUBLANES)], axis=1)


def _outproj_kernel(alpha, cu_ref, cup_ref, cun_ref, cb_ref, cw_ref, ret_ref, att_ref, x_ref, g1_ref,
                    w_ref, lng_ref, lnb_ref, sc_ref, sh_ref, wrh_ref, wrl_ref, br_ref, cnt0_ref,
                    x1_ref, h2_ref, ri_ref, rw_ref, cnt_ref, run_scr):
    i = pl.program_id(0)
    j = pl.program_id(1)

    @pl.when((i == 0) & (j == 0))
    def _():
        run_scr[...] = cnt0_ref[...]

    t = cu_ref[0]
    tm = t.shape[0]
    prev_row = jnp.where(j == 0, 0.0, cup_ref[0][7:8, :])
    next_row = jnp.where(j == pl.num_programs(1) - 1, 0.0, cun_ref[0][0:1, :])
    row = lax.broadcasted_iota(jnp.int32, t.shape, 0)
    t_prev = jnp.where(row == 0, prev_row, pltpu.roll(t, 1, axis=0))
    t_next = jnp.where(row == tm - 1, next_row, pltpu.roll(t, tm - 1, axis=0))
    conv = cb_ref[0] * (t_prev * cw_ref[0] + t * cw_ref[1] + t_next * cw_ref[2])
    y = (_dot(conv.astype(BF16), w_ref[0:256, :])
         + _dot(ret_ref[0].astype(BF16), w_ref[256:512, :])
         + _dot(att_ref[0].astype(BF16), w_ref[512:1024, :]))
    x1 = _layer_norm(alpha * x_ref[0] + g1_ref[0] * y, lng_ref[...], lnb_ref[...])
    x1_ref[0] = x1
    h2 = x1 * sc_ref[0] + sh_ref[0]
    _store_rows_tiled(h2_ref, h2)
    h_hi, h_lo = _split_bf16(h2)
    wrh = wrh_ref[...]
    logits = _dot(h_hi, wrh) + _dot(h_lo, wrh) + _dot(h_hi, wrl_ref[...]) + br_ref[...]

    lane = lax.broadcasted_iota(jnp.int32, logits.shape, 1)
    lane_f = lane.astype(F32)
    work = jnp.where(lane < N_EXPERTS, logits, -jnp.inf)
    vals, sels = [], []
    for _ in range(TOP_K):
        m = jnp.max(work, axis=-1, keepdims=True)
        first = jnp.min(jnp.where(work == m, lane_f, float(ROUTER_PAD)), axis=-1, keepdims=True)
        sel = lane_f == first
        vals.append(m)
        sels.append(sel)
        work = jnp.where(sel, -jnp.inf, work)
    exps = [jnp.exp(v - vals[0]) for v in vals]
    denom = exps[0]
    for e in exps[1:]:
        denom = denom + e

    cnt = jnp.zeros(logits.shape, F32)
    for sel in sels:
        cnt = cnt + jnp.where(sel, 1.0, 0.0)
    r_i = lax.broadcasted_iota(jnp.int32, (tm, tm), 0)
    c_i = lax.broadcasted_iota(jnp.int32, (tm, tm), 1)
    before = jnp.where(c_i < r_i, 1.0, 0.0).astype(BF16)
    base = _dot(before, cnt.astype(BF16)) + run_scr[...]
    run_scr[...] = run_scr[...] + jnp.sum(cnt, axis=0, keepdims=True)
    cnt_ref[...] = run_scr[...]

    ri = jnp.zeros(logits.shape, F32)
    rw = jnp.zeros(logits.shape, F32)
    for kk in range(TOP_K):
        e_idx = jnp.sum(jnp.where(sels[kk], lane_f, 0.0), axis=-1, keepdims=True)
        rank = jnp.sum(jnp.where(sels[kk], base, 0.0), axis=-1, keepdims=True)
        ri = jnp.where(lane == kk, e_idx, ri)
        ri = jnp.where(lane == TOP_K + kk, rank, ri)
        rw = jnp.where(lane == kk, exps[kk] / denom, rw)
    ri_ref[0] = ri.astype(jnp.int32)
    rw_ref[0] = rw


def _outproj(alpha, cu, cb, cw, ret, att, x, g1, w_bf, lng, lnb, sc2, sh2, wrh, wrl, br, cnt0, tm):
    b, l, d = x.shape
    nj = l // tm
    tok = lambda w: pl.BlockSpec((1, tm, w), lambda i, j: (i, j, 0))
    vec = pl.BlockSpec((1, 1, d), lambda i, j: (i, 0, 0))
    full = lambda s: pl.BlockSpec(s, lambda i, j: (0,) * len(s))
    r8 = tm // 8
    nb8 = l // 8
    return pl.pallas_call(
        functools.partial(_outproj_kernel, alpha),
        grid=(b, nj),
        in_specs=[tok(256),
                  pl.BlockSpec((1, 8, 256), lambda i, j: (i, jnp.maximum(j * r8 - 1, 0), 0)),
                  pl.BlockSpec((1, 8, 256), lambda i, j: (i, jnp.minimum((j + 1) * r8, nb8 - 1), 0)),
                  tok(256), full((3, 1, 256)), tok(256), tok(512), tok(d), vec,
                  full((d, d)), full((1, d)), full((1, d)), vec, vec,
                  full((d, ROUTER_PAD)), full((d, ROUTER_PAD)), full((1, ROUTER_PAD)), full((1, ROUTER_PAD))],
        out_specs=[tok(d), pl.BlockSpec((tm * SUBLANES, LANES), lambda i, j: (i * nj + j, 0)),
                   tok(ROUTER_PAD), tok(ROUTER_PAD), full((1, ROUTER_PAD))],
        out_shape=[jax.ShapeDtypeStruct((b, l, d), F32),
                   jax.ShapeDtypeStruct((b * l * SUBLANES, LANES), F32),
                   jax.ShapeDtypeStruct((b, l, ROUTER_PAD), jnp.int32),
                   jax.ShapeDtypeStruct((b, l, ROUTER_PAD), F32),
                   jax.ShapeDtypeStruct((1, ROUTER_PAD), F32)],
        scratch_shapes=[pltpu.VMEM((1, ROUTER_PAD), F32)],
        compiler_params=_cparams("arbitrary", "arbitrary"),
        name="outproj_ln_router",
    )(cu, cu, cu, cb, cw, ret, att, x, g1, w_bf, lng, lnb, sc2, sh2, wrh, wrl, br, cnt0)


def _ffn_kernel(be_ref, nu_ref, x_ref, wgu_ref, bgu_ref, wd_ref, bd_ref, o_ref, wgu_scr, wd_scr):
    j = pl.program_id(0)
    f = wd_ref.shape[1]
    tm = x_ref.shape[0] // SUBLANES

    @pl.when(j < nu_ref[0])
    def _():
        @pl.when((j == 0) | (be_ref[j] != be_ref[jnp.maximum(j - 1, 0)]))
        def _():
            wgu_scr[...] = wgu_ref[0].astype(BF16)
            wd_scr[...] = wd_ref[0].astype(BF16)

        x = _load_rows_tiled(x_ref, tm).astype(BF16)
        gu = _dot(x, wgu_scr[...]) + bgu_ref[0]
        gate = jnp.minimum(gu[:, :f], SWIGLU_LIMIT)
        up = jnp.clip(gu[:, f:], -SWIGLU_LIMIT, SWIGLU_LIMIT)
        act = (up + 1.0) * (gate * jax.nn.sigmoid(SWIGLU_ALPHA * gate))
        _store_rows_tiled(o_ref, _dot(act.astype(BF16), wd_scr[...]) + bd_ref[0])

    @pl.when(j >= nu_ref[0])
    def _():
        o_ref[...] = jnp.zeros_like(o_ref)


def _expert_ffn(block_e, n_used, xs, wgu, bgu, wd, bd, tm):
    n_rows = xs.shape[0] // SUBLANES
    ne, d, f2 = wgu.shape
    f = f2 // 2
    rows = pl.BlockSpec((tm * SUBLANES, LANES), lambda j, be, nu: (j, 0))
    grid_spec = pltpu.PrefetchScalarGridSpec(
        num_scalar_prefetch=2,
        grid=(n_rows // tm,),
        in_specs=[rows,
                  pl.BlockSpec((1, d, f2), lambda j, be, nu: (be[j], 0, 0)),
                  pl.BlockSpec((1, 1, f2), lambda j, be, nu: (be[j], 0, 0)),
                  pl.BlockSpec((1, f, d), lambda j, be, nu: (be[j], 0, 0)),
                  pl.BlockSpec((1, 1, d), lambda j, be, nu: (be[j], 0, 0))],
        out_specs=rows,
        scratch_shapes=[pltpu.VMEM((d, f2), BF16), pltpu.VMEM((f, d), BF16)],
    )
    return pl.pallas_call(
        _ffn_kernel,
        grid_spec=grid_spec,
        out_shape=jax.ShapeDtypeStruct((n_rows * SUBLANES, LANES), F32),
        compiler_params=_cparams("arbitrary"),
        name="expert_ffn",
    )(block_e, n_used, xs, wgu, bgu.reshape(ne, 1, f2), wd, bd.reshape(ne, 1, d))


def _sc_worker_base(per_worker):
    return (lax.axis_index("s") * SC_CORES + lax.axis_index("c")) * per_worker


def _sc_dispatch(rows, dest, n_out):
    t = rows.shape[0]
    kk = dest.shape[0] // t
    w = SC_WINDOW
    per_worker = t // (SC_CORES * SC_SUBCORES)
    mesh = plsc.VectorSubcoreMesh(core_axis_name="c", subcore_axis_name="s")

    @functools.partial(
        pl.kernel, mesh=mesh,
        out_type=jax.ShapeDtypeStruct((n_out,) + rows.shape[1:], rows.dtype),
        scratch_types=[pltpu.VMEM((w,), jnp.int32)] * kk + [pltpu.VMEM((w,) + rows.shape[1:], rows.dtype)])
    def scatter_rows(r_hbm, d_hbm, o_hbm, *scratch):
        idx_v, rows_v = scratch[:kk], scratch[kk]
        base = _sc_worker_base(per_worker)

        @pl.loop(0, per_worker // w)
        def _(i):
            off = base + i * w
            pltpu.sync_copy(r_hbm.at[pl.ds(off, w)], rows_v)
            for s in range(kk):
                pltpu.sync_copy(d_hbm.at[pl.ds(s * t + off, w)], idx_v[s])
            for s in range(kk):
                pltpu.sync_copy(rows_v, o_hbm.at[idx_v[s]])

    return scatter_rows(rows, dest)


def _sc_gather(table, idx):
    n = idx.shape[0]
    w = SC_WINDOW
    per_worker = n // (SC_CORES * SC_SUBCORES)
    mesh = plsc.VectorSubcoreMesh(core_axis_name="c", subcore_axis_name="s")

    @functools.partial(
        pl.kernel, mesh=mesh,
        out_type=jax.ShapeDtypeStruct((n,) + table.shape[1:], table.dtype),
        scratch_types=[pltpu.VMEM((w,), jnp.int32), pltpu.VMEM((w,) + table.shape[1:], table.dtype)])
    def gather_rows(t_hbm, i_hbm, o_hbm, idx_v, rows_v):
        base = _sc_worker_base(per_worker)

        @pl.loop(0, per_worker // w)
        def _(i):
            off = base + i * w
            pltpu.sync_copy(i_hbm.at[pl.ds(off, w)], idx_v)
            pltpu.sync_copy(t_hbm.at[idx_v], rows_v)
            pltpu.sync_copy(rows_v, o_hbm.at[pl.ds(off, w)])

    return gather_rows(table, idx)


def _combine_kernel(alpha, g_ref, w_ref, x_ref, g2_ref, lng_ref, lnb_ref, o_ref):
    tm = x_ref.shape[1]
    w = w_ref[0]
    f = _load_rows_tiled(g_ref.at[0], tm) * w[:, 0:1]
    for kk in range(1, TOP_K):
        f = f + _load_rows_tiled(g_ref.at[kk], tm) * w[:, kk:kk + 1]
    o_ref[0] = _layer_norm(alpha * x_ref[0] + g2_ref[0] * f, lng_ref[...], lnb_ref[...])


def _combine(alpha, gathered, tok_off, wts, x1, g2, lng, lnb, tm):
    b, l, d = x1.shape
    nj = l // tm
    blk_off = tok_off // tm
    tok = pl.BlockSpec((1, tm, d), lambda i, j: (i, j, 0))
    full = pl.BlockSpec((1, d), lambda i, j: (0, 0))
    return pl.pallas_call(
        functools.partial(_combine_kernel, alpha),
        grid=(b, nj),
        in_specs=[pl.BlockSpec((TOP_K, tm * SUBLANES, LANES), lambda i, j: (0, blk_off + i * nj + j, 0)),
                  pl.BlockSpec((1, tm, ROUTER_PAD), lambda i, j: (i, j, 0)),
                  tok, pl.BlockSpec((1, 1, d), lambda i, j: (i, 0, 0)), full, full],
        out_specs=tok,
        out_shape=jax.ShapeDtypeStruct((b, l, d), F32),
        compiler_params=_cparams("arbitrary", "arbitrary"),
        name="combine_ln",
    )(gathered, wts, x1, g2, lng, lnb)


def _rope_tables(l):
    rows = l // GRID_W
    axis_dim = HEAD_DIM // 2
    inv_freq = ROPE_BASE ** (-jnp.arange(0, axis_dim, 2, dtype=F32) / axis_dim)
    row = jnp.repeat(jnp.arange(rows, dtype=F32), GRID_W)
    col = jnp.tile(jnp.arange(GRID_W, dtype=F32), rows)
    ang = jnp.stack([row[:, None] * inv_freq, col[:, None] * inv_freq], axis=1)
    cos, sin = jnp.cos(ang), jnp.sin(ang)
    cos64 = jnp.broadcast_to(cos[:, :, None, :], (l, 2, 2, HEAD_DIM // 4)).reshape(l, HEAD_DIM)
    sin64 = jnp.stack([-sin, sin], axis=2).reshape(l, HEAD_DIM)
    return jnp.tile(cos64, (1, LANES // HEAD_DIM)), jnp.tile(sin64, (1, LANES // HEAD_DIM))


def _route_tables(ri, counts, tm):
    t = ri.shape[0]
    top_e = ri[:, :TOP_K]
    rank = ri[:, TOP_K:2 * TOP_K]
    padded = (counts + tm - 1) // tm * tm
    pad_end = jnp.cumsum(padded)
    pad_start = pad_end - padded
    experts = jnp.arange(N_EXPERTS, dtype=jnp.int32)
    start = jnp.sum(jnp.where(top_e[:, :, None] == experts, pad_start, 0), axis=-1)
    dest = (start + rank).T.reshape(TOP_K * t)
    n_tiles = (t * TOP_K + N_EXPERTS * (tm - 1) + tm - 1) // tm
    tile_start = jnp.arange(n_tiles, dtype=jnp.int32) * tm
    block_e = jnp.minimum(jnp.sum((pad_end[None, :] <= tile_start[:, None]).astype(jnp.int32), axis=1), N_EXPERTS - 1)
    n_used = (pad_end[-1] // tm).astype(jnp.int32).reshape(1)
    return dest, block_e.astype(jnp.int32), n_used, n_tiles


def kernel(x, c, ctx, c_ctx, w_mod, b_mod, w_in, conv_w, ret_decay_exp, ret_gn_g, q_norm_g, k_norm_g, w_out,
           ln_g, ln_b, w_router, b_router, w_gate_up, b_gate_up, w_down, b_down):
    depth = w_mod.shape[0]
    alpha = (2.0 * depth) ** 0.25
    b, l, d = x.shape
    lc = ctx.shape[1]
    tm_lat, tm_ctx, tm_moe, tq = 512, 256, 512, 128

    n_rows = (b + 1 + 7) // 8 * 8
    c_all = jnp.zeros((n_rows, d), F32).at[:b].set(c).at[b].set(c_ctx)
    mod = _modulation(c_all, w_mod, b_mod)

    cos_l, sin_l = _rope_tables(l)
    cos_c, sin_c = jnp.ones((lc, LANES), F32), jnp.zeros((lc, LANES), F32)
    lane_head = jnp.arange(LANES) // HEAD_DIM
    ones_bd = (lane_head[:, None] == lane_head[None, :]).astype(BF16)
    zero_state = jnp.zeros((b, RET_HEADS, HEAD_DIM, HEAD_DIM), F32)

    for layer in range(depth):
        last = layer == depth - 1
        m_lat = mod[layer, :b].reshape(b, 1, N_MOD, d)
        m_ctx = jnp.broadcast_to(mod[layer, b].reshape(1, 1, N_MOD, d), (b, 1, N_MOD, d))
        sh1, sc1, g1, sh2, sc2, g2 = (m_lat[:, :, i] for i in range(N_MOD))
        sh1c, sc1c, g1c, sh2c, sc2c, g2c = (m_ctx[:, :, i] for i in range(N_MOD))

        w_in_bf = w_in[layer].astype(BF16)
        w_out_bf = w_out[layer].astype(BF16)
        gq = jnp.tile(q_norm_g[layer], LANES // HEAD_DIM).reshape(1, LANES)
        gk = jnp.tile(k_norm_g[layer], LANES // HEAD_DIM).reshape(1, LANES)
        log_gamma = jnp.log1p(-jnp.exp2(-ret_decay_exp[layer].astype(F32)))
        lg_rows = jnp.repeat(log_gamma, HEAD_DIM, axis=1)
        lgf_row, lgb_row = lg_rows[0:1], lg_rows[1:2]
        gn_row = ret_gn_g[layer].reshape(1, -1)
        cw = conv_w[layer].T.reshape(3, 1, -1)
        wr_pad = jnp.zeros((d, ROUTER_PAD), F32).at[:, :N_EXPERTS].set(w_router[layer])
        wrh = wr_pad.astype(BF16)
        wrl = (wr_pad - wrh.astype(F32)).astype(BF16)
        br = jnp.zeros((1, ROUTER_PAD), F32).at[0, :N_EXPERTS].set(b_router[layer])
        lng1, lnb1 = ln_g[layer, 0].reshape(1, d), ln_b[layer, 0].reshape(1, d)
        lng2, lnb2 = ln_g[layer, 1].reshape(1, d), ln_b[layer, 1].reshape(1, d)

        zc = _inproj(ctx, 1.0 + sc1c, sh1c, w_in_bf, cos_c, sin_c, gq, gk, ones_bd, tm_ctx)
        zl = _inproj(x, 1.0 + sc1, sh1, w_in_bf, cos_l, sin_l, gq, gk, ones_bd, tm_lat)
        cu_c, cb_c, rq_c, rk_c, rv_c, rg_c, aq_c, ak_c, av_c = zc
        cu_l, cb_l, rq_l, rk_l, rv_l, rg_l, aq_l, ak_l, av_l = zl

        ret_c, s_fwd, s_bwd = _retention(rq_c, rk_c, rv_c, rg_c, zero_state, zero_state,
                                         log_gamma, lgf_row, lgb_row, gn_row)
        ret_l, _, _ = _retention(rq_l, rk_l, rv_l, rg_l, s_fwd, s_bwd, log_gamma, lgf_row, lgb_row, gn_row)

        def kv_heads(a):
            return a.reshape(b, a.shape[1], -1, HEAD_DIM).transpose(0, 2, 1, 3)

        k_all = kv_heads(jnp.concatenate([ak_c, ak_l], axis=1))
        v_all = kv_heads(jnp.concatenate([av_c, av_l], axis=1))
        att_l = _attention(aq_l, k_all, v_all, tq)

        cnt0 = jnp.zeros((1, ROUTER_PAD), F32)
        if not last:
            att_c = _attention(aq_c, kv_heads(ak_c), kv_heads(av_c), tq)
            ctx1, h2_c, ri_c, rw_c, cnt0 = _outproj(alpha, cu_c, cb_c, cw, ret_c, att_c, ctx, g1c, w_out_bf,
                                                    lng1, lnb1, 1.0 + sc2c, sh2c, wrh, wrl, br, cnt0, tm_ctx)
        x1, h2_l, ri_l, rw_l, cnt = _outproj(alpha, cu_l, cb_l, cw, ret_l, att_l, x, g1, w_out_bf, lng1, lnb1,
                                             1.0 + sc2, sh2, wrh, wrl, br, cnt0, tm_lat)
        if not last:
            n_c = b * lc
            h2 = jnp.concatenate([h2_c, h2_l], axis=0)
            ri = jnp.concatenate([ri_c.reshape(n_c, -1), ri_l.reshape(b * l, -1)], axis=0)
        else:
            n_c = 0
            h2 = h2_l
            ri = ri_l.reshape(b * l, -1)
        n_tok = n_c + b * l

        counts = cnt[0, :N_EXPERTS].astype(jnp.int32)
        dest, block_e, n_used, n_tiles = _route_tables(ri, counts, tm_moe)
        xs = _sc_dispatch(h2.reshape(n_tok, SUBLANES, LANES), dest, n_tiles * tm_moe)
        ys = _expert_ffn(block_e, n_used, xs.reshape(-1, LANES), w_gate_up[layer], b_gate_up[layer],
                         w_down[layer], b_down[layer], tm_moe)
        gathered = _sc_gather(ys.reshape(-1, SUBLANES, LANES), dest).reshape(TOP_K, n_tok * SUBLANES, LANES)
        if not last:
            ctx = _combine(alpha, gathered, 0, rw_c, ctx1, g2c, lng2, lnb2, tm_ctx)
        x = _combine(alpha, gathered, n_c, rw_l, x1, g2, lng2, lnb2, tm_lat)
    return x
```

```python
import functools

import jax
import jax.numpy as jnp
from jax import lax
from jax.experimental import pallas as pl
from jax.experimental.pallas import tpu as pltpu
from jax.experimental.pallas import tpu_sc as plsc

F32 = jnp.float32
BF16 = jnp.bfloat16

HEAD_DIM = 64
GRID_W = 64
ROPE_BASE = 10000.0
N_EXPERTS = 32
TOP_K = 4
SWIGLU_ALPHA = 1.702
SWIGLU_LIMIT = 7.0
N_MOD = 6
EPS = 1e-6
RET_HEADS = 4
RET_CHUNK = 128
ATT_Q_PER_KV = 4
ATT_KV_CHUNK = 256
LANES = 128
ROUTER_PAD = 128
VMEM_LIMIT = 56 * 1024 * 1024
SUBLANES = 8
SC_CORES = 2
SC_SUBCORES = 16
SC_WINDOW = 32


def _dot(a, b):
    return jnp.dot(a, b, preferred_element_type=F32)


def _dot_nt(a, b):
    return lax.dot_general(a, b, (((1,), (1,)), ((), ())), preferred_element_type=F32)


def _split_bf16(x):
    hi = x.astype(BF16)
    lo = (x - hi.astype(F32)).astype(BF16)
    return hi, lo


def _cparams(*sem):
    return pltpu.CompilerParams(dimension_semantics=sem, vmem_limit_bytes=VMEM_LIMIT)


def _mod_kernel(c_ref, w_ref, b_ref, o_ref):
    c = c_ref[...]
    a = c * jax.nn.sigmoid(c)
    a_hi, a_lo = _split_bf16(a)
    w_hi, w_lo = _split_bf16(w_ref[0])
    o_ref[0] = _dot(a_hi, w_hi) + _dot(a_lo, w_hi) + _dot(a_hi, w_lo) + b_ref[0]


def _modulation(c_all, w_mod, b_mod):
    depth, d, n = w_mod.shape
    r = c_all.shape[0]
    tn = 1536
    return pl.pallas_call(
        _mod_kernel,
        grid=(depth, n // tn),
        in_specs=[
            pl.BlockSpec((r, d), lambda l, j: (0, 0)),
            pl.BlockSpec((1, d, tn), lambda l, j: (l, 0, j)),
            pl.BlockSpec((1, 1, tn), lambda l, j: (l, 0, j)),
        ],
        out_specs=pl.BlockSpec((1, r, tn), lambda l, j: (l, 0, j)),
        out_shape=jax.ShapeDtypeStruct((depth, r, n), F32),
        compiler_params=_cparams("arbitrary", "arbitrary"),
        name="modulation",
    )(c_all, w_mod, b_mod.reshape(depth, 1, n))


def _inproj_kernel(x_ref, sc_ref, sh_ref, w_ref, cos_ref, sin_ref, gq_ref, gk_ref, ones_ref,
                   cu_ref, cb_ref, rq_ref, rk_ref, rv_ref, rg_ref, aq_ref, ak_ref, av_ref):
    h = (x_ref[0] * sc_ref[0] + sh_ref[0]).astype(BF16)
    z = _dot(h, w_ref[...])
    cos = cos_ref[...]
    sin = sin_ref[...]
    ones = ones_ref[...]
    lane = lax.broadcasted_iota(jnp.int32, cos.shape, 1)
    first_half = (lane & 31) < 16

    def rope(xs):
        nxt = pltpu.roll(xs, LANES - 16, axis=1)
        prv = pltpu.roll(xs, 16, axis=1)
        return xs * cos + jnp.where(first_half, nxt, prv) * sin

    def rms(xs, g):
        s_hi, s_lo = _split_bf16(xs * xs)
        ssq = _dot(s_hi, ones) + _dot(s_lo, ones)
        return xs * lax.rsqrt(ssq * (1.0 / HEAD_DIM) + EPS) * g

    cu_ref[0] = z[:, 512:768] * z[:, 0:256]
    cb_ref[0] = z[:, 256:512]
    for j in range(2):
        lo, hi = j * LANES, (j + 1) * LANES
        rq_ref[0, :, lo:hi] = rope(z[:, 768 + lo:768 + hi])
        rk_ref[0, :, lo:hi] = rope(z[:, 1024 + lo:1024 + hi] * (HEAD_DIM ** -0.5))
    rv_ref[0] = z[:, 1280:1536]
    rg_ref[0] = z[:, 1536:1792]
    gq = gq_ref[...]
    for j in range(4):
        lo, hi = j * LANES, (j + 1) * LANES
        q = rope(rms(z[:, 1792 + lo:1792 + hi], gq)) * (HEAD_DIM ** -0.5)
        aq_ref[0, :, lo:hi] = q.astype(BF16)
    ak_ref[0] = rope(rms(z[:, 2304:2432], gk_ref[...])).astype(BF16)
    av_ref[0] = z[:, 2432:2560].astype(BF16)


def _inproj(x, sc, sh, w_bf, cos, sin, gq, gk, ones_bd, tm):
    b, l, d = x.shape
    n = w_bf.shape[1]
    tok = lambda w: pl.BlockSpec((1, tm, w), lambda i, j: (i, j, 0))
    vec = pl.BlockSpec((1, 1, d), lambda i, j: (i, 0, 0))
    full = lambda s: pl.BlockSpec(s, lambda i, j: (0,) * len(s))
    widths = (256, 256, 256, 256, 256, 256, 512, 128, 128)
    dtypes = (F32,) * 6 + (BF16,) * 3
    return pl.pallas_call(
        _inproj_kernel,
        grid=(b, l // tm),
        in_specs=[tok(d), vec, vec, full((d, n)),
                  pl.BlockSpec((tm, LANES), lambda i, j: (j, 0)),
                  pl.BlockSpec((tm, LANES), lambda i, j: (j, 0)),
                  full((1, LANES)), full((1, LANES)), full((LANES, LANES))],
        out_specs=[tok(w) for w in widths],
        out_shape=[jax.ShapeDtypeStruct((b, l, w), dt) for w, dt in zip(widths, dtypes)],
        compiler_params=_cparams("arbitrary", "arbitrary"),
        name="inproj",
    )(x, sc, sh, w_bf, cos, sin, gq, gk, ones_bd)


def _attn_kernel(q_ref, k_ref, v_ref, o_ref):
    tq = q_ref.shape[1]
    q4 = q_ref[0]
    q = jnp.concatenate([q4[:, h * HEAD_DIM:(h + 1) * HEAD_DIM] for h in range(ATT_Q_PER_KV)], axis=0)
    lk = k_ref.shape[2]
    tk = min(ATT_KV_CHUNK, lk)
    m = jnp.full((q.shape[0], 1), -jnp.inf, F32)
    acc = jnp.zeros((q.shape[0], LANES), F32)
    for c in range(lk // tk):
        s = _dot_nt(q, k_ref[0, 0, c * tk:(c + 1) * tk, :])
        m_new = jnp.maximum(m, jnp.max(s, axis=-1, keepdims=True))
        p = jnp.exp(s - m_new)
        acc = acc * jnp.exp(m - m_new) + _dot(p.astype(BF16), v_ref[0, 0, c * tk:(c + 1) * tk, :])
        m = m_new
    o = acc[:, :HEAD_DIM] / acc[:, HEAD_DIM:HEAD_DIM + 1]
    o_ref[0] = jnp.concatenate([o[h * tq:(h + 1) * tq] for h in range(ATT_Q_PER_KV)], axis=1)


def _attention(q, k, v, tq):
    b, lq, wq = q.shape
    hkv, lk = k.shape[1], k.shape[2]
    wg = ATT_Q_PER_KV * HEAD_DIM
    return pl.pallas_call(
        _attn_kernel,
        grid=(b, hkv, lq // tq),
        in_specs=[pl.BlockSpec((1, tq, wg), lambda i, g, j: (i, j, g)),
                  pl.BlockSpec((1, 1, lk, HEAD_DIM), lambda i, g, j: (i, g, 0, 0)),
                  pl.BlockSpec((1, 1, lk, LANES), lambda i, g, j: (i, g, 0, 0))],
        out_specs=pl.BlockSpec((1, tq, wg), lambda i, g, j: (i, j, g)),
        out_shape=jax.ShapeDtypeStruct((b, lq, wq), F32),
        compiler_params=_cparams("arbitrary", "arbitrary", "arbitrary"),
        name="attention",
    )(q, k, v)


def _ret_bwd_kernel(k_ref, v_ref, s0_ref, lgb_ref, sb_ref, sfin_ref, s_scr):
    i = pl.program_id(1)
    c = k_ref.shape[1]
    w = k_ref.shape[2]

    @pl.when(i == 0)
    def _():
        s_scr[...] = s0_ref[0]

    sb_ref[0, 0] = s_scr[...]
    lgb = lgb_ref[...]
    pos = lax.broadcasted_iota(jnp.int32, (c, w), 0).astype(F32)
    kzt = (k_ref[0] * jnp.exp(lgb * pos)).T.astype(BF16)
    v = v_ref[0].astype(BF16)
    cdec = jnp.exp(lgb * float(c))
    for h in range(RET_HEADS):
        lo, hi = h * HEAD_DIM, (h + 1) * HEAD_DIM
        s_scr[h] = cdec[:, lo:hi] * s_scr[h] + _dot(kzt[lo:hi, :], v[:, lo:hi])

    @pl.when(i == pl.num_programs(1) - 1)
    def _():
        sfin_ref[0] = s_scr[...]


def _ret_fwd_kernel(lg_ref, q_ref, k_ref, v_ref, g_ref, sb_ref, s0_ref, lgf_ref, lgb_ref, gn_ref,
                    o_ref, sfin_ref, s_scr):
    i = pl.program_id(1)
    c = q_ref.shape[1]
    w = q_ref.shape[2]

    @pl.when(i == 0)
    def _():
        s_scr[...] = s0_ref[0]

    q = q_ref[0]
    k = k_ref[0]
    lgf = lgf_ref[...]
    lgb = lgb_ref[...]
    pos = lax.broadcasted_iota(jnp.int32, (c, w), 0).astype(F32)
    qf = (q * jnp.exp(lgf * (pos + 1.0))).astype(BF16)
    qb = (q * jnp.exp(lgb * (float(c) - pos))).astype(BF16)
    kzt = (k * jnp.exp(lgf * (float(c) - 1.0 - pos))).T.astype(BF16)
    qh = q.astype(BF16)
    kh = k.astype(BF16)
    vh = v_ref[0].astype(BF16)
    diff = (lax.broadcasted_iota(jnp.int32, (c, c), 0) - lax.broadcasted_iota(jnp.int32, (c, c), 1)).astype(F32)
    cdec = jnp.exp(lgf * float(c))
    outs = []
    for h in range(RET_HEADS):
        lo, hi = h * HEAD_DIM, (h + 1) * HEAD_DIM
        sc = _dot_nt(qh[:, lo:hi], kh[:, lo:hi])
        dec = jnp.where(diff >= 0.0, jnp.exp(lg_ref[0, h] * jnp.maximum(diff, 0.0)),
                        jnp.exp(lg_ref[1, h] * jnp.maximum(-diff, 0.0)))
        y = (_dot((sc * dec).astype(BF16), vh[:, lo:hi])
             + _dot(qf[:, lo:hi], s_scr[h].astype(BF16))
             + _dot(qb[:, lo:hi], sb_ref[0, 0, h].astype(BF16)))
        mu = jnp.mean(y, axis=-1, keepdims=True)
        yc = y - mu
        var = jnp.mean(yc * yc, axis=-1, keepdims=True)
        outs.append(yc * lax.rsqrt(var + EPS))
        s_scr[h] = cdec[:, lo:hi] * s_scr[h] + _dot(kzt[lo:hi, :], vh[:, lo:hi])
    gate = g_ref[0]
    o_ref[0] = jnp.concatenate(outs, axis=1) * gn_ref[...] * (gate * jax.nn.sigmoid(gate))

    @pl.when(i == pl.num_programs(1) - 1)
    def _():
        sfin_ref[0] = s_scr[...]


def _retention(q, k, v, gate, s0_f, s0_b, lg, lgf_row, lgb_row, gn_row):
    b, l, w = q.shape
    c = RET_CHUNK
    nc = l // c
    st = (1, RET_HEADS, HEAD_DIM, HEAD_DIM)
    st_spec = pl.BlockSpec(st, lambda i, j: (i, 0, 0, 0))
    row = pl.BlockSpec((1, w), lambda i, j: (0, 0))
    sb_all, sfin_b = pl.pallas_call(
        _ret_bwd_kernel,
        grid=(b, nc),
        in_specs=[pl.BlockSpec((1, c, w), lambda i, j: (i, nc - 1 - j, 0)),
                  pl.BlockSpec((1, c, w), lambda i, j: (i, nc - 1 - j, 0)),
                  st_spec, row],
        out_specs=[pl.BlockSpec((1,) + st, lambda i, j: (i, nc - 1 - j, 0, 0, 0)), st_spec],
        out_shape=[jax.ShapeDtypeStruct((b, nc) + st[1:], F32), jax.ShapeDtypeStruct((b,) + st[1:], F32)],
        scratch_shapes=[pltpu.VMEM(st[1:], F32)],
        compiler_params=_cparams("arbitrary", "arbitrary"),
        name="retention_right_states",
    )(k, v, s0_b, lgb_row)
    tok = pl.BlockSpec((1, c, w), lambda i, j: (i, j, 0))
    out, sfin_f = pl.pallas_call(
        _ret_fwd_kernel,
        grid=(b, nc),
        in_specs=[pl.BlockSpec(memory_space=pltpu.SMEM), tok, tok, tok, tok,
                  pl.BlockSpec((1,) + st, lambda i, j: (i, j, 0, 0, 0)), st_spec, row, row, row],
        out_specs=[tok, st_spec],
        out_shape=[jax.ShapeDtypeStruct((b, l, w), F32), jax.ShapeDtypeStruct((b,) + st[1:], F32)],
        scratch_shapes=[pltpu.VMEM(st[1:], F32)],
        compiler_params=_cparams("arbitrary", "arbitrary"),
        name="retention",
    )(lg, q, k, v, gate, sb_all, s0_f, lgf_row, lgb_row, gn_row)
    return out, sfin_f, sfin_b


def _layer_norm(x, g, b):
    mu = jnp.mean(x, axis=-1, keepdims=True)
    xc = x - mu
    var = jnp.mean(xc * xc, axis=-1, keepdims=True)
    return xc * lax.rsqrt(var + EPS) * g + b


def _store_rows_tiled(ref, x):
    tm = x.shape[0]
    for s in range(SUBLANES):
        ref[pl.ds(s, tm, stride=SUBLANES), :] = x[:, s * LANES:(s + 1) * LANES]


def _load_rows_tiled(ref, tm):
    return jnp.concatenate([ref[pl.ds(s, tm, stride=SUBLANES), :] for s in range(SUBLANES)], axis=1)


def _outproj_kernel(alpha, cu_ref, cup_ref, cun_ref, cb_ref, cw_ref, ret_ref, att_ref, x_ref, g1_ref,
                    w_ref, lng_ref, lnb_ref, sc_ref, sh_ref, wrh_ref, wrl_ref, br_ref, cnt0_ref,
                    x1_ref, h2_ref, ri_ref, rw_ref, cnt_ref, run_scr):
    i = pl.program_id(0)
    j = pl.program_id(1)

    @pl.when((i == 0) & (j == 0))
    def _():
        run_scr[...] = cnt0_ref[...]

    t = cu_ref[0]
    tm = t.shape[0]
    prev_row = jnp.where(j == 0, 0.0, cup_ref[0][7:8, :])
    next_row = jnp.where(j == pl.num_programs(1) - 1, 0.0, cun_ref[0][0:1, :])
    row = lax.broadcasted_iota(jnp.int32, t.shape, 0)
    t_prev = jnp.where(row == 0, prev_row, pltpu.roll(t, 1, axis=0))
    t_next = jnp.where(row == tm - 1, next_row, pltpu.roll(t, tm - 1, axis=0))
    conv = cb_ref[0] * (t_prev * cw_ref[0] + t * cw_ref[1] + t_next * cw_ref[2])
    y = (_dot(conv.astype(BF16), w_ref[0:256, :])
         + _dot(ret_ref[0].astype(BF16), w_ref[256:512, :])
         + _dot(att_ref[0].astype(BF16), w_ref[512:1024, :]))
    x1 = _layer_norm(alpha * x_ref[0] + g1_ref[0] * y, lng_ref[...], lnb_ref[...])
    x1_ref[0] = x1
    h2 = x1 * sc_ref[0] + sh_ref[0]
    _store_rows_tiled(h2_ref, h2)
    h_hi, h_lo = _split_bf16(h2)
    wrh = wrh_ref[...]
    logits = _dot(h_hi, wrh) + _dot(h_lo, wrh) + _dot(h_hi, wrl_ref[...]) + br_ref[...]

    lane = lax.broadcasted_iota(jnp.int32, logits.shape, 1)
    lane_f = lane.astype(F32)
    work = jnp.where(lane < N_EXPERTS, logits, -jnp.inf)
    vals, sels = [], []
    for _ in range(TOP_K):
        m = jnp.max(work, axis=-1, keepdims=True)
        first = jnp.min(jnp.where(work == m, lane_f, float(ROUTER_PAD)), axis=-1, keepdims=True)
        sel = lane_f == first
        vals.append(m)
        sels.append(sel)
        work = jnp.where(sel, -jnp.inf, work)
    exps = [jnp.exp(v - vals[0]) for v in vals]
    denom = exps[0]
    for e in exps[1:]:
        denom = denom + e

    cnt = jnp.zeros(logits.shape, F32)
    for sel in sels:
        cnt = cnt + jnp.where(sel, 1.0, 0.0)
    r_i = lax.broadcasted_iota(jnp.int32, (tm, tm), 0)
    c_i = lax.broadcasted_iota(jnp.int32, (tm, tm), 1)
    before = jnp.where(c_i < r_i, 1.0, 0.0).astype(BF16)
    base = _dot(before, cnt.astype(BF16)) + run_scr[...]
    run_scr[...] = run_scr[...] + jnp.sum(cnt, axis=0, keepdims=True)
    cnt_ref[...] = run_scr[...]

    ri = jnp.zeros(logits.shape, F32)
    rw = jnp.zeros(logits.shape, F32)
    for kk in range(TOP_K):
        e_idx = jnp.sum(jnp.where(sels[kk], lane_f, 0.0), axis=-1, keepdims=True)
        rank = jnp.sum(jnp.where(sels[kk], base, 0.0), axis=-1, keepdims=True)
        ri = jnp.where(lane == kk, e_idx, ri)
        ri = jnp.where(lane == TOP_K + kk, rank, ri)
        rw = jnp.where(lane == kk, exps[kk] / denom, rw)
    ri_ref[0] = ri.astype(jnp.int32)
    rw_ref[0] = rw


def _outproj(alpha, cu, cb, cw, ret, att, x, g1, w_bf, lng, lnb, sc2, sh2, wrh, wrl, br, cnt0, tm):
    b, l, d = x.shape
    nj = l // tm
    tok = lambda w: pl.BlockSpec((1, tm, w), lambda i, j: (i, j, 0))
    vec = pl.BlockSpec((1, 1, d), lambda i, j: (i, 0, 0))
    full = lambda s: pl.BlockSpec(s, lambda i, j: (0,) * len(s))
    r8 = tm // 8
    nb8 = l // 8
    return pl.pallas_call(
        functools.partial(_outproj_kernel, alpha),
        grid=(b, nj),
        in_specs=[tok(256),
                  pl.BlockSpec((1, 8, 256), lambda i, j: (i, jnp.maximum(j * r8 - 1, 0), 0)),
                  pl.BlockSpec((1, 8, 256), lambda i, j: (i, jnp.minimum((j + 1) * r8, nb8 - 1), 0)),
                  tok(256), full((3, 1, 256)), tok(256), tok(512), tok(d), vec,
                  full((d, d)), full((1, d)), full((1, d)), vec, vec,
                  full((d, ROUTER_PAD)), full((d, ROUTER_PAD)), full((1, ROUTER_PAD)), full((1, ROUTER_PAD))],
        out_specs=[tok(d), pl.BlockSpec((tm * SUBLANES, LANES), lambda i, j: (i * nj + j, 0)),
                   tok(ROUTER_PAD), tok(ROUTER_PAD), full((1, ROUTER_PAD))],
        out_shape=[jax.ShapeDtypeStruct((b, l, d), F32),
                   jax.ShapeDtypeStruct((b * l * SUBLANES, LANES), F32),
                   jax.ShapeDtypeStruct((b, l, ROUTER_PAD), jnp.int32),
                   jax.ShapeDtypeStruct((b, l, ROUTER_PAD), F32),
                   jax.ShapeDtypeStruct((1, ROUTER_PAD), F32)],
        scratch_shapes=[pltpu.VMEM((1, ROUTER_PAD), F32)],
        compiler_params=_cparams("arbitrary", "arbitrary"),
        name="outproj_ln_router",
    )(cu, cu, cu, cb, cw, ret, att, x, g1, w_bf, lng, lnb, sc2, sh2, wrh, wrl, br, cnt0)


def _ffn_kernel(be_ref, nu_ref, x_ref, wgu_ref, bgu_ref, wd_ref, bd_ref, o_ref, wgu_scr, wd_scr):
    j = pl.program_id(0)
    f = wd_ref.shape[2]
    tm = x_ref.shape[0] // SUBLANES

    @pl.when(j < nu_ref[0])
    def _():
        @pl.when((j == 0) | (be_ref[j] != be_ref[jnp.maximum(j - 1, 0)]))
        def _():
            wgu_scr[...] = wgu_ref[0, 0].astype(BF16)
            wd_scr[...] = wd_ref[0, 0].astype(BF16)

        x = _load_rows_tiled(x_ref, tm).astype(BF16)
        gu = _dot(x, wgu_scr[...]) + bgu_ref[0, 0]
        gate = jnp.minimum(gu[:, :f], SWIGLU_LIMIT)
        up = jnp.clip(gu[:, f:], -SWIGLU_LIMIT, SWIGLU_LIMIT)
        act = (up + 1.0) * (gate * jax.nn.sigmoid(SWIGLU_ALPHA * gate))
        _store_rows_tiled(o_ref, _dot(act.astype(BF16), wd_scr[...]) + bd_ref[0, 0])

    @pl.when(j >= nu_ref[0])
    def _():
        o_ref[...] = jnp.zeros_like(o_ref)


def _expert_ffn(layer, block_e, n_used, xs, wgu, bgu, wd, bd, tm):
    n_rows = xs.shape[0] // SUBLANES
    depth, ne, d, f2 = wgu.shape
    f = f2 // 2
    rows = pl.BlockSpec((tm * SUBLANES, LANES), lambda j, be, nu: (j, 0))
    grid_spec = pltpu.PrefetchScalarGridSpec(
        num_scalar_prefetch=2,
        grid=(n_rows // tm,),
        in_specs=[rows,
                  pl.BlockSpec((1, 1, d, f2), lambda j, be, nu: (layer, be[j], 0, 0)),
                  pl.BlockSpec((1, 1, 1, f2), lambda j, be, nu: (layer, be[j], 0, 0)),
                  pl.BlockSpec((1, 1, f, d), lambda j, be, nu: (layer, be[j], 0, 0)),
                  pl.BlockSpec((1, 1, 1, d), lambda j, be, nu: (layer, be[j], 0, 0))],
        out_specs=rows,
        scratch_shapes=[pltpu.VMEM((d, f2), BF16), pltpu.VMEM((f, d), BF16)],
    )
    return pl.pallas_call(
        _ffn_kernel,
        grid_spec=grid_spec,
        out_shape=jax.ShapeDtypeStruct((n_rows * SUBLANES, LANES), F32),
        compiler_params=_cparams("arbitrary"),
        name="expert_ffn",
    )(block_e, n_used, xs, wgu, bgu.reshape(depth, ne, 1, f2), wd, bd.reshape(depth, ne, 1, d))


def _sc_worker_base(per_worker):
    return (lax.axis_index("s") * SC_CORES + lax.axis_index("c")) * per_worker


def _sc_dispatch(rows, dest, n_out):
    t = rows.shape[0]
    kk = dest.shape[0] // t
    w = SC_WINDOW
    per_worker = t // (SC_CORES * SC_SUBCORES)
    mesh = plsc.VectorSubcoreMesh(core_axis_name="c", subcore_axis_name="s")

    @functools.partial(
        pl.kernel, mesh=mesh,
        out_type=jax.ShapeDtypeStruct((n_out,) + rows.shape[1:], rows.dtype),
        scratch_types=[pltpu.VMEM((w,), jnp.int32)] * kk + [pltpu.VMEM((w,) + rows.shape[1:], rows.dtype)])
    def scatter_rows(r_hbm, d_hbm, o_hbm, *scratch):
        idx_v, rows_v = scratch[:kk], scratch[kk]
        base = _sc_worker_base(per_worker)

        @pl.loop(0, per_worker // w)
        def _(i):
            off = base + i * w
            pltpu.sync_copy(r_hbm.at[pl.ds(off, w)], rows_v)
            for s in range(kk):
                pltpu.sync_copy(d_hbm.at[pl.ds(s * t + off, w)], idx_v[s])
            for s in range(kk):
                pltpu.sync_copy(rows_v, o_hbm.at[idx_v[s]])

    return scatter_rows(rows, dest)


def _sc_gather(table, idx):
    n = idx.shape[0]
    w = SC_WINDOW
    per_worker = n // (SC_CORES * SC_SUBCORES)
    mesh = plsc.VectorSubcoreMesh(core_axis_name="c", subcore_axis_name="s")

    @functools.partial(
        pl.kernel, mesh=mesh,
        out_type=jax.ShapeDtypeStruct((n,) + table.shape[1:], table.dtype),
        scratch_types=[pltpu.VMEM((w,), jnp.int32), pltpu.VMEM((w,) + table.shape[1:], table.dtype)])
    def gather_rows(t_hbm, i_hbm, o_hbm, idx_v, rows_v):
        base = _sc_worker_base(per_worker)

        @pl.loop(0, per_worker // w)
        def _(i):
            off = base + i * w
            pltpu.sync_copy(i_hbm.at[pl.ds(off, w)], idx_v)
            pltpu.sync_copy(t_hbm.at[idx_v], rows_v)
            pltpu.sync_copy(rows_v, o_hbm.at[pl.ds(off, w)])

    return gather_rows(table, idx)


def _combine_kernel(alpha, g_ref, w_ref, x_ref, g2_ref, lng_ref, lnb_ref, o_ref):
    tm = x_ref.shape[1]
    w = w_ref[0]
    f = _load_rows_tiled(g_ref.at[0], tm) * w[:, 0:1]
    for kk in range(1, TOP_K):
        f = f + _load_rows_tiled(g_ref.at[kk], tm) * w[:, kk:kk + 1]
    o_ref[0] = _layer_norm(alpha * x_ref[0] + g2_ref[0] * f, lng_ref[...], lnb_ref[...])


def _combine(alpha, gathered, tok_off, wts, x1, g2, lng, lnb, tm):
    b, l, d = x1.shape
    nj = l // tm
    blk_off = tok_off // tm
    tok = pl.BlockSpec((1, tm, d), lambda i, j: (i, j, 0))
    full = pl.BlockSpec((1, d), lambda i, j: (0, 0))
    return pl.pallas_call(
        functools.partial(_combine_kernel, alpha),
        grid=(b, nj),
        in_specs=[pl.BlockSpec((TOP_K, tm * SUBLANES, LANES), lambda i, j: (0, blk_off + i * nj + j, 0)),
                  pl.BlockSpec((1, tm, ROUTER_PAD), lambda i, j: (i, j, 0)),
                  tok, pl.BlockSpec((1, 1, d), lambda i, j: (i, 0, 0)), full, full],
        out_specs=tok,
        out_shape=jax.ShapeDtypeStruct((b, l, d), F32),
        compiler_params=_cparams("arbitrary", "arbitrary"),
        name="combine_ln",
    )(gathered, wts, x1, g2, lng, lnb)


def _rope_tables(l):
    rows = l // GRID_W
    axis_dim = HEAD_DIM // 2
    inv_freq = ROPE_BASE ** (-jnp.arange(0, axis_dim, 2, dtype=F32) / axis_dim)
    row = jnp.repeat(jnp.arange(rows, dtype=F32), GRID_W)
    col = jnp.tile(jnp.arange(GRID_W, dtype=F32), rows)
    ang = jnp.stack([row[:, None] * inv_freq, col[:, None] * inv_freq], axis=1)
    cos, sin = jnp.cos(ang), jnp.sin(ang)
    cos64 = jnp.broadcast_to(cos[:, :, None, :], (l, 2, 2, HEAD_DIM // 4)).reshape(l, HEAD_DIM)
    sin64 = jnp.stack([-sin, sin], axis=2).reshape(l, HEAD_DIM)
    return jnp.tile(cos64, (1, LANES // HEAD_DIM)), jnp.tile(sin64, (1, LANES // HEAD_DIM))


def _route_tables(ri, counts, tm):
    t = ri.shape[0]
    top_e = ri[:, :TOP_K]
    rank = ri[:, TOP_K:2 * TOP_K]
    padded = (counts + tm - 1) // tm * tm
    pad_end = jnp.cumsum(padded)
    pad_start = pad_end - padded
    experts = jnp.arange(N_EXPERTS, dtype=jnp.int32)
    start = jnp.sum(jnp.where(top_e[:, :, None] == experts, pad_start, 0), axis=-1)
    dest = (start + rank).T.reshape(TOP_K * t)
    n_tiles = (t * TOP_K + N_EXPERTS * (tm - 1) + tm - 1) // tm
    tile_start = jnp.arange(n_tiles, dtype=jnp.int32) * tm
    block_e = jnp.minimum(jnp.sum((pad_end[None, :] <= tile_start[:, None]).astype(jnp.int32), axis=1), N_EXPERTS - 1)
    n_used = (pad_end[-1] // tm).astype(jnp.int32).reshape(1)
    return dest, block_e.astype(jnp.int32), n_used, n_tiles


def kernel(x, c, ctx, c_ctx, w_mod, b_mod, w_in, conv_w, ret_decay_exp, ret_gn_g, q_norm_g, k_norm_g, w_out,
           ln_g, ln_b, w_router, b_router, w_gate_up, b_gate_up, w_down, b_down):
    depth = w_mod.shape[0]
    alpha = (2.0 * depth) ** 0.25
    b, l, d = x.shape
    lc = ctx.shape[1]
    tm_lat, tm_ctx, tm_moe, tq = 512, 256, 512, 128

    n_rows = (b + 1 + 7) // 8 * 8
    c_all = jnp.zeros((n_rows, d), F32).at[:b].set(c).at[b].set(c_ctx)
    mod = _modulation(c_all, w_mod, b_mod)

    cos_l, sin_l = _rope_tables(l)
    cos_c, sin_c = jnp.ones((lc, LANES), F32), jnp.zeros((lc, LANES), F32)
    lane_head = jnp.arange(LANES) // HEAD_DIM
    ones_bd = (lane_head[:, None] == lane_head[None, :]).astype(BF16)
    zero_state = jnp.zeros((b, RET_HEADS, HEAD_DIM, HEAD_DIM), F32)

    for layer in range(depth):
        last = layer == depth - 1
        m_lat = mod[layer, :b].reshape(b, 1, N_MOD, d)
        m_ctx = jnp.broadcast_to(mod[layer, b].reshape(1, 1, N_MOD, d), (b, 1, N_MOD, d))
        sh1, sc1, g1, sh2, sc2, g2 = (m_lat[:, :, i] for i in range(N_MOD))
        sh1c, sc1c, g1c, sh2c, sc2c, g2c = (m_ctx[:, :, i] for i in range(N_MOD))

        w_in_bf = w_in[layer].astype(BF16)
        w_out_bf = w_out[layer].astype(BF16)
        gq = jnp.tile(q_norm_g[layer], LANES // HEAD_DIM).reshape(1, LANES)
        gk = jnp.tile(k_norm_g[layer], LANES // HEAD_DIM).reshape(1, LANES)
        log_gamma = jnp.log1p(-jnp.exp2(-ret_decay_exp[layer].astype(F32)))
        lg_rows = jnp.repeat(log_gamma, HEAD_DIM, axis=1)
        lgf_row, lgb_row = lg_rows[0:1], lg_rows[1:2]
        gn_row = ret_gn_g[layer].reshape(1, -1)
        cw = conv_w[layer].T.reshape(3, 1, -1)
        wr_pad = jnp.zeros((d, ROUTER_PAD), F32).at[:, :N_EXPERTS].set(w_router[layer])
        wrh = wr_pad.astype(BF16)
        wrl = (wr_pad - wrh.astype(F32)).astype(BF16)
        br = jnp.zeros((1, ROUTER_PAD), F32).at[0, :N_EXPERTS].set(b_router[layer])
        lng1, lnb1 = ln_g[layer, 0].reshape(1, d), ln_b[layer, 0].reshape(1, d)
        lng2, lnb2 = ln_g[layer, 1].reshape(1, d), ln_b[layer, 1].reshape(1, d)

        zc = _inproj(ctx, 1.0 + sc1c, sh1c, w_in_bf, cos_c, sin_c, gq, gk, ones_bd, tm_ctx)
        zl = _inproj(x, 1.0 + sc1, sh1, w_in_bf, cos_l, sin_l, gq, gk, ones_bd, tm_lat)
        cu_c, cb_c, rq_c, rk_c, rv_c, rg_c, aq_c, ak_c, av_c = zc
        cu_l, cb_l, rq_l, rk_l, rv_l, rg_l, aq_l, ak_l, av_l = zl

        ret_c, s_fwd, s_bwd = _retention(rq_c, rk_c, rv_c, rg_c, zero_state, zero_state,
                                         log_gamma, lgf_row, lgb_row, gn_row)
        ret_l, _, _ = _retention(rq_l, rk_l, rv_l, rg_l, s_fwd, s_bwd, log_gamma, lgf_row, lgb_row, gn_row)

        def kv_heads(a):
            return a.reshape(b, a.shape[1], -1, HEAD_DIM).transpose(0, 2, 1, 3)

        def v_heads(a):
            vh = kv_heads(a)
            pad = jnp.zeros(vh.shape[:3] + (LANES - HEAD_DIM - 1,), BF16)
            return jnp.concatenate([vh, jnp.ones(vh.shape[:3] + (1,), BF16), pad], axis=-1)

        k_all = kv_heads(jnp.concatenate([ak_c, ak_l], axis=1))
        v_all = v_heads(jnp.concatenate([av_c, av_l], axis=1))
        att_l = _attention(aq_l, k_all, v_all, tq)

        cnt0 = jnp.zeros((1, ROUTER_PAD), F32)
        if not last:
            att_c = _attention(aq_c, kv_heads(ak_c), v_heads(av_c), tq)
            ctx1, h2_c, ri_c, rw_c, cnt0 = _outproj(alpha, cu_c, cb_c, cw, ret_c, att_c, ctx, g1c, w_out_bf,
                                                    lng1, lnb1, 1.0 + sc2c, sh2c, wrh, wrl, br, cnt0, tm_ctx)
        x1, h2_l, ri_l, rw_l, cnt = _outproj(alpha, cu_l, cb_l, cw, ret_l, att_l, x, g1, w_out_bf, lng1, lnb1,
                                             1.0 + sc2, sh2, wrh, wrl, br, cnt0, tm_lat)
        if not last:
            n_c = b * lc
            h2 = jnp.concatenate([h2_c, h2_l], axis=0)
            ri = jnp.concatenate([ri_c.reshape(n_c, -1), ri_l.reshape(b * l, -1)], axis=0)
        else:
            n_c = 0
            h2 = h2_l
            ri = ri_l.reshape(b * l, -1)
        n_tok = n_c + b * l

        counts = cnt[0, :N_EXPERTS].astype(jnp.int32)
        dest, block_e, n_used, n_tiles = _route_tables(ri, counts, tm_moe)
        xs = _sc_dispatch(h2.reshape(n_tok, SUBLANES, LANES), dest, n_tiles * tm_moe)
        ys = _expert_ffn(layer, block_e, n_used, xs.reshape(-1, LANES), w_gate_up, b_gate_up, w_down, b_down, tm_moe)
        gathered = _sc_gather(ys.reshape(-1, SUBLANES, LANES), dest).reshape(TOP_K, n_tok * SUBLANES, LANES)
        if not last:
            ctx = _combine(alpha, gathered, 0, rw_c, ctx1, g2c, lng2, lnb2, tm_ctx)
        x = _combine(alpha, gathered, n_c, rw_l, x1, g2, lng2, lnb2, tm_lat)
    return x
```

```python
import functools

import jax
import jax.numpy as jnp
from jax import lax
from jax.experimental import pallas as pl
from jax.experimental.pallas import tpu as pltpu
from jax.experimental.pallas import tpu_sc as plsc

F32 = jnp.float32
BF16 = jnp.bfloat16

HEAD_DIM = 64
GRID_W = 64
ROPE_BASE = 10000.0
N_EXPERTS = 32
TOP_K = 4
SWIGLU_ALPHA = 1.702
SWIGLU_LIMIT = 7.0
N_MOD = 6
EPS = 1e-6
RET_HEADS = 4
RET_CHUNK = 256
ATT_Q_PER_KV = 4
ATT_KV_CHUNK = 256
LANES = 128
ROUTER_PAD = 128
VMEM_LIMIT = 56 * 1024 * 1024
SUBLANES = 8
PACKED_ROWS = 4
SC_CORES = 2
SC_SUBCORES = 16
SC_WINDOW = 32


def _dot(a, b):
    return jnp.dot(a, b, preferred_element_type=F32)


def _dot_nt(a, b):
    return lax.dot_general(a, b, (((1,), (1,)), ((), ())), preferred_element_type=F32)


def _split_bf16(x):
    hi = x.astype(BF16)
    lo = (x - hi.astype(F32)).astype(BF16)
    return hi, lo


def _cparams(*sem):
    return pltpu.CompilerParams(dimension_semantics=sem, vmem_limit_bytes=VMEM_LIMIT)


def _mod_kernel(c_ref, w_ref, b_ref, o_ref):
    c = c_ref[...]
    a = c * jax.nn.sigmoid(c)
    a_hi, a_lo = _split_bf16(a)
    w_hi, w_lo = _split_bf16(w_ref[0])
    o_ref[0] = _dot(a_hi, w_hi) + _dot(a_lo, w_hi) + _dot(a_hi, w_lo) + b_ref[0]


def _modulation(c_all, w_mod, b_mod):
    depth, d, n = w_mod.shape
    r = c_all.shape[0]
    tn = 1536
    return pl.pallas_call(
        _mod_kernel,
        grid=(depth, n // tn),
        in_specs=[
            pl.BlockSpec((r, d), lambda l, j: (0, 0)),
            pl.BlockSpec((1, d, tn), lambda l, j: (l, 0, j)),
            pl.BlockSpec((1, 1, tn), lambda l, j: (l, 0, j)),
        ],
        out_specs=pl.BlockSpec((1, r, tn), lambda l, j: (l, 0, j)),
        out_shape=jax.ShapeDtypeStruct((depth, r, n), F32),
        compiler_params=_cparams("arbitrary", "arbitrary"),
        name="modulation",
    )(c_all, w_mod, b_mod.reshape(depth, 1, n))


def _inproj_kernel(x_ref, sc_ref, sh_ref, w_ref, cos_ref, sin_ref, gq_ref, gk_ref, ones_ref,
                   cu_ref, cb_ref, rq_ref, rk_ref, rv_ref, rg_ref, aq_ref, ak_ref, av_ref):
    h = (x_ref[0] * sc_ref[0] + sh_ref[0]).astype(BF16)
    z = _dot(h, w_ref[...])
    cos = cos_ref[...]
    sin = sin_ref[...]
    ones = ones_ref[...]
    lane = lax.broadcasted_iota(jnp.int32, cos.shape, 1)
    first_half = (lane & 31) < 16

    def rope(xs):
        nxt = pltpu.roll(xs, LANES - 16, axis=1)
        prv = pltpu.roll(xs, 16, axis=1)
        return xs * cos + jnp.where(first_half, nxt, prv) * sin

    def rms(xs, g):
        s_hi, s_lo = _split_bf16(xs * xs)
        ssq = _dot(s_hi, ones) + _dot(s_lo, ones)
        return xs * lax.rsqrt(ssq * (1.0 / HEAD_DIM) + EPS) * g

    cu_ref[0] = z[:, 512:768] * z[:, 0:256]
    cb_ref[0] = z[:, 256:512]
    for j in range(2):
        lo, hi = j * LANES, (j + 1) * LANES
        rq_ref[0, :, lo:hi] = rope(z[:, 768 + lo:768 + hi])
        rk_ref[0, :, lo:hi] = rope(z[:, 1024 + lo:1024 + hi] * (HEAD_DIM ** -0.5))
    rv_ref[0] = z[:, 1280:1536]
    rg_ref[0] = z[:, 1536:1792]
    gq = gq_ref[...]
    for j in range(4):
        lo, hi = j * LANES, (j + 1) * LANES
        q = rope(rms(z[:, 1792 + lo:1792 + hi], gq)) * (HEAD_DIM ** -0.5)
        aq_ref[0, :, lo:hi] = q.astype(BF16)
    ak_ref[0] = rope(rms(z[:, 2304:2432], gk_ref[...])).astype(BF16)
    av_ref[0] = z[:, 2432:2560].astype(BF16)


def _inproj(x, sc, sh, w_bf, cos, sin, gq, gk, ones_bd, tm):
    b, l, d = x.shape
    n = w_bf.shape[1]
    tok = lambda w: pl.BlockSpec((1, tm, w), lambda i, j: (i, j, 0))
    vec = pl.BlockSpec((1, 1, d), lambda i, j: (i, 0, 0))
    full = lambda s: pl.BlockSpec(s, lambda i, j: (0,) * len(s))
    widths = (256, 256, 256, 256, 256, 256, 512, 128, 128)
    dtypes = (F32,) * 6 + (BF16,) * 3
    return pl.pallas_call(
        _inproj_kernel,
        grid=(b, l // tm),
        in_specs=[tok(d), vec, vec, full((d, n)),
                  pl.BlockSpec((tm, LANES), lambda i, j: (j, 0)),
                  pl.BlockSpec((tm, LANES), lambda i, j: (j, 0)),
                  full((1, LANES)), full((1, LANES)), full((LANES, LANES))],
        out_specs=[tok(w) for w in widths],
        out_shape=[jax.ShapeDtypeStruct((b, l, w), dt) for w, dt in zip(widths, dtypes)],
        compiler_params=_cparams("arbitrary", "arbitrary"),
        name="inproj",
    )(x, sc, sh, w_bf, cos, sin, gq, gk, ones_bd)


def _attn_kernel(q_ref, k_ref, v_ref, o_ref):
    tq = q_ref.shape[1]
    q4 = q_ref[0]
    q = jnp.concatenate([q4[:, h * HEAD_DIM:(h + 1) * HEAD_DIM] for h in range(ATT_Q_PER_KV)], axis=0)
    lk = k_ref.shape[2]
    m = jnp.full((q.shape[0], 1), -jnp.inf, F32)
    acc = jnp.zeros((q.shape[0], LANES), F32)
    for lo in range(0, lk, ATT_KV_CHUNK):
        hi = min(lo + ATT_KV_CHUNK, lk)
        s = _dot_nt(q, k_ref[0, 0, lo:hi, :])
        m_new = jnp.maximum(m, jnp.max(s, axis=-1, keepdims=True))
        p = jnp.exp(s - m_new)
        acc = acc * jnp.exp(m - m_new) + _dot(p.astype(BF16), v_ref[0, 0, lo:hi, :])
        m = m_new
    o = acc[:, :HEAD_DIM] / acc[:, HEAD_DIM:HEAD_DIM + 1]
    o_ref[0] = jnp.concatenate([o[h * tq:(h + 1) * tq] for h in range(ATT_Q_PER_KV)], axis=1)


def _attention(q, k, v, tq):
    b, lq, wq = q.shape
    hkv, lk = k.shape[1], k.shape[2]
    wg = ATT_Q_PER_KV * HEAD_DIM
    return pl.pallas_call(
        _attn_kernel,
        grid=(b, hkv, lq // tq),
        in_specs=[pl.BlockSpec((1, tq, wg), lambda i, g, j: (i, j, g)),
                  pl.BlockSpec((1, 1, lk, HEAD_DIM), lambda i, g, j: (i, g, 0, 0)),
                  pl.BlockSpec((1, 1, lk, LANES), lambda i, g, j: (i, g, 0, 0))],
        out_specs=pl.BlockSpec((1, tq, wg), lambda i, g, j: (i, j, g)),
        out_shape=jax.ShapeDtypeStruct((b, lq, wq), F32),
        compiler_params=_cparams("arbitrary", "arbitrary", "arbitrary"),
        name="attention",
    )(q, k, v)


def _ret_kernel(lg_ref, q_ref, k_ref, v_ref, g_ref, s0f_ref, s0b_ref, lgf_ref, lgb_ref, gn_ref,
                o_ref, sff_ref, sfb_ref, s_scr, sb_scr, dec_scr):
    i, p, j = pl.program_id(0), pl.program_id(1), pl.program_id(2)
    nc = pl.num_programs(2)
    c = k_ref.shape[1]
    w = k_ref.shape[2]
    pos = lax.broadcasted_iota(jnp.int32, (c, w), 0).astype(F32)
    lgf = lgf_ref[...]
    lgb = lgb_ref[...]

    @pl.when((i == 0) & (p == 0) & (j == 0))
    def _():
        diff = (lax.broadcasted_iota(jnp.int32, (c, c), 0) - lax.broadcasted_iota(jnp.int32, (c, c), 1)).astype(F32)
        for h in range(RET_HEADS):
            dec_scr[h] = jnp.where(diff >= 0.0, jnp.exp(lg_ref[0, h] * jnp.maximum(diff, 0.0)),
                                   jnp.exp(lg_ref[1, h] * jnp.maximum(-diff, 0.0)))

    def update_state(kzt, vh, cdec):
        for h in range(RET_HEADS):
            lo, hi = h * HEAD_DIM, (h + 1) * HEAD_DIM
            s_scr[h] = cdec[:, lo:hi] * s_scr[h] + _dot(kzt[lo:hi, :], vh[:, lo:hi])

    @pl.when(p == 0)
    def _():
        @pl.when(j == 0)
        def _():
            s_scr[...] = s0b_ref[0]

        sb_scr[nc - 1 - j] = s_scr[...]
        kzt = (k_ref[0] * jnp.exp(lgb * pos)).T.astype(BF16)
        update_state(kzt, v_ref[0].astype(BF16), jnp.exp(lgb * float(c)))

        @pl.when(j == nc - 1)
        def _():
            sfb_ref[0] = s_scr[...]

    @pl.when(p == 1)
    def _():
        @pl.when(j == 0)
        def _():
            s_scr[...] = s0f_ref[0]

        q = q_ref[0]
        k = k_ref[0]
        qf = (q * jnp.exp(lgf * (pos + 1.0))).astype(BF16)
        qb = (q * jnp.exp(lgb * (float(c) - pos))).astype(BF16)
        kzt = (k * jnp.exp(lgf * (float(c) - 1.0 - pos))).T.astype(BF16)
        qh = q.astype(BF16)
        kh = k.astype(BF16)
        vh = v_ref[0].astype(BF16)
        outs = []
        for h in range(RET_HEADS):
            lo, hi = h * HEAD_DIM, (h + 1) * HEAD_DIM
            sc = _dot_nt(qh[:, lo:hi], kh[:, lo:hi])
            y = (_dot((sc * dec_scr[h]).astype(BF16), vh[:, lo:hi])
                 + _dot(qf[:, lo:hi], s_scr[h].astype(BF16))
                 + _dot(qb[:, lo:hi], sb_scr[j, h].astype(BF16)))
            mu = jnp.mean(y, axis=-1, keepdims=True)
            yc = y - mu
            var = jnp.mean(yc * yc, axis=-1, keepdims=True)
            outs.append(yc * lax.rsqrt(var + EPS))
        update_state(kzt, vh, jnp.exp(lgf * float(c)))
        gate = g_ref[0]
        o_ref[0] = jnp.concatenate(outs, axis=1) * gn_ref[...] * (gate * jax.nn.sigmoid(gate))

        @pl.when(j == nc - 1)
        def _():
            sff_ref[0] = s_scr[...]


def _retention(q, k, v, gate, s0_f, s0_b, lg, lgf_row, lgb_row, gn_row):
    b, l, w = q.shape
    c = min(RET_CHUNK, l)
    nc = l // c
    st = (RET_HEADS, HEAD_DIM, HEAD_DIM)
    st_spec = pl.BlockSpec((1,) + st, lambda i, p, j: (i, 0, 0, 0))
    row = pl.BlockSpec((1, w), lambda i, p, j: (0, 0))
    both = pl.BlockSpec((1, c, w), lambda i, p, j: (i, jnp.where(p == 0, nc - 1 - j, j), 0))
    fwd_only = pl.BlockSpec((1, c, w), lambda i, p, j: (i, p * j, 0))
    return pl.pallas_call(
        _ret_kernel,
        grid=(b, 2, nc),
        in_specs=[pl.BlockSpec(memory_space=pltpu.SMEM), fwd_only, both, both, fwd_only,
                  st_spec, st_spec, row, row, row],
        out_specs=[fwd_only, st_spec, st_spec],
        out_shape=[jax.ShapeDtypeStruct((b, l, w), F32), jax.ShapeDtypeStruct((b,) + st, F32),
                   jax.ShapeDtypeStruct((b,) + st, F32)],
        scratch_shapes=[pltpu.VMEM(st, F32), pltpu.VMEM((nc,) + st, F32), pltpu.VMEM((RET_HEADS, c, c), F32)],
        compiler_params=_cparams("arbitrary", "arbitrary", "arbitrary"),
        name="retention",
    )(lg, q, k, v, gate, s0_f, s0_b, lgf_row, lgb_row, gn_row)


def _layer_norm(x, g, b):
    mu = jnp.mean(x, axis=-1, keepdims=True)
    xc = x - mu
    var = jnp.mean(xc * xc, axis=-1, keepdims=True)
    return xc * lax.rsqrt(var + EPS) * g + b


def _store_rows_tiled(ref, x):
    tm = x.shape[0]
    for s in range(SUBLANES):
        ref[pl.ds(s, tm, stride=SUBLANES), :] = x[:, s * LANES:(s + 1) * LANES]


def _load_rows_tiled(ref, tm):
    return jnp.concatenate([ref[pl.ds(s, tm, stride=SUBLANES), :] for s in range(SUBLANES)], axis=1)


def _store_rows_packed(ref, x):
    tm, d = x.shape
    bits = pltpu.bitcast(x.astype(BF16).astype(F32), jnp.uint32)
    word = (bits[:, :d // 2] >> 16) | bits[:, d // 2:]
    for s in range(PACKED_ROWS):
        ref[pl.ds(s, tm, stride=PACKED_ROWS), :] = word[:, s * LANES:(s + 1) * LANES]


def _load_rows_packed(ref, tm):
    word = jnp.concatenate([ref[pl.ds(s, tm, stride=PACKED_ROWS), :] for s in range(PACKED_ROWS)], axis=1)
    lo = pltpu.bitcast(word << 16, F32).astype(BF16)
    hi = pltpu.bitcast(word & jnp.uint32(0xFFFF0000), F32).astype(BF16)
    return jnp.concatenate([lo, hi], axis=1)


def _outproj_kernel(alpha, cu_ref, cup_ref, cun_ref, cb_ref, cw_ref, ret_ref, att_ref, x_ref, g1_ref,
                    w_ref, lng_ref, lnb_ref, sc_ref, sh_ref, wrh_ref, wrl_ref, br_ref, cnt0_ref,
                    x1_ref, h2_ref, ri_ref, rw_ref, cnt_ref, run_scr):
    i = pl.program_id(0)
    j = pl.program_id(1)

    @pl.when((i == 0) & (j == 0))
    def _():
        run_scr[...] = cnt0_ref[...]

    t = cu_ref[0]
    tm = t.shape[0]
    prev_row = jnp.where(j == 0, 0.0, cup_ref[0][7:8, :])
    next_row = jnp.where(j == pl.num_programs(1) - 1, 0.0, cun_ref[0][0:1, :])
    row = lax.broadcasted_iota(jnp.int32, t.shape, 0)
    t_prev = jnp.where(row == 0, prev_row, pltpu.roll(t, 1, axis=0))
    t_next = jnp.where(row == tm - 1, next_row, pltpu.roll(t, tm - 1, axis=0))
    conv = cb_ref[0] * (t_prev * cw_ref[0] + t * cw_ref[1] + t_next * cw_ref[2])
    y = (_dot(conv.astype(BF16), w_ref[0:256, :])
         + _dot(ret_ref[0].astype(BF16), w_ref[256:512, :])
         + _dot(att_ref[0].astype(BF16), w_ref[512:1024, :]))
    x1 = _layer_norm(alpha * x_ref[0] + g1_ref[0] * y, lng_ref[...], lnb_ref[...])
    x1_ref[0] = x1
    h2 = x1 * sc_ref[0] + sh_ref[0]
    _store_rows_packed(h2_ref, h2)
    h_hi, h_lo = _split_bf16(h2)
    wrh = wrh_ref[...]
    logits = _dot(h_hi, wrh) + _dot(h_lo, wrh) + _dot(h_hi, wrl_ref[...]) + br_ref[...]

    lane = lax.broadcasted_iota(jnp.int32, logits.shape, 1)
    lane_f = lane.astype(F32)
    work = jnp.where(lane < N_EXPERTS, logits, -jnp.inf)
    vals, sels = [], []
    for _ in range(TOP_K):
        m = jnp.max(work, axis=-1, keepdims=True)
        first = jnp.min(jnp.where(work == m, lane_f, float(ROUTER_PAD)), axis=-1, keepdims=True)
        sel = lane_f == first
        vals.append(m)
        sels.append(sel)
        work = jnp.where(sel, -jnp.inf, work)
    exps = [jnp.exp(v - vals[0]) for v in vals]
    denom = exps[0]
    for e in exps[1:]:
        denom = denom + e

    cnt = jnp.zeros(logits.shape, F32)
    for sel in sels:
        cnt = cnt + jnp.where(sel, 1.0, 0.0)
    r_i = lax.broadcasted_iota(jnp.int32, (tm, tm), 0)
    c_i = lax.broadcasted_iota(jnp.int32, (tm, tm), 1)
    before = jnp.where(c_i < r_i, 1.0, 0.0).astype(BF16)
    base = _dot(before, cnt.astype(BF16)) + run_scr[...]
    run_scr[...] = run_scr[...] + jnp.sum(cnt, axis=0, keepdims=True)
    cnt_ref[...] = run_scr[...]

    ri = jnp.zeros(logits.shape, F32)
    rw = jnp.zeros(logits.shape, F32)
    for kk in range(TOP_K):
        e_idx = jnp.sum(jnp.where(sels[kk], lane_f, 0.0), axis=-1, keepdims=True)
        rank = jnp.sum(jnp.where(sels[kk], base, 0.0), axis=-1, keepdims=True)
        ri = jnp.where(lane == kk, e_idx, ri)
        ri = jnp.where(lane == TOP_K + kk, rank, ri)
        rw = jnp.where(lane == kk, exps[kk] / denom, rw)
    ri_ref[0] = ri.astype(jnp.int32)
    rw_ref[0] = rw


def _outproj(alpha, cu, cb, cw, ret, att, x, g1, w_bf, lng, lnb, sc2, sh2, wrh, wrl, br, cnt0, tm):
    b, l, d = x.shape
    nj = l // tm
    tok = lambda w: pl.BlockSpec((1, tm, w), lambda i, j: (i, j, 0))
    vec = pl.BlockSpec((1, 1, d), lambda i, j: (i, 0, 0))
    full = lambda s: pl.BlockSpec(s, lambda i, j: (0,) * len(s))
    r8 = tm // 8
    nb8 = l // 8
    return pl.pallas_call(
        functools.partial(_outproj_kernel, alpha),
        grid=(b, nj),
        in_specs=[tok(256),
                  pl.BlockSpec((1, 8, 256), lambda i, j: (i, jnp.maximum(j * r8 - 1, 0), 0)),
                  pl.BlockSpec((1, 8, 256), lambda i, j: (i, jnp.minimum((j + 1) * r8, nb8 - 1), 0)),
                  tok(256), full((3, 1, 256)), tok(256), tok(512), tok(d), vec,
                  full((d, d)), full((1, d)), full((1, d)), vec, vec,
                  full((d, ROUTER_PAD)), full((d, ROUTER_PAD)), full((1, ROUTER_PAD)), full((1, ROUTER_PAD))],
        out_specs=[tok(d), pl.BlockSpec((tm * PACKED_ROWS, LANES), lambda i, j: (i * nj + j, 0)),
                   tok(ROUTER_PAD), tok(ROUTER_PAD), full((1, ROUTER_PAD))],
        out_shape=[jax.ShapeDtypeStruct((b, l, d), F32),
                   jax.ShapeDtypeStruct((b * l * PACKED_ROWS, LANES), jnp.uint32),
                   jax.ShapeDtypeStruct((b, l, ROUTER_PAD), jnp.int32),
                   jax.ShapeDtypeStruct((b, l, ROUTER_PAD), F32),
                   jax.ShapeDtypeStruct((1, ROUTER_PAD), F32)],
        scratch_shapes=[pltpu.VMEM((1, ROUTER_PAD), F32)],
        compiler_params=_cparams("arbitrary", "arbitrary"),
        name="outproj_ln_router",
    )(cu, cu, cu, cb, cw, ret, att, x, g1, w_bf, lng, lnb, sc2, sh2, wrh, wrl, br, cnt0)


def _ffn_kernel(be_ref, nu_ref, x_ref, wgu_ref, bgu_ref, wd_ref, bd_ref, o_ref, wgu_scr, wd_scr):
    j = pl.program_id(0)
    f = wd_ref.shape[2]
    tm = x_ref.shape[0] // PACKED_ROWS

    @pl.when(j < nu_ref[0])
    def _():
        @pl.when((j == 0) | (be_ref[j] != be_ref[jnp.maximum(j - 1, 0)]))
        def _():
            wgu_scr[...] = wgu_ref[0, 0].astype(BF16)
            wd_scr[...] = wd_ref[0, 0].astype(BF16)

        x = _load_rows_packed(x_ref, tm)
        gu = _dot(x, wgu_scr[...]) + bgu_ref[0, 0]
        gate = jnp.minimum(gu[:, :f], SWIGLU_LIMIT)
        up = jnp.clip(gu[:, f:], -SWIGLU_LIMIT, SWIGLU_LIMIT)
        act = (up + 1.0) * (gate * jax.nn.sigmoid(SWIGLU_ALPHA * gate))
        _store_rows_tiled(o_ref, _dot(act.astype(BF16), wd_scr[...]) + bd_ref[0, 0])

    @pl.when(j >= nu_ref[0])
    def _():
        o_ref[...] = jnp.zeros_like(o_ref)


def _expert_ffn(layer, block_e, n_used, xs, wgu, bgu, wd, bd, tm):
    n_rows = xs.shape[0] // PACKED_ROWS
    depth, ne, d, f2 = wgu.shape
    f = f2 // 2
    rows = pl.BlockSpec((tm * SUBLANES, LANES), lambda j, be, nu: (j, 0))
    grid_spec = pltpu.PrefetchScalarGridSpec(
        num_scalar_prefetch=2,
        grid=(n_rows // tm,),
        in_specs=[pl.BlockSpec((tm * PACKED_ROWS, LANES), lambda j, be, nu: (j, 0)),
                  pl.BlockSpec((1, 1, d, f2), lambda j, be, nu: (layer, be[j], 0, 0)),
                  pl.BlockSpec((1, 1, 1, f2), lambda j, be, nu: (layer, be[j], 0, 0)),
                  pl.BlockSpec((1, 1, f, d), lambda j, be, nu: (layer, be[j], 0, 0)),
                  pl.BlockSpec((1, 1, 1, d), lambda j, be, nu: (layer, be[j], 0, 0))],
        out_specs=rows,
        scratch_shapes=[pltpu.VMEM((d, f2), BF16), pltpu.VMEM((f, d), BF16)],
    )
    return pl.pallas_call(
        _ffn_kernel,
        grid_spec=grid_spec,
        out_shape=jax.ShapeDtypeStruct((n_rows * SUBLANES, LANES), F32),
        compiler_params=_cparams("arbitrary"),
        name="expert_ffn",
    )(block_e, n_used, xs, wgu, bgu.reshape(depth, ne, 1, f2), wd, bd.reshape(depth, ne, 1, d))


def _sc_worker_base(per_worker):
    return (lax.axis_index("s") * SC_CORES + lax.axis_index("c")) * per_worker


def _sc_dispatch(rows, dest, n_out):
    t = rows.shape[0]
    kk = dest.shape[0] // t
    w = SC_WINDOW
    per_worker = t // (SC_CORES * SC_SUBCORES)
    mesh = plsc.VectorSubcoreMesh(core_axis_name="c", subcore_axis_name="s")

    @functools.partial(
        pl.kernel, mesh=mesh,
        out_type=jax.ShapeDtypeStruct((n_out,) + rows.shape[1:], rows.dtype),
        scratch_types=[pltpu.VMEM((w,), jnp.int32)] * kk + [pltpu.VMEM((w,) + rows.shape[1:], rows.dtype)])
    def scatter_rows(r_hbm, d_hbm, o_hbm, *scratch):
        idx_v, rows_v = scratch[:kk], scratch[kk]
        base = _sc_worker_base(per_worker)

        @pl.loop(0, per_worker // w)
        def _(i):
            off = base + i * w
            pltpu.sync_copy(r_hbm.at[pl.ds(off, w)], rows_v)
            for s in range(kk):
                pltpu.sync_copy(d_hbm.at[pl.ds(s * t + off, w)], idx_v[s])
            for s in range(kk):
                pltpu.sync_copy(rows_v, o_hbm.at[idx_v[s]])

    return scatter_rows(rows, dest)


def _sc_gather(table, idx):
    n = idx.shape[0]
    w = SC_WINDOW
    per_worker = n // (SC_CORES * SC_SUBCORES)
    mesh = plsc.VectorSubcoreMesh(core_axis_name="c", subcore_axis_name="s")

    @functools.partial(
        pl.kernel, mesh=mesh,
        out_type=jax.ShapeDtypeStruct((n,) + table.shape[1:], table.dtype),
        scratch_types=[pltpu.VMEM((w,), jnp.int32), pltpu.VMEM((w,) + table.shape[1:], table.dtype)])
    def gather_rows(t_hbm, i_hbm, o_hbm, idx_v, rows_v):
        base = _sc_worker_base(per_worker)

        @pl.loop(0, per_worker // w)
        def _(i):
            off = base + i * w
            pltpu.sync_copy(i_hbm.at[pl.ds(off, w)], idx_v)
            pltpu.sync_copy(t_hbm.at[idx_v], rows_v)
            pltpu.sync_copy(rows_v, o_hbm.at[pl.ds(off, w)])

    return gather_rows(table, idx)


def _combine_kernel(alpha, g_ref, w_ref, x_ref, g2_ref, lng_ref, lnb_ref, o_ref):
    tm = x_ref.shape[1]
    w = w_ref[0]
    f = _load_rows_tiled(g_ref.at[0], tm) * w[:, 0:1]
    for kk in range(1, TOP_K):
        f = f + _load_rows_tiled(g_ref.at[kk], tm) * w[:, kk:kk + 1]
    o_ref[0] = _layer_norm(alpha * x_ref[0] + g2_ref[0] * f, lng_ref[...], lnb_ref[...])


def _combine(alpha, gathered, tok_off, wts, x1, g2, lng, lnb, tm):
    b, l, d = x1.shape
    nj = l // tm
    blk_off = tok_off // tm
    tok = pl.BlockSpec((1, tm, d), lambda i, j: (i, j, 0))
    full = pl.BlockSpec((1, d), lambda i, j: (0, 0))
    return pl.pallas_call(
        functools.partial(_combine_kernel, alpha),
        grid=(b, nj),
        in_specs=[pl.BlockSpec((TOP_K, tm * SUBLANES, LANES), lambda i, j: (0, blk_off + i * nj + j, 0)),
                  pl.BlockSpec((1, tm, ROUTER_PAD), lambda i, j: (i, j, 0)),
                  tok, pl.BlockSpec((1, 1, d), lambda i, j: (i, 0, 0)), full, full],
        out_specs=tok,
        out_shape=jax.ShapeDtypeStruct((b, l, d), F32),
        compiler_params=_cparams("arbitrary", "arbitrary"),
        name="combine_ln",
    )(gathered, wts, x1, g2, lng, lnb)


def _rope_tables(l):
    rows = l // GRID_W
    axis_dim = HEAD_DIM // 2
    inv_freq = ROPE_BASE ** (-jnp.arange(0, axis_dim, 2, dtype=F32) / axis_dim)
    row = jnp.repeat(jnp.arange(rows, dtype=F32), GRID_W)
    col = jnp.tile(jnp.arange(GRID_W, dtype=F32), rows)
    ang = jnp.stack([row[:, None] * inv_freq, col[:, None] * inv_freq], axis=1)
    cos, sin = jnp.cos(ang), jnp.sin(ang)
    cos64 = jnp.broadcast_to(cos[:, :, None, :], (l, 2, 2, HEAD_DIM // 4)).reshape(l, HEAD_DIM)
    sin64 = jnp.stack([-sin, sin], axis=2).reshape(l, HEAD_DIM)
    return jnp.tile(cos64, (1, LANES // HEAD_DIM)), jnp.tile(sin64, (1, LANES // HEAD_DIM))


def _route_tables(ri, counts, tm):
    t = ri.shape[0]
    top_e = ri[:, :TOP_K]
    rank = ri[:, TOP_K:2 * TOP_K]
    padded = (counts + tm - 1) // tm * tm
    pad_end = jnp.cumsum(padded)
    pad_start = pad_end - padded
    experts = jnp.arange(N_EXPERTS, dtype=jnp.int32)
    start = jnp.sum(jnp.where(top_e[:, :, None] == experts, pad_start, 0), axis=-1)
    dest = (start + rank).T.reshape(TOP_K * t)
    n_tiles = (t * TOP_K + N_EXPERTS * (tm - 1) + tm - 1) // tm
    tile_start = jnp.arange(n_tiles, dtype=jnp.int32) * tm
    block_e = jnp.minimum(jnp.sum((pad_end[None, :] <= tile_start[:, None]).astype(jnp.int32), axis=1), N_EXPERTS - 1)
    n_used = (pad_end[-1] // tm).astype(jnp.int32).reshape(1)
    return dest, block_e.astype(jnp.int32), n_used, n_tiles


def kernel(x, c, ctx, c_ctx, w_mod, b_mod, w_in, conv_w, ret_decay_exp, ret_gn_g, q_norm_g, k_norm_g, w_out,
           ln_g, ln_b, w_router, b_router, w_gate_up, b_gate_up, w_down, b_down):
    depth = w_mod.shape[0]
    alpha = (2.0 * depth) ** 0.25
    b, l, d = x.shape
    lc = ctx.shape[1]
    tm_lat, tm_ctx, tm_moe, tq = 512, 256, 512, 128

    n_rows = (b + 1 + 7) // 8 * 8
    c_all = jnp.zeros((n_rows, d), F32).at[:b].set(c).at[b].set(c_ctx)
    mod = _modulation(c_all, w_mod, b_mod)

    cos_l, sin_l = _rope_tables(l)
    cos_c, sin_c = jnp.ones((lc, LANES), F32), jnp.zeros((lc, LANES), F32)
    lane_head = jnp.arange(LANES) // HEAD_DIM
    ones_bd = (lane_head[:, None] == lane_head[None, :]).astype(BF16)
    zero_state = jnp.zeros((b, RET_HEADS, HEAD_DIM, HEAD_DIM), F32)

    for layer in range(depth):
        last = layer == depth - 1
        m_lat = mod[layer, :b].reshape(b, 1, N_MOD, d)
        m_ctx = jnp.broadcast_to(mod[layer, b].reshape(1, 1, N_MOD, d), (b, 1, N_MOD, d))
        sh1, sc1, g1, sh2, sc2, g2 = (m_lat[:, :, i] for i in range(N_MOD))
        sh1c, sc1c, g1c, sh2c, sc2c, g2c = (m_ctx[:, :, i] for i in range(N_MOD))

        w_in_bf = w_in[layer].astype(BF16)
        w_out_bf = w_out[layer].astype(BF16)
        gq = jnp.tile(q_norm_g[layer], LANES // HEAD_DIM).reshape(1, LANES)
        gk = jnp.tile(k_norm_g[layer], LANES // HEAD_DIM).reshape(1, LANES)
        log_gamma = jnp.log1p(-jnp.exp2(-ret_decay_exp[layer].astype(F32)))
        lg_rows = jnp.repeat(log_gamma, HEAD_DIM, axis=1)
        lgf_row, lgb_row = lg_rows[0:1], lg_rows[1:2]
        gn_row = ret_gn_g[layer].reshape(1, -1)
        cw = conv_w[layer].T.reshape(3, 1, -1)
        wr_pad = jnp.zeros((d, ROUTER_PAD), F32).at[:, :N_EXPERTS].set(w_router[layer])
        wrh = wr_pad.astype(BF16)
        wrl = (wr_pad - wrh.astype(F32)).astype(BF16)
        br = jnp.zeros((1, ROUTER_PAD), F32).at[0, :N_EXPERTS].set(b_router[layer])
        lng1, lnb1 = ln_g[layer, 0].reshape(1, d), ln_b[layer, 0].reshape(1, d)
        lng2, lnb2 = ln_g[layer, 1].reshape(1, d), ln_b[layer, 1].reshape(1, d)

        zc = _inproj(ctx, 1.0 + sc1c, sh1c, w_in_bf, cos_c, sin_c, gq, gk, ones_bd, tm_ctx)
        zl = _inproj(x, 1.0 + sc1, sh1, w_in_bf, cos_l, sin_l, gq, gk, ones_bd, tm_lat)
        cu_c, cb_c, rq_c, rk_c, rv_c, rg_c, aq_c, ak_c, av_c = zc
        cu_l, cb_l, rq_l, rk_l, rv_l, rg_l, aq_l, ak_l, av_l = zl

        ret_c, s_fwd, s_bwd = _retention(rq_c, rk_c, rv_c, rg_c, zero_state, zero_state,
                                         log_gamma, lgf_row, lgb_row, gn_row)
        ret_l, _, _ = _retention(rq_l, rk_l, rv_l, rg_l, s_fwd, s_bwd, log_gamma, lgf_row, lgb_row, gn_row)

        def kv_heads(a):
            return a.reshape(b, a.shape[1], -1, HEAD_DIM).transpose(0, 2, 1, 3)

        def v_heads(a):
            vh = kv_heads(a)
            pad = jnp.zeros(vh.shape[:3] + (LANES - HEAD_DIM - 1,), BF16)
            return jnp.concatenate([vh, jnp.ones(vh.shape[:3] + (1,), BF16), pad], axis=-1)

        k_all = kv_heads(jnp.concatenate([ak_c, ak_l], axis=1))
        v_all = v_heads(jnp.concatenate([av_c, av_l], axis=1))
        att_l = _attention(aq_l, k_all, v_all, tq)

        cnt0 = jnp.zeros((1, ROUTER_PAD), F32)
        if not last:
            att_c = _attention(aq_c, kv_heads(ak_c), v_heads(av_c), tq)
            ctx1, h2_c, ri_c, rw_c, cnt0 = _outproj(alpha, cu_c, cb_c, cw, ret_c, att_c, ctx, g1c, w_out_bf,
                                                    lng1, lnb1, 1.0 + sc2c, sh2c, wrh, wrl, br, cnt0, tm_ctx)
        x1, h2_l, ri_l, rw_l, cnt = _outproj(alpha, cu_l, cb_l, cw, ret_l, att_l, x, g1, w_out_bf, lng1, lnb1,
                                             1.0 + sc2, sh2, wrh, wrl, br, cnt0, tm_lat)
        if not last:
            n_c = b * lc
            h2 = jnp.concatenate([h2_c, h2_l], axis=0)
            ri = jnp.concatenate([ri_c.reshape(n_c, -1), ri_l.reshape(b * l, -1)], axis=0)
        else:
            n_c = 0
            h2 = h2_l
            ri = ri_l.reshape(b * l, -1)
        n_tok = n_c + b * l

        counts = cnt[0, :N_EXPERTS].astype(jnp.int32)
        dest, block_e, n_used, n_tiles = _route_tables(ri, counts, tm_moe)
        xs = _sc_dispatch(h2.reshape(n_tok, PACKED_ROWS, LANES), dest, n_tiles * tm_moe)
        ys = _expert_ffn(layer, block_e, n_used, xs.reshape(-1, LANES), w_gate_up, b_gate_up, w_down, b_down, tm_moe)
        gathered = _sc_gather(ys.reshape(-1, SUBLANES, LANES), dest).reshape(TOP_K, n_tok * SUBLANES, LANES)
        if not last:
            ctx = _combine(alpha, gathered, 0, rw_c, ctx1, g2c, lng2, lnb2, tm_ctx)
        x = _combine(alpha, gathered, n_c, rw_l, x1, g2, lng2, lnb2, tm_lat)
    return x
```

```python
import functools

import jax
import jax.numpy as jnp
from jax import lax
from jax.experimental import pallas as pl
from jax.experimental.pallas import tpu as pltpu
from jax.experimental.pallas import tpu_sc as plsc

F32 = jnp.float32
BF16 = jnp.bfloat16

HEAD_DIM = 64
GRID_W = 64
ROPE_BASE = 10000.0
N_EXPERTS = 32
TOP_K = 4
SWIGLU_ALPHA = 1.702
SWIGLU_LIMIT = 7.0
N_MOD = 6
EPS = 1e-6
RET_HEADS = 4
RET_CHUNK = 256
ATT_Q_PER_KV = 4
ATT_KV_CHUNK = 256
LANES = 128
ROUTER_PAD = 128
VMEM_LIMIT = 56 * 1024 * 1024
SUBLANES = 8
PACKED_ROWS = 4
SC_CORES = 2
SC_SUBCORES = 16
SC_WINDOW = 64


def _dot(a, b):
    return jnp.dot(a, b, preferred_element_type=F32)


def _dot_nt(a, b):
    return lax.dot_general(a, b, (((1,), (1,)), ((), ())), preferred_element_type=F32)


def _split_bf16(x):
    hi = x.astype(BF16)
    lo = (x - hi.astype(F32)).astype(BF16)
    return hi, lo


def _cparams(*sem):
    return pltpu.CompilerParams(dimension_semantics=sem, vmem_limit_bytes=VMEM_LIMIT)


def _mod_kernel(c_ref, w_ref, b_ref, o_ref):
    c = c_ref[...]
    a = c * jax.nn.sigmoid(c)
    a_hi, a_lo = _split_bf16(a)
    w_hi, w_lo = _split_bf16(w_ref[0])
    o_ref[0] = _dot(a_hi, w_hi) + _dot(a_lo, w_hi) + _dot(a_hi, w_lo) + b_ref[0]


def _modulation(c_all, w_mod, b_mod):
    depth, d, n = w_mod.shape
    r = c_all.shape[0]
    tn = 1536
    return pl.pallas_call(
        _mod_kernel,
        grid=(depth, n // tn),
        in_specs=[
            pl.BlockSpec((r, d), lambda l, j: (0, 0)),
            pl.BlockSpec((1, d, tn), lambda l, j: (l, 0, j)),
            pl.BlockSpec((1, 1, tn), lambda l, j: (l, 0, j)),
        ],
        out_specs=pl.BlockSpec((1, r, tn), lambda l, j: (l, 0, j)),
        out_shape=jax.ShapeDtypeStruct((depth, r, n), F32),
        compiler_params=_cparams("arbitrary", "arbitrary"),
        name="modulation",
    )(c_all, w_mod, b_mod.reshape(depth, 1, n))


def _inproj_kernel(x_ref, sc_ref, sh_ref, w_ref, cos_ref, sin_ref, gq_ref, gk_ref, ones_ref,
                   cu_ref, cb_ref, rq_ref, rk_ref, rv_ref, rg_ref, aq_ref, ak_ref, av_ref):
    h = (x_ref[0] * sc_ref[0] + sh_ref[0]).astype(BF16)
    z = _dot(h, w_ref[...])
    cos = cos_ref[...]
    sin = sin_ref[...]
    ones = ones_ref[...]
    lane = lax.broadcasted_iota(jnp.int32, cos.shape, 1)
    first_half = (lane & 31) < 16

    def rope(xs):
        nxt = pltpu.roll(xs, LANES - 16, axis=1)
        prv = pltpu.roll(xs, 16, axis=1)
        return xs * cos + jnp.where(first_half, nxt, prv) * sin

    def rms(xs, g):
        s_hi, s_lo = _split_bf16(xs * xs)
        ssq = _dot(s_hi, ones) + _dot(s_lo, ones)
        return xs * lax.rsqrt(ssq * (1.0 / HEAD_DIM) + EPS) * g

    cu_ref[0] = z[:, 512:768] * z[:, 0:256]
    cb_ref[0] = z[:, 256:512]
    for j in range(2):
        lo, hi = j * LANES, (j + 1) * LANES
        rq_ref[0, :, lo:hi] = rope(z[:, 768 + lo:768 + hi])
        rk_ref[0, :, lo:hi] = rope(z[:, 1024 + lo:1024 + hi] * (HEAD_DIM ** -0.5))
    rv_ref[0] = z[:, 1280:1536]
    rg_ref[0] = z[:, 1536:1792]
    gq = gq_ref[...]
    for j in range(4):
        lo, hi = j * LANES, (j + 1) * LANES
        q = rope(rms(z[:, 1792 + lo:1792 + hi], gq)) * (HEAD_DIM ** -0.5)
        aq_ref[0, :, lo:hi] = q.astype(BF16)
    ak_ref[0] = rope(rms(z[:, 2304:2432], gk_ref[...])).astype(BF16)
    av_ref[0] = z[:, 2432:2560].astype(BF16)


def _inproj(x, sc, sh, w_bf, cos, sin, gq, gk, ones_bd, tm):
    b, l, d = x.shape
    n = w_bf.shape[1]
    tok = lambda w: pl.BlockSpec((1, tm, w), lambda i, j: (i, j, 0))
    vec = pl.BlockSpec((1, 1, d), lambda i, j: (i, 0, 0))
    full = lambda s: pl.BlockSpec(s, lambda i, j: (0,) * len(s))
    widths = (256, 256, 256, 256, 256, 256, 512, 128, 128)
    dtypes = (F32,) * 6 + (BF16,) * 3
    return pl.pallas_call(
        _inproj_kernel,
        grid=(b, l // tm),
        in_specs=[tok(d), vec, vec, full((d, n)),
                  pl.BlockSpec((tm, LANES), lambda i, j: (j, 0)),
                  pl.BlockSpec((tm, LANES), lambda i, j: (j, 0)),
                  full((1, LANES)), full((1, LANES)), full((LANES, LANES))],
        out_specs=[tok(w) for w in widths],
        out_shape=[jax.ShapeDtypeStruct((b, l, w), dt) for w, dt in zip(widths, dtypes)],
        compiler_params=_cparams("arbitrary", "arbitrary"),
        name="inproj",
    )(x, sc, sh, w_bf, cos, sin, gq, gk, ones_bd)


def _attn_kernel(q_ref, k_ref, v_ref, o_ref):
    tq = q_ref.shape[1]
    q4 = q_ref[0]
    q = jnp.concatenate([q4[:, h * HEAD_DIM:(h + 1) * HEAD_DIM] for h in range(ATT_Q_PER_KV)], axis=0)
    lk = k_ref.shape[2]
    m = jnp.full((q.shape[0], 1), -jnp.inf, F32)
    acc = jnp.zeros((q.shape[0], LANES), F32)
    for lo in range(0, lk, ATT_KV_CHUNK):
        hi = min(lo + ATT_KV_CHUNK, lk)
        s = _dot_nt(q, k_ref[0, 0, lo:hi, :])
        m_new = jnp.maximum(m, jnp.max(s, axis=-1, keepdims=True))
        p = jnp.exp(s - m_new)
        acc = acc * jnp.exp(m - m_new) + _dot(p.astype(BF16), v_ref[0, 0, lo:hi, :])
        m = m_new
    o = acc[:, :HEAD_DIM] / acc[:, HEAD_DIM:HEAD_DIM + 1]
    o_ref[0] = jnp.concatenate([o[h * tq:(h + 1) * tq] for h in range(ATT_Q_PER_KV)], axis=1)


def _attention(q, k, v, tq):
    b, lq, wq = q.shape
    hkv, lk = k.shape[1], k.shape[2]
    wg = ATT_Q_PER_KV * HEAD_DIM
    return pl.pallas_call(
        _attn_kernel,
        grid=(b, hkv, lq // tq),
        in_specs=[pl.BlockSpec((1, tq, wg), lambda i, g, j: (i, j, g)),
                  pl.BlockSpec((1, 1, lk, HEAD_DIM), lambda i, g, j: (i, g, 0, 0)),
                  pl.BlockSpec((1, 1, lk, LANES), lambda i, g, j: (i, g, 0, 0))],
        out_specs=pl.BlockSpec((1, tq, wg), lambda i, g, j: (i, j, g)),
        out_shape=jax.ShapeDtypeStruct((b, lq, wq), F32),
        compiler_params=_cparams("arbitrary", "arbitrary", "arbitrary"),
        name="attention",
    )(q, k, v)


def _ret_kernel(lg_ref, q_ref, k_ref, v_ref, g_ref, s0f_ref, s0b_ref, lgf_ref, lgb_ref, gn_ref,
                o_ref, sff_ref, sfb_ref, s_scr, sb_scr, dec_scr):
    i, p, j = pl.program_id(0), pl.program_id(1), pl.program_id(2)
    nc = pl.num_programs(2)
    c = k_ref.shape[1]
    w = k_ref.shape[2]
    pos = lax.broadcasted_iota(jnp.int32, (c, w), 0).astype(F32)
    lgf = lgf_ref[...]
    lgb = lgb_ref[...]

    @pl.when((i == 0) & (p == 0) & (j == 0))
    def _():
        diff = (lax.broadcasted_iota(jnp.int32, (c, c), 0) - lax.broadcasted_iota(jnp.int32, (c, c), 1)).astype(F32)
        for h in range(RET_HEADS):
            dec_scr[h] = jnp.where(diff >= 0.0, jnp.exp(lg_ref[0, h] * jnp.maximum(diff, 0.0)),
                                   jnp.exp(lg_ref[1, h] * jnp.maximum(-diff, 0.0)))

    def update_state(kzt, vh, cdec):
        for h in range(RET_HEADS):
            lo, hi = h * HEAD_DIM, (h + 1) * HEAD_DIM
            s_scr[h] = cdec[:, lo:hi] * s_scr[h] + _dot(kzt[lo:hi, :], vh[:, lo:hi])

    @pl.when(p == 0)
    def _():
        @pl.when(j == 0)
        def _():
            s_scr[...] = s0b_ref[0]

        sb_scr[nc - 1 - j] = s_scr[...]
        kzt = (k_ref[0] * jnp.exp(lgb * pos)).T.astype(BF16)
        update_state(kzt, v_ref[0].astype(BF16), jnp.exp(lgb * float(c)))

        @pl.when(j == nc - 1)
        def _():
            sfb_ref[0] = s_scr[...]

    @pl.when(p == 1)
    def _():
        @pl.when(j == 0)
        def _():
            s_scr[...] = s0f_ref[0]

        q = q_ref[0]
        k = k_ref[0]
        qf = (q * jnp.exp(lgf * (pos + 1.0))).astype(BF16)
        qb = (q * jnp.exp(lgb * (float(c) - pos))).astype(BF16)
        kzt = (k * jnp.exp(lgf * (float(c) - 1.0 - pos))).T.astype(BF16)
        qh = q.astype(BF16)
        kh = k.astype(BF16)
        vh = v_ref[0].astype(BF16)
        outs = []
        for h in range(RET_HEADS):
            lo, hi = h * HEAD_DIM, (h + 1) * HEAD_DIM
            sc = _dot_nt(qh[:, lo:hi], kh[:, lo:hi])
            y = (_dot((sc * dec_scr[h]).astype(BF16), vh[:, lo:hi])
                 + _dot(qf[:, lo:hi], s_scr[h].astype(BF16))
                 + _dot(qb[:, lo:hi], sb_scr[j, h].astype(BF16)))
            mu = jnp.mean(y, axis=-1, keepdims=True)
            yc = y - mu
            var = jnp.mean(yc * yc, axis=-1, keepdims=True)
            outs.append(yc * lax.rsqrt(var + EPS))
        update_state(kzt, vh, jnp.exp(lgf * float(c)))
        gate = g_ref[0]
        o_ref[0] = jnp.concatenate(outs, axis=1) * gn_ref[...] * (gate * jax.nn.sigmoid(gate))

        @pl.when(j == nc - 1)
        def _():
            sff_ref[0] = s_scr[...]


def _retention(q, k, v, gate, s0_f, s0_b, lg, lgf_row, lgb_row, gn_row):
    b, l, w = q.shape
    c = min(RET_CHUNK, l)
    nc = l // c
    st = (RET_HEADS, HEAD_DIM, HEAD_DIM)
    st_spec = pl.BlockSpec((1,) + st, lambda i, p, j: (i, 0, 0, 0))
    row = pl.BlockSpec((1, w), lambda i, p, j: (0, 0))
    both = pl.BlockSpec((1, c, w), lambda i, p, j: (i, jnp.where(p == 0, nc - 1 - j, j), 0))
    fwd_only = pl.BlockSpec((1, c, w), lambda i, p, j: (i, p * j, 0))
    return pl.pallas_call(
        _ret_kernel,
        grid=(b, 2, nc),
        in_specs=[pl.BlockSpec(memory_space=pltpu.SMEM), fwd_only, both, both, fwd_only,
                  st_spec, st_spec, row, row, row],
        out_specs=[fwd_only, st_spec, st_spec],
        out_shape=[jax.ShapeDtypeStruct((b, l, w), F32), jax.ShapeDtypeStruct((b,) + st, F32),
                   jax.ShapeDtypeStruct((b,) + st, F32)],
        scratch_shapes=[pltpu.VMEM(st, F32), pltpu.VMEM((nc,) + st, F32), pltpu.VMEM((RET_HEADS, c, c), F32)],
        compiler_params=_cparams("arbitrary", "arbitrary", "arbitrary"),
        name="retention",
    )(lg, q, k, v, gate, s0_f, s0_b, lgf_row, lgb_row, gn_row)


def _layer_norm(x, g, b):
    mu = jnp.mean(x, axis=-1, keepdims=True)
    xc = x - mu
    var = jnp.mean(xc * xc, axis=-1, keepdims=True)
    return xc * lax.rsqrt(var + EPS) * g + b


def _store_rows_tiled(ref, x):
    tm = x.shape[0]
    for s in range(SUBLANES):
        ref[pl.ds(s, tm, stride=SUBLANES), :] = x[:, s * LANES:(s + 1) * LANES]


def _load_rows_tiled(ref, tm):
    return jnp.concatenate([ref[pl.ds(s, tm, stride=SUBLANES), :] for s in range(SUBLANES)], axis=1)


def _store_rows_packed(ref, x):
    tm, d = x.shape
    bits = pltpu.bitcast(x.astype(BF16).astype(F32), jnp.uint32)
    word = (bits[:, :d // 2] >> 16) | bits[:, d // 2:]
    for s in range(PACKED_ROWS):
        ref[pl.ds(s, tm, stride=PACKED_ROWS), :] = word[:, s * LANES:(s + 1) * LANES]


def _load_rows_packed(ref, tm):
    word = jnp.concatenate([ref[pl.ds(s, tm, stride=PACKED_ROWS), :] for s in range(PACKED_ROWS)], axis=1)
    lo = pltpu.bitcast(word << 16, F32).astype(BF16)
    hi = pltpu.bitcast(word & jnp.uint32(0xFFFF0000), F32).astype(BF16)
    return jnp.concatenate([lo, hi], axis=1)


def _outproj_kernel(alpha, cu_ref, cup_ref, cun_ref, cb_ref, cw_ref, ret_ref, att_ref, x_ref, g1_ref,
                    w_ref, lng_ref, lnb_ref, sc_ref, sh_ref, wrh_ref, wrl_ref, br_ref, cnt0_ref,
                    x1_ref, h2_ref, ri_ref, rw_ref, cnt_ref, run_scr):
    i = pl.program_id(0)
    j = pl.program_id(1)

    @pl.when((i == 0) & (j == 0))
    def _():
        run_scr[...] = cnt0_ref[...]

    t = cu_ref[0]
    tm = t.shape[0]
    prev_row = jnp.where(j == 0, 0.0, cup_ref[0][7:8, :])
    next_row = jnp.where(j == pl.num_programs(1) - 1, 0.0, cun_ref[0][0:1, :])
    row = lax.broadcasted_iota(jnp.int32, t.shape, 0)
    t_prev = jnp.where(row == 0, prev_row, pltpu.roll(t, 1, axis=0))
    t_next = jnp.where(row == tm - 1, next_row, pltpu.roll(t, tm - 1, axis=0))
    conv = cb_ref[0] * (t_prev * cw_ref[0] + t * cw_ref[1] + t_next * cw_ref[2])
    y = (_dot(conv.astype(BF16), w_ref[0:256, :])
         + _dot(ret_ref[0].astype(BF16), w_ref[256:512, :])
         + _dot(att_ref[0].astype(BF16), w_ref[512:1024, :]))
    x1 = _layer_norm(alpha * x_ref[0] + g1_ref[0] * y, lng_ref[...], lnb_ref[...])
    x1_ref[0] = x1
    h2 = x1 * sc_ref[0] + sh_ref[0]
    _store_rows_packed(h2_ref, h2)
    h_hi, h_lo = _split_bf16(h2)
    wrh = wrh_ref[...]
    logits = _dot(h_hi, wrh) + _dot(h_lo, wrh) + _dot(h_hi, wrl_ref[...]) + br_ref[...]

    lane = lax.broadcasted_iota(jnp.int32, logits.shape, 1)
    lane_f = lane.astype(F32)
    work = jnp.where(lane < N_EXPERTS, logits, -jnp.inf)
    vals, sels = [], []
    for _ in range(TOP_K):
        m = jnp.max(work, axis=-1, keepdims=True)
        first = jnp.min(jnp.where(work == m, lane_f, float(ROUTER_PAD)), axis=-1, keepdims=True)
        sel = lane_f == first
        vals.append(m)
        sels.append(sel)
        work = jnp.where(sel, -jnp.inf, work)
    exps = [jnp.exp(v - vals[0]) for v in vals]
    denom = exps[0]
    for e in exps[1:]:
        denom = denom + e

    cnt = jnp.zeros(logits.shape, F32)
    for sel in sels:
        cnt = cnt + jnp.where(sel, 1.0, 0.0)
    r_i = lax.broadcasted_iota(jnp.int32, (tm, tm), 0)
    c_i = lax.broadcasted_iota(jnp.int32, (tm, tm), 1)
    before = jnp.where(c_i < r_i, 1.0, 0.0).astype(BF16)
    base = _dot(before, cnt.astype(BF16)) + run_scr[...]
    run_scr[...] = run_scr[...] + jnp.sum(cnt, axis=0, keepdims=True)
    cnt_ref[...] = run_scr[...]

    ri = jnp.zeros(logits.shape, F32)
    rw = jnp.zeros(logits.shape, F32)
    for kk in range(TOP_K):
        e_idx = jnp.sum(jnp.where(sels[kk], lane_f, 0.0), axis=-1, keepdims=True)
        rank = jnp.sum(jnp.where(sels[kk], base, 0.0), axis=-1, keepdims=True)
        ri = jnp.where(lane == kk, e_idx, ri)
        ri = jnp.where(lane == TOP_K + kk, rank, ri)
        rw = jnp.where(lane == kk, exps[kk] / denom, rw)
    ri_ref[0] = ri.astype(jnp.int32)
    rw_ref[0] = rw


def _outproj(alpha, cu, cb, cw, ret, att, x, g1, w_bf, lng, lnb, sc2, sh2, wrh, wrl, br, cnt0, tm):
    b, l, d = x.shape
    nj = l // tm
    tok = lambda w: pl.BlockSpec((1, tm, w), lambda i, j: (i, j, 0))
    vec = pl.BlockSpec((1, 1, d), lambda i, j: (i, 0, 0))
    full = lambda s: pl.BlockSpec(s, lambda i, j: (0,) * len(s))
    r8 = tm // 8
    nb8 = l // 8
    return pl.pallas_call(
        functools.partial(_outproj_kernel, alpha),
        grid=(b, nj),
        in_specs=[tok(256),
                  pl.BlockSpec((1, 8, 256), lambda i, j: (i, jnp.maximum(j * r8 - 1, 0), 0)),
                  pl.BlockSpec((1, 8, 256), lambda i, j: (i, jnp.minimum((j + 1) * r8, nb8 - 1), 0)),
                  tok(256), full((3, 1, 256)), tok(256), tok(512), tok(d), vec,
                  full((d, d)), full((1, d)), full((1, d)), vec, vec,
                  full((d, ROUTER_PAD)), full((d, ROUTER_PAD)), full((1, ROUTER_PAD)), full((1, ROUTER_PAD))],
        out_specs=[tok(d), pl.BlockSpec((tm * PACKED_ROWS, LANES), lambda i, j: (i * nj + j, 0)),
                   tok(ROUTER_PAD), tok(ROUTER_PAD), full((1, ROUTER_PAD))],
        out_shape=[jax.ShapeDtypeStruct((b, l, d), F32),
                   jax.ShapeDtypeStruct((b * l * PACKED_ROWS, LANES), jnp.uint32),
                   jax.ShapeDtypeStruct((b, l, ROUTER_PAD), jnp.int32),
                   jax.ShapeDtypeStruct((b, l, ROUTER_PAD), F32),
                   jax.ShapeDtypeStruct((1, ROUTER_PAD), F32)],
        scratch_shapes=[pltpu.VMEM((1, ROUTER_PAD), F32)],
        compiler_params=_cparams("arbitrary", "arbitrary"),
        name="outproj_ln_router",
    )(cu, cu, cu, cb, cw, ret, att, x, g1, w_bf, lng, lnb, sc2, sh2, wrh, wrl, br, cnt0)


def _ffn_kernel(be_ref, nu_ref, x_ref, wgu_ref, bgu_ref, wd_ref, bd_ref, o_ref, wgu_scr, wd_scr):
    j = pl.program_id(0)
    f = wd_ref.shape[2]
    tm = x_ref.shape[0] // PACKED_ROWS

    @pl.when(j < nu_ref[0])
    def _():
        @pl.when((j == 0) | (be_ref[j] != be_ref[jnp.maximum(j - 1, 0)]))
        def _():
            wgu_scr[...] = wgu_ref[0, 0].astype(BF16)
            wd_scr[...] = wd_ref[0, 0].astype(BF16)

        x = _load_rows_packed(x_ref, tm)
        gu = _dot(x, wgu_scr[...]) + bgu_ref[0, 0]
        gate = jnp.minimum(gu[:, :f], SWIGLU_LIMIT)
        up = jnp.clip(gu[:, f:], -SWIGLU_LIMIT, SWIGLU_LIMIT)
        act = (up + 1.0) * (gate * jax.nn.sigmoid(SWIGLU_ALPHA * gate))
        _store_rows_packed(o_ref, _dot(act.astype(BF16), wd_scr[...]) + bd_ref[0, 0])

    @pl.when(j >= nu_ref[0])
    def _():
        o_ref[...] = jnp.zeros_like(o_ref)


def _expert_ffn(layer, block_e, n_used, xs, wgu, bgu, wd, bd, tm):
    n_rows = xs.shape[0] // PACKED_ROWS
    depth, ne, d, f2 = wgu.shape
    f = f2 // 2
    rows = pl.BlockSpec((tm * PACKED_ROWS, LANES), lambda j, be, nu: (j, 0))
    grid_spec = pltpu.PrefetchScalarGridSpec(
        num_scalar_prefetch=2,
        grid=(n_rows // tm,),
        in_specs=[rows,
                  pl.BlockSpec((1, 1, d, f2), lambda j, be, nu: (layer, be[j], 0, 0)),
                  pl.BlockSpec((1, 1, 1, f2), lambda j, be, nu: (layer, be[j], 0, 0)),
                  pl.BlockSpec((1, 1, f, d), lambda j, be, nu: (layer, be[j], 0, 0)),
                  pl.BlockSpec((1, 1, 1, d), lambda j, be, nu: (layer, be[j], 0, 0))],
        out_specs=rows,
        scratch_shapes=[pltpu.VMEM((d, f2), BF16), pltpu.VMEM((f, d), BF16)],
    )
    return pl.pallas_call(
        _ffn_kernel,
        grid_spec=grid_spec,
        out_shape=jax.ShapeDtypeStruct((n_rows * PACKED_ROWS, LANES), jnp.uint32),
        compiler_params=_cparams("arbitrary"),
        name="expert_ffn",
    )(block_e, n_used, xs, wgu, bgu.reshape(depth, ne, 1, f2), wd, bd.reshape(depth, ne, 1, d))


def _sc_worker_base(per_worker):
    return (lax.axis_index("s") * SC_CORES + lax.axis_index("c")) * per_worker


def _sc_dispatch(rows, dest, n_out):
    t = rows.shape[0]
    kk = dest.shape[0] // t
    w = SC_WINDOW
    per_worker = t // (SC_CORES * SC_SUBCORES)
    mesh = plsc.VectorSubcoreMesh(core_axis_name="c", subcore_axis_name="s")

    @functools.partial(
        pl.kernel, mesh=mesh,
        out_type=jax.ShapeDtypeStruct((n_out,) + rows.shape[1:], rows.dtype),
        scratch_types=[pltpu.VMEM((w,), jnp.int32)] * kk + [pltpu.VMEM((w,) + rows.shape[1:], rows.dtype)])
    def scatter_rows(r_hbm, d_hbm, o_hbm, *scratch):
        idx_v, rows_v = scratch[:kk], scratch[kk]
        base = _sc_worker_base(per_worker)

        @pl.loop(0, per_worker // w)
        def _(i):
            off = base + i * w
            pltpu.sync_copy(r_hbm.at[pl.ds(off, w)], rows_v)
            for s in range(kk):
                pltpu.sync_copy(d_hbm.at[pl.ds(s * t + off, w)], idx_v[s])
            for s in range(kk):
                pltpu.sync_copy(rows_v, o_hbm.at[idx_v[s]])

    return scatter_rows(rows, dest)


def _sc_gather(table, idx):
    n = idx.shape[0]
    w = SC_WINDOW
    per_worker = n // (SC_CORES * SC_SUBCORES)
    mesh = plsc.VectorSubcoreMesh(core_axis_name="c", subcore_axis_name="s")

    @functools.partial(
        pl.kernel, mesh=mesh,
        out_type=jax.ShapeDtypeStruct((n,) + table.shape[1:], table.dtype),
        scratch_types=[pltpu.VMEM((w,), jnp.int32), pltpu.VMEM((w,) + table.shape[1:], table.dtype)])
    def gather_rows(t_hbm, i_hbm, o_hbm, idx_v, rows_v):
        base = _sc_worker_base(per_worker)

        @pl.loop(0, per_worker // w)
        def _(i):
            off = base + i * w
            pltpu.sync_copy(i_hbm.at[pl.ds(off, w)], idx_v)
            pltpu.sync_copy(t_hbm.at[idx_v], rows_v)
            pltpu.sync_copy(rows_v, o_hbm.at[pl.ds(off, w)])

    return gather_rows(table, idx)


def _combine_kernel(alpha, g_ref, w_ref, x_ref, g2_ref, lng_ref, lnb_ref, o_ref):
    tm = x_ref.shape[1]
    w = w_ref[0]
    f = _load_rows_packed(g_ref.at[0], tm).astype(F32) * w[:, 0:1]
    for kk in range(1, TOP_K):
        f = f + _load_rows_packed(g_ref.at[kk], tm).astype(F32) * w[:, kk:kk + 1]
    o_ref[0] = _layer_norm(alpha * x_ref[0] + g2_ref[0] * f, lng_ref[...], lnb_ref[...])


def _combine(alpha, gathered, tok_off, wts, x1, g2, lng, lnb, tm):
    b, l, d = x1.shape
    nj = l // tm
    blk_off = tok_off // tm
    tok = pl.BlockSpec((1, tm, d), lambda i, j: (i, j, 0))
    full = pl.BlockSpec((1, d), lambda i, j: (0, 0))
    return pl.pallas_call(
        functools.partial(_combine_kernel, alpha),
        grid=(b, nj),
        in_specs=[pl.BlockSpec((TOP_K, tm * PACKED_ROWS, LANES), lambda i, j: (0, blk_off + i * nj + j, 0)),
                  pl.BlockSpec((1, tm, ROUTER_PAD), lambda i, j: (i, j, 0)),
                  tok, pl.BlockSpec((1, 1, d), lambda i, j: (i, 0, 0)), full, full],
        out_specs=tok,
        out_shape=jax.ShapeDtypeStruct((b, l, d), F32),
        compiler_params=_cparams("arbitrary", "arbitrary"),
        name="combine_ln",
    )(gathered, wts, x1, g2, lng, lnb)


def _rope_tables(l):
    rows = l // GRID_W
    axis_dim = HEAD_DIM // 2
    inv_freq = ROPE_BASE ** (-jnp.arange(0, axis_dim, 2, dtype=F32) / axis_dim)
    row = jnp.repeat(jnp.arange(rows, dtype=F32), GRID_W)
    col = jnp.tile(jnp.arange(GRID_W, dtype=F32), rows)
    ang = jnp.stack([row[:, None] * inv_freq, col[:, None] * inv_freq], axis=1)
    cos, sin = jnp.cos(ang), jnp.sin(ang)
    cos64 = jnp.broadcast_to(cos[:, :, None, :], (l, 2, 2, HEAD_DIM // 4)).reshape(l, HEAD_DIM)
    sin64 = jnp.stack([-sin, sin], axis=2).reshape(l, HEAD_DIM)
    return jnp.tile(cos64, (1, LANES // HEAD_DIM)), jnp.tile(sin64, (1, LANES // HEAD_DIM))


def _route_tables(ri, counts, tm):
    t = ri.shape[0]
    top_e = ri[:, :TOP_K]
    rank = ri[:, TOP_K:2 * TOP_K]
    padded = (counts + tm - 1) // tm * tm
    pad_end = jnp.cumsum(padded)
    pad_start = pad_end - padded
    experts = jnp.arange(N_EXPERTS, dtype=jnp.int32)
    start = jnp.sum(jnp.where(top_e[:, :, None] == experts, pad_start, 0), axis=-1)
    dest = (start + rank).T.reshape(TOP_K * t)
    n_tiles = (t * TOP_K + N_EXPERTS * (tm - 1) + tm - 1) // tm
    tile_start = jnp.arange(n_tiles, dtype=jnp.int32) * tm
    block_e = jnp.minimum(jnp.sum((pad_end[None, :] <= tile_start[:, None]).astype(jnp.int32), axis=1), N_EXPERTS - 1)
    n_used = (pad_end[-1] // tm).astype(jnp.int32).reshape(1)
    return dest, block_e.astype(jnp.int32), n_used, n_tiles


def kernel(x, c, ctx, c_ctx, w_mod, b_mod, w_in, conv_w, ret_decay_exp, ret_gn_g, q_norm_g, k_norm_g, w_out,
           ln_g, ln_b, w_router, b_router, w_gate_up, b_gate_up, w_down, b_down):
    depth = w_mod.shape[0]
    alpha = (2.0 * depth) ** 0.25
    b, l, d = x.shape
    lc = ctx.shape[1]
    tm_lat, tm_ctx, tm_moe, tq = 512, 256, 512, 256

    n_rows = (b + 1 + 7) // 8 * 8
    c_all = jnp.zeros((n_rows, d), F32).at[:b].set(c).at[b].set(c_ctx)
    mod = _modulation(c_all, w_mod, b_mod)

    cos_l, sin_l = _rope_tables(l)
    cos_c, sin_c = jnp.ones((lc, LANES), F32), jnp.zeros((lc, LANES), F32)
    lane_head = jnp.arange(LANES) // HEAD_DIM
    ones_bd = (lane_head[:, None] == lane_head[None, :]).astype(BF16)
    zero_state = jnp.zeros((b, RET_HEADS, HEAD_DIM, HEAD_DIM), F32)

    for layer in range(depth):
        last = layer == depth - 1
        m_lat = mod[layer, :b].reshape(b, 1, N_MOD, d)
        m_ctx = jnp.broadcast_to(mod[layer, b].reshape(1, 1, N_MOD, d), (b, 1, N_MOD, d))
        sh1, sc1, g1, sh2, sc2, g2 = (m_lat[:, :, i] for i in range(N_MOD))
        sh1c, sc1c, g1c, sh2c, sc2c, g2c = (m_ctx[:, :, i] for i in range(N_MOD))

        w_in_bf = w_in[layer].astype(BF16)
        w_out_bf = w_out[layer].astype(BF16)
        gq = jnp.tile(q_norm_g[layer], LANES // HEAD_DIM).reshape(1, LANES)
        gk = jnp.tile(k_norm_g[layer], LANES // HEAD_DIM).reshape(1, LANES)
        log_gamma = jnp.log1p(-jnp.exp2(-ret_decay_exp[layer].astype(F32)))
        lg_rows = jnp.repeat(log_gamma, HEAD_DIM, axis=1)
        lgf_row, lgb_row = lg_rows[0:1], lg_rows[1:2]
        gn_row = ret_gn_g[layer].reshape(1, -1)
        cw = conv_w[layer].T.reshape(3, 1, -1)
        wr_pad = jnp.zeros((d, ROUTER_PAD), F32).at[:, :N_EXPERTS].set(w_router[layer])
        wrh = wr_pad.astype(BF16)
        wrl = (wr_pad - wrh.astype(F32)).astype(BF16)
        br = jnp.zeros((1, ROUTER_PAD), F32).at[0, :N_EXPERTS].set(b_router[layer])
        lng1, lnb1 = ln_g[layer, 0].reshape(1, d), ln_b[layer, 0].reshape(1, d)
        lng2, lnb2 = ln_g[layer, 1].reshape(1, d), ln_b[layer, 1].reshape(1, d)

        zc = _inproj(ctx, 1.0 + sc1c, sh1c, w_in_bf, cos_c, sin_c, gq, gk, ones_bd, tm_ctx)
        zl = _inproj(x, 1.0 + sc1, sh1, w_in_bf, cos_l, sin_l, gq, gk, ones_bd, tm_lat)
        cu_c, cb_c, rq_c, rk_c, rv_c, rg_c, aq_c, ak_c, av_c = zc
        cu_l, cb_l, rq_l, rk_l, rv_l, rg_l, aq_l, ak_l, av_l = zl

        ret_c, s_fwd, s_bwd = _retention(rq_c, rk_c, rv_c, rg_c, zero_state, zero_state,
                                         log_gamma, lgf_row, lgb_row, gn_row)
        ret_l, _, _ = _retention(rq_l, rk_l, rv_l, rg_l, s_fwd, s_bwd, log_gamma, lgf_row, lgb_row, gn_row)

        def kv_heads(a):
            return a.reshape(b, a.shape[1], -1, HEAD_DIM).transpose(0, 2, 1, 3)

        def v_heads(a):
            vh = kv_heads(a)
            pad = jnp.zeros(vh.shape[:3] + (LANES - HEAD_DIM - 1,), BF16)
            return jnp.concatenate([vh, jnp.ones(vh.shape[:3] + (1,), BF16), pad], axis=-1)

        k_all = kv_heads(jnp.concatenate([ak_c, ak_l], axis=1))
        v_all = v_heads(jnp.concatenate([av_c, av_l], axis=1))
        att_l = _attention(aq_l, k_all, v_all, tq)

        cnt0 = jnp.zeros((1, ROUTER_PAD), F32)
        if not last:
            att_c = _attention(aq_c, kv_heads(ak_c), v_heads(av_c), tq)
            ctx1, h2_c, ri_c, rw_c, cnt0 = _outproj(alpha, cu_c, cb_c, cw, ret_c, att_c, ctx, g1c, w_out_bf,
                                                    lng1, lnb1, 1.0 + sc2c, sh2c, wrh, wrl, br, cnt0, tm_ctx)
        x1, h2_l, ri_l, rw_l, cnt = _outproj(alpha, cu_l, cb_l, cw, ret_l, att_l, x, g1, w_out_bf, lng1, lnb1,
                                             1.0 + sc2, sh2, wrh, wrl, br, cnt0, tm_lat)
        if not last:
            n_c = b * lc
            h2 = jnp.concatenate([h2_c, h2_l], axis=0)
            ri = jnp.concatenate([ri_c.reshape(n_c, -1), ri_l.reshape(b * l, -1)], axis=0)
        else:
            n_c = 0
            h2 = h2_l
            ri = ri_l.reshape(b * l, -1)
        n_tok = n_c + b * l

        counts = cnt[0, :N_EXPERTS].astype(jnp.int32)
        dest, block_e, n_used, n_tiles = _route_tables(ri, counts, tm_moe)
        xs = _sc_dispatch(h2.reshape(n_tok, PACKED_ROWS, LANES), dest, n_tiles * tm_moe)
        ys = _expert_ffn(layer, block_e, n_used, xs.reshape(-1, LANES), w_gate_up, b_gate_up, w_down, b_down, tm_moe)
        gathered = _sc_gather(ys.reshape(-1, PACKED_ROWS, LANES), dest).reshape(TOP_K, n_tok * PACKED_ROWS, LANES)
        if not last:
            ctx = _combine(alpha, gathered, 0, rw_c, ctx1, g2c, lng2, lnb2, tm_ctx)
        x = _combine(alpha, gathered, n_c, rw_l, x1, g2, lng2, lnb2, tm_lat)
    return x
```

```python
import functools

import jax
import jax.numpy as jnp
from jax import lax
from jax.experimental import pallas as pl
from jax.experimental.pallas import tpu as pltpu
from jax.experimental.pallas import tpu_sc as plsc

F32 = jnp.float32
BF16 = jnp.bfloat16

HEAD_DIM = 64
GRID_W = 64
ROPE_BASE = 10000.0
N_EXPERTS = 32
TOP_K = 4
SWIGLU_ALPHA = 1.702
SWIGLU_LIMIT = 7.0
N_MOD = 6
EPS = 1e-6
RET_HEADS = 4
RET_CHUNK = 256
ATT_Q_PER_KV = 4
ATT_KV_CHUNK = 256
LANES = 128
ROUTER_PAD = 128
VMEM_LIMIT = 56 * 1024 * 1024
SUBLANES = 8
PACKED_ROWS = 4
FFN_BLOCK = 256
SC_CORES = 2
SC_SUBCORES = 16
SC_WINDOW = 64


def _dot(a, b):
    return jnp.dot(a, b, preferred_element_type=F32)


def _dot_nt(a, b):
    return lax.dot_general(a, b, (((1,), (1,)), ((), ())), preferred_element_type=F32)


def _split_bf16(x):
    hi = x.astype(BF16)
    lo = (x - hi.astype(F32)).astype(BF16)
    return hi, lo


def _cparams(*sem):
    return pltpu.CompilerParams(dimension_semantics=sem, vmem_limit_bytes=VMEM_LIMIT)


def _mod_kernel(c_ref, w_ref, b_ref, o_ref):
    c = c_ref[...]
    a = c * jax.nn.sigmoid(c)
    a_hi, a_lo = _split_bf16(a)
    w_hi, w_lo = _split_bf16(w_ref[0])
    o_ref[0] = _dot(a_hi, w_hi) + _dot(a_lo, w_hi) + _dot(a_hi, w_lo) + b_ref[0]


def _modulation(c_all, w_mod, b_mod):
    depth, d, n = w_mod.shape
    r = c_all.shape[0]
    tn = 1536
    return pl.pallas_call(
        _mod_kernel,
        grid=(depth, n // tn),
        in_specs=[
            pl.BlockSpec((r, d), lambda l, j: (0, 0)),
            pl.BlockSpec((1, d, tn), lambda l, j: (l, 0, j)),
            pl.BlockSpec((1, 1, tn), lambda l, j: (l, 0, j)),
        ],
        out_specs=pl.BlockSpec((1, r, tn), lambda l, j: (l, 0, j)),
        out_shape=jax.ShapeDtypeStruct((depth, r, n), F32),
        compiler_params=_cparams("arbitrary", "arbitrary"),
        name="modulation",
    )(c_all, w_mod, b_mod.reshape(depth, 1, n))


def _inproj_kernel(x_ref, sc_ref, sh_ref, w_ref, cos_ref, sin_ref, gq_ref, gk_ref, ones_ref,
                   cu_ref, cb_ref, rq_ref, rk_ref, rv_ref, rg_ref, aq_ref, ak_ref, av_ref):
    h = (x_ref[0] * sc_ref[0] + sh_ref[0]).astype(BF16)
    z = _dot(h, w_ref[...])
    cos = cos_ref[...]
    sin = sin_ref[...]
    ones = ones_ref[...]
    lane = lax.broadcasted_iota(jnp.int32, cos.shape, 1)
    first_half = (lane & 31) < 16

    def rope(xs):
        nxt = pltpu.roll(xs, LANES - 16, axis=1)
        prv = pltpu.roll(xs, 16, axis=1)
        return xs * cos + jnp.where(first_half, nxt, prv) * sin

    def rms(xs, g):
        s_hi, s_lo = _split_bf16(xs * xs)
        ssq = _dot(s_hi, ones) + _dot(s_lo, ones)
        return xs * lax.rsqrt(ssq * (1.0 / HEAD_DIM) + EPS) * g

    cu_ref[0] = z[:, 512:768] * z[:, 0:256]
    cb_ref[0] = z[:, 256:512]
    for j in range(2):
        lo, hi = j * LANES, (j + 1) * LANES
        rq_ref[0, :, lo:hi] = rope(z[:, 768 + lo:768 + hi])
        rk_ref[0, :, lo:hi] = rope(z[:, 1024 + lo:1024 + hi] * (HEAD_DIM ** -0.5))
    rv_ref[0] = z[:, 1280:1536]
    rg_ref[0] = z[:, 1536:1792]
    gq = gq_ref[...]
    for j in range(4):
        lo, hi = j * LANES, (j + 1) * LANES
        q = rope(rms(z[:, 1792 + lo:1792 + hi], gq)) * (HEAD_DIM ** -0.5)
        aq_ref[0, :, lo:hi] = q.astype(BF16)
    ak_ref[0] = rope(rms(z[:, 2304:2432], gk_ref[...])).astype(BF16)
    av_ref[0] = z[:, 2432:2560].astype(BF16)


def _inproj(x, sc, sh, w_bf, cos, sin, gq, gk, ones_bd, tm):
    b, l, d = x.shape
    n = w_bf.shape[1]
    tok = lambda w: pl.BlockSpec((1, tm, w), lambda i, j: (i, j, 0))
    vec = pl.BlockSpec((1, 1, d), lambda i, j: (i, 0, 0))
    full = lambda s: pl.BlockSpec(s, lambda i, j: (0,) * len(s))
    widths = (256, 256, 256, 256, 256, 256, 512, 128, 128)
    dtypes = (F32,) * 6 + (BF16,) * 3
    return pl.pallas_call(
        _inproj_kernel,
        grid=(b, l // tm),
        in_specs=[tok(d), vec, vec, full((d, n)),
                  pl.BlockSpec((tm, LANES), lambda i, j: (j, 0)),
                  pl.BlockSpec((tm, LANES), lambda i, j: (j, 0)),
                  full((1, LANES)), full((1, LANES)), full((LANES, LANES))],
        out_specs=[tok(w) for w in widths],
        out_shape=[jax.ShapeDtypeStruct((b, l, w), dt) for w, dt in zip(widths, dtypes)],
        compiler_params=_cparams("arbitrary", "arbitrary"),
        name="inproj",
    )(x, sc, sh, w_bf, cos, sin, gq, gk, ones_bd)


def _attn_kernel(q_ref, k_ref, v_ref, o_ref):
    tq = q_ref.shape[1]
    q4 = q_ref[0]
    q = jnp.concatenate([q4[:, h * HEAD_DIM:(h + 1) * HEAD_DIM] for h in range(ATT_Q_PER_KV)], axis=0)
    lk = k_ref.shape[2]
    m = jnp.full((q.shape[0], 1), -jnp.inf, F32)
    acc = jnp.zeros((q.shape[0], LANES), F32)
    for lo in range(0, lk, ATT_KV_CHUNK):
        hi = min(lo + ATT_KV_CHUNK, lk)
        s = _dot_nt(q, k_ref[0, 0, lo:hi, :])
        m_new = jnp.maximum(m, jnp.max(s, axis=-1, keepdims=True))
        p = jnp.exp(s - m_new)
        acc = acc * jnp.exp(m - m_new) + _dot(p.astype(BF16), v_ref[0, 0, lo:hi, :])
        m = m_new
    o = acc[:, :HEAD_DIM] / acc[:, HEAD_DIM:HEAD_DIM + 1]
    o_ref[0] = jnp.concatenate([o[h * tq:(h + 1) * tq] for h in range(ATT_Q_PER_KV)], axis=1)


def _attention(q, k, v, tq):
    b, lq, wq = q.shape
    hkv, lk = k.shape[1], k.shape[2]
    wg = ATT_Q_PER_KV * HEAD_DIM
    return pl.pallas_call(
        _attn_kernel,
        grid=(b, hkv, lq // tq),
        in_specs=[pl.BlockSpec((1, tq, wg), lambda i, g, j: (i, j, g)),
                  pl.BlockSpec((1, 1, lk, HEAD_DIM), lambda i, g, j: (i, g, 0, 0)),
                  pl.BlockSpec((1, 1, lk, LANES), lambda i, g, j: (i, g, 0, 0))],
        out_specs=pl.BlockSpec((1, tq, wg), lambda i, g, j: (i, j, g)),
        out_shape=jax.ShapeDtypeStruct((b, lq, wq), F32),
        compiler_params=_cparams("arbitrary", "arbitrary", "arbitrary"),
        name="attention",
    )(q, k, v)


def _ret_kernel(lg_ref, q_ref, k_ref, v_ref, g_ref, s0f_ref, s0b_ref, lgf_ref, lgb_ref, gn_ref,
                o_ref, sff_ref, sfb_ref, s_scr, sb_scr, dec_scr):
    i, p, j = pl.program_id(0), pl.program_id(1), pl.program_id(2)
    nc = pl.num_programs(2)
    c = k_ref.shape[1]
    w = k_ref.shape[2]
    pos = lax.broadcasted_iota(jnp.int32, (c, w), 0).astype(F32)
    lgf = lgf_ref[...]
    lgb = lgb_ref[...]

    @pl.when((i == 0) & (p == 0) & (j == 0))
    def _():
        diff = (lax.broadcasted_iota(jnp.int32, (c, c), 0) - lax.broadcasted_iota(jnp.int32, (c, c), 1)).astype(F32)
        for h in range(RET_HEADS):
            dec_scr[h] = jnp.where(diff >= 0.0, jnp.exp(lg_ref[0, h] * jnp.maximum(diff, 0.0)),
                                   jnp.exp(lg_ref[1, h] * jnp.maximum(-diff, 0.0)))

    def update_state(kzt, vh, cdec):
        for h in range(RET_HEADS):
            lo, hi = h * HEAD_DIM, (h + 1) * HEAD_DIM
            s_scr[h] = cdec[:, lo:hi] * s_scr[h] + _dot(kzt[lo:hi, :], vh[:, lo:hi])

    @pl.when(p == 0)
    def _():
        @pl.when(j == 0)
        def _():
            s_scr[...] = s0b_ref[0]

        sb_scr[nc - 1 - j] = s_scr[...]
        kzt = (k_ref[0] * jnp.exp(lgb * pos)).T.astype(BF16)
        update_state(kzt, v_ref[0].astype(BF16), jnp.exp(lgb * float(c)))

        @pl.when(j == nc - 1)
        def _():
            sfb_ref[0] = s_scr[...]

    @pl.when(p == 1)
    def _():
        @pl.when(j == 0)
        def _():
            s_scr[...] = s0f_ref[0]

        q = q_ref[0]
        k = k_ref[0]
        qf = (q * jnp.exp(lgf * (pos + 1.0))).astype(BF16)
        qb = (q * jnp.exp(lgb * (float(c) - pos))).astype(BF16)
        kzt = (k * jnp.exp(lgf * (float(c) - 1.0 - pos))).T.astype(BF16)
        qh = q.astype(BF16)
        kh = k.astype(BF16)
        vh = v_ref[0].astype(BF16)
        outs = []
        for h in range(RET_HEADS):
            lo, hi = h * HEAD_DIM, (h + 1) * HEAD_DIM
            sc = _dot_nt(qh[:, lo:hi], kh[:, lo:hi])
            y = (_dot((sc * dec_scr[h]).astype(BF16), vh[:, lo:hi])
                 + _dot(qf[:, lo:hi], s_scr[h].astype(BF16))
                 + _dot(qb[:, lo:hi], sb_scr[j, h].astype(BF16)))
            mu = jnp.mean(y, axis=-1, keepdims=True)
            yc = y - mu
            var = jnp.mean(yc * yc, axis=-1, keepdims=True)
            outs.append(yc * lax.rsqrt(var + EPS))
        update_state(kzt, vh, jnp.exp(lgf * float(c)))
        gate = g_ref[0]
        o_ref[0] = jnp.concatenate(outs, axis=1) * gn_ref[...] * (gate * jax.nn.sigmoid(gate))

        @pl.when(j == nc - 1)
        def _():
            sff_ref[0] = s_scr[...]


def _retention(q, k, v, gate, s0_f, s0_b, lg, lgf_row, lgb_row, gn_row):
    b, l, w = q.shape
    c = min(RET_CHUNK, l)
    nc = l // c
    st = (RET_HEADS, HEAD_DIM, HEAD_DIM)
    st_spec = pl.BlockSpec((1,) + st, lambda i, p, j: (i, 0, 0, 0))
    row = pl.BlockSpec((1, w), lambda i, p, j: (0, 0))
    both = pl.BlockSpec((1, c, w), lambda i, p, j: (i, jnp.where(p == 0, nc - 1 - j, j), 0))
    fwd_only = pl.BlockSpec((1, c, w), lambda i, p, j: (i, p * j, 0))
    return pl.pallas_call(
        _ret_kernel,
        grid=(b, 2, nc),
        in_specs=[pl.BlockSpec(memory_space=pltpu.SMEM), fwd_only, both, both, fwd_only,
                  st_spec, st_spec, row, row, row],
        out_specs=[fwd_only, st_spec, st_spec],
        out_shape=[jax.ShapeDtypeStruct((b, l, w), F32), jax.ShapeDtypeStruct((b,) + st, F32),
                   jax.ShapeDtypeStruct((b,) + st, F32)],
        scratch_shapes=[pltpu.VMEM(st, F32), pltpu.VMEM((nc,) + st, F32), pltpu.VMEM((RET_HEADS, c, c), F32)],
        compiler_params=_cparams("arbitrary", "arbitrary", "arbitrary"),
        name="retention",
    )(lg, q, k, v, gate, s0_f, s0_b, lgf_row, lgb_row, gn_row)


def _layer_norm(x, g, b):
    mu = jnp.mean(x, axis=-1, keepdims=True)
    xc = x - mu
    var = jnp.mean(xc * xc, axis=-1, keepdims=True)
    return xc * lax.rsqrt(var + EPS) * g + b


def _store_rows_tiled(ref, x):
    tm = x.shape[0]
    for s in range(SUBLANES):
        ref[pl.ds(s, tm, stride=SUBLANES), :] = x[:, s * LANES:(s + 1) * LANES]


def _load_rows_tiled(ref, tm):
    return jnp.concatenate([ref[pl.ds(s, tm, stride=SUBLANES), :] for s in range(SUBLANES)], axis=1)


def _store_rows_packed(ref, x):
    tm, d = x.shape
    bits = pltpu.bitcast(x.astype(BF16).astype(F32), jnp.uint32)
    word = (bits[:, :d // 2] >> 16) | bits[:, d // 2:]
    for s in range(PACKED_ROWS):
        ref[pl.ds(s, tm, stride=PACKED_ROWS), :] = word[:, s * LANES:(s + 1) * LANES]


def _load_rows_packed(ref, tm):
    word = jnp.concatenate([ref[pl.ds(s, tm, stride=PACKED_ROWS), :] for s in range(PACKED_ROWS)], axis=1)
    lo = pltpu.bitcast(word << 16, F32).astype(BF16)
    hi = pltpu.bitcast(word & jnp.uint32(0xFFFF0000), F32).astype(BF16)
    return jnp.concatenate([lo, hi], axis=1)


def _outproj_kernel(alpha, cu_ref, cup_ref, cun_ref, cb_ref, cw_ref, ret_ref, att_ref, x_ref, g1_ref,
                    w_ref, lng_ref, lnb_ref, sc_ref, sh_ref, wrh_ref, wrl_ref, br_ref, cnt0_ref,
                    x1_ref, h2_ref, ri_ref, rw_ref, cnt_ref, run_scr):
    i = pl.program_id(0)
    j = pl.program_id(1)

    @pl.when((i == 0) & (j == 0))
    def _():
        run_scr[...] = cnt0_ref[...]

    t = cu_ref[0]
    tm = t.shape[0]
    prev_row = jnp.where(j == 0, 0.0, cup_ref[0][7:8, :])
    next_row = jnp.where(j == pl.num_programs(1) - 1, 0.0, cun_ref[0][0:1, :])
    row = lax.broadcasted_iota(jnp.int32, t.shape, 0)
    t_prev = jnp.where(row == 0, prev_row, pltpu.roll(t, 1, axis=0))
    t_next = jnp.where(row == tm - 1, next_row, pltpu.roll(t, tm - 1, axis=0))
    conv = cb_ref[0] * (t_prev * cw_ref[0] + t * cw_ref[1] + t_next * cw_ref[2])
    y = (_dot(conv.astype(BF16), w_ref[0:256, :])
         + _dot(ret_ref[0].astype(BF16), w_ref[256:512, :])
         + _dot(att_ref[0].astype(BF16), w_ref[512:1024, :]))
    x1 = _layer_norm(alpha * x_ref[0] + g1_ref[0] * y, lng_ref[...], lnb_ref[...])
    x1_ref[0] = x1
    h2 = x1 * sc_ref[0] + sh_ref[0]
    _store_rows_packed(h2_ref, h2)
    h_hi, h_lo = _split_bf16(h2)
    wrh = wrh_ref[...]
    logits = _dot(h_hi, wrh) + _dot(h_lo, wrh) + _dot(h_hi, wrl_ref[...]) + br_ref[...]

    lane = lax.broadcasted_iota(jnp.int32, logits.shape, 1)
    lane_f = lane.astype(F32)
    work = jnp.where(lane < N_EXPERTS, logits, -jnp.inf)
    vals, firsts, sels = [], [], []
    for _ in range(TOP_K):
        m = jnp.max(work, axis=-1, keepdims=True)
        first = jnp.min(jnp.where(work == m, lane_f, float(ROUTER_PAD)), axis=-1, keepdims=True)
        sel = lane_f == first
        vals.append(m)
        firsts.append(first)
        sels.append(sel)
        work = jnp.where(sel, -jnp.inf, work)
    exps = [jnp.exp(v - vals[0]) for v in vals]
    denom = exps[0]
    for e in exps[1:]:
        denom = denom + e

    cnt = jnp.zeros(logits.shape, F32)
    for sel in sels:
        cnt = cnt + jnp.where(sel, 1.0, 0.0)
    r_i = lax.broadcasted_iota(jnp.int32, (tm, tm), 0)
    c_i = lax.broadcasted_iota(jnp.int32, (tm, tm), 1)
    before = jnp.where(c_i < r_i, 1.0, 0.0).astype(BF16)
    base = _dot(before, cnt.astype(BF16)) + run_scr[...]
    run_scr[...] = run_scr[...] + jnp.sum(cnt, axis=0, keepdims=True)
    cnt_ref[...] = run_scr[...]

    ri = jnp.zeros(logits.shape, F32)
    rw = jnp.zeros(logits.shape, F32)
    for kk in range(TOP_K):
        rank = jnp.sum(jnp.where(sels[kk], base, 0.0), axis=-1, keepdims=True)
        ri = jnp.where(lane == kk, firsts[kk], ri)
        ri = jnp.where(lane == TOP_K + kk, rank, ri)
        rw = jnp.where(lane == kk, exps[kk] / denom, rw)
    ri_ref[0] = ri.astype(jnp.int32)
    rw_ref[0] = rw


def _outproj(alpha, cu, cb, cw, ret, att, x, g1, w_bf, lng, lnb, sc2, sh2, wrh, wrl, br, cnt0, tm):
    b, l, d = x.shape
    nj = l // tm
    tok = lambda w: pl.BlockSpec((1, tm, w), lambda i, j: (i, j, 0))
    vec = pl.BlockSpec((1, 1, d), lambda i, j: (i, 0, 0))
    full = lambda s: pl.BlockSpec(s, lambda i, j: (0,) * len(s))
    r8 = tm // 8
    nb8 = l // 8
    return pl.pallas_call(
        functools.partial(_outproj_kernel, alpha),
        grid=(b, nj),
        in_specs=[tok(256),
                  pl.BlockSpec((1, 8, 256), lambda i, j: (i, jnp.maximum(j * r8 - 1, 0), 0)),
                  pl.BlockSpec((1, 8, 256), lambda i, j: (i, jnp.minimum((j + 1) * r8, nb8 - 1), 0)),
                  tok(256), full((3, 1, 256)), tok(256), tok(512), tok(d), vec,
                  full((d, d)), full((1, d)), full((1, d)), vec, vec,
                  full((d, ROUTER_PAD)), full((d, ROUTER_PAD)), full((1, ROUTER_PAD)), full((1, ROUTER_PAD))],
        out_specs=[tok(d), pl.BlockSpec((tm * PACKED_ROWS, LANES), lambda i, j: (i * nj + j, 0)),
                   tok(ROUTER_PAD), tok(ROUTER_PAD), full((1, ROUTER_PAD))],
        out_shape=[jax.ShapeDtypeStruct((b, l, d), F32),
                   jax.ShapeDtypeStruct((b * l * PACKED_ROWS, LANES), jnp.uint32),
                   jax.ShapeDtypeStruct((b, l, ROUTER_PAD), jnp.int32),
                   jax.ShapeDtypeStruct((b, l, ROUTER_PAD), F32),
                   jax.ShapeDtypeStruct((1, ROUTER_PAD), F32)],
        scratch_shapes=[pltpu.VMEM((1, ROUTER_PAD), F32)],
        compiler_params=_cparams("arbitrary", "arbitrary"),
        name="outproj_ln_router",
    )(cu, cu, cu, cb, cw, ret, att, x, g1, w_bf, lng, lnb, sc2, sh2, wrh, wrl, br, cnt0)


def _ffn_kernel(be_ref, nu_ref, x_ref, wgu_ref, bgu_ref, wd_ref, bd_ref, o_ref, wgu_scr, wd_scr):
    j = pl.program_id(0)
    f = wd_ref.shape[2]
    tm = x_ref.shape[0] // PACKED_ROWS

    @pl.when(j < nu_ref[0])
    def _():
        @pl.when((j == 0) | (be_ref[j] != be_ref[jnp.maximum(j - 1, 0)]))
        def _():
            wgu_scr[...] = wgu_ref[0, 0].astype(BF16)
            wd_scr[...] = wd_ref[0, 0].astype(BF16)

        x = _load_rows_packed(x_ref, tm)
        bgu = bgu_ref[0, 0]
        acts = []
        for lo in range(0, f, FFN_BLOCK):
            hi = lo + FFN_BLOCK
            gate = jnp.minimum(_dot(x, wgu_scr[:, lo:hi]) + bgu[:, lo:hi], SWIGLU_LIMIT)
            up = jnp.clip(_dot(x, wgu_scr[:, f + lo:f + hi]) + bgu[:, f + lo:f + hi], -SWIGLU_LIMIT, SWIGLU_LIMIT)
            acts.append(((up + 1.0) * (gate * jax.nn.sigmoid(SWIGLU_ALPHA * gate))).astype(BF16))
        act = jnp.concatenate(acts, axis=1)
        _store_rows_packed(o_ref, _dot(act, wd_scr[...]) + bd_ref[0, 0])

    @pl.when(j >= nu_ref[0])
    def _():
        o_ref[...] = jnp.zeros_like(o_ref)


def _expert_ffn(layer, block_e, n_used, xs, wgu, bgu, wd, bd, tm):
    n_rows = xs.shape[0] // PACKED_ROWS
    depth, ne, d, f2 = wgu.shape
    f = f2 // 2
    rows = pl.BlockSpec((tm * PACKED_ROWS, LANES), lambda j, be, nu: (j, 0))
    grid_spec = pltpu.PrefetchScalarGridSpec(
        num_scalar_prefetch=2,
        grid=(n_rows // tm,),
        in_specs=[rows,
                  pl.BlockSpec((1, 1, d, f2), lambda j, be, nu: (layer, be[j], 0, 0)),
                  pl.BlockSpec((1, 1, 1, f2), lambda j, be, nu: (layer, be[j], 0, 0)),
                  pl.BlockSpec((1, 1, f, d), lambda j, be, nu: (layer, be[j], 0, 0)),
                  pl.BlockSpec((1, 1, 1, d), lambda j, be, nu: (layer, be[j], 0, 0))],
        out_specs=rows,
        scratch_shapes=[pltpu.VMEM((d, f2), BF16), pltpu.VMEM((f, d), BF16)],
    )
    return pl.pallas_call(
        _ffn_kernel,
        grid_spec=grid_spec,
        out_shape=jax.ShapeDtypeStruct((n_rows * PACKED_ROWS, LANES), jnp.uint32),
        compiler_params=_cparams("arbitrary"),
        name="expert_ffn",
    )(block_e, n_used, xs, wgu, bgu.reshape(depth, ne, 1, f2), wd, bd.reshape(depth, ne, 1, d))


def _sc_worker_base(per_worker):
    return (lax.axis_index("s") * SC_CORES + lax.axis_index("c")) * per_worker


def _sc_dispatch(rows, dest, n_out):
    t = rows.shape[0]
    kk = dest.shape[0] // t
    w = SC_WINDOW
    per_worker = t // (SC_CORES * SC_SUBCORES)
    mesh = plsc.VectorSubcoreMesh(core_axis_name="c", subcore_axis_name="s")

    @functools.partial(
        pl.kernel, mesh=mesh,
        out_type=jax.ShapeDtypeStruct((n_out,) + rows.shape[1:], rows.dtype),
        scratch_types=[pltpu.VMEM((w,), jnp.int32)] * kk + [pltpu.VMEM((w,) + rows.shape[1:], rows.dtype)])
    def scatter_rows(r_hbm, d_hbm, o_hbm, *scratch):
        idx_v, rows_v = scratch[:kk], scratch[kk]
        base = _sc_worker_base(per_worker)

        @pl.loop(0, per_worker // w)
        def _(i):
            off = base + i * w
            pltpu.sync_copy(r_hbm.at[pl.ds(off, w)], rows_v)
            for s in range(kk):
                pltpu.sync_copy(d_hbm.at[pl.ds(s * t + off, w)], idx_v[s])
            for s in range(kk):
                pltpu.sync_copy(rows_v, o_hbm.at[idx_v[s]])

    return scatter_rows(rows, dest)


def _sc_gather(table, idx):
    n = idx.shape[0]
    w = SC_WINDOW
    per_worker = n // (SC_CORES * SC_SUBCORES)
    mesh = plsc.VectorSubcoreMesh(core_axis_name="c", subcore_axis_name="s")

    @functools.partial(
        pl.kernel, mesh=mesh,
        out_type=jax.ShapeDtypeStruct((n,) + table.shape[1:], table.dtype),
        scratch_types=[pltpu.VMEM((w,), jnp.int32), pltpu.VMEM((w,) + table.shape[1:], table.dtype)])
    def gather_rows(t_hbm, i_hbm, o_hbm, idx_v, rows_v):
        base = _sc_worker_base(per_worker)

        @pl.loop(0, per_worker // w)
        def _(i):
            off = base + i * w
            pltpu.sync_copy(i_hbm.at[pl.ds(off, w)], idx_v)
            pltpu.sync_copy(t_hbm.at[idx_v], rows_v)
            pltpu.sync_copy(rows_v, o_hbm.at[pl.ds(off, w)])

    return gather_rows(table, idx)


def _combine_kernel(alpha, g_ref, w_ref, x_ref, g2_ref, lng_ref, lnb_ref, o_ref):
    tm = x_ref.shape[1]
    w = w_ref[0]
    f = _load_rows_packed(g_ref.at[0], tm).astype(F32) * w[:, 0:1]
    for kk in range(1, TOP_K):
        f = f + _load_rows_packed(g_ref.at[kk], tm).astype(F32) * w[:, kk:kk + 1]
    o_ref[0] = _layer_norm(alpha * x_ref[0] + g2_ref[0] * f, lng_ref[...], lnb_ref[...])


def _combine(alpha, gathered, tok_off, wts, x1, g2, lng, lnb, tm):
    b, l, d = x1.shape
    nj = l // tm
    blk_off = tok_off // tm
    tok = pl.BlockSpec((1, tm, d), lambda i, j: (i, j, 0))
    full = pl.BlockSpec((1, d), lambda i, j: (0, 0))
    return pl.pallas_call(
        functools.partial(_combine_kernel, alpha),
        grid=(b, nj),
        in_specs=[pl.BlockSpec((TOP_K, tm * PACKED_ROWS, LANES), lambda i, j: (0, blk_off + i * nj + j, 0)),
                  pl.BlockSpec((1, tm, ROUTER_PAD), lambda i, j: (i, j, 0)),
                  tok, pl.BlockSpec((1, 1, d), lambda i, j: (i, 0, 0)), full, full],
        out_specs=tok,
        out_shape=jax.ShapeDtypeStruct((b, l, d), F32),
        compiler_params=_cparams("arbitrary", "arbitrary"),
        name="combine_ln",
    )(gathered, wts, x1, g2, lng, lnb)


def _rope_tables(l):
    rows = l // GRID_W
    axis_dim = HEAD_DIM // 2
    inv_freq = ROPE_BASE ** (-jnp.arange(0, axis_dim, 2, dtype=F32) / axis_dim)
    row = jnp.repeat(jnp.arange(rows, dtype=F32), GRID_W)
    col = jnp.tile(jnp.arange(GRID_W, dtype=F32), rows)
    ang = jnp.stack([row[:, None] * inv_freq, col[:, None] * inv_freq], axis=1)
    cos, sin = jnp.cos(ang), jnp.sin(ang)
    cos64 = jnp.broadcast_to(cos[:, :, None, :], (l, 2, 2, HEAD_DIM // 4)).reshape(l, HEAD_DIM)
    sin64 = jnp.stack([-sin, sin], axis=2).reshape(l, HEAD_DIM)
    return jnp.tile(cos64, (1, LANES // HEAD_DIM)), jnp.tile(sin64, (1, LANES // HEAD_DIM))


def _route_tables(ri, counts, tm):
    t = ri.shape[0]
    top_e = ri[:, :TOP_K]
    rank = ri[:, TOP_K:2 * TOP_K]
    padded = (counts + tm - 1) // tm * tm
    pad_end = jnp.cumsum(padded)
    pad_start = pad_end - padded
    experts = jnp.arange(N_EXPERTS, dtype=jnp.int32)
    start = jnp.sum(jnp.where(top_e[:, :, None] == experts, pad_start, 0), axis=-1)
    dest = (start + rank).T.reshape(TOP_K * t)
    n_tiles = (t * TOP_K + N_EXPERTS * (tm - 1) + tm - 1) // tm
    tile_start = jnp.arange(n_tiles, dtype=jnp.int32) * tm
    block_e = jnp.minimum(jnp.sum((pad_end[None, :] <= tile_start[:, None]).astype(jnp.int32), axis=1), N_EXPERTS - 1)
    n_used = (pad_end[-1] // tm).astype(jnp.int32).reshape(1)
    return dest, block_e.astype(jnp.int32), n_used, n_tiles


def kernel(x, c, ctx, c_ctx, w_mod, b_mod, w_in, conv_w, ret_decay_exp, ret_gn_g, q_norm_g, k_norm_g, w_out,
           ln_g, ln_b, w_router, b_router, w_gate_up, b_gate_up, w_down, b_down):
    depth = w_mod.shape[0]
    alpha = (2.0 * depth) ** 0.25
    b, l, d = x.shape
    lc = ctx.shape[1]
    tm_lat, tm_ctx, tm_moe, tq = 512, 256, 1024, 256

    n_rows = (b + 1 + 7) // 8 * 8
    c_all = jnp.zeros((n_rows, d), F32).at[:b].set(c).at[b].set(c_ctx)
    mod = _modulation(c_all, w_mod, b_mod)

    cos_l, sin_l = _rope_tables(l)
    cos_c, sin_c = jnp.ones((lc, LANES), F32), jnp.zeros((lc, LANES), F32)
    lane_head = jnp.arange(LANES) // HEAD_DIM
    ones_bd = (lane_head[:, None] == lane_head[None, :]).astype(BF16)
    zero_state = jnp.zeros((b, RET_HEADS, HEAD_DIM, HEAD_DIM), F32)

    for layer in range(depth):
        last = layer == depth - 1
        m_lat = mod[layer, :b].reshape(b, 1, N_MOD, d)
        m_ctx = jnp.broadcast_to(mod[layer, b].reshape(1, 1, N_MOD, d), (b, 1, N_MOD, d))
        sh1, sc1, g1, sh2, sc2, g2 = (m_lat[:, :, i] for i in range(N_MOD))
        sh1c, sc1c, g1c, sh2c, sc2c, g2c = (m_ctx[:, :, i] for i in range(N_MOD))

        w_in_bf = w_in[layer].astype(BF16)
        w_out_bf = w_out[layer].astype(BF16)
        gq = jnp.tile(q_norm_g[layer], LANES // HEAD_DIM).reshape(1, LANES)
        gk = jnp.tile(k_norm_g[layer], LANES // HEAD_DIM).reshape(1, LANES)
        log_gamma = jnp.log1p(-jnp.exp2(-ret_decay_exp[layer].astype(F32)))
        lg_rows = jnp.repeat(log_gamma, HEAD_DIM, axis=1)
        lgf_row, lgb_row = lg_rows[0:1], lg_rows[1:2]
        gn_row = ret_gn_g[layer].reshape(1, -1)
        cw = conv_w[layer].T.reshape(3, 1, -1)
        wr_pad = jnp.zeros((d, ROUTER_PAD), F32).at[:, :N_EXPERTS].set(w_router[layer])
        wrh = wr_pad.astype(BF16)
        wrl = (wr_pad - wrh.astype(F32)).astype(BF16)
        br = jnp.zeros((1, ROUTER_PAD), F32).at[0, :N_EXPERTS].set(b_router[layer])
        lng1, lnb1 = ln_g[layer, 0].reshape(1, d), ln_b[layer, 0].reshape(1, d)
        lng2, lnb2 = ln_g[layer, 1].reshape(1, d), ln_b[layer, 1].reshape(1, d)

        zc = _inproj(ctx, 1.0 + sc1c, sh1c, w_in_bf, cos_c, sin_c, gq, gk, ones_bd, tm_ctx)
        zl = _inproj(x, 1.0 + sc1, sh1, w_in_bf, cos_l, sin_l, gq, gk, ones_bd, tm_lat)
        cu_c, cb_c, rq_c, rk_c, rv_c, rg_c, aq_c, ak_c, av_c = zc
        cu_l, cb_l, rq_l, rk_l, rv_l, rg_l, aq_l, ak_l, av_l = zl

        ret_c, s_fwd, s_bwd = _retention(rq_c, rk_c, rv_c, rg_c, zero_state, zero_state,
                                         log_gamma, lgf_row, lgb_row, gn_row)
        ret_l, _, _ = _retention(rq_l, rk_l, rv_l, rg_l, s_fwd, s_bwd, log_gamma, lgf_row, lgb_row, gn_row)

        def kv_heads(a):
            return a.reshape(b, a.shape[1], -1, HEAD_DIM).transpose(0, 2, 1, 3)

        def v_heads(a):
            vh = kv_heads(a)
            pad = jnp.zeros(vh.shape[:3] + (LANES - HEAD_DIM - 1,), BF16)
            return jnp.concatenate([vh, jnp.ones(vh.shape[:3] + (1,), BF16), pad], axis=-1)

        k_all = kv_heads(jnp.concatenate([ak_c, ak_l], axis=1))
        v_all = v_heads(jnp.concatenate([av_c, av_l], axis=1))
        att_l = _attention(aq_l, k_all, v_all, tq)

        cnt0 = jnp.zeros((1, ROUTER_PAD), F32)
        if not last:
            att_c = _attention(aq_c, kv_heads(ak_c), v_heads(av_c), tq)
            ctx1, h2_c, ri_c, rw_c, cnt0 = _outproj(alpha, cu_c, cb_c, cw, ret_c, att_c, ctx, g1c, w_out_bf,
                                                    lng1, lnb1, 1.0 + sc2c, sh2c, wrh, wrl, br, cnt0, tm_ctx)
        x1, h2_l, ri_l, rw_l, cnt = _outproj(alpha, cu_l, cb_l, cw, ret_l, att_l, x, g1, w_out_bf, lng1, lnb1,
                                             1.0 + sc2, sh2, wrh, wrl, br, cnt0, tm_lat)
        if not last:
            n_c = b * lc
            h2 = jnp.concatenate([h2_c, h2_l], axis=0)
            ri = jnp.concatenate([ri_c.reshape(n_c, -1), ri_l.reshape(b * l, -1)], axis=0)
        else:
            n_c = 0
            h2 = h2_l
            ri = ri_l.reshape(b * l, -1)
        n_tok = n_c + b * l

        counts = cnt[0, :N_EXPERTS].astype(jnp.int32)
        dest, block_e, n_used, n_tiles = _route_tables(ri, counts, tm_moe)
        xs = _sc_dispatch(h2.reshape(n_tok, PACKED_ROWS, LANES), dest, n_tiles * tm_moe)
        ys = _expert_ffn(layer, block_e, n_used, xs.reshape(-1, LANES), w_gate_up, b_gate_up, w_down, b_down, tm_moe)
        gathered = _sc_gather(ys.reshape(-1, PACKED_ROWS, LANES), dest).reshape(TOP_K, n_tok * PACKED_ROWS, LANES)
        if not last:
            ctx = _combine(alpha, gathered, 0, rw_c, ctx1, g2c, lng2, lnb2, tm_ctx)
        x = _combine(alpha, gathered, n_c, rw_l, x1, g2, lng2, lnb2, tm_lat)
    return x
```

```python
import functools

import jax
import jax.numpy as jnp
from jax import lax
from jax.experimental import pallas as pl
from jax.experimental.pallas import tpu as pltpu
from jax.experimental.pallas import tpu_sc as plsc

F32 = jnp.float32
BF16 = jnp.bfloat16

HEAD_DIM = 64
GRID_W = 64
ROPE_BASE = 10000.0
N_EXPERTS = 32
TOP_K = 4
SWIGLU_ALPHA = 1.702
SWIGLU_LIMIT = 7.0
N_MOD = 6
EPS = 1e-6
RET_HEADS = 4
RET_CHUNK = 256
ATT_Q_PER_KV = 4
ATT_KV_CHUNK = 256
LANES = 128
ROUTER_PAD = 128
VMEM_LIMIT = 56 * 1024 * 1024
SUBLANES = 8
PACKED_ROWS = 4
FFN_BLOCK = 256
SC_CORES = 2
SC_SUBCORES = 16
SC_WINDOW = 32
BATCH_GROUPS = 2


def _dot(a, b):
    return jnp.dot(a, b, preferred_element_type=F32)


def _dot_nt(a, b):
    return lax.dot_general(a, b, (((1,), (1,)), ((), ())), preferred_element_type=F32)


def _split_bf16(x):
    hi = x.astype(BF16)
    lo = (x - hi.astype(F32)).astype(BF16)
    return hi, lo


def _cparams(*sem):
    return pltpu.CompilerParams(dimension_semantics=sem, vmem_limit_bytes=VMEM_LIMIT)


def _mod_kernel(c_ref, w_ref, b_ref, o_ref):
    c = c_ref[...]
    a = c * jax.nn.sigmoid(c)
    a_hi, a_lo = _split_bf16(a)
    w_hi, w_lo = _split_bf16(w_ref[0])
    o_ref[0] = _dot(a_hi, w_hi) + _dot(a_lo, w_hi) + _dot(a_hi, w_lo) + b_ref[0]


def _modulation(c_all, w_mod, b_mod):
    depth, d, n = w_mod.shape
    r = c_all.shape[0]
    tn = 1536
    return pl.pallas_call(
        _mod_kernel,
        grid=(depth, n // tn),
        in_specs=[
            pl.BlockSpec((r, d), lambda l, j: (0, 0)),
            pl.BlockSpec((1, d, tn), lambda l, j: (l, 0, j)),
            pl.BlockSpec((1, 1, tn), lambda l, j: (l, 0, j)),
        ],
        out_specs=pl.BlockSpec((1, r, tn), lambda l, j: (l, 0, j)),
        out_shape=jax.ShapeDtypeStruct((depth, r, n), F32),
        compiler_params=_cparams("arbitrary", "arbitrary"),
        name="modulation",
    )(c_all, w_mod, b_mod.reshape(depth, 1, n))


def _inproj_kernel(x_ref, sc_ref, sh_ref, w_ref, cos_ref, sin_ref, gq_ref, gk_ref, ones_ref,
                   cu_ref, cb_ref, rq_ref, rk_ref, rv_ref, rg_ref, aq_ref, ak_ref, av_ref):
    h = (x_ref[0] * sc_ref[0] + sh_ref[0]).astype(BF16)
    z = _dot(h, w_ref[...])
    cos = cos_ref[...]
    sin = sin_ref[...]
    ones = ones_ref[...]
    lane = lax.broadcasted_iota(jnp.int32, cos.shape, 1)
    first_half = (lane & 31) < 16

    def rope(xs):
        nxt = pltpu.roll(xs, LANES - 16, axis=1)
        prv = pltpu.roll(xs, 16, axis=1)
        return xs * cos + jnp.where(first_half, nxt, prv) * sin

    def rms(xs, g):
        s_hi, s_lo = _split_bf16(xs * xs)
        ssq = _dot(s_hi, ones) + _dot(s_lo, ones)
        return xs * lax.rsqrt(ssq * (1.0 / HEAD_DIM) + EPS) * g

    cu_ref[0] = z[:, 512:768] * z[:, 0:256]
    cb_ref[0] = z[:, 256:512]
    for j in range(2):
        lo, hi = j * LANES, (j + 1) * LANES
        rq_ref[0, :, lo:hi] = rope(z[:, 768 + lo:768 + hi])
        rk_ref[0, :, lo:hi] = rope(z[:, 1024 + lo:1024 + hi] * (HEAD_DIM ** -0.5))
    rv_ref[0] = z[:, 1280:1536]
    rg_ref[0] = z[:, 1536:1792]
    gq = gq_ref[...]
    for j in range(4):
        lo, hi = j * LANES, (j + 1) * LANES
        q = rope(rms(z[:, 1792 + lo:1792 + hi], gq)) * (HEAD_DIM ** -0.5)
        aq_ref[0, :, lo:hi] = q.astype(BF16)
    ak_ref[0] = rope(rms(z[:, 2304:2432], gk_ref[...])).astype(BF16)
    av_ref[0] = z[:, 2432:2560].astype(BF16)


def _inproj(x, sc, sh, w_bf, cos, sin, gq, gk, ones_bd, tm):
    b, l, d = x.shape
    n = w_bf.shape[1]
    tok = lambda w: pl.BlockSpec((1, tm, w), lambda i, j: (i, j, 0))
    vec = pl.BlockSpec((1, 1, d), lambda i, j: (i, 0, 0))
    full = lambda s: pl.BlockSpec(s, lambda i, j: (0,) * len(s))
    widths = (256, 256, 256, 256, 256, 256, 512, 128, 128)
    dtypes = (F32,) * 6 + (BF16,) * 3
    return pl.pallas_call(
        _inproj_kernel,
        grid=(b, l // tm),
        in_specs=[tok(d), vec, vec, full((d, n)),
                  pl.BlockSpec((tm, LANES), lambda i, j: (j, 0)),
                  pl.BlockSpec((tm, LANES), lambda i, j: (j, 0)),
                  full((1, LANES)), full((1, LANES)), full((LANES, LANES))],
        out_specs=[tok(w) for w in widths],
        out_shape=[jax.ShapeDtypeStruct((b, l, w), dt) for w, dt in zip(widths, dtypes)],
        compiler_params=_cparams("arbitrary", "arbitrary"),
        name="inproj",
    )(x, sc, sh, w_bf, cos, sin, gq, gk, ones_bd)


def _attn_kernel(q_ref, k_ref, v_ref, o_ref):
    tq = q_ref.shape[1]
    q4 = q_ref[0]
    q = jnp.concatenate([q4[:, h * HEAD_DIM:(h + 1) * HEAD_DIM] for h in range(ATT_Q_PER_KV)], axis=0)
    lk = k_ref.shape[2]
    m = jnp.full((q.shape[0], 1), -jnp.inf, F32)
    acc = jnp.zeros((q.shape[0], LANES), F32)
    for lo in range(0, lk, ATT_KV_CHUNK):
        hi = min(lo + ATT_KV_CHUNK, lk)
        s = _dot_nt(q, k_ref[0, 0, lo:hi, :])
        m_new = jnp.maximum(m, jnp.max(s, axis=-1, keepdims=True))
        p = jnp.exp(s - m_new)
        acc = acc * jnp.exp(m - m_new) + _dot(p.astype(BF16), v_ref[0, 0, lo:hi, :])
        m = m_new
    o = acc[:, :HEAD_DIM] / acc[:, HEAD_DIM:HEAD_DIM + 1]
    o_ref[0] = jnp.concatenate([o[h * tq:(h + 1) * tq] for h in range(ATT_Q_PER_KV)], axis=1)


def _attention(q, k, v, tq):
    b, lq, wq = q.shape
    hkv, lk = k.shape[1], k.shape[2]
    wg = ATT_Q_PER_KV * HEAD_DIM
    return pl.pallas_call(
        _attn_kernel,
        grid=(b, hkv, lq // tq),
        in_specs=[pl.BlockSpec((1, tq, wg), lambda i, g, j: (i, j, g)),
                  pl.BlockSpec((1, 1, lk, HEAD_DIM), lambda i, g, j: (i, g, 0, 0)),
                  pl.BlockSpec((1, 1, lk, LANES), lambda i, g, j: (i, g, 0, 0))],
        out_specs=pl.BlockSpec((1, tq, wg), lambda i, g, j: (i, j, g)),
        out_shape=jax.ShapeDtypeStruct((b, lq, wq), F32),
        compiler_params=_cparams("arbitrary", "arbitrary", "arbitrary"),
        name="attention",
    )(q, k, v)


def _ret_kernel(lg_ref, q_ref, k_ref, v_ref, g_ref, s0f_ref, s0b_ref, lgf_ref, lgb_ref, gn_ref,
                o_ref, sff_ref, sfb_ref, s_scr, sb_scr, dec_scr):
    i, p, j = pl.program_id(0), pl.program_id(1), pl.program_id(2)
    nc = pl.num_programs(2)
    c = k_ref.shape[1]
    w = k_ref.shape[2]
    pos = lax.broadcasted_iota(jnp.int32, (c, w), 0).astype(F32)
    lgf = lgf_ref[...]
    lgb = lgb_ref[...]

    @pl.when((i == 0) & (p == 0) & (j == 0))
    def _():
        diff = (lax.broadcasted_iota(jnp.int32, (c, c), 0) - lax.broadcasted_iota(jnp.int32, (c, c), 1)).astype(F32)
        for h in range(RET_HEADS):
            dec_scr[h] = jnp.where(diff >= 0.0, jnp.exp(lg_ref[0, h] * jnp.maximum(diff, 0.0)),
                                   jnp.exp(lg_ref[1, h] * jnp.maximum(-diff, 0.0)))

    def update_state(kzt, vh, cdec):
        for h in range(RET_HEADS):
            lo, hi = h * HEAD_DIM, (h + 1) * HEAD_DIM
            s_scr[h] = cdec[:, lo:hi] * s_scr[h] + _dot(kzt[lo:hi, :], vh[:, lo:hi])

    @pl.when(p == 0)
    def _():
        @pl.when(j == 0)
        def _():
            s_scr[...] = s0b_ref[0]

        sb_scr[nc - 1 - j] = s_scr[...]
        kzt = (k_ref[0] * jnp.exp(lgb * pos)).T.astype(BF16)
        update_state(kzt, v_ref[0].astype(BF16), jnp.exp(lgb * float(c)))

        @pl.when(j == nc - 1)
        def _():
            sfb_ref[0] = s_scr[...]

    @pl.when(p == 1)
    def _():
        @pl.when(j == 0)
        def _():
            s_scr[...] = s0f_ref[0]

        q = q_ref[0]
        k = k_ref[0]
        qf = (q * jnp.exp(lgf * (pos + 1.0))).astype(BF16)
        qb = (q * jnp.exp(lgb * (float(c) - pos))).astype(BF16)
        kzt = (k * jnp.exp(lgf * (float(c) - 1.0 - pos))).T.astype(BF16)
        qh = q.astype(BF16)
        kh = k.astype(BF16)
        vh = v_ref[0].astype(BF16)
        outs = []
        for h in range(RET_HEADS):
            lo, hi = h * HEAD_DIM, (h + 1) * HEAD_DIM
            sc = _dot_nt(qh[:, lo:hi], kh[:, lo:hi])
            y = (_dot((sc * dec_scr[h]).astype(BF16), vh[:, lo:hi])
                 + _dot(qf[:, lo:hi], s_scr[h].astype(BF16))
                 + _dot(qb[:, lo:hi], sb_scr[j, h].astype(BF16)))
            mu = jnp.mean(y, axis=-1, keepdims=True)
            yc = y - mu
            var = jnp.mean(yc * yc, axis=-1, keepdims=True)
            outs.append(yc * lax.rsqrt(var + EPS))
        update_state(kzt, vh, jnp.exp(lgf * float(c)))
        gate = g_ref[0]
        o_ref[0] = jnp.concatenate(outs, axis=1) * gn_ref[...] * (gate * jax.nn.sigmoid(gate))

        @pl.when(j == nc - 1)
        def _():
            sff_ref[0] = s_scr[...]


def _retention(q, k, v, gate, s0_f, s0_b, lg, lgf_row, lgb_row, gn_row):
    b, l, w = q.shape
    c = min(RET_CHUNK, l)
    nc = l // c
    st = (RET_HEADS, HEAD_DIM, HEAD_DIM)
    st_spec = pl.BlockSpec((1,) + st, lambda i, p, j: (i, 0, 0, 0))
    row = pl.BlockSpec((1, w), lambda i, p, j: (0, 0))
    both = pl.BlockSpec((1, c, w), lambda i, p, j: (i, jnp.where(p == 0, nc - 1 - j, j), 0))
    fwd_only = pl.BlockSpec((1, c, w), lambda i, p, j: (i, p * j, 0))
    return pl.pallas_call(
        _ret_kernel,
        grid=(b, 2, nc),
        in_specs=[pl.BlockSpec(memory_space=pltpu.SMEM), fwd_only, both, both, fwd_only,
                  st_spec, st_spec, row, row, row],
        out_specs=[fwd_only, st_spec, st_spec],
        out_shape=[jax.ShapeDtypeStruct((b, l, w), F32), jax.ShapeDtypeStruct((b,) + st, F32),
                   jax.ShapeDtypeStruct((b,) + st, F32)],
        scratch_shapes=[pltpu.VMEM(st, F32), pltpu.VMEM((nc,) + st, F32), pltpu.VMEM((RET_HEADS, c, c), F32)],
        compiler_params=_cparams("arbitrary", "arbitrary", "arbitrary"),
        name="retention",
    )(lg, q, k, v, gate, s0_f, s0_b, lgf_row, lgb_row, gn_row)


def _layer_norm(x, g, b):
    mu = jnp.mean(x, axis=-1, keepdims=True)
    xc = x - mu
    var = jnp.mean(xc * xc, axis=-1, keepdims=True)
    return xc * lax.rsqrt(var + EPS) * g + b


def _store_rows_packed(ref, x):
    tm, d = x.shape
    bits = pltpu.bitcast(x.astype(BF16).astype(F32), jnp.uint32)
    word = (bits[:, :d // 2] >> 16) | bits[:, d // 2:]
    for s in range(PACKED_ROWS):
        ref[pl.ds(s, tm, stride=PACKED_ROWS), :] = word[:, s * LANES:(s + 1) * LANES]


def _load_rows_packed(ref, tm):
    word = jnp.concatenate([ref[pl.ds(s, tm, stride=PACKED_ROWS), :] for s in range(PACKED_ROWS)], axis=1)
    lo = pltpu.bitcast(word << 16, F32).astype(BF16)
    hi = pltpu.bitcast(word & jnp.uint32(0xFFFF0000), F32).astype(BF16)
    return jnp.concatenate([lo, hi], axis=1)


def _outproj_kernel(alpha, cu_ref, cup_ref, cun_ref, cb_ref, cw_ref, ret_ref, att_ref, x_ref, g1_ref,
                    w_ref, lng_ref, lnb_ref, sc_ref, sh_ref, wrh_ref, wrl_ref, br_ref, cnt0_ref, before_ref,
                    x1_ref, h2_ref, ri_ref, rw_ref, cnt_ref, run_scr):
    i = pl.program_id(0)
    j = pl.program_id(1)

    @pl.when((i == 0) & (j == 0))
    def _():
        run_scr[...] = cnt0_ref[...]

    t = cu_ref[0]
    tm = t.shape[0]
    prev_row = jnp.where(j == 0, 0.0, cup_ref[0][7:8, :])
    next_row = jnp.where(j == pl.num_programs(1) - 1, 0.0, cun_ref[0][0:1, :])
    row = lax.broadcasted_iota(jnp.int32, t.shape, 0)
    t_prev = jnp.where(row == 0, prev_row, pltpu.roll(t, 1, axis=0))
    t_next = jnp.where(row == tm - 1, next_row, pltpu.roll(t, tm - 1, axis=0))
    conv = cb_ref[0] * (t_prev * cw_ref[0] + t * cw_ref[1] + t_next * cw_ref[2])
    y = (_dot(conv.astype(BF16), w_ref[0:256, :])
         + _dot(ret_ref[0].astype(BF16), w_ref[256:512, :])
         + _dot(att_ref[0].astype(BF16), w_ref[512:1024, :]))
    x1 = _layer_norm(alpha * x_ref[0] + g1_ref[0] * y, lng_ref[...], lnb_ref[...])
    x1_ref[0] = x1
    h2 = x1 * sc_ref[0] + sh_ref[0]
    _store_rows_packed(h2_ref, h2)
    h_hi, h_lo = _split_bf16(h2)
    wrh = wrh_ref[...]
    logits = _dot(h_hi, wrh) + _dot(h_lo, wrh) + _dot(h_hi, wrl_ref[...]) + br_ref[...]

    lane = lax.broadcasted_iota(jnp.int32, logits.shape, 1)
    lane_f = lane.astype(F32)
    work = jnp.where(lane < N_EXPERTS, logits, -jnp.inf)
    vals, firsts, sels = [], [], []
    for _ in range(TOP_K):
        m = jnp.max(work, axis=-1, keepdims=True)
        first = jnp.min(jnp.where(work == m, lane_f, float(ROUTER_PAD)), axis=-1, keepdims=True)
        sel = lane_f == first
        vals.append(m)
        firsts.append(first)
        sels.append(sel)
        work = jnp.where(sel, -jnp.inf, work)
    exps = [jnp.exp(v - vals[0]) for v in vals]
    denom = exps[0]
    for e in exps[1:]:
        denom = denom + e

    cnt = jnp.zeros(logits.shape, F32)
    for sel in sels:
        cnt = cnt + jnp.where(sel, 1.0, 0.0)
    base = _dot(before_ref[...], cnt.astype(BF16)) + run_scr[...]
    run_scr[...] = run_scr[...] + jnp.sum(cnt, axis=0, keepdims=True)
    cnt_ref[...] = run_scr[...]

    ri = jnp.zeros(logits.shape, F32)
    rw = jnp.zeros(logits.shape, F32)
    for kk in range(TOP_K):
        rank = jnp.sum(jnp.where(sels[kk], base, 0.0), axis=-1, keepdims=True)
        ri = jnp.where(lane == kk, firsts[kk], ri)
        ri = jnp.where(lane == TOP_K + kk, rank, ri)
        rw = jnp.where(lane == kk, exps[kk] / denom, rw)
    ri_ref[0] = ri.astype(jnp.int32)
    rw_ref[0] = rw


def _outproj(alpha, cu, cb, cw, ret, att, x, g1, w_bf, lng, lnb, sc2, sh2, wrh, wrl, br, cnt0, tm):
    b, l, d = x.shape
    nj = l // tm
    tok = lambda w: pl.BlockSpec((1, tm, w), lambda i, j: (i, j, 0))
    vec = pl.BlockSpec((1, 1, d), lambda i, j: (i, 0, 0))
    full = lambda s: pl.BlockSpec(s, lambda i, j: (0,) * len(s))
    r8 = tm // 8
    nb8 = l // 8
    tri = jnp.arange(tm)
    before = (tri[None, :] < tri[:, None]).astype(BF16)
    return pl.pallas_call(
        functools.partial(_outproj_kernel, alpha),
        grid=(b, nj),
        in_specs=[tok(256),
                  pl.BlockSpec((1, 8, 256), lambda i, j: (i, jnp.maximum(j * r8 - 1, 0), 0)),
                  pl.BlockSpec((1, 8, 256), lambda i, j: (i, jnp.minimum((j + 1) * r8, nb8 - 1), 0)),
                  tok(256), full((3, 1, 256)), tok(256), tok(512), tok(d), vec,
                  full((d, d)), full((1, d)), full((1, d)), vec, vec,
                  full((d, ROUTER_PAD)), full((d, ROUTER_PAD)), full((1, ROUTER_PAD)), full((1, ROUTER_PAD)),
                  full((tm, tm))],
        out_specs=[tok(d), pl.BlockSpec((tm * PACKED_ROWS, LANES), lambda i, j: (i * nj + j, 0)),
                   tok(ROUTER_PAD), tok(ROUTER_PAD), full((1, ROUTER_PAD))],
        out_shape=[jax.ShapeDtypeStruct((b, l, d), F32),
                   jax.ShapeDtypeStruct((b * l * PACKED_ROWS, LANES), jnp.uint32),
                   jax.ShapeDtypeStruct((b, l, ROUTER_PAD), jnp.int32),
                   jax.ShapeDtypeStruct((b, l, ROUTER_PAD), F32),
                   jax.ShapeDtypeStruct((1, ROUTER_PAD), F32)],
        scratch_shapes=[pltpu.VMEM((1, ROUTER_PAD), F32)],
        compiler_params=_cparams("arbitrary", "arbitrary"),
        name="outproj_ln_router",
    )(cu, cu, cu, cb, cw, ret, att, x, g1, w_bf, lng, lnb, sc2, sh2, wrh, wrl, br, cnt0, before)


def _ffn_kernel(be_ref, nu_ref, x_ref, wgu_ref, bgu_ref, wd_ref, bd_ref, o_ref, wgu_scr, wd_scr):
    j = pl.program_id(0)
    f = wd_ref.shape[2]
    tm = x_ref.shape[0] // PACKED_ROWS

    @pl.when(j < nu_ref[0])
    def _():
        @pl.when((j == 0) | (be_ref[j] != be_ref[jnp.maximum(j - 1, 0)]))
        def _():
            wgu_scr[...] = wgu_ref[0, 0].astype(BF16)
            wd_scr[...] = wd_ref[0, 0].astype(BF16)

        x = _load_rows_packed(x_ref, tm)
        bgu = bgu_ref[0, 0]
        acts = []
        for lo in range(0, f, FFN_BLOCK):
            hi = lo + FFN_BLOCK
            gate = jnp.minimum(_dot(x, wgu_scr[:, lo:hi]) + bgu[:, lo:hi], SWIGLU_LIMIT)
            up = jnp.clip(_dot(x, wgu_scr[:, f + lo:f + hi]) + bgu[:, f + lo:f + hi], -SWIGLU_LIMIT, SWIGLU_LIMIT)
            acts.append(((up + 1.0) * (gate * jax.nn.sigmoid(SWIGLU_ALPHA * gate))).astype(BF16))
        act = jnp.concatenate(acts, axis=1)
        _store_rows_packed(o_ref, _dot(act, wd_scr[...]) + bd_ref[0, 0])

    @pl.when(j >= nu_ref[0])
    def _():
        o_ref[...] = jnp.zeros_like(o_ref)


def _expert_ffn(layer, block_e, n_used, xs, wgu, bgu, wd, bd, tm):
    n_rows = xs.shape[0] // PACKED_ROWS
    depth, ne, d, f2 = wgu.shape
    f = f2 // 2
    rows = pl.BlockSpec((tm * PACKED_ROWS, LANES), lambda j, be, nu: (j, 0))
    grid_spec = pltpu.PrefetchScalarGridSpec(
        num_scalar_prefetch=2,
        grid=(n_rows // tm,),
        in_specs=[rows,
                  pl.BlockSpec((1, 1, d, f2), lambda j, be, nu: (layer, be[j], 0, 0)),
                  pl.BlockSpec((1, 1, 1, f2), lambda j, be, nu: (layer, be[j], 0, 0)),
                  pl.BlockSpec((1, 1, f, d), lambda j, be, nu: (layer, be[j], 0, 0)),
                  pl.BlockSpec((1, 1, 1, d), lambda j, be, nu: (layer, be[j], 0, 0))],
        out_specs=rows,
        scratch_shapes=[pltpu.VMEM((d, f2), BF16), pltpu.VMEM((f, d), BF16)],
    )
    return pl.pallas_call(
        _ffn_kernel,
        grid_spec=grid_spec,
        out_shape=jax.ShapeDtypeStruct((n_rows * PACKED_ROWS, LANES), jnp.uint32),
        compiler_params=_cparams("arbitrary"),
        name="expert_ffn",
    )(block_e, n_used, xs, wgu, bgu.reshape(depth, ne, 1, f2), wd, bd.reshape(depth, ne, 1, d))


def _sc_worker_base(per_worker):
    return (lax.axis_index("s") * SC_CORES + lax.axis_index("c")) * per_worker


def _sc_dispatch(rows, dest, n_out):
    t = rows.shape[0]
    kk = dest.shape[0] // t
    w = SC_WINDOW
    per_worker = t // (SC_CORES * SC_SUBCORES)
    assert per_worker * SC_CORES * SC_SUBCORES == t and per_worker % w == 0
    mesh = plsc.VectorSubcoreMesh(core_axis_name="c", subcore_axis_name="s")

    @functools.partial(
        pl.kernel, mesh=mesh,
        out_type=jax.ShapeDtypeStruct((n_out,) + rows.shape[1:], rows.dtype),
        scratch_types=[pltpu.VMEM((w,), jnp.int32)] * kk + [pltpu.VMEM((w,) + rows.shape[1:], rows.dtype)])
    def scatter_rows(r_hbm, d_hbm, o_hbm, *scratch):
        idx_v, rows_v = scratch[:kk], scratch[kk]
        base = _sc_worker_base(per_worker)

        @pl.loop(0, per_worker // w)
        def _(i):
            off = base + i * w
            pltpu.sync_copy(r_hbm.at[pl.ds(off, w)], rows_v)
            for s in range(kk):
                pltpu.sync_copy(d_hbm.at[pl.ds(s * t + off, w)], idx_v[s])
            for s in range(kk):
                pltpu.sync_copy(rows_v, o_hbm.at[idx_v[s]])

    return scatter_rows(rows, dest)


def _sc_gather(table, idx):
    n = idx.shape[0]
    w = SC_WINDOW
    per_worker = n // (SC_CORES * SC_SUBCORES)
    assert per_worker * SC_CORES * SC_SUBCORES == n and per_worker % w == 0
    mesh = plsc.VectorSubcoreMesh(core_axis_name="c", subcore_axis_name="s")

    @functools.partial(
        pl.kernel, mesh=mesh,
        out_type=jax.ShapeDtypeStruct((n,) + table.shape[1:], table.dtype),
        scratch_types=[pltpu.VMEM((w,), jnp.int32), pltpu.VMEM((w,) + table.shape[1:], table.dtype)])
    def gather_rows(t_hbm, i_hbm, o_hbm, idx_v, rows_v):
        base = _sc_worker_base(per_worker)

        @pl.loop(0, per_worker // w)
        def _(i):
            off = base + i * w
            pltpu.sync_copy(i_hbm.at[pl.ds(off, w)], idx_v)
            pltpu.sync_copy(t_hbm.at[idx_v], rows_v)
            pltpu.sync_copy(rows_v, o_hbm.at[pl.ds(off, w)])

    return gather_rows(table, idx)


def _combine_kernel(alpha, g_ref, w_ref, x_ref, g2_ref, lng_ref, lnb_ref, o_ref):
    tm = x_ref.shape[1]
    w = w_ref[0]
    f = _load_rows_packed(g_ref.at[0], tm).astype(F32) * w[:, 0:1]
    for kk in range(1, TOP_K):
        f = f + _load_rows_packed(g_ref.at[kk], tm).astype(F32) * w[:, kk:kk + 1]
    o_ref[0] = _layer_norm(alpha * x_ref[0] + g2_ref[0] * f, lng_ref[...], lnb_ref[...])


def _combine(alpha, gathered, tok_off, wts, x1, g2, lng, lnb, tm):
    b, l, d = x1.shape
    nj = l // tm
    assert tok_off % tm == 0
    blk_off = tok_off // tm
    tok = pl.BlockSpec((1, tm, d), lambda i, j: (i, j, 0))
    full = pl.BlockSpec((1, d), lambda i, j: (0, 0))
    return pl.pallas_call(
        functools.partial(_combine_kernel, alpha),
        grid=(b, nj),
        in_specs=[pl.BlockSpec((TOP_K, tm * PACKED_ROWS, LANES), lambda i, j: (0, blk_off + i * nj + j, 0)),
                  pl.BlockSpec((1, tm, ROUTER_PAD), lambda i, j: (i, j, 0)),
                  tok, pl.BlockSpec((1, 1, d), lambda i, j: (i, 0, 0)), full, full],
        out_specs=tok,
        out_shape=jax.ShapeDtypeStruct((b, l, d), F32),
        compiler_params=_cparams("arbitrary", "arbitrary"),
        name="combine_ln",
    )(gathered, wts, x1, g2, lng, lnb)


def _rope_tables(l):
    rows = l // GRID_W
    axis_dim = HEAD_DIM // 2
    inv_freq = ROPE_BASE ** (-jnp.arange(0, axis_dim, 2, dtype=F32) / axis_dim)
    row = jnp.repeat(jnp.arange(rows, dtype=F32), GRID_W)
    col = jnp.tile(jnp.arange(GRID_W, dtype=F32), rows)
    ang = jnp.stack([row[:, None] * inv_freq, col[:, None] * inv_freq], axis=1)
    cos, sin = jnp.cos(ang), jnp.sin(ang)
    cos64 = jnp.broadcast_to(cos[:, :, None, :], (l, 2, 2, HEAD_DIM // 4)).reshape(l, HEAD_DIM)
    sin64 = jnp.stack([-sin, sin], axis=2).reshape(l, HEAD_DIM)
    return jnp.tile(cos64, (1, LANES // HEAD_DIM)), jnp.tile(sin64, (1, LANES // HEAD_DIM))


def _route_tables(ri, counts, tm):
    t = ri.shape[0]
    top_e = ri[:, :TOP_K]
    rank = ri[:, TOP_K:2 * TOP_K]
    padded = (counts + tm - 1) // tm * tm
    pad_end = jnp.cumsum(padded)
    pad_start = pad_end - padded
    experts = jnp.arange(N_EXPERTS, dtype=jnp.int32)
    start = jnp.sum(jnp.where(top_e[:, :, None] == experts, pad_start, 0), axis=-1)
    dest = (start + rank).T.reshape(TOP_K * t)
    n_tiles = (t * TOP_K + N_EXPERTS * (tm - 1) + tm - 1) // tm
    tile_start = jnp.arange(n_tiles, dtype=jnp.int32) * tm
    block_e = jnp.minimum(jnp.sum((pad_end[None, :] <= tile_start[:, None]).astype(jnp.int32), axis=1), N_EXPERTS - 1)
    n_used = (pad_end[-1] // tm).astype(jnp.int32).reshape(1)
    return dest, block_e.astype(jnp.int32), n_used, n_tiles


def _layer_params(layer, d, w_in, conv_w, ret_decay_exp, ret_gn_g, q_norm_g, k_norm_g, w_out, ln_g, ln_b,
                  w_router, b_router):
    log_gamma = jnp.log1p(-jnp.exp2(-ret_decay_exp[layer].astype(F32)))
    lg_rows = jnp.repeat(log_gamma, HEAD_DIM, axis=1)
    wr_pad = jnp.zeros((d, ROUTER_PAD), F32).at[:, :N_EXPERTS].set(w_router[layer])
    wrh = wr_pad.astype(BF16)
    return dict(
        w_in=w_in[layer].astype(BF16), w_out=w_out[layer].astype(BF16),
        gq=jnp.tile(q_norm_g[layer], LANES // HEAD_DIM).reshape(1, LANES),
        gk=jnp.tile(k_norm_g[layer], LANES // HEAD_DIM).reshape(1, LANES),
        log_gamma=log_gamma, lgf_row=lg_rows[0:1], lgb_row=lg_rows[1:2], gn_row=ret_gn_g[layer].reshape(1, -1),
        cw=conv_w[layer].T.reshape(3, 1, -1), wrh=wrh, wrl=(wr_pad - wrh.astype(F32)).astype(BF16),
        br=jnp.zeros((1, ROUTER_PAD), F32).at[0, :N_EXPERTS].set(b_router[layer]),
        lng1=ln_g[layer, 0].reshape(1, d), lnb1=ln_b[layer, 0].reshape(1, d),
        lng2=ln_g[layer, 1].reshape(1, d), lnb2=ln_b[layer, 1].reshape(1, d))


def _layer(layer, last, alpha, x, ctx, m_lat, m_ctx, p, tables, experts):
    b, l, d = x.shape
    lc = ctx.shape[1]
    tm_lat, tm_ctx, tm_moe, tq = 512, 256, 512, 256
    cos_l, sin_l, cos_c, sin_c, ones_bd, zero_state = tables
    sh1, sc1, g1, sh2, sc2, g2 = (m_lat[:, :, i] for i in range(N_MOD))
    sh1c, sc1c, g1c, sh2c, sc2c, g2c = (m_ctx[:, :, i] for i in range(N_MOD))

    zc = _inproj(ctx, 1.0 + sc1c, sh1c, p["w_in"], cos_c, sin_c, p["gq"], p["gk"], ones_bd, tm_ctx)
    zl = _inproj(x, 1.0 + sc1, sh1, p["w_in"], cos_l, sin_l, p["gq"], p["gk"], ones_bd, tm_lat)
    cu_c, cb_c, rq_c, rk_c, rv_c, rg_c, aq_c, ak_c, av_c = zc
    cu_l, cb_l, rq_l, rk_l, rv_l, rg_l, aq_l, ak_l, av_l = zl

    ret_args = (p["log_gamma"], p["lgf_row"], p["lgb_row"], p["gn_row"])
    ret_c, s_fwd, s_bwd = _retention(rq_c, rk_c, rv_c, rg_c, zero_state, zero_state, *ret_args)
    ret_l, _, _ = _retention(rq_l, rk_l, rv_l, rg_l, s_fwd, s_bwd, *ret_args)

    def kv_heads(a):
        return a.reshape(b, a.shape[1], -1, HEAD_DIM).transpose(0, 2, 1, 3)

    def v_heads(a):
        vh = kv_heads(a)
        pad = jnp.zeros(vh.shape[:3] + (LANES - HEAD_DIM - 1,), BF16)
        return jnp.concatenate([vh, jnp.ones(vh.shape[:3] + (1,), BF16), pad], axis=-1)

    k_all = kv_heads(jnp.concatenate([ak_c, ak_l], axis=1))
    v_all = v_heads(jnp.concatenate([av_c, av_l], axis=1))
    att_l = _attention(aq_l, k_all, v_all, tq)

    out_args = (p["w_out"], p["lng1"], p["lnb1"])
    rt_args = (p["wrh"], p["wrl"], p["br"])
    cnt0 = jnp.zeros((1, ROUTER_PAD), F32)
    if not last:
        att_c = _attention(aq_c, kv_heads(ak_c), v_heads(av_c), tq)
        ctx1, h2_c, ri_c, rw_c, cnt0 = _outproj(alpha, cu_c, cb_c, p["cw"], ret_c, att_c, ctx, g1c, *out_args,
                                                1.0 + sc2c, sh2c, *rt_args, cnt0, tm_ctx)
    x1, h2_l, ri_l, rw_l, cnt = _outproj(alpha, cu_l, cb_l, p["cw"], ret_l, att_l, x, g1, *out_args,
                                         1.0 + sc2, sh2, *rt_args, cnt0, tm_lat)
    if not last:
        n_c = b * lc
        h2 = jnp.concatenate([h2_c, h2_l], axis=0)
        ri = jnp.concatenate([ri_c.reshape(n_c, -1), ri_l.reshape(b * l, -1)], axis=0)
    else:
        n_c = 0
        h2 = h2_l
        ri = ri_l.reshape(b * l, -1)
    n_tok = n_c + b * l

    counts = cnt[0, :N_EXPERTS].astype(jnp.int32)
    dest, block_e, n_used, n_tiles = _route_tables(ri, counts, tm_moe)
    xs = _sc_dispatch(h2.reshape(n_tok, PACKED_ROWS, LANES), dest, n_tiles * tm_moe)
    ys = _expert_ffn(layer, block_e, n_used, xs.reshape(-1, LANES), *experts, tm_moe)
    gathered = _sc_gather(ys.reshape(-1, PACKED_ROWS, LANES), dest).reshape(TOP_K, n_tok * PACKED_ROWS, LANES)
    if not last:
        ctx = _combine(alpha, gathered, 0, rw_c, ctx1, g2c, p["lng2"], p["lnb2"], tm_ctx)
    x = _combine(alpha, gathered, n_c, rw_l, x1, g2, p["lng2"], p["lnb2"], tm_lat)
    return x, ctx


def kernel(x, c, ctx, c_ctx, w_mod, b_mod, w_in, conv_w, ret_decay_exp, ret_gn_g, q_norm_g, k_norm_g, w_out,
           ln_g, ln_b, w_router, b_router, w_gate_up, b_gate_up, w_down, b_down):
    depth = w_mod.shape[0]
    alpha = (2.0 * depth) ** 0.25
    b, l, d = x.shape
    lc = ctx.shape[1]
    groups = BATCH_GROUPS if b % BATCH_GROUPS == 0 else 1
    bg = b // groups

    n_rows = (b + 1 + 7) // 8 * 8
    c_all = jnp.zeros((n_rows, d), F32).at[:b].set(c).at[b].set(c_ctx)
    mod = _modulation(c_all, w_mod, b_mod)

    cos_l, sin_l = _rope_tables(l)
    cos_c, sin_c = jnp.ones((lc, LANES), F32), jnp.zeros((lc, LANES), F32)
    lane_head = jnp.arange(LANES) // HEAD_DIM
    ones_bd = (lane_head[:, None] == lane_head[None, :]).astype(BF16)
    zero_state = jnp.zeros((bg, RET_HEADS, HEAD_DIM, HEAD_DIM), F32)
    tables = (cos_l, sin_l, cos_c, sin_c, ones_bd, zero_state)
    experts = (w_gate_up, b_gate_up, w_down, b_down)

    xs = [x[g * bg:(g + 1) * bg] for g in range(groups)]
    cs = [ctx[g * bg:(g + 1) * bg] for g in range(groups)]
    for layer in range(depth):
        last = layer == depth - 1
        p = _layer_params(layer, d, w_in, conv_w, ret_decay_exp, ret_gn_g, q_norm_g, k_norm_g, w_out, ln_g, ln_b,
                          w_router, b_router)
        m_ctx = jnp.broadcast_to(mod[layer, b].reshape(1, 1, N_MOD, d), (bg, 1, N_MOD, d))
        for g in range(groups):
            m_lat = mod[layer, g * bg:(g + 1) * bg].reshape(bg, 1, N_MOD, d)
            xs[g], cs[g] = _layer(layer, last, alpha, xs[g], cs[g], m_lat, m_ctx, p, tables, experts)
    return jnp.concatenate(xs, axis=0) if groups > 1 else xs[0]
```

```python
import functools

import jax
import jax.numpy as jnp
from jax import lax
from jax.experimental import pallas as pl
from jax.experimental.pallas import tpu as pltpu
from jax.experimental.pallas import tpu_sc as plsc

F32 = jnp.float32
BF16 = jnp.bfloat16

HEAD_DIM = 64
GRID_W = 64
ROPE_BASE = 10000.0
N_EXPERTS = 32
TOP_K = 4
SWIGLU_ALPHA = 1.702
SWIGLU_LIMIT = 7.0
N_MOD = 6
EPS = 1e-6
RET_HEADS = 4
RET_CHUNK = 256
RET_BATCH = 2
ATT_Q_PER_KV = 4
ATT_KV_CHUNK = 256
LANES = 128
ROUTER_PAD = 128
VMEM_LIMIT = 56 * 1024 * 1024
SUBLANES = 8
PACKED_ROWS = 4
FFN_BLOCK = 256
FFN_ROWS = 512
SC_CORES = 2
SC_SUBCORES = 16
SC_WINDOW = 64
BATCH_GROUPS = 1


def _dot(a, b):
    return jnp.dot(a, b, preferred_element_type=F32)


def _dot_nt(a, b):
    return lax.dot_general(a, b, (((1,), (1,)), ((), ())), preferred_element_type=F32)


def _split_bf16(x):
    hi = x.astype(BF16)
    lo = (x - hi.astype(F32)).astype(BF16)
    return hi, lo


def _cparams(*sem):
    return pltpu.CompilerParams(dimension_semantics=sem, vmem_limit_bytes=VMEM_LIMIT)


def _mod_kernel(c_ref, w_ref, b_ref, o_ref):
    c = c_ref[...]
    a = c * jax.nn.sigmoid(c)
    a_hi, a_lo = _split_bf16(a)
    w_hi, w_lo = _split_bf16(w_ref[0])
    o_ref[0] = _dot(a_hi, w_hi) + _dot(a_lo, w_hi) + _dot(a_hi, w_lo) + b_ref[0]


def _modulation(c_all, w_mod, b_mod):
    depth, d, n = w_mod.shape
    r = c_all.shape[0]
    tn = 1536
    return pl.pallas_call(
        _mod_kernel,
        grid=(depth, n // tn),
        in_specs=[
            pl.BlockSpec((r, d), lambda l, j: (0, 0)),
            pl.BlockSpec((1, d, tn), lambda l, j: (l, 0, j)),
            pl.BlockSpec((1, 1, tn), lambda l, j: (l, 0, j)),
        ],
        out_specs=pl.BlockSpec((1, r, tn), lambda l, j: (l, 0, j)),
        out_shape=jax.ShapeDtypeStruct((depth, r, n), F32),
        compiler_params=_cparams("arbitrary", "arbitrary"),
        name="modulation",
    )(c_all, w_mod, b_mod.reshape(depth, 1, n))


def _inproj_kernel(x_ref, sc_ref, sh_ref, w_ref, cos_ref, sin_ref, gq_ref, gk_ref, ones_ref,
                   cu_ref, cb_ref, rq_ref, rk_ref, rv_ref, rg_ref, aq_ref, ak_ref, av_ref):
    h = (x_ref[0] * sc_ref[0] + sh_ref[0]).astype(BF16)
    z = _dot(h, w_ref[...])
    cos = cos_ref[...]
    sin = sin_ref[...]
    ones = ones_ref[...]
    lane = lax.broadcasted_iota(jnp.int32, cos.shape, 1)
    first_half = (lane & 31) < 16

    def rope(xs):
        nxt = pltpu.roll(xs, LANES - 16, axis=1)
        prv = pltpu.roll(xs, 16, axis=1)
        return xs * cos + jnp.where(first_half, nxt, prv) * sin

    def rms(xs, g):
        s_hi, s_lo = _split_bf16(xs * xs)
        ssq = _dot(s_hi, ones) + _dot(s_lo, ones)
        return xs * lax.rsqrt(ssq * (1.0 / HEAD_DIM) + EPS) * g

    cu_ref[0] = z[:, 512:768] * z[:, 0:256]
    cb_ref[0] = z[:, 256:512]
    for j in range(2):
        lo, hi = j * LANES, (j + 1) * LANES
        rq_ref[0, :, lo:hi] = rope(z[:, 768 + lo:768 + hi])
        rk_ref[0, :, lo:hi] = rope(z[:, 1024 + lo:1024 + hi] * (HEAD_DIM ** -0.5))
    rv_ref[0] = z[:, 1280:1536]
    rg_ref[0] = z[:, 1536:1792]
    gq = gq_ref[...]
    for j in range(4):
        lo, hi = j * LANES, (j + 1) * LANES
        q = rope(rms(z[:, 1792 + lo:1792 + hi], gq)) * (HEAD_DIM ** -0.5)
        aq_ref[0, :, lo:hi] = q.astype(BF16)
    ak_ref[0] = rope(rms(z[:, 2304:2432], gk_ref[...])).astype(BF16)
    av_ref[0] = z[:, 2432:2560].astype(BF16)


def _inproj(x, sc, sh, w_bf, cos, sin, gq, gk, ones_bd, tm):
    b, l, d = x.shape
    n = w_bf.shape[1]
    tok = lambda w: pl.BlockSpec((1, tm, w), lambda i, j: (i, j, 0))
    vec = pl.BlockSpec((1, 1, d), lambda i, j: (i, 0, 0))
    full = lambda s: pl.BlockSpec(s, lambda i, j: (0,) * len(s))
    widths = (256, 256, 256, 256, 256, 256, 512, 128, 128)
    dtypes = (F32,) * 6 + (BF16,) * 3
    return pl.pallas_call(
        _inproj_kernel,
        grid=(b, l // tm),
        in_specs=[tok(d), vec, vec, full((d, n)),
                  pl.BlockSpec((tm, LANES), lambda i, j: (j, 0)),
                  pl.BlockSpec((tm, LANES), lambda i, j: (j, 0)),
                  full((1, LANES)), full((1, LANES)), full((LANES, LANES))],
        out_specs=[tok(w) for w in widths],
        out_shape=[jax.ShapeDtypeStruct((b, l, w), dt) for w, dt in zip(widths, dtypes)],
        compiler_params=_cparams("arbitrary", "arbitrary"),
        name="inproj",
    )(x, sc, sh, w_bf, cos, sin, gq, gk, ones_bd)


def _attn_kernel(q_ref, k_ref, v_ref, o_ref):
    tq = q_ref.shape[1]
    q4 = q_ref[0]
    q = jnp.concatenate([q4[:, h * HEAD_DIM:(h + 1) * HEAD_DIM] for h in range(ATT_Q_PER_KV)], axis=0)
    lk = k_ref.shape[2]
    m = jnp.full((q.shape[0], 1), -jnp.inf, F32)
    acc = jnp.zeros((q.shape[0], LANES), F32)
    for lo in range(0, lk, ATT_KV_CHUNK):
        hi = min(lo + ATT_KV_CHUNK, lk)
        s = _dot_nt(q, k_ref[0, 0, lo:hi, :])
        m_new = jnp.maximum(m, jnp.max(s, axis=-1, keepdims=True))
        p = jnp.exp(s - m_new)
        acc = acc * jnp.exp(m - m_new) + _dot(p.astype(BF16), v_ref[0, 0, lo:hi, :])
        m = m_new
    o = acc[:, :HEAD_DIM] / acc[:, HEAD_DIM:HEAD_DIM + 1]
    o_ref[0] = jnp.concatenate([o[h * tq:(h + 1) * tq] for h in range(ATT_Q_PER_KV)], axis=1)


def _attention(q, k, v, tq):
    b, lq, wq = q.shape
    hkv, lk = k.shape[1], k.shape[2]
    wg = ATT_Q_PER_KV * HEAD_DIM
    return pl.pallas_call(
        _attn_kernel,
        grid=(b, hkv, lq // tq),
        in_specs=[pl.BlockSpec((1, tq, wg), lambda i, g, j: (i, j, g)),
                  pl.BlockSpec((1, 1, lk, HEAD_DIM), lambda i, g, j: (i, g, 0, 0)),
                  pl.BlockSpec((1, 1, lk, LANES), lambda i, g, j: (i, g, 0, 0))],
        out_specs=pl.BlockSpec((1, tq, wg), lambda i, g, j: (i, j, g)),
        out_shape=jax.ShapeDtypeStruct((b, lq, wq), F32),
        compiler_params=_cparams("arbitrary", "arbitrary", "arbitrary"),
        name="attention",
    )(q, k, v)


def _ret_kernel(lg_ref, q_ref, k_ref, v_ref, g_ref, s0f_ref, s0b_ref, lgf_ref, lgb_ref, gn_ref,
                o_ref, sff_ref, sfb_ref, s_scr, sb_scr, dec_scr):
    i, p, j = pl.program_id(0), pl.program_id(1), pl.program_id(2)
    nc = pl.num_programs(2)
    c = k_ref.shape[1]
    w = k_ref.shape[2]
    pos = lax.broadcasted_iota(jnp.int32, (c, w), 0).astype(F32)
    lgf = lgf_ref[...]
    lgb = lgb_ref[...]

    @pl.when((i == 0) & (p == 0) & (j == 0))
    def _():
        diff = (lax.broadcasted_iota(jnp.int32, (c, c), 0) - lax.broadcasted_iota(jnp.int32, (c, c), 1)).astype(F32)
        for h in range(RET_HEADS):
            dec_scr[h] = jnp.where(diff >= 0.0, jnp.exp(lg_ref[0, h] * jnp.maximum(diff, 0.0)),
                                   jnp.exp(lg_ref[1, h] * jnp.maximum(-diff, 0.0)))

    nb = k_ref.shape[0]

    def update_state(e, kzt, vh, cdec):
        for h in range(RET_HEADS):
            lo, hi = h * HEAD_DIM, (h + 1) * HEAD_DIM
            s_scr[e, h] = cdec[:, lo:hi] * s_scr[e, h] + _dot(kzt[lo:hi, :], vh[:, lo:hi])

    @pl.when(p == 0)
    def _():
        @pl.when(j == 0)
        def _():
            s_scr[...] = s0b_ref[...]

        sb_scr[nc - 1 - j] = s_scr[...]
        for e in range(nb):
            kzt = (k_ref[e] * jnp.exp(lgb * pos)).T.astype(BF16)
            update_state(e, kzt, v_ref[e].astype(BF16), jnp.exp(lgb * float(c)))

        @pl.when(j == nc - 1)
        def _():
            sfb_ref[...] = s_scr[...]

    @pl.when(p == 1)
    def _():
        @pl.when(j == 0)
        def _():
            s_scr[...] = s0f_ref[...]

        for e in range(nb):
            q = q_ref[e]
            k = k_ref[e]
            qf = (q * jnp.exp(lgf * (pos + 1.0))).astype(BF16)
            qb = (q * jnp.exp(lgb * (float(c) - pos))).astype(BF16)
            kzt = (k * jnp.exp(lgf * (float(c) - 1.0 - pos))).T.astype(BF16)
            qh = q.astype(BF16)
            kh = k.astype(BF16)
            vh = v_ref[e].astype(BF16)
            outs = []
            for h in range(RET_HEADS):
                lo, hi = h * HEAD_DIM, (h + 1) * HEAD_DIM
                sc = _dot_nt(qh[:, lo:hi], kh[:, lo:hi])
                y = (_dot((sc * dec_scr[h]).astype(BF16), vh[:, lo:hi])
                     + _dot(qf[:, lo:hi], s_scr[e, h].astype(BF16))
                     + _dot(qb[:, lo:hi], sb_scr[j, e, h].astype(BF16)))
                mu = jnp.mean(y, axis=-1, keepdims=True)
                yc = y - mu
                var = jnp.mean(yc * yc, axis=-1, keepdims=True)
                outs.append(yc * lax.rsqrt(var + EPS))
            update_state(e, kzt, vh, jnp.exp(lgf * float(c)))
            gate = g_ref[e]
            o_ref[e] = jnp.concatenate(outs, axis=1) * gn_ref[...] * (gate * jax.nn.sigmoid(gate))

        @pl.when(j == nc - 1)
        def _():
            sff_ref[...] = s_scr[...]


def _retention(q, k, v, gate, s0_f, s0_b, lg, lgf_row, lgb_row, gn_row):
    b, l, w = q.shape
    c = min(RET_CHUNK, l)
    nc = l // c
    nb = RET_BATCH if b % RET_BATCH == 0 else 1
    st = (nb, RET_HEADS, HEAD_DIM, HEAD_DIM)
    st_spec = pl.BlockSpec(st, lambda i, p, j: (i, 0, 0, 0))
    row = pl.BlockSpec((1, w), lambda i, p, j: (0, 0))
    both = pl.BlockSpec((nb, c, w), lambda i, p, j: (i, jnp.where(p == 0, nc - 1 - j, j), 0))
    fwd_only = pl.BlockSpec((nb, c, w), lambda i, p, j: (i, p * j, 0))
    state_shape = jax.ShapeDtypeStruct((b,) + st[1:], F32)
    return pl.pallas_call(
        _ret_kernel,
        grid=(b // nb, 2, nc),
        in_specs=[pl.BlockSpec(memory_space=pltpu.SMEM), fwd_only, both, both, fwd_only,
                  st_spec, st_spec, row, row, row],
        out_specs=[fwd_only, st_spec, st_spec],
        out_shape=[jax.ShapeDtypeStruct((b, l, w), F32), state_shape, state_shape],
        scratch_shapes=[pltpu.VMEM(st, F32), pltpu.VMEM((nc,) + st, F32), pltpu.VMEM((RET_HEADS, c, c), F32)],
        compiler_params=_cparams("arbitrary", "arbitrary", "arbitrary"),
        name="retention",
    )(lg, q, k, v, gate, s0_f, s0_b, lgf_row, lgb_row, gn_row)


def _layer_norm(x, g, b):
    mu = jnp.mean(x, axis=-1, keepdims=True)
    xc = x - mu
    var = jnp.mean(xc * xc, axis=-1, keepdims=True)
    return xc * lax.rsqrt(var + EPS) * g + b


def _store_rows_packed(ref, x):
    tm, d = x.shape
    bits = pltpu.bitcast(x.astype(BF16).astype(F32), jnp.uint32)
    word = (bits[:, :d // 2] >> 16) | bits[:, d // 2:]
    for s in range(PACKED_ROWS):
        ref[pl.ds(s, tm, stride=PACKED_ROWS), :] = word[:, s * LANES:(s + 1) * LANES]


def _load_rows_packed(ref, tm):
    word = jnp.concatenate([ref[pl.ds(s, tm, stride=PACKED_ROWS), :] for s in range(PACKED_ROWS)], axis=1)
    lo = pltpu.bitcast(word << 16, F32).astype(BF16)
    hi = pltpu.bitcast(word & jnp.uint32(0xFFFF0000), F32).astype(BF16)
    return jnp.concatenate([lo, hi], axis=1)


def _outproj_kernel(alpha, cu_ref, cup_ref, cun_ref, cb_ref, cw_ref, ret_ref, att_ref, x_ref, g1_ref,
                    w_ref, lng_ref, lnb_ref, sc_ref, sh_ref, wrh_ref, wrl_ref, br_ref, cnt0_ref, before_ref,
                    x1_ref, h2_ref, ri_ref, rw_ref, cnt_ref, run_scr):
    i = pl.program_id(0)
    j = pl.program_id(1)

    @pl.when((i == 0) & (j == 0))
    def _():
        run_scr[...] = cnt0_ref[...]

    t = cu_ref[0]
    tm = t.shape[0]
    prev_row = jnp.where(j == 0, 0.0, cup_ref[0][7:8, :])
    next_row = jnp.where(j == pl.num_programs(1) - 1, 0.0, cun_ref[0][0:1, :])
    row = lax.broadcasted_iota(jnp.int32, t.shape, 0)
    t_prev = jnp.where(row == 0, prev_row, pltpu.roll(t, 1, axis=0))
    t_next = jnp.where(row == tm - 1, next_row, pltpu.roll(t, tm - 1, axis=0))
    conv = cb_ref[0] * (t_prev * cw_ref[0] + t * cw_ref[1] + t_next * cw_ref[2])
    y = (_dot(conv.astype(BF16), w_ref[0:256, :])
         + _dot(ret_ref[0].astype(BF16), w_ref[256:512, :])
         + _dot(att_ref[0].astype(BF16), w_ref[512:1024, :]))
    x1 = _layer_norm(alpha * x_ref[0] + g1_ref[0] * y, lng_ref[...], lnb_ref[...])
    x1_ref[0] = x1
    h2 = x1 * sc_ref[0] + sh_ref[0]
    _store_rows_packed(h2_ref, h2)
    h_hi, h_lo = _split_bf16(h2)
    wrh = wrh_ref[...]
    logits = _dot(h_hi, wrh) + _dot(h_lo, wrh) + _dot(h_hi, wrl_ref[...]) + br_ref[...]

    lane = lax.broadcasted_iota(jnp.int32, logits.shape, 1)
    lane_f = lane.astype(F32)
    work = jnp.where(lane < N_EXPERTS, logits, -jnp.inf)
    vals, firsts, sels = [], [], []
    for _ in range(TOP_K):
        m = jnp.max(work, axis=-1, keepdims=True)
        first = jnp.min(jnp.where(work == m, lane_f, float(ROUTER_PAD)), axis=-1, keepdims=True)
        sel = lane_f == first
        vals.append(m)
        firsts.append(first)
        sels.append(sel)
        work = jnp.where(sel, -jnp.inf, work)
    exps = [jnp.exp(v - vals[0]) for v in vals]
    denom = exps[0]
    for e in exps[1:]:
        denom = denom + e

    cnt = jnp.zeros(logits.shape, F32)
    for sel in sels:
        cnt = cnt + jnp.where(sel, 1.0, 0.0)
    base = _dot(before_ref[...], cnt.astype(BF16)) + run_scr[...]
    run_scr[...] = run_scr[...] + jnp.sum(cnt, axis=0, keepdims=True)
    cnt_ref[...] = run_scr[...]

    ri = jnp.zeros(logits.shape, F32)
    rw = jnp.zeros(logits.shape, F32)
    for kk in range(TOP_K):
        rank = jnp.sum(jnp.where(sels[kk], base, 0.0), axis=-1, keepdims=True)
        ri = jnp.where(lane == kk, firsts[kk], ri)
        ri = jnp.where(lane == TOP_K + kk, rank, ri)
        rw = jnp.where(lane == kk, exps[kk] / denom, rw)
    ri_ref[0] = ri.astype(jnp.int32)
    rw_ref[0] = rw


def _outproj(alpha, cu, cb, cw, ret, att, x, g1, w_bf, lng, lnb, sc2, sh2, wrh, wrl, br, cnt0, tm):
    b, l, d = x.shape
    nj = l // tm
    tok = lambda w: pl.BlockSpec((1, tm, w), lambda i, j: (i, j, 0))
    vec = pl.BlockSpec((1, 1, d), lambda i, j: (i, 0, 0))
    full = lambda s: pl.BlockSpec(s, lambda i, j: (0,) * len(s))
    r8 = tm // 8
    nb8 = l // 8
    tri = jnp.arange(tm)
    before = (tri[None, :] < tri[:, None]).astype(BF16)
    return pl.pallas_call(
        functools.partial(_outproj_kernel, alpha),
        grid=(b, nj),
        in_specs=[tok(256),
                  pl.BlockSpec((1, 8, 256), lambda i, j: (i, jnp.maximum(j * r8 - 1, 0), 0)),
                  pl.BlockSpec((1, 8, 256), lambda i, j: (i, jnp.minimum((j + 1) * r8, nb8 - 1), 0)),
                  tok(256), full((3, 1, 256)), tok(256), tok(512), tok(d), vec,
                  full((d, d)), full((1, d)), full((1, d)), vec, vec,
                  full((d, ROUTER_PAD)), full((d, ROUTER_PAD)), full((1, ROUTER_PAD)), full((1, ROUTER_PAD)),
                  full((tm, tm))],
        out_specs=[tok(d), pl.BlockSpec((tm * PACKED_ROWS, LANES), lambda i, j: (i * nj + j, 0)),
                   tok(ROUTER_PAD), tok(ROUTER_PAD), full((1, ROUTER_PAD))],
        out_shape=[jax.ShapeDtypeStruct((b, l, d), F32),
                   jax.ShapeDtypeStruct((b * l * PACKED_ROWS, LANES), jnp.uint32),
                   jax.ShapeDtypeStruct((b, l, ROUTER_PAD), jnp.int32),
                   jax.ShapeDtypeStruct((b, l, ROUTER_PAD), F32),
                   jax.ShapeDtypeStruct((1, ROUTER_PAD), F32)],
        scratch_shapes=[pltpu.VMEM((1, ROUTER_PAD), F32)],
        compiler_params=_cparams("arbitrary", "arbitrary"),
        name="outproj_ln_router",
    )(cu, cu, cu, cb, cw, ret, att, x, g1, w_bf, lng, lnb, sc2, sh2, wrh, wrl, br, cnt0, before)


def _ffn_kernel(be_ref, nv_ref, x_ref, wgu_ref, bgu_ref, wd_ref, bd_ref, o_ref, wgu_scr, wd_scr):
    j = pl.program_id(0)
    f = wd_ref.shape[2]
    tm = x_ref.shape[0] // PACKED_ROWS
    part = min(FFN_ROWS, tm)

    @pl.when(nv_ref[j] > 0)
    def _():
        @pl.when((j == 0) | (be_ref[j] != be_ref[jnp.maximum(j - 1, 0)]))
        def _():
            wgu_scr[...] = wgu_ref[0, 0].astype(BF16)
            wd_scr[...] = wd_ref[0, 0].astype(BF16)

    bgu = bgu_ref[0, 0]
    for r0 in range(0, tm, part):
        @pl.when(nv_ref[j] > r0)
        def _():
            rows = pl.ds(r0 * PACKED_ROWS, part * PACKED_ROWS)
            x = _load_rows_packed(x_ref.at[rows], part)
            acts = []
            for lo in range(0, f, FFN_BLOCK):
                hi = lo + FFN_BLOCK
                gate = jnp.minimum(_dot(x, wgu_scr[:, lo:hi]) + bgu[:, lo:hi], SWIGLU_LIMIT)
                up = jnp.clip(_dot(x, wgu_scr[:, f + lo:f + hi]) + bgu[:, f + lo:f + hi],
                              -SWIGLU_LIMIT, SWIGLU_LIMIT)
                acts.append(((up + 1.0) * (gate * jax.nn.sigmoid(SWIGLU_ALPHA * gate))).astype(BF16))
            act = jnp.concatenate(acts, axis=1)
            _store_rows_packed(o_ref.at[rows], _dot(act, wd_scr[...]) + bd_ref[0, 0])


def _expert_ffn(layer, block_e, n_valid, xs, wgu, bgu, wd, bd, tm):
    n_rows = xs.shape[0] // PACKED_ROWS
    depth, ne, d, f2 = wgu.shape
    f = f2 // 2
    rows = pl.BlockSpec((tm * PACKED_ROWS, LANES), lambda j, be, nu: (j, 0))
    grid_spec = pltpu.PrefetchScalarGridSpec(
        num_scalar_prefetch=2,
        grid=(n_rows // tm,),
        in_specs=[rows,
                  pl.BlockSpec((1, 1, d, f2), lambda j, be, nu: (layer, be[j], 0, 0)),
                  pl.BlockSpec((1, 1, 1, f2), lambda j, be, nu: (layer, be[j], 0, 0)),
                  pl.BlockSpec((1, 1, f, d), lambda j, be, nu: (layer, be[j], 0, 0)),
                  pl.BlockSpec((1, 1, 1, d), lambda j, be, nu: (layer, be[j], 0, 0))],
        out_specs=rows,
        scratch_shapes=[pltpu.VMEM((d, f2), BF16), pltpu.VMEM((f, d), BF16)],
    )
    return pl.pallas_call(
        _ffn_kernel,
        grid_spec=grid_spec,
        out_shape=jax.ShapeDtypeStruct((n_rows * PACKED_ROWS, LANES), jnp.uint32),
        compiler_params=_cparams("arbitrary"),
        name="expert_ffn",
    )(block_e, n_valid, xs, wgu, bgu.reshape(depth, ne, 1, f2), wd, bd.reshape(depth, ne, 1, d))


def _sc_worker_base(per_worker):
    return (lax.axis_index("s") * SC_CORES + lax.axis_index("c")) * per_worker


def _sc_dispatch(rows, dest, n_out):
    t = rows.shape[0]
    kk = dest.shape[0] // t
    w = SC_WINDOW
    per_worker = t // (SC_CORES * SC_SUBCORES)
    assert per_worker * SC_CORES * SC_SUBCORES == t and per_worker % w == 0
    mesh = plsc.VectorSubcoreMesh(core_axis_name="c", subcore_axis_name="s")

    @functools.partial(
        pl.kernel, mesh=mesh,
        out_type=jax.ShapeDtypeStruct((n_out,) + rows.shape[1:], rows.dtype),
        scratch_types=[pltpu.VMEM((w,), jnp.int32)] * kk + [pltpu.VMEM((w,) + rows.shape[1:], rows.dtype)])
    def scatter_rows(r_hbm, d_hbm, o_hbm, *scratch):
        idx_v, rows_v = scratch[:kk], scratch[kk]
        base = _sc_worker_base(per_worker)

        @pl.loop(0, per_worker // w)
        def _(i):
            off = base + i * w
            pltpu.sync_copy(r_hbm.at[pl.ds(off, w)], rows_v)
            for s in range(kk):
                pltpu.sync_copy(d_hbm.at[pl.ds(s * t + off, w)], idx_v[s])
            for s in range(kk):
                pltpu.sync_copy(rows_v, o_hbm.at[idx_v[s]])

    return scatter_rows(rows, dest)


def _sc_gather(table, idx):
    n = idx.shape[0]
    w = SC_WINDOW
    per_worker = n // (SC_CORES * SC_SUBCORES)
    assert per_worker * SC_CORES * SC_SUBCORES == n and per_worker % w == 0
    mesh = plsc.VectorSubcoreMesh(core_axis_name="c", subcore_axis_name="s")

    @functools.partial(
        pl.kernel, mesh=mesh,
        out_type=jax.ShapeDtypeStruct((n,) + table.shape[1:], table.dtype),
        scratch_types=[pltpu.VMEM((w,), jnp.int32), pltpu.VMEM((w,) + table.shape[1:], table.dtype)])
    def gather_rows(t_hbm, i_hbm, o_hbm, idx_v, rows_v):
        base = _sc_worker_base(per_worker)

        @pl.loop(0, per_worker // w)
        def _(i):
            off = base + i * w
            pltpu.sync_copy(i_hbm.at[pl.ds(off, w)], idx_v)
            pltpu.sync_copy(t_hbm.at[idx_v], rows_v)
            pltpu.sync_copy(rows_v, o_hbm.at[pl.ds(off, w)])

    return gather_rows(table, idx)


def _combine_kernel(alpha, g_ref, w_ref, x_ref, g2_ref, lng_ref, lnb_ref, o_ref):
    tm = x_ref.shape[1]
    w = w_ref[0]
    f = _load_rows_packed(g_ref.at[0], tm).astype(F32) * w[:, 0:1]
    for kk in range(1, TOP_K):
        f = f + _load_rows_packed(g_ref.at[kk], tm).astype(F32) * w[:, kk:kk + 1]
    o_ref[0] = _layer_norm(alpha * x_ref[0] + g2_ref[0] * f, lng_ref[...], lnb_ref[...])


def _combine(alpha, gathered, tok_off, wts, x1, g2, lng, lnb, tm):
    b, l, d = x1.shape
    nj = l // tm
    assert tok_off % tm == 0
    blk_off = tok_off // tm
    tok = pl.BlockSpec((1, tm, d), lambda i, j: (i, j, 0))
    full = pl.BlockSpec((1, d), lambda i, j: (0, 0))
    return pl.pallas_call(
        functools.partial(_combine_kernel, alpha),
        grid=(b, nj),
        in_specs=[pl.BlockSpec((TOP_K, tm * PACKED_ROWS, LANES), lambda i, j: (0, blk_off + i * nj + j, 0)),
                  pl.BlockSpec((1, tm, ROUTER_PAD), lambda i, j: (i, j, 0)),
                  tok, pl.BlockSpec((1, 1, d), lambda i, j: (i, 0, 0)), full, full],
        out_specs=tok,
        out_shape=jax.ShapeDtypeStruct((b, l, d), F32),
        compiler_params=_cparams("arbitrary", "arbitrary"),
        name="combine_ln",
    )(gathered, wts, x1, g2, lng, lnb)


def _rope_tables(l):
    rows = l // GRID_W
    axis_dim = HEAD_DIM // 2
    inv_freq = ROPE_BASE ** (-jnp.arange(0, axis_dim, 2, dtype=F32) / axis_dim)
    row = jnp.repeat(jnp.arange(rows, dtype=F32), GRID_W)
    col = jnp.tile(jnp.arange(GRID_W, dtype=F32), rows)
    ang = jnp.stack([row[:, None] * inv_freq, col[:, None] * inv_freq], axis=1)
    cos, sin = jnp.cos(ang), jnp.sin(ang)
    cos64 = jnp.broadcast_to(cos[:, :, None, :], (l, 2, 2, HEAD_DIM // 4)).reshape(l, HEAD_DIM)
    sin64 = jnp.stack([-sin, sin], axis=2).reshape(l, HEAD_DIM)
    return jnp.tile(cos64, (1, LANES // HEAD_DIM)), jnp.tile(sin64, (1, LANES // HEAD_DIM))


def _route_tables(ri, counts, tm):
    t = ri.shape[0]
    top_e = ri[:, :TOP_K]
    rank = ri[:, TOP_K:2 * TOP_K]
    padded = (counts + tm - 1) // tm * tm
    pad_end = jnp.cumsum(padded)
    pad_start = pad_end - padded
    experts = jnp.arange(N_EXPERTS, dtype=jnp.int32)
    start = jnp.sum(jnp.where(top_e[:, :, None] == experts, pad_start, 0), axis=-1)
    dest = (start + rank).T.reshape(TOP_K * t)
    n_tiles = (t * TOP_K + N_EXPERTS * (tm - 1) + tm - 1) // tm
    tile_start = jnp.arange(n_tiles, dtype=jnp.int32) * tm
    block_e = jnp.minimum(jnp.sum((pad_end[None, :] <= tile_start[:, None]).astype(jnp.int32), axis=1), N_EXPERTS - 1)
    in_expert = jnp.where(block_e[:, None] == experts[None, :], (pad_start + counts)[None, :], 0).sum(axis=1)
    n_valid = jnp.clip(in_expert - tile_start, 0, tm)
    return dest, block_e.astype(jnp.int32), n_valid.astype(jnp.int32), n_tiles


def _layer_params(layer, d, w_in, conv_w, ret_decay_exp, ret_gn_g, q_norm_g, k_norm_g, w_out, ln_g, ln_b,
                  w_router, b_router):
    log_gamma = jnp.log1p(-jnp.exp2(-ret_decay_exp[layer].astype(F32)))
    lg_rows = jnp.repeat(log_gamma, HEAD_DIM, axis=1)
    wr_pad = jnp.zeros((d, ROUTER_PAD), F32).at[:, :N_EXPERTS].set(w_router[layer])
    wrh = wr_pad.astype(BF16)
    return dict(
        w_in=w_in[layer].astype(BF16), w_out=w_out[layer].astype(BF16),
        gq=jnp.tile(q_norm_g[layer], LANES // HEAD_DIM).reshape(1, LANES),
        gk=jnp.tile(k_norm_g[layer], LANES // HEAD_DIM).reshape(1, LANES),
        log_gamma=log_gamma, lgf_row=lg_rows[0:1], lgb_row=lg_rows[1:2], gn_row=ret_gn_g[layer].reshape(1, -1),
        cw=conv_w[layer].T.reshape(3, 1, -1), wrh=wrh, wrl=(wr_pad - wrh.astype(F32)).astype(BF16),
        br=jnp.zeros((1, ROUTER_PAD), F32).at[0, :N_EXPERTS].set(b_router[layer]),
        lng1=ln_g[layer, 0].reshape(1, d), lnb1=ln_b[layer, 0].reshape(1, d),
        lng2=ln_g[layer, 1].reshape(1, d), lnb2=ln_b[layer, 1].reshape(1, d))


def _layer(layer, last, alpha, x, ctx, m_lat, m_ctx, p, tables, experts):
    b, l, d = x.shape
    lc = ctx.shape[1]
    tm_lat, tm_ctx, tm_moe, tq = 512, 256, 1024, 256
    cos_l, sin_l, cos_c, sin_c, ones_bd, zero_state = tables
    sh1, sc1, g1, sh2, sc2, g2 = (m_lat[:, :, i] for i in range(N_MOD))
    sh1c, sc1c, g1c, sh2c, sc2c, g2c = (m_ctx[:, :, i] for i in range(N_MOD))

    zc = _inproj(ctx, 1.0 + sc1c, sh1c, p["w_in"], cos_c, sin_c, p["gq"], p["gk"], ones_bd, tm_ctx)
    zl = _inproj(x, 1.0 + sc1, sh1, p["w_in"], cos_l, sin_l, p["gq"], p["gk"], ones_bd, tm_lat)
    cu_c, cb_c, rq_c, rk_c, rv_c, rg_c, aq_c, ak_c, av_c = zc
    cu_l, cb_l, rq_l, rk_l, rv_l, rg_l, aq_l, ak_l, av_l = zl

    ret_args = (p["log_gamma"], p["lgf_row"], p["lgb_row"], p["gn_row"])
    ret_c, s_fwd, s_bwd = _retention(rq_c, rk_c, rv_c, rg_c, zero_state, zero_state, *ret_args)
    ret_l, _, _ = _retention(rq_l, rk_l, rv_l, rg_l, s_fwd, s_bwd, *ret_args)

    def kv_heads(a):
        return a.reshape(b, a.shape[1], -1, HEAD_DIM).transpose(0, 2, 1, 3)

    def v_heads(a):
        vh = kv_heads(a)
        pad = jnp.zeros(vh.shape[:3] + (LANES - HEAD_DIM - 1,), BF16)
        return jnp.concatenate([vh, jnp.ones(vh.shape[:3] + (1,), BF16), pad], axis=-1)

    k_all = kv_heads(jnp.concatenate([ak_c, ak_l], axis=1))
    v_all = v_heads(jnp.concatenate([av_c, av_l], axis=1))
    att_l = _attention(aq_l, k_all, v_all, tq)

    out_args = (p["w_out"], p["lng1"], p["lnb1"])
    rt_args = (p["wrh"], p["wrl"], p["br"])
    cnt0 = jnp.zeros((1, ROUTER_PAD), F32)
    if not last:
        att_c = _attention(aq_c, kv_heads(ak_c), v_heads(av_c), tq)
        ctx1, h2_c, ri_c, rw_c, cnt0 = _outproj(alpha, cu_c, cb_c, p["cw"], ret_c, att_c, ctx, g1c, *out_args,
                                                1.0 + sc2c, sh2c, *rt_args, cnt0, tm_ctx)
    x1, h2_l, ri_l, rw_l, cnt = _outproj(alpha, cu_l, cb_l, p["cw"], ret_l, att_l, x, g1, *out_args,
                                         1.0 + sc2, sh2, *rt_args, cnt0, tm_lat)
    if not last:
        n_c = b * lc
        h2 = jnp.concatenate([h2_c, h2_l], axis=0)
        ri = jnp.concatenate([ri_c.reshape(n_c, -1), ri_l.reshape(b * l, -1)], axis=0)
    else:
        n_c = 0
        h2 = h2_l
        ri = ri_l.reshape(b * l, -1)
    n_tok = n_c + b * l

    counts = cnt[0, :N_EXPERTS].astype(jnp.int32)
    dest, block_e, n_valid, n_tiles = _route_tables(ri, counts, tm_moe)
    xs = _sc_dispatch(h2.reshape(n_tok, PACKED_ROWS, LANES), dest, n_tiles * tm_moe)
    ys = _expert_ffn(layer, block_e, n_valid, xs.reshape(-1, LANES), *experts, tm_moe)
    gathered = _sc_gather(ys.reshape(-1, PACKED_ROWS, LANES), dest).reshape(TOP_K, n_tok * PACKED_ROWS, LANES)
    if not last:
        ctx = _combine(alpha, gathered, 0, rw_c, ctx1, g2c, p["lng2"], p["lnb2"], tm_ctx)
    x = _combine(alpha, gathered, n_c, rw_l, x1, g2, p["lng2"], p["lnb2"], tm_lat)
    return x, ctx


def kernel(x, c, ctx, c_ctx, w_mod, b_mod, w_in, conv_w, ret_decay_exp, ret_gn_g, q_norm_g, k_norm_g, w_out,
           ln_g, ln_b, w_router, b_router, w_gate_up, b_gate_up, w_down, b_down):
    depth = w_mod.shape[0]
    alpha = (2.0 * depth) ** 0.25
    b, l, d = x.shape
    lc = ctx.shape[1]
    groups = BATCH_GROUPS if b % BATCH_GROUPS == 0 else 1
    bg = b // groups

    n_rows = (b + 1 + 7) // 8 * 8
    c_all = jnp.zeros((n_rows, d), F32).at[:b].set(c).at[b].set(c_ctx)
    mod = _modulation(c_all, w_mod, b_mod)

    cos_l, sin_l = _rope_tables(l)
    cos_c, sin_c = jnp.ones((lc, LANES), F32), jnp.zeros((lc, LANES), F32)
    lane_head = jnp.arange(LANES) // HEAD_DIM
    ones_bd = (lane_head[:, None] == lane_head[None, :]).astype(BF16)
    zero_state = jnp.zeros((bg, RET_HEADS, HEAD_DIM, HEAD_DIM), F32)
    tables = (cos_l, sin_l, cos_c, sin_c, ones_bd, zero_state)
    experts = (w_gate_up, b_gate_up, w_down, b_down)

    xs = [x[g * bg:(g + 1) * bg] for g in range(groups)]
    cs = [ctx[g * bg:(g + 1) * bg] for g in range(groups)]
    for layer in range(depth):
        last = layer == depth - 1
        p = _layer_params(layer, d, w_in, conv_w, ret_decay_exp, ret_gn_g, q_norm_g, k_norm_g, w_out, ln_g, ln_b,
                          w_router, b_router)
        m_ctx = jnp.broadcast_to(mod[layer, b].reshape(1, 1, N_MOD, d), (bg, 1, N_MOD, d))
        for g in range(groups):
            m_lat = mod[layer, g * bg:(g + 1) * bg].reshape(bg, 1, N_MOD, d)
            xs[g], cs[g] = _layer(layer, last, alpha, xs[g], cs[g], m_lat, m_ctx, p, tables, experts)
    return jnp.concatenate(xs, axis=0) if groups > 1 else xs[0]
```

```python
import functools

import jax
import jax.numpy as jnp
from jax import lax
from jax.experimental import pallas as pl
from jax.experimental.pallas import tpu as pltpu
from jax.experimental.pallas import tpu_sc as plsc

F32 = jnp.float32
BF16 = jnp.bfloat16

HEAD_DIM = 64
GRID_W = 64
ROPE_BASE = 10000.0
N_EXPERTS = 32
TOP_K = 4
SWIGLU_ALPHA = 1.702
SWIGLU_LIMIT = 7.0
N_MOD = 6
EPS = 1e-6
RET_HEADS = 4
RET_CHUNK = 256
RET_BATCH = 2
ATT_Q_PER_KV = 4
ATT_KV_CHUNK = 256
ATT_Q_SCALE = HEAD_DIM ** -0.5 * 1.4426950408889634
ATT_STREAMS = 2
ATT_V_ROWS = 80
LANES = 128
ROUTER_PAD = 128
VMEM_LIMIT = 56 * 1024 * 1024
SUBLANES = 8
PACKED_ROWS = 4
FFN_BLOCK = 256
SC_CORES = 2
SC_SUBCORES = 16
SC_WINDOW = 64
BATCH_GROUPS = 1


def _dot(a, b):
    return jnp.dot(a, b, preferred_element_type=F32)


def _dot_nt(a, b):
    return lax.dot_general(a, b, (((1,), (1,)), ((), ())), preferred_element_type=F32)


def _split_bf16(x):
    hi = x.astype(BF16)
    lo = (x - hi.astype(F32)).astype(BF16)
    return hi, lo


def _cparams(*sem):
    return pltpu.CompilerParams(dimension_semantics=sem, vmem_limit_bytes=VMEM_LIMIT)


def _mod_kernel(c_ref, w_ref, b_ref, o_ref):
    c = c_ref[...]
    a = c * jax.nn.sigmoid(c)
    a_hi, a_lo = _split_bf16(a)
    w_hi, w_lo = _split_bf16(w_ref[0])
    o_ref[0] = _dot(a_hi, w_hi) + _dot(a_lo, w_hi) + _dot(a_hi, w_lo) + b_ref[0]


def _modulation(c_all, w_mod, b_mod):
    depth, d, n = w_mod.shape
    r = c_all.shape[0]
    tn = 1536
    return pl.pallas_call(
        _mod_kernel,
        grid=(depth, n // tn),
        in_specs=[
            pl.BlockSpec((r, d), lambda l, j: (0, 0)),
            pl.BlockSpec((1, d, tn), lambda l, j: (l, 0, j)),
            pl.BlockSpec((1, 1, tn), lambda l, j: (l, 0, j)),
        ],
        out_specs=pl.BlockSpec((1, r, tn), lambda l, j: (l, 0, j)),
        out_shape=jax.ShapeDtypeStruct((depth, r, n), F32),
        compiler_params=_cparams("arbitrary", "arbitrary"),
        name="modulation",
    )(c_all, w_mod, b_mod.reshape(depth, 1, n))


def _inproj_kernel(x_ref, sc_ref, sh_ref, w_ref, cos_ref, sin_ref, gq_ref, gk_ref, ones_ref,
                   cu_ref, cb_ref, rq_ref, rk_ref, rv_ref, rg_ref, aq_ref, ak_ref, av_ref):
    h = (x_ref[0] * sc_ref[0] + sh_ref[0]).astype(BF16)
    z = _dot(h, w_ref[...])
    cos = cos_ref[...]
    sin = sin_ref[...]
    ones = ones_ref[...]
    lane = lax.broadcasted_iota(jnp.int32, cos.shape, 1)
    first_half = (lane & 31) < 16

    def rope(xs):
        nxt = pltpu.roll(xs, LANES - 16, axis=1)
        prv = pltpu.roll(xs, 16, axis=1)
        return xs * cos + jnp.where(first_half, nxt, prv) * sin

    def rms(xs, g):
        s_hi, s_lo = _split_bf16(xs * xs)
        ssq = _dot(s_hi, ones) + _dot(s_lo, ones)
        return xs * lax.rsqrt(ssq * (1.0 / HEAD_DIM) + EPS) * g

    cu_ref[0] = z[:, 512:768] * z[:, 0:256]
    cb_ref[0] = z[:, 256:512]
    for j in range(2):
        lo, hi = j * LANES, (j + 1) * LANES
        rq_ref[0, :, lo:hi] = rope(z[:, 768 + lo:768 + hi])
        rk_ref[0, :, lo:hi] = rope(z[:, 1024 + lo:1024 + hi] * (HEAD_DIM ** -0.5))
    rv_ref[0] = z[:, 1280:1536]
    rg_ref[0] = z[:, 1536:1792]
    gq = gq_ref[...]
    for j in range(4):
        lo, hi = j * LANES, (j + 1) * LANES
        q = rope(rms(z[:, 1792 + lo:1792 + hi], gq)) * ATT_Q_SCALE
        aq_ref[0, lo:hi, :] = q.T.astype(BF16)
    ak_ref[0] = rope(rms(z[:, 2304:2432], gk_ref[...])).astype(BF16)
    av_ref[0] = z[:, 2432:2560].astype(BF16)


def _inproj(x, sc, sh, w_bf, cos, sin, gq, gk, ones_bd, tm):
    b, l, d = x.shape
    n = w_bf.shape[1]
    tok = lambda w: pl.BlockSpec((1, tm, w), lambda i, j: (i, j, 0))
    vec = pl.BlockSpec((1, 1, d), lambda i, j: (i, 0, 0))
    full = lambda s: pl.BlockSpec(s, lambda i, j: (0,) * len(s))
    widths = (256, 256, 256, 256, 256, 256, 512, 128, 128)
    dtypes = (F32,) * 6 + (BF16,) * 3
    out_specs = [tok(w) for w in widths]
    out_shape = [jax.ShapeDtypeStruct((b, l, w), dt) for w, dt in zip(widths, dtypes)]
    out_specs[6] = pl.BlockSpec((1, widths[6], tm), lambda i, j: (i, 0, j))
    out_shape[6] = jax.ShapeDtypeStruct((b, widths[6], l), BF16)
    return pl.pallas_call(
        _inproj_kernel,
        grid=(b, l // tm),
        in_specs=[tok(d), vec, vec, full((d, n)),
                  pl.BlockSpec((tm, LANES), lambda i, j: (j, 0)),
                  pl.BlockSpec((tm, LANES), lambda i, j: (j, 0)),
                  full((1, LANES)), full((1, LANES)), full((LANES, LANES))],
        out_specs=out_specs,
        out_shape=out_shape,
        compiler_params=_cparams("arbitrary", "arbitrary"),
        name="inproj",
    )(x, sc, sh, w_bf, cos, sin, gq, gk, ones_bd)


def _attn_kernel(q_ref, k_ref, vt_ref, o_ref):
    tq = q_ref.shape[2]
    q4 = q_ref[0]
    qt = jnp.concatenate([q4[h * HEAD_DIM:(h + 1) * HEAD_DIM, :] for h in range(ATT_Q_PER_KV)], axis=1)
    lk = k_ref.shape[2]
    bounds = [(lo, min(lo + ATT_KV_CHUNK, lk)) for lo in range(0, lk, ATT_KV_CHUNK)]
    r = qt.shape[1] // ATT_STREAMS
    qs = [qt[:, i * r:(i + 1) * r] for i in range(ATT_STREAMS)]
    ms = [jnp.full((1, r), -jnp.inf, F32) for _ in qs]
    accs = [jnp.zeros((ATT_V_ROWS, r), F32) for _ in qs]
    nxt = [_dot(k_ref[0, 0, bounds[0][0]:bounds[0][1], :], q) for q in qs]
    for c, (lo, hi) in enumerate(bounds):
        for i, q in enumerate(qs):
            s = nxt[i]
            if c + 1 < len(bounds):
                nxt[i] = _dot(k_ref[0, 0, bounds[c + 1][0]:bounds[c + 1][1], :], q)
            m_new = jnp.maximum(ms[i], jnp.max(s, axis=0, keepdims=True))
            p = jnp.exp2(s - m_new)
            accs[i] = accs[i] * jnp.exp2(ms[i] - m_new) + _dot(vt_ref[0, 0, :, lo:hi], p.astype(BF16))
            ms[i] = m_new
    acc = jnp.concatenate(accs, axis=1)
    o = acc[:HEAD_DIM] / acc[HEAD_DIM:HEAD_DIM + 1]
    for h in range(ATT_Q_PER_KV):
        o_ref[0, h] = o[:, h * tq:(h + 1) * tq]


def _attention(q, k, vt, tq):
    b, wq, lq = q.shape
    hkv, lk = k.shape[1], k.shape[2]
    wg = ATT_Q_PER_KV * HEAD_DIM
    return pl.pallas_call(
        _attn_kernel,
        grid=(b, hkv, lq // tq),
        in_specs=[pl.BlockSpec((1, wg, tq), lambda i, g, j: (i, g, j)),
                  pl.BlockSpec((1, 1, lk, HEAD_DIM), lambda i, g, j: (i, g, 0, 0)),
                  pl.BlockSpec((1, 1, ATT_V_ROWS, lk), lambda i, g, j: (i, g, 0, 0))],
        out_specs=pl.BlockSpec((1, ATT_Q_PER_KV, HEAD_DIM, tq), lambda i, g, j: (i, g, 0, j)),
        out_shape=jax.ShapeDtypeStruct((b, wq // HEAD_DIM, HEAD_DIM, lq), F32),
        compiler_params=_cparams("arbitrary", "arbitrary", "arbitrary"),
        name="attention",
    )(q, k, vt)


def _ret_kernel(lg_ref, q_ref, k_ref, v_ref, g_ref, s0f_ref, s0b_ref, lgf_ref, lgb_ref, gn_ref,
                o_ref, sff_ref, sfb_ref, s_scr, sb_scr, dec_scr):
    i, p, j = pl.program_id(0), pl.program_id(1), pl.program_id(2)
    nc = pl.num_programs(2)
    c = k_ref.shape[1]
    w = k_ref.shape[2]
    pos = lax.broadcasted_iota(jnp.int32, (c, w), 0).astype(F32)
    lgf = lgf_ref[...]
    lgb = lgb_ref[...]

    @pl.when((i == 0) & (p == 0) & (j == 0))
    def _():
        diff = (lax.broadcasted_iota(jnp.int32, (c, c), 0) - lax.broadcasted_iota(jnp.int32, (c, c), 1)).astype(F32)
        for h in range(RET_HEADS):
            dec_scr[h] = jnp.where(diff >= 0.0, jnp.exp(lg_ref[0, h] * jnp.maximum(diff, 0.0)),
                                   jnp.exp(lg_ref[1, h] * jnp.maximum(-diff, 0.0)))

    nb = k_ref.shape[0]

    def update_state(e, kzt, vh, cdec):
        for h in range(RET_HEADS):
            lo, hi = h * HEAD_DIM, (h + 1) * HEAD_DIM
            s_scr[e, h] = cdec[:, lo:hi] * s_scr[e, h] + _dot(kzt[lo:hi, :], vh[:, lo:hi])

    @pl.when(p == 0)
    def _():
        @pl.when(j == 0)
        def _():
            s_scr[...] = s0b_ref[...]

        sb_scr[nc - 1 - j] = s_scr[...]
        for e in range(nb):
            kzt = (k_ref[e] * jnp.exp(lgb * pos)).T.astype(BF16)
            update_state(e, kzt, v_ref[e].astype(BF16), jnp.exp(lgb * float(c)))

        @pl.when(j == nc - 1)
        def _():
            sfb_ref[...] = s_scr[...]

    @pl.when(p == 1)
    def _():
        @pl.when(j == 0)
        def _():
            s_scr[...] = s0f_ref[...]

        for e in range(nb):
            q = q_ref[e]
            k = k_ref[e]
            qf = (q * jnp.exp(lgf * (pos + 1.0))).astype(BF16)
            qb = (q * jnp.exp(lgb * (float(c) - pos))).astype(BF16)
            kzt = (k * jnp.exp(lgf * (float(c) - 1.0 - pos))).T.astype(BF16)
            qh = q.astype(BF16)
            kh = k.astype(BF16)
            vh = v_ref[e].astype(BF16)
            outs = []
            for h in range(RET_HEADS):
                lo, hi = h * HEAD_DIM, (h + 1) * HEAD_DIM
                sc = _dot_nt(qh[:, lo:hi], kh[:, lo:hi])
                y = (_dot((sc * dec_scr[h]).astype(BF16), vh[:, lo:hi])
                     + _dot(qf[:, lo:hi], s_scr[e, h].astype(BF16))
                     + _dot(qb[:, lo:hi], sb_scr[j, e, h].astype(BF16)))
                mu = jnp.mean(y, axis=-1, keepdims=True)
                yc = y - mu
                var = jnp.mean(yc * yc, axis=-1, keepdims=True)
                outs.append(yc * lax.rsqrt(var + EPS))
            update_state(e, kzt, vh, jnp.exp(lgf * float(c)))
            gate = g_ref[e]
            o_ref[e] = jnp.concatenate(outs, axis=1) * gn_ref[...] * (gate * jax.nn.sigmoid(gate))

        @pl.when(j == nc - 1)
        def _():
            sff_ref[...] = s_scr[...]


def _retention(q, k, v, gate, s0_f, s0_b, lg, lgf_row, lgb_row, gn_row):
    b, l, w = q.shape
    c = min(RET_CHUNK, l)
    nc = l // c
    nb = RET_BATCH if b % RET_BATCH == 0 else 1
    st = (nb, RET_HEADS, HEAD_DIM, HEAD_DIM)
    st_spec = pl.BlockSpec(st, lambda i, p, j: (i, 0, 0, 0))
    row = pl.BlockSpec((1, w), lambda i, p, j: (0, 0))
    both = pl.BlockSpec((nb, c, w), lambda i, p, j: (i, jnp.where(p == 0, nc - 1 - j, j), 0))
    fwd_only = pl.BlockSpec((nb, c, w), lambda i, p, j: (i, p * j, 0))
    state_shape = jax.ShapeDtypeStruct((b,) + st[1:], F32)
    return pl.pallas_call(
        _ret_kernel,
        grid=(b // nb, 2, nc),
        in_specs=[pl.BlockSpec(memory_space=pltpu.SMEM), fwd_only, both, both, fwd_only,
                  st_spec, st_spec, row, row, row],
        out_specs=[fwd_only, st_spec, st_spec],
        out_shape=[jax.ShapeDtypeStruct((b, l, w), F32), state_shape, state_shape],
        scratch_shapes=[pltpu.VMEM(st, F32), pltpu.VMEM((nc,) + st, F32), pltpu.VMEM((RET_HEADS, c, c), F32)],
        compiler_params=_cparams("arbitrary", "arbitrary", "arbitrary"),
        name="retention",
    )(lg, q, k, v, gate, s0_f, s0_b, lgf_row, lgb_row, gn_row)


def _layer_norm(x, g, b):
    mu = jnp.mean(x, axis=-1, keepdims=True)
    xc = x - mu
    var = jnp.mean(xc * xc, axis=-1, keepdims=True)
    return xc * lax.rsqrt(var + EPS) * g + b


def _store_rows_packed(ref, x):
    tm, d = x.shape
    bits = pltpu.bitcast(x.astype(BF16).astype(F32), jnp.uint32)
    word = (bits[:, :d // 2] >> 16) | bits[:, d // 2:]
    for s in range(PACKED_ROWS):
        ref[pl.ds(s, tm, stride=PACKED_ROWS), :] = word[:, s * LANES:(s + 1) * LANES]


def _load_rows_packed(ref, tm):
    word = jnp.concatenate([ref[pl.ds(s, tm, stride=PACKED_ROWS), :] for s in range(PACKED_ROWS)], axis=1)
    lo = pltpu.bitcast(word << 16, F32).astype(BF16)
    hi = pltpu.bitcast(word & jnp.uint32(0xFFFF0000), F32).astype(BF16)
    return jnp.concatenate([lo, hi], axis=1)


def _outproj_kernel(alpha, cu_ref, cup_ref, cun_ref, cb_ref, cw_ref, ret_ref, att_ref, x_ref, g1_ref,
                    w_ref, lng_ref, lnb_ref, sc_ref, sh_ref, wrh_ref, wrl_ref, br_ref, cnt0_ref, before_ref,
                    x1_ref, h2_ref, ri_ref, rw_ref, cnt_ref, run_scr):
    i = pl.program_id(0)
    j = pl.program_id(1)

    @pl.when((i == 0) & (j == 0))
    def _():
        run_scr[...] = cnt0_ref[...]

    t = cu_ref[0]
    tm = t.shape[0]
    prev_row = jnp.where(j == 0, 0.0, cup_ref[0][7:8, :])
    next_row = jnp.where(j == pl.num_programs(1) - 1, 0.0, cun_ref[0][0:1, :])
    row = lax.broadcasted_iota(jnp.int32, t.shape, 0)
    t_prev = jnp.where(row == 0, prev_row, pltpu.roll(t, 1, axis=0))
    t_next = jnp.where(row == tm - 1, next_row, pltpu.roll(t, tm - 1, axis=0))
    conv = cb_ref[0] * (t_prev * cw_ref[0] + t * cw_ref[1] + t_next * cw_ref[2])
    att_t = att_ref[0].reshape(att_ref.shape[1] * HEAD_DIM, tm).astype(BF16)
    y = (_dot(conv.astype(BF16), w_ref[0:256, :])
         + _dot(ret_ref[0].astype(BF16), w_ref[256:512, :])
         + lax.dot_general(att_t, w_ref[512:1024, :], (((0,), (0,)), ((), ())), preferred_element_type=F32))
    x1 = _layer_norm(alpha * x_ref[0] + g1_ref[0] * y, lng_ref[...], lnb_ref[...])
    x1_ref[0] = x1
    h2 = x1 * sc_ref[0] + sh_ref[0]
    _store_rows_packed(h2_ref, h2)
    h_hi, h_lo = _split_bf16(h2)
    wrh = wrh_ref[...]
    logits = _dot(h_hi, wrh) + _dot(h_lo, wrh) + _dot(h_hi, wrl_ref[...]) + br_ref[...]

    lane = lax.broadcasted_iota(jnp.int32, logits.shape, 1)
    lane_f = lane.astype(F32)
    work = jnp.where(lane < N_EXPERTS, logits, -jnp.inf)
    vals, firsts, sels = [], [], []
    for _ in range(TOP_K):
        m = jnp.max(work, axis=-1, keepdims=True)
        first = jnp.min(jnp.where(work == m, lane_f, float(ROUTER_PAD)), axis=-1, keepdims=True)
        sel = lane_f == first
        vals.append(m)
        firsts.append(first)
        sels.append(sel)
        work = jnp.where(sel, -jnp.inf, work)
    exps = [jnp.exp(v - vals[0]) for v in vals]
    denom = exps[0]
    for e in exps[1:]:
        denom = denom + e

    cnt = jnp.zeros(logits.shape, F32)
    for sel in sels:
        cnt = cnt + jnp.where(sel, 1.0, 0.0)
    base = _dot(before_ref[...], cnt.astype(BF16)) + run_scr[...]
    run_scr[...] = run_scr[...] + jnp.sum(cnt, axis=0, keepdims=True)
    cnt_ref[...] = run_scr[...]

    ri = jnp.zeros(logits.shape, F32)
    rw = jnp.zeros(logits.shape, F32)
    for kk in range(TOP_K):
        rank = jnp.sum(jnp.where(sels[kk], base, 0.0), axis=-1, keepdims=True)
        ri = jnp.where(lane == kk, firsts[kk], ri)
        ri = jnp.where(lane == TOP_K + kk, rank, ri)
        rw = jnp.where(lane == kk, exps[kk] / denom, rw)
    ri_ref[0] = ri.astype(jnp.int32)
    rw_ref[0] = rw


def _outproj(alpha, cu, cb, cw, ret, att, x, g1, w_bf, lng, lnb, sc2, sh2, wrh, wrl, br, cnt0, tm):
    b, l, d = x.shape
    nj = l // tm
    tok = lambda w: pl.BlockSpec((1, tm, w), lambda i, j: (i, j, 0))
    vec = pl.BlockSpec((1, 1, d), lambda i, j: (i, 0, 0))
    full = lambda s: pl.BlockSpec(s, lambda i, j: (0,) * len(s))
    r8 = tm // 8
    nb8 = l // 8
    tri = jnp.arange(tm)
    before = (tri[None, :] < tri[:, None]).astype(BF16)
    return pl.pallas_call(
        functools.partial(_outproj_kernel, alpha),
        grid=(b, nj),
        in_specs=[tok(256),
                  pl.BlockSpec((1, 8, 256), lambda i, j: (i, jnp.maximum(j * r8 - 1, 0), 0)),
                  pl.BlockSpec((1, 8, 256), lambda i, j: (i, jnp.minimum((j + 1) * r8, nb8 - 1), 0)),
                  tok(256), full((3, 1, 256)), tok(256),
                  pl.BlockSpec((1,) + att.shape[1:3] + (tm,), lambda i, j: (i, 0, 0, j)), tok(d), vec,
                  full((d, d)), full((1, d)), full((1, d)), vec, vec,
                  full((d, ROUTER_PAD)), full((d, ROUTER_PAD)), full((1, ROUTER_PAD)), full((1, ROUTER_PAD)),
                  full((tm, tm))],
        out_specs=[tok(d), pl.BlockSpec((tm * PACKED_ROWS, LANES), lambda i, j: (i * nj + j, 0)),
                   tok(ROUTER_PAD), tok(ROUTER_PAD), full((1, ROUTER_PAD))],
        out_shape=[jax.ShapeDtypeStruct((b, l, d), F32),
                   jax.ShapeDtypeStruct((b * l * PACKED_ROWS, LANES), jnp.uint32),
                   jax.ShapeDtypeStruct((b, l, ROUTER_PAD), jnp.int32),
                   jax.ShapeDtypeStruct((b, l, ROUTER_PAD), F32),
                   jax.ShapeDtypeStruct((1, ROUTER_PAD), F32)],
        scratch_shapes=[pltpu.VMEM((1, ROUTER_PAD), F32)],
        compiler_params=_cparams("arbitrary", "arbitrary"),
        name="outproj_ln_router",
    )(cu, cu, cu, cb, cw, ret, att, x, g1, w_bf, lng, lnb, sc2, sh2, wrh, wrl, br, cnt0, before)


def _ffn_kernel(be_ref, nv_ref, x_ref, wgu_ref, bgu_ref, wd_ref, bd_ref, o_ref, wgu_scr, wd_scr):
    j = pl.program_id(0)
    f = wd_ref.shape[2]
    tm = x_ref.shape[0] // PACKED_ROWS

    @pl.when(nv_ref[j] > 0)
    def _():
        @pl.when((j == 0) | (be_ref[j] != be_ref[jnp.maximum(j - 1, 0)]))
        def _():
            wgu_scr[...] = wgu_ref[0, 0].astype(BF16)
            wd_scr[...] = wd_ref[0, 0].astype(BF16)

        x = _load_rows_packed(x_ref, tm)
        bgu = bgu_ref[0, 0]
        acts = []
        for lo in range(0, f, FFN_BLOCK):
            hi = lo + FFN_BLOCK
            gate = jnp.minimum(_dot(x, wgu_scr[:, lo:hi]) + bgu[:, lo:hi], SWIGLU_LIMIT)
            up = jnp.clip(_dot(x, wgu_scr[:, f + lo:f + hi]) + bgu[:, f + lo:f + hi], -SWIGLU_LIMIT, SWIGLU_LIMIT)
            acts.append(((up + 1.0) * (gate * jax.nn.sigmoid(SWIGLU_ALPHA * gate))).astype(BF16))
        act = jnp.concatenate(acts, axis=1)
        _store_rows_packed(o_ref, _dot(act, wd_scr[...]) + bd_ref[0, 0])


def _expert_ffn(layer, block_e, n_valid, xs, wgu, bgu, wd, bd, tm):
    n_rows = xs.shape[0] // PACKED_ROWS
    depth, ne, d, f2 = wgu.shape
    f = f2 // 2
    rows = pl.BlockSpec((tm * PACKED_ROWS, LANES), lambda j, be, nu: (j, 0))
    grid_spec = pltpu.PrefetchScalarGridSpec(
        num_scalar_prefetch=2,
        grid=(n_rows // tm,),
        in_specs=[rows,
                  pl.BlockSpec((1, 1, d, f2), lambda j, be, nu: (layer, be[j], 0, 0)),
                  pl.BlockSpec((1, 1, 1, f2), lambda j, be, nu: (layer, be[j], 0, 0)),
                  pl.BlockSpec((1, 1, f, d), lambda j, be, nu: (layer, be[j], 0, 0)),
                  pl.BlockSpec((1, 1, 1, d), lambda j, be, nu: (layer, be[j], 0, 0))],
        out_specs=rows,
        scratch_shapes=[pltpu.VMEM((d, f2), BF16), pltpu.VMEM((f, d), BF16)],
    )
    return pl.pallas_call(
        _ffn_kernel,
        grid_spec=grid_spec,
        out_shape=jax.ShapeDtypeStruct((n_rows * PACKED_ROWS, LANES), jnp.uint32),
        compiler_params=_cparams("arbitrary"),
        name="expert_ffn",
    )(block_e, n_valid, xs, wgu, bgu.reshape(depth, ne, 1, f2), wd, bd.reshape(depth, ne, 1, d))


def _sc_worker_base(per_worker):
    return (lax.axis_index("s") * SC_CORES + lax.axis_index("c")) * per_worker


def _sc_dispatch(rows, dest, n_out):
    t = rows.shape[0]
    kk = dest.shape[0] // t
    w = SC_WINDOW
    per_worker = t // (SC_CORES * SC_SUBCORES)
    assert per_worker * SC_CORES * SC_SUBCORES == t and per_worker % w == 0
    mesh = plsc.VectorSubcoreMesh(core_axis_name="c", subcore_axis_name="s")

    @functools.partial(
        pl.kernel, mesh=mesh,
        out_type=jax.ShapeDtypeStruct((n_out,) + rows.shape[1:], rows.dtype),
        scratch_types=[pltpu.VMEM((w,), jnp.int32)] * kk + [pltpu.VMEM((w,) + rows.shape[1:], rows.dtype)])
    def scatter_rows(r_hbm, d_hbm, o_hbm, *scratch):
        idx_v, rows_v = scratch[:kk], scratch[kk]
        base = _sc_worker_base(per_worker)

        @pl.loop(0, per_worker // w)
        def _(i):
            off = base + i * w
            pltpu.sync_copy(r_hbm.at[pl.ds(off, w)], rows_v)
            for s in range(kk):
                pltpu.sync_copy(d_hbm.at[pl.ds(s * t + off, w)], idx_v[s])
            for s in range(kk):
                pltpu.sync_copy(rows_v, o_hbm.at[idx_v[s]])

    return scatter_rows(rows, dest)


def _sc_gather(table, idx):
    n = idx.shape[0]
    w = SC_WINDOW
    per_worker = n // (SC_CORES * SC_SUBCORES)
    assert per_worker * SC_CORES * SC_SUBCORES == n and per_worker % w == 0
    mesh = plsc.VectorSubcoreMesh(core_axis_name="c", subcore_axis_name="s")

    @functools.partial(
        pl.kernel, mesh=mesh,
        out_type=jax.ShapeDtypeStruct((n,) + table.shape[1:], table.dtype),
        scratch_types=[pltpu.VMEM((w,), jnp.int32), pltpu.VMEM((w,) + table.shape[1:], table.dtype)])
    def gather_rows(t_hbm, i_hbm, o_hbm, idx_v, rows_v):
        base = _sc_worker_base(per_worker)

        @pl.loop(0, per_worker // w)
        def _(i):
            off = base + i * w
            pltpu.sync_copy(i_hbm.at[pl.ds(off, w)], idx_v)
            pltpu.sync_copy(t_hbm.at[idx_v], rows_v)
            pltpu.sync_copy(rows_v, o_hbm.at[pl.ds(off, w)])

    return gather_rows(table, idx)


def _combine_kernel(alpha, g_ref, w_ref, x_ref, g2_ref, lng_ref, lnb_ref, o_ref):
    tm = x_ref.shape[1]
    w = w_ref[0]
    f = _load_rows_packed(g_ref.at[0], tm).astype(F32) * w[:, 0:1]
    for kk in range(1, TOP_K):
        f = f + _load_rows_packed(g_ref.at[kk], tm).astype(F32) * w[:, kk:kk + 1]
    o_ref[0] = _layer_norm(alpha * x_ref[0] + g2_ref[0] * f, lng_ref[...], lnb_ref[...])


def _combine(alpha, gathered, tok_off, wts, x1, g2, lng, lnb, tm):
    b, l, d = x1.shape
    nj = l // tm
    assert tok_off % tm == 0
    blk_off = tok_off // tm
    tok = pl.BlockSpec((1, tm, d), lambda i, j: (i, j, 0))
    full = pl.BlockSpec((1, d), lambda i, j: (0, 0))
    return pl.pallas_call(
        functools.partial(_combine_kernel, alpha),
        grid=(b, nj),
        in_specs=[pl.BlockSpec((TOP_K, tm * PACKED_ROWS, LANES), lambda i, j: (0, blk_off + i * nj + j, 0)),
                  pl.BlockSpec((1, tm, ROUTER_PAD), lambda i, j: (i, j, 0)),
                  tok, pl.BlockSpec((1, 1, d), lambda i, j: (i, 0, 0)), full, full],
        out_specs=tok,
        out_shape=jax.ShapeDtypeStruct((b, l, d), F32),
        compiler_params=_cparams("arbitrary", "arbitrary"),
        name="combine_ln",
    )(gathered, wts, x1, g2, lng, lnb)


def _rope_tables(l):
    rows = l // GRID_W
    axis_dim = HEAD_DIM // 2
    inv_freq = ROPE_BASE ** (-jnp.arange(0, axis_dim, 2, dtype=F32) / axis_dim)
    row = jnp.repeat(jnp.arange(rows, dtype=F32), GRID_W)
    col = jnp.tile(jnp.arange(GRID_W, dtype=F32), rows)
    ang = jnp.stack([row[:, None] * inv_freq, col[:, None] * inv_freq], axis=1)
    cos, sin = jnp.cos(ang), jnp.sin(ang)
    cos64 = jnp.broadcast_to(cos[:, :, None, :], (l, 2, 2, HEAD_DIM // 4)).reshape(l, HEAD_DIM)
    sin64 = jnp.stack([-sin, sin], axis=2).reshape(l, HEAD_DIM)
    return jnp.tile(cos64, (1, LANES // HEAD_DIM)), jnp.tile(sin64, (1, LANES // HEAD_DIM))


def _route_tables(ri, counts, tm):
    t = ri.shape[0]
    top_e = ri[:, :TOP_K]
    rank = ri[:, TOP_K:2 * TOP_K]
    padded = (counts + tm - 1) // tm * tm
    pad_end = jnp.cumsum(padded)
    pad_start = pad_end - padded
    experts = jnp.arange(N_EXPERTS, dtype=jnp.int32)
    start = jnp.sum(jnp.where(top_e[:, :, None] == experts, pad_start, 0), axis=-1)
    dest = (start + rank).T.reshape(TOP_K * t)
    n_tiles = (t * TOP_K + N_EXPERTS * (tm - 1) + tm - 1) // tm
    tile_start = jnp.arange(n_tiles, dtype=jnp.int32) * tm
    block_e = jnp.minimum(jnp.sum((pad_end[None, :] <= tile_start[:, None]).astype(jnp.int32), axis=1), N_EXPERTS - 1)
    in_expert = jnp.where(block_e[:, None] == experts[None, :], (pad_start + counts)[None, :], 0).sum(axis=1)
    n_valid = jnp.clip(in_expert - tile_start, 0, tm)
    return dest, block_e.astype(jnp.int32), n_valid.astype(jnp.int32), n_tiles


def _layer_params(layer, d, w_in, conv_w, ret_decay_exp, ret_gn_g, q_norm_g, k_norm_g, w_out, ln_g, ln_b,
                  w_router, b_router):
    log_gamma = jnp.log1p(-jnp.exp2(-ret_decay_exp[layer].astype(F32)))
    lg_rows = jnp.repeat(log_gamma, HEAD_DIM, axis=1)
    wr_pad = jnp.zeros((d, ROUTER_PAD), F32).at[:, :N_EXPERTS].set(w_router[layer])
    wrh = wr_pad.astype(BF16)
    return dict(
        w_in=w_in[layer].astype(BF16), w_out=w_out[layer].astype(BF16),
        gq=jnp.tile(q_norm_g[layer], LANES // HEAD_DIM).reshape(1, LANES),
        gk=jnp.tile(k_norm_g[layer], LANES // HEAD_DIM).reshape(1, LANES),
        log_gamma=log_gamma, lgf_row=lg_rows[0:1], lgb_row=lg_rows[1:2], gn_row=ret_gn_g[layer].reshape(1, -1),
        cw=conv_w[layer].T.reshape(3, 1, -1), wrh=wrh, wrl=(wr_pad - wrh.astype(F32)).astype(BF16),
        br=jnp.zeros((1, ROUTER_PAD), F32).at[0, :N_EXPERTS].set(b_router[layer]),
        lng1=ln_g[layer, 0].reshape(1, d), lnb1=ln_b[layer, 0].reshape(1, d),
        lng2=ln_g[layer, 1].reshape(1, d), lnb2=ln_b[layer, 1].reshape(1, d))


def _layer(layer, last, alpha, x, ctx, m_lat, m_ctx, p, tables, experts):
    b, l, d = x.shape
    lc = ctx.shape[1]
    tm_lat, tm_ctx, tm_moe, tq = 512, 256, 1024, 256
    cos_l, sin_l, cos_c, sin_c, ones_bd, zero_state = tables
    sh1, sc1, g1, sh2, sc2, g2 = (m_lat[:, :, i] for i in range(N_MOD))
    sh1c, sc1c, g1c, sh2c, sc2c, g2c = (m_ctx[:, :, i] for i in range(N_MOD))

    zc = _inproj(ctx, 1.0 + sc1c, sh1c, p["w_in"], cos_c, sin_c, p["gq"], p["gk"], ones_bd, tm_ctx)
    zl = _inproj(x, 1.0 + sc1, sh1, p["w_in"], cos_l, sin_l, p["gq"], p["gk"], ones_bd, tm_lat)
    cu_c, cb_c, rq_c, rk_c, rv_c, rg_c, aq_c, ak_c, av_c = zc
    cu_l, cb_l, rq_l, rk_l, rv_l, rg_l, aq_l, ak_l, av_l = zl

    ret_args = (p["log_gamma"], p["lgf_row"], p["lgb_row"], p["gn_row"])
    ret_c, s_fwd, s_bwd = _retention(rq_c, rk_c, rv_c, rg_c, zero_state, zero_state, *ret_args)
    ret_l, _, _ = _retention(rq_l, rk_l, rv_l, rg_l, s_fwd, s_bwd, *ret_args)

    def kv_heads(a):
        return a.reshape(b, a.shape[1], -1, HEAD_DIM).transpose(0, 2, 1, 3)

    def v_heads(a):
        vt = a.reshape(b, a.shape[1], -1, HEAD_DIM).transpose(0, 2, 3, 1)
        ones = jnp.ones(vt.shape[:2] + (1, vt.shape[3]), BF16)
        pad = jnp.zeros(vt.shape[:2] + (ATT_V_ROWS - HEAD_DIM - 1, vt.shape[3]), BF16)
        return jnp.concatenate([vt, ones, pad], axis=2)

    k_all = kv_heads(jnp.concatenate([ak_c, ak_l], axis=1))
    v_all = v_heads(jnp.concatenate([av_c, av_l], axis=1))
    att_l = _attention(aq_l, k_all, v_all, tq)

    out_args = (p["w_out"], p["lng1"], p["lnb1"])
    rt_args = (p["wrh"], p["wrl"], p["br"])
    cnt0 = jnp.zeros((1, ROUTER_PAD), F32)
    if not last:
        att_c = _attention(aq_c, kv_heads(ak_c), v_heads(av_c), tq)
        ctx1, h2_c, ri_c, rw_c, cnt0 = _outproj(alpha, cu_c, cb_c, p["cw"], ret_c, att_c, ctx, g1c, *out_args,
                                                1.0 + sc2c, sh2c, *rt_args, cnt0, tm_ctx)
    x1, h2_l, ri_l, rw_l, cnt = _outproj(alpha, cu_l, cb_l, p["cw"], ret_l, att_l, x, g1, *out_args,
                                         1.0 + sc2, sh2, *rt_args, cnt0, tm_lat)
    if not last:
        n_c = b * lc
        h2 = jnp.concatenate([h2_c, h2_l], axis=0)
        ri = jnp.concatenate([ri_c.reshape(n_c, -1), ri_l.reshape(b * l, -1)], axis=0)
    else:
        n_c = 0
        h2 = h2_l
        ri = ri_l.reshape(b * l, -1)
    n_tok = n_c + b * l

    counts = cnt[0, :N_EXPERTS].astype(jnp.int32)
    dest, block_e, n_valid, n_tiles = _route_tables(ri, counts, tm_moe)
    xs = _sc_dispatch(h2.reshape(n_tok, PACKED_ROWS, LANES), dest, n_tiles * tm_moe)
    ys = _expert_ffn(layer, block_e, n_valid, xs.reshape(-1, LANES), *experts, tm_moe)
    gathered = _sc_gather(ys.reshape(-1, PACKED_ROWS, LANES), dest).reshape(TOP_K, n_tok * PACKED_ROWS, LANES)
    if not last:
        ctx = _combine(alpha, gathered, 0, rw_c, ctx1, g2c, p["lng2"], p["lnb2"], tm_ctx)
    x = _combine(alpha, gathered, n_c, rw_l, x1, g2, p["lng2"], p["lnb2"], tm_lat)
    return x, ctx


def kernel(x, c, ctx, c_ctx, w_mod, b_mod, w_in, conv_w, ret_decay_exp, ret_gn_g, q_norm_g, k_norm_g, w_out,
           ln_g, ln_b, w_router, b_router, w_gate_up, b_gate_up, w_down, b_down):
    depth = w_mod.shape[0]
    alpha = (2.0 * depth) ** 0.25
    b, l, d = x.shape
    lc = ctx.shape[1]
    groups = BATCH_GROUPS if b % BATCH_GROUPS == 0 else 1
    bg = b // groups

    n_rows = (b + 1 + 7) // 8 * 8
    c_all = jnp.zeros((n_rows, d), F32).at[:b].set(c).at[b].set(c_ctx)
    mod = _modulation(c_all, w_mod, b_mod)

    cos_l, sin_l = _rope_tables(l)
    cos_c, sin_c = jnp.ones((lc, LANES), F32), jnp.zeros((lc, LANES), F32)
    lane_head = jnp.arange(LANES) // HEAD_DIM
    ones_bd = (lane_head[:, None] == lane_head[None, :]).astype(BF16)
    zero_state = jnp.zeros((bg, RET_HEADS, HEAD_DIM, HEAD_DIM), F32)
    tables = (cos_l, sin_l, cos_c, sin_c, ones_bd, zero_state)
    experts = (w_gate_up, b_gate_up, w_down, b_down)

    xs = [x[g * bg:(g + 1) * bg] for g in range(groups)]
    cs = [ctx[g * bg:(g + 1) * bg] for g in range(groups)]
    for layer in range(depth):
        last = layer == depth - 1
        p = _layer_params(layer, d, w_in, conv_w, ret_decay_exp, ret_gn_g, q_norm_g, k_norm_g, w_out, ln_g, ln_b,
                          w_router, b_router)
        m_ctx = jnp.broadcast_to(mod[layer, b].reshape(1, 1, N_MOD, d), (bg, 1, N_MOD, d))
        for g in range(groups):
            m_lat = mod[layer, g * bg:(g + 1) * bg].reshape(bg, 1, N_MOD, d)
            xs[g], cs[g] = _layer(layer, last, alpha, xs[g], cs[g], m_lat, m_ctx, p, tables, experts)
    return jnp.concatenate(xs, axis=0) if groups > 1 else xs[0]
```

```python
import functools

import jax
import jax.numpy as jnp
from jax import lax
from jax.experimental import pallas as pl
from jax.experimental.pallas import tpu as pltpu
from jax.experimental.pallas import tpu_sc as plsc

F32 = jnp.float32
BF16 = jnp.bfloat16

HEAD_DIM = 64
GRID_W = 64
ROPE_BASE = 10000.0
N_EXPERTS = 32
TOP_K = 4
SWIGLU_ALPHA = 1.702
SWIGLU_LIMIT = 7.0
N_MOD = 6
EPS = 1e-6
RET_HEADS = 4
RET_CHUNK = 256
RET_BATCH = 2
ATT_Q_PER_KV = 4
ATT_KV_CHUNK = 256
ATT_Q_SCALE = HEAD_DIM ** -0.5 * 1.4426950408889634
ATT_STREAMS = 2
ATT_V_ROWS = 80
LANES = 128
ROUTER_PAD = 128
VMEM_LIMIT = 56 * 1024 * 1024
SUBLANES = 8
PACKED_ROWS = 4
FFN_BLOCK = 256
OUT_STREAMS = 2
SC_CORES = 2
SC_SUBCORES = 16
SC_WINDOW = 64
BATCH_GROUPS = 1


def _dot(a, b):
    return jnp.dot(a, b, preferred_element_type=F32)


def _dot_nt(a, b):
    return lax.dot_general(a, b, (((1,), (1,)), ((), ())), preferred_element_type=F32)


def _split_bf16(x):
    hi = x.astype(BF16)
    lo = (x - hi.astype(F32)).astype(BF16)
    return hi, lo


def _cparams(*sem):
    return pltpu.CompilerParams(dimension_semantics=sem, vmem_limit_bytes=VMEM_LIMIT)


def _mod_kernel(c_ref, w_ref, b_ref, o_ref):
    c = c_ref[...]
    a = c * jax.nn.sigmoid(c)
    a_hi, a_lo = _split_bf16(a)
    w_hi, w_lo = _split_bf16(w_ref[0])
    o_ref[0] = _dot(a_hi, w_hi) + _dot(a_lo, w_hi) + _dot(a_hi, w_lo) + b_ref[0]


def _modulation(c_all, w_mod, b_mod):
    depth, d, n = w_mod.shape
    r = c_all.shape[0]
    tn = 1536
    return pl.pallas_call(
        _mod_kernel,
        grid=(depth, n // tn),
        in_specs=[
            pl.BlockSpec((r, d), lambda l, j: (0, 0)),
            pl.BlockSpec((1, d, tn), lambda l, j: (l, 0, j)),
            pl.BlockSpec((1, 1, tn), lambda l, j: (l, 0, j)),
        ],
        out_specs=pl.BlockSpec((1, r, tn), lambda l, j: (l, 0, j)),
        out_shape=jax.ShapeDtypeStruct((depth, r, n), F32),
        compiler_params=_cparams("arbitrary", "arbitrary"),
        name="modulation",
    )(c_all, w_mod, b_mod.reshape(depth, 1, n))


def _inproj_kernel(x_ref, sc_ref, sh_ref, w_ref, cos_ref, sin_ref, gq_ref, gk_ref, ones_ref,
                   cu_ref, cb_ref, rq_ref, rk_ref, rv_ref, rg_ref, aq_ref, ak_ref, av_ref):
    h = (x_ref[0] * sc_ref[0] + sh_ref[0]).astype(BF16)
    cos = cos_ref[...]
    sin = sin_ref[...]
    ones = ones_ref[...]
    lane = lax.broadcasted_iota(jnp.int32, cos.shape, 1)
    first_half = (lane & 31) < 16

    def rope(xs):
        nxt = pltpu.roll(xs, LANES - 16, axis=1)
        prv = pltpu.roll(xs, 16, axis=1)
        return xs * cos + jnp.where(first_half, nxt, prv) * sin

    def rms(xs, g):
        s_hi, s_lo = _split_bf16(xs * xs)
        ssq = _dot(s_hi, ones) + _dot(s_lo, ones)
        return xs * lax.rsqrt(ssq * (1.0 / HEAD_DIM) + EPS) * g

    za = _dot(h, w_ref[:, 1792:2560])
    gq = gq_ref[...]
    for j in range(4):
        lo, hi = j * LANES, (j + 1) * LANES
        q = rope(rms(za[:, lo:hi], gq)) * ATT_Q_SCALE
        aq_ref[0, lo:hi, :] = q.T.astype(BF16)
    ak_ref[0] = rope(rms(za[:, 512:640], gk_ref[...])).astype(BF16)
    av_ref[0] = za[:, 640:768].astype(BF16)
    zr = _dot(h, w_ref[:, 768:1792])
    for j in range(2):
        lo, hi = j * LANES, (j + 1) * LANES
        rq_ref[0, :, lo:hi] = rope(zr[:, lo:hi])
        rk_ref[0, :, lo:hi] = rope(zr[:, 256 + lo:256 + hi] * (HEAD_DIM ** -0.5))
    rv_ref[0] = zr[:, 512:768]
    rg_ref[0] = zr[:, 768:1024]
    zc = _dot(h, w_ref[:, 0:768])
    cu_ref[0] = zc[:, 512:768] * zc[:, 0:256]
    cb_ref[0] = zc[:, 256:512]


def _inproj(x, sc, sh, w_bf, cos, sin, gq, gk, ones_bd, tm):
    b, l, d = x.shape
    n = w_bf.shape[1]
    tok = lambda w: pl.BlockSpec((1, tm, w), lambda i, j: (i, j, 0))
    vec = pl.BlockSpec((1, 1, d), lambda i, j: (i, 0, 0))
    full = lambda s: pl.BlockSpec(s, lambda i, j: (0,) * len(s))
    widths = (256, 256, 256, 256, 256, 256, 512, 128, 128)
    dtypes = (F32,) * 6 + (BF16,) * 3
    out_specs = [tok(w) for w in widths]
    out_shape = [jax.ShapeDtypeStruct((b, l, w), dt) for w, dt in zip(widths, dtypes)]
    out_specs[6] = pl.BlockSpec((1, widths[6], tm), lambda i, j: (i, 0, j))
    out_shape[6] = jax.ShapeDtypeStruct((b, widths[6], l), BF16)
    return pl.pallas_call(
        _inproj_kernel,
        grid=(b, l // tm),
        in_specs=[tok(d), vec, vec, full((d, n)),
                  pl.BlockSpec((tm, LANES), lambda i, j: (j, 0)),
                  pl.BlockSpec((tm, LANES), lambda i, j: (j, 0)),
                  full((1, LANES)), full((1, LANES)), full((LANES, LANES))],
        out_specs=out_specs,
        out_shape=out_shape,
        compiler_params=_cparams("arbitrary", "arbitrary"),
        name="inproj",
    )(x, sc, sh, w_bf, cos, sin, gq, gk, ones_bd)


def _attn_kernel(q_ref, k_ref, vt_ref, o_ref):
    tq = q_ref.shape[2]
    q4 = q_ref[0]
    qt = jnp.concatenate([q4[h * HEAD_DIM:(h + 1) * HEAD_DIM, :] for h in range(ATT_Q_PER_KV)], axis=1)
    lk = k_ref.shape[2]
    bounds = [(lo, min(lo + ATT_KV_CHUNK, lk)) for lo in range(0, lk, ATT_KV_CHUNK)]
    r = qt.shape[1] // ATT_STREAMS
    qs = [qt[:, i * r:(i + 1) * r] for i in range(ATT_STREAMS)]
    ms = [jnp.full((1, r), -jnp.inf, F32) for _ in qs]
    accs = [jnp.zeros((ATT_V_ROWS, r), F32) for _ in qs]
    nxt = [_dot(k_ref[0, 0, bounds[0][0]:bounds[0][1], :], q) for q in qs]
    for c, (lo, hi) in enumerate(bounds):
        for i, q in enumerate(qs):
            s = nxt[i]
            if c + 1 < len(bounds):
                nxt[i] = _dot(k_ref[0, 0, bounds[c + 1][0]:bounds[c + 1][1], :], q)
            m_new = jnp.maximum(ms[i], jnp.max(s, axis=0, keepdims=True))
            p = jnp.exp2(s - m_new)
            accs[i] = accs[i] * jnp.exp2(ms[i] - m_new) + _dot(vt_ref[0, 0, :, lo:hi], p.astype(BF16))
            ms[i] = m_new
    acc = jnp.concatenate(accs, axis=1)
    o = acc[:HEAD_DIM] / acc[HEAD_DIM:HEAD_DIM + 1]
    for h in range(ATT_Q_PER_KV):
        o_ref[0, h] = o[:, h * tq:(h + 1) * tq]


def _attention(q, k, vt, tq):
    b, wq, lq = q.shape
    hkv, lk = k.shape[1], k.shape[2]
    wg = ATT_Q_PER_KV * HEAD_DIM
    return pl.pallas_call(
        _attn_kernel,
        grid=(b, hkv, lq // tq),
        in_specs=[pl.BlockSpec((1, wg, tq), lambda i, g, j: (i, g, j)),
                  pl.BlockSpec((1, 1, lk, HEAD_DIM), lambda i, g, j: (i, g, 0, 0)),
                  pl.BlockSpec((1, 1, ATT_V_ROWS, lk), lambda i, g, j: (i, g, 0, 0))],
        out_specs=pl.BlockSpec((1, ATT_Q_PER_KV, HEAD_DIM, tq), lambda i, g, j: (i, g, 0, j)),
        out_shape=jax.ShapeDtypeStruct((b, wq // HEAD_DIM, HEAD_DIM, lq), F32),
        compiler_params=_cparams("arbitrary", "arbitrary", "arbitrary"),
        name="attention",
    )(q, k, vt)


def _ret_kernel(lg_ref, q_ref, k_ref, v_ref, g_ref, s0f_ref, s0b_ref, lgf_ref, lgb_ref, gn_ref,
                o_ref, sff_ref, sfb_ref, s_scr, sb_scr, dec_scr):
    i, p, j = pl.program_id(0), pl.program_id(1), pl.program_id(2)
    nc = pl.num_programs(2)
    c = k_ref.shape[1]
    w = k_ref.shape[2]
    pos = lax.broadcasted_iota(jnp.int32, (c, w), 0).astype(F32)
    lgf = lgf_ref[...]
    lgb = lgb_ref[...]

    @pl.when((i == 0) & (p == 0) & (j == 0))
    def _():
        diff = (lax.broadcasted_iota(jnp.int32, (c, c), 0) - lax.broadcasted_iota(jnp.int32, (c, c), 1)).astype(F32)
        for h in range(RET_HEADS):
            dec_scr[h] = jnp.where(diff >= 0.0, jnp.exp(lg_ref[0, h] * jnp.maximum(diff, 0.0)),
                                   jnp.exp(lg_ref[1, h] * jnp.maximum(-diff, 0.0)))

    nb = k_ref.shape[0]

    def update_state(e, kzt, vh, cdec):
        for h in range(RET_HEADS):
            lo, hi = h * HEAD_DIM, (h + 1) * HEAD_DIM
            s_scr[e, h] = cdec[:, lo:hi] * s_scr[e, h] + _dot(kzt[lo:hi, :], vh[:, lo:hi])

    @pl.when(p == 0)
    def _():
        @pl.when(j == 0)
        def _():
            s_scr[...] = s0b_ref[...]

        sb_scr[nc - 1 - j] = s_scr[...]
        for e in range(nb):
            kzt = (k_ref[e] * jnp.exp(lgb * pos)).T.astype(BF16)
            update_state(e, kzt, v_ref[e].astype(BF16), jnp.exp(lgb * float(c)))

        @pl.when(j == nc - 1)
        def _():
            sfb_ref[...] = s_scr[...]

    @pl.when(p == 1)
    def _():
        @pl.when(j == 0)
        def _():
            s_scr[...] = s0f_ref[...]

        for e in range(nb):
            q = q_ref[e]
            k = k_ref[e]
            qf = (q * jnp.exp(lgf * (pos + 1.0))).astype(BF16)
            qb = (q * jnp.exp(lgb * (float(c) - pos))).astype(BF16)
            kzt = (k * jnp.exp(lgf * (float(c) - 1.0 - pos))).T.astype(BF16)
            qh = q.astype(BF16)
            kh = k.astype(BF16)
            vh = v_ref[e].astype(BF16)
            outs = []
            for h in range(RET_HEADS):
                lo, hi = h * HEAD_DIM, (h + 1) * HEAD_DIM
                sc = _dot_nt(qh[:, lo:hi], kh[:, lo:hi])
                y = (_dot((sc * dec_scr[h]).astype(BF16), vh[:, lo:hi])
                     + _dot(qf[:, lo:hi], s_scr[e, h].astype(BF16))
                     + _dot(qb[:, lo:hi], sb_scr[j, e, h].astype(BF16)))
                mu = jnp.mean(y, axis=-1, keepdims=True)
                yc = y - mu
                var = jnp.mean(yc * yc, axis=-1, keepdims=True)
                outs.append(yc * lax.rsqrt(var + EPS))
            update_state(e, kzt, vh, jnp.exp(lgf * float(c)))
            gate = g_ref[e]
            o_ref[e] = jnp.concatenate(outs, axis=1) * gn_ref[...] * (gate * jax.nn.sigmoid(gate))

        @pl.when(j == nc - 1)
        def _():
            sff_ref[...] = s_scr[...]


def _retention(q, k, v, gate, s0_f, s0_b, lg, lgf_row, lgb_row, gn_row):
    b, l, w = q.shape
    c = min(RET_CHUNK, l)
    nc = l // c
    nb = RET_BATCH if b % RET_BATCH == 0 else 1
    st = (nb, RET_HEADS, HEAD_DIM, HEAD_DIM)
    st_spec = pl.BlockSpec(st, lambda i, p, j: (i, 0, 0, 0))
    row = pl.BlockSpec((1, w), lambda i, p, j: (0, 0))
    both = pl.BlockSpec((nb, c, w), lambda i, p, j: (i, jnp.where(p == 0, nc - 1 - j, j), 0))
    fwd_only = pl.BlockSpec((nb, c, w), lambda i, p, j: (i, p * j, 0))
    state_shape = jax.ShapeDtypeStruct((b,) + st[1:], F32)
    return pl.pallas_call(
        _ret_kernel,
        grid=(b // nb, 2, nc),
        in_specs=[pl.BlockSpec(memory_space=pltpu.SMEM), fwd_only, both, both, fwd_only,
                  st_spec, st_spec, row, row, row],
        out_specs=[fwd_only, st_spec, st_spec],
        out_shape=[jax.ShapeDtypeStruct((b, l, w), F32), state_shape, state_shape],
        scratch_shapes=[pltpu.VMEM(st, F32), pltpu.VMEM((nc,) + st, F32), pltpu.VMEM((RET_HEADS, c, c), F32)],
        compiler_params=_cparams("arbitrary", "arbitrary", "arbitrary"),
        name="retention",
    )(lg, q, k, v, gate, s0_f, s0_b, lgf_row, lgb_row, gn_row)


def _layer_norm(x, g, b):
    mu = jnp.mean(x, axis=-1, keepdims=True)
    xc = x - mu
    var = jnp.mean(xc * xc, axis=-1, keepdims=True)
    return xc * lax.rsqrt(var + EPS) * g + b


def _store_rows_packed(ref, x):
    tm, d = x.shape
    bits = pltpu.bitcast(x.astype(BF16).astype(F32), jnp.uint32)
    word = (bits[:, :d // 2] >> 16) | bits[:, d // 2:]
    for s in range(PACKED_ROWS):
        ref[pl.ds(s, tm, stride=PACKED_ROWS), :] = word[:, s * LANES:(s + 1) * LANES]


def _load_rows_packed(ref, tm):
    word = jnp.concatenate([ref[pl.ds(s, tm, stride=PACKED_ROWS), :] for s in range(PACKED_ROWS)], axis=1)
    lo = pltpu.bitcast(word << 16, F32).astype(BF16)
    hi = pltpu.bitcast(word & jnp.uint32(0xFFFF0000), F32).astype(BF16)
    return jnp.concatenate([lo, hi], axis=1)


def _outproj_kernel(alpha, cu_ref, cup_ref, cun_ref, cb_ref, cw_ref, ret_ref, att_ref, x_ref, g1_ref,
                    w_ref, lng_ref, lnb_ref, sc_ref, sh_ref, wr_ref, br_ref, cnt0_ref, before_ref,
                    x1_ref, h2_ref, ri_ref, rw_ref, cnt_ref, run_scr):
    i = pl.program_id(0)
    j = pl.program_id(1)

    @pl.when((i == 0) & (j == 0))
    def _():
        run_scr[...] = cnt0_ref[...]

    t = cu_ref[0]
    tm = t.shape[0]
    prev_row = jnp.where(j == 0, 0.0, cup_ref[0][7:8, :])
    next_row = jnp.where(j == pl.num_programs(1) - 1, 0.0, cun_ref[0][0:1, :])
    row = lax.broadcasted_iota(jnp.int32, t.shape, 0)
    t_prev = jnp.where(row == 0, prev_row, pltpu.roll(t, 1, axis=0))
    t_next = jnp.where(row == tm - 1, next_row, pltpu.roll(t, tm - 1, axis=0))
    conv = cb_ref[0] * (t_prev * cw_ref[0] + t * cw_ref[1] + t_next * cw_ref[2])
    conv = conv.astype(BF16)
    wrh = wr_ref[:, :ROUTER_PAD]
    rb = tm // OUT_STREAMS
    lane = lax.broadcasted_iota(jnp.int32, (rb, ROUTER_PAD), 1)
    lane_f = lane.astype(F32)
    routed = []
    for r0 in range(0, tm, rb):
        att_t = att_ref[0, :, :, r0:r0 + rb].reshape(att_ref.shape[1] * HEAD_DIM, rb).astype(BF16)
        y = (_dot(conv[r0:r0 + rb], w_ref[0:256, :])
             + _dot(ret_ref[0, r0:r0 + rb, :].astype(BF16), w_ref[256:512, :])
             + lax.dot_general(att_t, w_ref[512:1024, :], (((0,), (0,)), ((), ())), preferred_element_type=F32))
        x1 = _layer_norm(alpha * x_ref[0, r0:r0 + rb, :] + g1_ref[0] * y, lng_ref[...], lnb_ref[...])
        x1_ref[0, r0:r0 + rb, :] = x1
        h2 = x1 * sc_ref[0] + sh_ref[0]
        _store_rows_packed(h2_ref.at[pl.ds(r0 * PACKED_ROWS, rb * PACKED_ROWS)], h2)
        h_hi, h_lo = _split_bf16(h2)
        hw = _dot(h_hi, wr_ref[...])
        logits = hw[:, :ROUTER_PAD] + hw[:, ROUTER_PAD:] + _dot(h_lo, wrh) + br_ref[...]

        work = jnp.where(lane < N_EXPERTS, logits, -jnp.inf)
        vals, firsts, sels = [], [], []
        for _ in range(TOP_K):
            m = jnp.max(work, axis=-1, keepdims=True)
            first = jnp.min(jnp.where(work == m, lane_f, float(ROUTER_PAD)), axis=-1, keepdims=True)
            sel = lane_f == first
            vals.append(m)
            firsts.append(first)
            sels.append(sel)
            work = jnp.where(sel, -jnp.inf, work)
        exps = [jnp.exp(v - vals[0]) for v in vals]
        denom = exps[0]
        for e in exps[1:]:
            denom = denom + e
        cnt = jnp.zeros(logits.shape, F32)
        for sel in sels:
            cnt = cnt + jnp.where(sel, 1.0, 0.0)
        routed.append((firsts, sels, [e / denom for e in exps], cnt))

    cnt = jnp.concatenate([r[3] for r in routed], axis=0)
    base = _dot(before_ref[...], cnt.astype(BF16)) + run_scr[...]
    run_scr[...] = run_scr[...] + jnp.sum(cnt, axis=0, keepdims=True)
    cnt_ref[...] = run_scr[...]

    for blk, (firsts, sels, wts, _) in enumerate(routed):
        r0 = blk * rb
        ri = jnp.zeros((rb, ROUTER_PAD), F32)
        rw = jnp.zeros((rb, ROUTER_PAD), F32)
        for kk in range(TOP_K):
            rank = jnp.sum(jnp.where(sels[kk], base[r0:r0 + rb], 0.0), axis=-1, keepdims=True)
            ri = jnp.where(lane == kk, firsts[kk], ri)
            ri = jnp.where(lane == TOP_K + kk, rank, ri)
            rw = jnp.where(lane == kk, wts[kk], rw)
        ri_ref[0, r0:r0 + rb, :] = ri.astype(jnp.int32)
        rw_ref[0, r0:r0 + rb, :] = rw


def _outproj(alpha, cu, cb, cw, ret, att, x, g1, w_bf, lng, lnb, sc2, sh2, wr, br, cnt0, tm):
    b, l, d = x.shape
    nj = l // tm
    tok = lambda w: pl.BlockSpec((1, tm, w), lambda i, j: (i, j, 0))
    vec = pl.BlockSpec((1, 1, d), lambda i, j: (i, 0, 0))
    full = lambda s: pl.BlockSpec(s, lambda i, j: (0,) * len(s))
    r8 = tm // 8
    nb8 = l // 8
    tri = jnp.arange(tm)
    before = (tri[None, :] < tri[:, None]).astype(BF16)
    return pl.pallas_call(
        functools.partial(_outproj_kernel, alpha),
        grid=(b, nj),
        in_specs=[tok(256),
                  pl.BlockSpec((1, 8, 256), lambda i, j: (i, jnp.maximum(j * r8 - 1, 0), 0)),
                  pl.BlockSpec((1, 8, 256), lambda i, j: (i, jnp.minimum((j + 1) * r8, nb8 - 1), 0)),
                  tok(256), full((3, 1, 256)), tok(256),
                  pl.BlockSpec((1,) + att.shape[1:3] + (tm,), lambda i, j: (i, 0, 0, j)), tok(d), vec,
                  full((d, d)), full((1, d)), full((1, d)), vec, vec,
                  full((d, 2 * ROUTER_PAD)), full((1, ROUTER_PAD)), full((1, ROUTER_PAD)),
                  full((tm, tm))],
        out_specs=[tok(d), pl.BlockSpec((tm * PACKED_ROWS, LANES), lambda i, j: (i * nj + j, 0)),
                   tok(ROUTER_PAD), tok(ROUTER_PAD), full((1, ROUTER_PAD))],
        out_shape=[jax.ShapeDtypeStruct((b, l, d), F32),
                   jax.ShapeDtypeStruct((b * l * PACKED_ROWS, LANES), jnp.uint32),
                   jax.ShapeDtypeStruct((b, l, ROUTER_PAD), jnp.int32),
                   jax.ShapeDtypeStruct((b, l, ROUTER_PAD), F32),
                   jax.ShapeDtypeStruct((1, ROUTER_PAD), F32)],
        scratch_shapes=[pltpu.VMEM((1, ROUTER_PAD), F32)],
        compiler_params=_cparams("arbitrary", "arbitrary"),
        name="outproj_ln_router",
    )(cu, cu, cu, cb, cw, ret, att, x, g1, w_bf, lng, lnb, sc2, sh2, wr, br, cnt0, before)


def _ffn_kernel(be_ref, nv_ref, x_ref, wgu_ref, bgu_ref, wd_ref, bd_ref, o_ref, wgu_scr, wd_scr):
    j = pl.program_id(0)
    f = wd_ref.shape[2]
    tm = x_ref.shape[0] // PACKED_ROWS

    @pl.when(nv_ref[j] > 0)
    def _():
        @pl.when((j == 0) | (be_ref[j] != be_ref[jnp.maximum(j - 1, 0)]))
        def _():
            wgu_scr[...] = wgu_ref[0, 0].astype(BF16)
            wd_scr[...] = wd_ref[0, 0].astype(BF16)

        x = _load_rows_packed(x_ref, tm)
        bgu = bgu_ref[0, 0]
        acts = []
        for lo in range(0, f, FFN_BLOCK):
            hi = lo + FFN_BLOCK
            gate = jnp.minimum(_dot(x, wgu_scr[:, lo:hi]) + bgu[:, lo:hi], SWIGLU_LIMIT)
            up = jnp.clip(_dot(x, wgu_scr[:, f + lo:f + hi]) + bgu[:, f + lo:f + hi], -SWIGLU_LIMIT, SWIGLU_LIMIT)
            acts.append(((up + 1.0) * (gate * jax.nn.sigmoid(SWIGLU_ALPHA * gate))).astype(BF16))
        act = jnp.concatenate(acts, axis=1)
        _store_rows_packed(o_ref, _dot(act, wd_scr[...]) + bd_ref[0, 0])


def _expert_ffn(layer, block_e, n_valid, xs, wgu, bgu, wd, bd, tm):
    n_rows = xs.shape[0] // PACKED_ROWS
    depth, ne, d, f2 = wgu.shape
    f = f2 // 2
    rows = pl.BlockSpec((tm * PACKED_ROWS, LANES), lambda j, be, nu: (j, 0))
    grid_spec = pltpu.PrefetchScalarGridSpec(
        num_scalar_prefetch=2,
        grid=(n_rows // tm,),
        in_specs=[rows,
                  pl.BlockSpec((1, 1, d, f2), lambda j, be, nu: (layer, be[j], 0, 0)),
                  pl.BlockSpec((1, 1, 1, f2), lambda j, be, nu: (layer, be[j], 0, 0)),
                  pl.BlockSpec((1, 1, f, d), lambda j, be, nu: (layer, be[j], 0, 0)),
                  pl.BlockSpec((1, 1, 1, d), lambda j, be, nu: (layer, be[j], 0, 0))],
        out_specs=rows,
        scratch_shapes=[pltpu.VMEM((d, f2), BF16), pltpu.VMEM((f, d), BF16)],
    )
    return pl.pallas_call(
        _ffn_kernel,
        grid_spec=grid_spec,
        out_shape=jax.ShapeDtypeStruct((n_rows * PACKED_ROWS, LANES), jnp.uint32),
        compiler_params=_cparams("arbitrary"),
        name="expert_ffn",
    )(block_e, n_valid, xs, wgu, bgu.reshape(depth, ne, 1, f2), wd, bd.reshape(depth, ne, 1, d))


def _sc_worker_base(per_worker):
    return (lax.axis_index("s") * SC_CORES + lax.axis_index("c")) * per_worker


def _sc_dispatch(rows, dest, n_out):
    t = rows.shape[0]
    kk = dest.shape[0] // t
    w = SC_WINDOW
    per_worker = t // (SC_CORES * SC_SUBCORES)
    assert per_worker * SC_CORES * SC_SUBCORES == t and per_worker % w == 0
    mesh = plsc.VectorSubcoreMesh(core_axis_name="c", subcore_axis_name="s")

    @functools.partial(
        pl.kernel, mesh=mesh,
        out_type=jax.ShapeDtypeStruct((n_out,) + rows.shape[1:], rows.dtype),
        scratch_types=[pltpu.VMEM((w,), jnp.int32)] * kk + [pltpu.VMEM((w,) + rows.shape[1:], rows.dtype)])
    def scatter_rows(r_hbm, d_hbm, o_hbm, *scratch):
        idx_v, rows_v = scratch[:kk], scratch[kk]
        base = _sc_worker_base(per_worker)

        @pl.loop(0, per_worker // w)
        def _(i):
            off = base + i * w
            pltpu.sync_copy(r_hbm.at[pl.ds(off, w)], rows_v)
            for s in range(kk):
                pltpu.sync_copy(d_hbm.at[pl.ds(s * t + off, w)], idx_v[s])
            for s in range(kk):
                pltpu.sync_copy(rows_v, o_hbm.at[idx_v[s]])

    return scatter_rows(rows, dest)


def _sc_gather(table, idx):
    n = idx.shape[0]
    w = SC_WINDOW
    per_worker = n // (SC_CORES * SC_SUBCORES)
    assert per_worker * SC_CORES * SC_SUBCORES == n and per_worker % w == 0
    mesh = plsc.VectorSubcoreMesh(core_axis_name="c", subcore_axis_name="s")

    @functools.partial(
        pl.kernel, mesh=mesh,
        out_type=jax.ShapeDtypeStruct((n,) + table.shape[1:], table.dtype),
        scratch_types=[pltpu.VMEM((w,), jnp.int32), pltpu.VMEM((w,) + table.shape[1:], table.dtype)])
    def gather_rows(t_hbm, i_hbm, o_hbm, idx_v, rows_v):
        base = _sc_worker_base(per_worker)

        @pl.loop(0, per_worker // w)
        def _(i):
            off = base + i * w
            pltpu.sync_copy(i_hbm.at[pl.ds(off, w)], idx_v)
            pltpu.sync_copy(t_hbm.at[idx_v], rows_v)
            pltpu.sync_copy(rows_v, o_hbm.at[pl.ds(off, w)])

    return gather_rows(table, idx)


def _combine_kernel(alpha, g_ref, w_ref, x_ref, g2_ref, lng_ref, lnb_ref, o_ref):
    tm = x_ref.shape[1]
    w = w_ref[0]
    f = _load_rows_packed(g_ref.at[0], tm).astype(F32) * w[:, 0:1]
    for kk in range(1, TOP_K):
        f = f + _load_rows_packed(g_ref.at[kk], tm).astype(F32) * w[:, kk:kk + 1]
    o_ref[0] = _layer_norm(alpha * x_ref[0] + g2_ref[0] * f, lng_ref[...], lnb_ref[...])


def _combine(alpha, gathered, tok_off, wts, x1, g2, lng, lnb, tm):
    b, l, d = x1.shape
    nj = l // tm
    assert tok_off % tm == 0
    blk_off = tok_off // tm
    tok = pl.BlockSpec((1, tm, d), lambda i, j: (i, j, 0))
    full = pl.BlockSpec((1, d), lambda i, j: (0, 0))
    return pl.pallas_call(
        functools.partial(_combine_kernel, alpha),
        grid=(b, nj),
        in_specs=[pl.BlockSpec((TOP_K, tm * PACKED_ROWS, LANES), lambda i, j: (0, blk_off + i * nj + j, 0)),
                  pl.BlockSpec((1, tm, ROUTER_PAD), lambda i, j: (i, j, 0)),
                  tok, pl.BlockSpec((1, 1, d), lambda i, j: (i, 0, 0)), full, full],
        out_specs=tok,
        out_shape=jax.ShapeDtypeStruct((b, l, d), F32),
        compiler_params=_cparams("arbitrary", "arbitrary"),
        name="combine_ln",
    )(gathered, wts, x1, g2, lng, lnb)


def _rope_tables(l):
    rows = l // GRID_W
    axis_dim = HEAD_DIM // 2
    inv_freq = ROPE_BASE ** (-jnp.arange(0, axis_dim, 2, dtype=F32) / axis_dim)
    row = jnp.repeat(jnp.arange(rows, dtype=F32), GRID_W)
    col = jnp.tile(jnp.arange(GRID_W, dtype=F32), rows)
    ang = jnp.stack([row[:, None] * inv_freq, col[:, None] * inv_freq], axis=1)
    cos, sin = jnp.cos(ang), jnp.sin(ang)
    cos64 = jnp.broadcast_to(cos[:, :, None, :], (l, 2, 2, HEAD_DIM // 4)).reshape(l, HEAD_DIM)
    sin64 = jnp.stack([-sin, sin], axis=2).reshape(l, HEAD_DIM)
    return jnp.tile(cos64, (1, LANES // HEAD_DIM)), jnp.tile(sin64, (1, LANES // HEAD_DIM))


def _route_tables(ri, counts, tm):
    t = ri.shape[0]
    top_e = ri[:, :TOP_K]
    rank = ri[:, TOP_K:2 * TOP_K]
    padded = (counts + tm - 1) // tm * tm
    pad_end = jnp.cumsum(padded)
    pad_start = pad_end - padded
    experts = jnp.arange(N_EXPERTS, dtype=jnp.int32)
    start = jnp.sum(jnp.where(top_e[:, :, None] == experts, pad_start, 0), axis=-1)
    dest = (start + rank).T.reshape(TOP_K * t)
    n_tiles = (t * TOP_K + N_EXPERTS * (tm - 1) + tm - 1) // tm
    tile_start = jnp.arange(n_tiles, dtype=jnp.int32) * tm
    block_e = jnp.minimum(jnp.sum((pad_end[None, :] <= tile_start[:, None]).astype(jnp.int32), axis=1), N_EXPERTS - 1)
    in_expert = jnp.where(block_e[:, None] == experts[None, :], (pad_start + counts)[None, :], 0).sum(axis=1)
    n_valid = jnp.clip(in_expert - tile_start, 0, tm)
    return dest, block_e.astype(jnp.int32), n_valid.astype(jnp.int32), n_tiles


def _layer_params(layer, d, w_in, conv_w, ret_decay_exp, ret_gn_g, q_norm_g, k_norm_g, w_out, ln_g, ln_b,
                  w_router, b_router):
    log_gamma = jnp.log1p(-jnp.exp2(-ret_decay_exp[layer].astype(F32)))
    lg_rows = jnp.repeat(log_gamma, HEAD_DIM, axis=1)
    wr_pad = jnp.zeros((d, ROUTER_PAD), F32).at[:, :N_EXPERTS].set(w_router[layer])
    wrh = wr_pad.astype(BF16)
    return dict(
        w_in=w_in[layer].astype(BF16), w_out=w_out[layer].astype(BF16),
        gq=jnp.tile(q_norm_g[layer], LANES // HEAD_DIM).reshape(1, LANES),
        gk=jnp.tile(k_norm_g[layer], LANES // HEAD_DIM).reshape(1, LANES),
        log_gamma=log_gamma, lgf_row=lg_rows[0:1], lgb_row=lg_rows[1:2], gn_row=ret_gn_g[layer].reshape(1, -1),
        cw=conv_w[layer].T.reshape(3, 1, -1),
        wr=jnp.concatenate([wrh, (wr_pad - wrh.astype(F32)).astype(BF16)], axis=1),
        br=jnp.zeros((1, ROUTER_PAD), F32).at[0, :N_EXPERTS].set(b_router[layer]),
        lng1=ln_g[layer, 0].reshape(1, d), lnb1=ln_b[layer, 0].reshape(1, d),
        lng2=ln_g[layer, 1].reshape(1, d), lnb2=ln_b[layer, 1].reshape(1, d))


def _layer(layer, last, alpha, x, ctx, m_lat, m_ctx, p, tables, experts):
    b, l, d = x.shape
    lc = ctx.shape[1]
    tm_lat, tm_ctx, tm_moe, tq = 512, 256, 1024, 256
    cos_l, sin_l, cos_c, sin_c, ones_bd, zero_state = tables
    sh1, sc1, g1, sh2, sc2, g2 = (m_lat[:, :, i] for i in range(N_MOD))
    sh1c, sc1c, g1c, sh2c, sc2c, g2c = (m_ctx[:, :, i] for i in range(N_MOD))

    zc = _inproj(ctx, 1.0 + sc1c, sh1c, p["w_in"], cos_c, sin_c, p["gq"], p["gk"], ones_bd, tm_ctx)
    zl = _inproj(x, 1.0 + sc1, sh1, p["w_in"], cos_l, sin_l, p["gq"], p["gk"], ones_bd, tm_lat)
    cu_c, cb_c, rq_c, rk_c, rv_c, rg_c, aq_c, ak_c, av_c = zc
    cu_l, cb_l, rq_l, rk_l, rv_l, rg_l, aq_l, ak_l, av_l = zl

    ret_args = (p["log_gamma"], p["lgf_row"], p["lgb_row"], p["gn_row"])
    ret_c, s_fwd, s_bwd = _retention(rq_c, rk_c, rv_c, rg_c, zero_state, zero_state, *ret_args)
    ret_l, _, _ = _retention(rq_l, rk_l, rv_l, rg_l, s_fwd, s_bwd, *ret_args)

    def kv_heads(a):
        return a.reshape(b, a.shape[1], -1, HEAD_DIM).transpose(0, 2, 1, 3)

    def v_heads(a):
        vt = a.reshape(b, a.shape[1], -1, HEAD_DIM).transpose(0, 2, 3, 1)
        ones = jnp.ones(vt.shape[:2] + (1, vt.shape[3]), BF16)
        pad = jnp.zeros(vt.shape[:2] + (ATT_V_ROWS - HEAD_DIM - 1, vt.shape[3]), BF16)
        return jnp.concatenate([vt, ones, pad], axis=2)

    k_all = kv_heads(jnp.concatenate([ak_c, ak_l], axis=1))
    v_all = v_heads(jnp.concatenate([av_c, av_l], axis=1))
    att_l = _attention(aq_l, k_all, v_all, tq)

    out_args = (p["w_out"], p["lng1"], p["lnb1"])
    rt_args = (p["wr"], p["br"])
    cnt0 = jnp.zeros((1, ROUTER_PAD), F32)
    if not last:
        att_c = _attention(aq_c, kv_heads(ak_c), v_heads(av_c), tq)
        ctx1, h2_c, ri_c, rw_c, cnt0 = _outproj(alpha, cu_c, cb_c, p["cw"], ret_c, att_c, ctx, g1c, *out_args,
                                                1.0 + sc2c, sh2c, *rt_args, cnt0, tm_ctx)
    x1, h2_l, ri_l, rw_l, cnt = _outproj(alpha, cu_l, cb_l, p["cw"], ret_l, att_l, x, g1, *out_args,
                                         1.0 + sc2, sh2, *rt_args, cnt0, tm_lat)
    if not last:
        n_c = b * lc
        h2 = jnp.concatenate([h2_c, h2_l], axis=0)
        ri = jnp.concatenate([ri_c.reshape(n_c, -1), ri_l.reshape(b * l, -1)], axis=0)
    else:
        n_c = 0
        h2 = h2_l
        ri = ri_l.reshape(b * l, -1)
    n_tok = n_c + b * l

    counts = cnt[0, :N_EXPERTS].astype(jnp.int32)
    dest, block_e, n_valid, n_tiles = _route_tables(ri, counts, tm_moe)
    xs = _sc_dispatch(h2.reshape(n_tok, PACKED_ROWS, LANES), dest, n_tiles * tm_moe)
    ys = _expert_ffn(layer, block_e, n_valid, xs.reshape(-1, LANES), *experts, tm_moe)
    gathered = _sc_gather(ys.reshape(-1, PACKED_ROWS, LANES), dest).reshape(TOP_K, n_tok * PACKED_ROWS, LANES)
    if not last:
        ctx = _combine(alpha, gathered, 0, rw_c, ctx1, g2c, p["lng2"], p["lnb2"], tm_ctx)
    x = _combine(alpha, gathered, n_c, rw_l, x1, g2, p["lng2"], p["lnb2"], tm_lat)
    return x, ctx


def kernel(x, c, ctx, c_ctx, w_mod, b_mod, w_in, conv_w, ret_decay_exp, ret_gn_g, q_norm_g, k_norm_g, w_out,
           ln_g, ln_b, w_router, b_router, w_gate_up, b_gate_up, w_down, b_down):
    depth = w_mod.shape[0]
    alpha = (2.0 * depth) ** 0.25
    b, l, d = x.shape
    lc = ctx.shape[1]
    groups = BATCH_GROUPS if b % BATCH_GROUPS == 0 else 1
    bg = b // groups

    n_rows = (b + 1 + 7) // 8 * 8
    c_all = jnp.zeros((n_rows, d), F32).at[:b].set(c).at[b].set(c_ctx)
    mod = _modulation(c_all, w_mod, b_mod)

    cos_l, sin_l = _rope_tables(l)
    cos_c, sin_c = jnp.ones((lc, LANES), F32), jnp.zeros((lc, LANES), F32)
    lane_head = jnp.arange(LANES) // HEAD_DIM
    ones_bd = (lane_head[:, None] == lane_head[None, :]).astype(BF16)
    zero_state = jnp.zeros((bg, RET_HEADS, HEAD_DIM, HEAD_DIM), F32)
    tables = (cos_l, sin_l, cos_c, sin_c, ones_bd, zero_state)
    experts = (w_gate_up, b_gate_up, w_down, b_down)

    xs = [x[g * bg:(g + 1) * bg] for g in range(groups)]
    cs = [ctx[g * bg:(g + 1) * bg] for g in range(groups)]
    for layer in range(depth):
        last = layer == depth - 1
        p = _layer_params(layer, d, w_in, conv_w, ret_decay_exp, ret_gn_g, q_norm_g, k_norm_g, w_out, ln_g, ln_b,
                          w_router, b_router)
        m_ctx = jnp.broadcast_to(mod[layer, b].reshape(1, 1, N_MOD, d), (bg, 1, N_MOD, d))
        for g in range(groups):
            m_lat = mod[layer, g * bg:(g + 1) * bg].reshape(bg, 1, N_MOD, d)
            xs[g], cs[g] = _layer(layer, last, alpha, xs[g], cs[g], m_lat, m_ctx, p, tables, experts)
    return jnp.concatenate(xs, axis=0) if groups > 1 else xs[0]
```

```python
import functools

import jax
import jax.numpy as jnp
from jax import lax
from jax.experimental import pallas as pl
from jax.experimental.pallas import tpu as pltpu
from jax.experimental.pallas import tpu_sc as plsc

F32 = jnp.float32
BF16 = jnp.bfloat16

HEAD_DIM = 64
GRID_W = 64
ROPE_BASE = 10000.0
N_EXPERTS = 32
TOP_K = 4
SWIGLU_ALPHA = 1.702
SWIGLU_LIMIT = 7.0
N_MOD = 6
EPS = 1e-6
RET_HEADS = 4
RET_CHUNK = 256
RET_BATCH = 2
ATT_Q_PER_KV = 4
ATT_KV_CHUNK = 256
ATT_Q_SCALE = HEAD_DIM ** -0.5 * 1.4426950408889634
ATT_STREAMS = 2
ATT_V_ROWS = 80
LANES = 128
ROUTER_PAD = 128
VMEM_LIMIT = 56 * 1024 * 1024
SUBLANES = 8
PACKED_ROWS = 4
FFN_BLOCK = 256
OUT_STREAMS = 2
SC_CORES = 2
SC_SUBCORES = 16
SC_WINDOW = 64
BATCH_GROUPS = 1


def _dot(a, b):
    return jnp.dot(a, b, preferred_element_type=F32)


def _dot_nt(a, b):
    return lax.dot_general(a, b, (((1,), (1,)), ((), ())), preferred_element_type=F32)


def _split_bf16(x):
    hi = x.astype(BF16)
    lo = (x - hi.astype(F32)).astype(BF16)
    return hi, lo


def _cparams(*sem):
    return pltpu.CompilerParams(dimension_semantics=sem, vmem_limit_bytes=VMEM_LIMIT)


def _mod_kernel(c_ref, w_ref, b_ref, o_ref):
    c = c_ref[...]
    a = c * jax.nn.sigmoid(c)
    a_hi, a_lo = _split_bf16(a)
    w_hi, w_lo = _split_bf16(w_ref[0])
    o_ref[0] = _dot(a_hi, w_hi) + _dot(a_lo, w_hi) + _dot(a_hi, w_lo) + b_ref[0]


def _modulation(c_all, w_mod, b_mod):
    depth, d, n = w_mod.shape
    r = c_all.shape[0]
    tn = 1536
    return pl.pallas_call(
        _mod_kernel,
        grid=(depth, n // tn),
        in_specs=[
            pl.BlockSpec((r, d), lambda l, j: (0, 0)),
            pl.BlockSpec((1, d, tn), lambda l, j: (l, 0, j)),
            pl.BlockSpec((1, 1, tn), lambda l, j: (l, 0, j)),
        ],
        out_specs=pl.BlockSpec((1, r, tn), lambda l, j: (l, 0, j)),
        out_shape=jax.ShapeDtypeStruct((depth, r, n), F32),
        compiler_params=_cparams("arbitrary", "arbitrary"),
        name="modulation",
    )(c_all, w_mod, b_mod.reshape(depth, 1, n))


def _inproj_kernel(x_ref, sc_ref, sh_ref, w_ref, cos_ref, sin_ref, gq_ref, gk_ref, ones_ref,
                   cu_ref, cb_ref, rq_ref, rk_ref, rv_ref, rg_ref, aq_ref, ak_ref, av_ref):
    h = (x_ref[0] * sc_ref[0] + sh_ref[0]).astype(BF16)
    cos = cos_ref[...]
    sin = sin_ref[...]
    ones = ones_ref[...]
    lane = lax.broadcasted_iota(jnp.int32, cos.shape, 1)
    first_half = (lane & 31) < 16

    def rope(xs):
        nxt = pltpu.roll(xs, LANES - 16, axis=1)
        prv = pltpu.roll(xs, 16, axis=1)
        return xs * cos + jnp.where(first_half, nxt, prv) * sin

    def rms(xs, g):
        s_hi, s_lo = _split_bf16(xs * xs)
        ssq = _dot(s_hi, ones) + _dot(s_lo, ones)
        return xs * lax.rsqrt(ssq * (1.0 / HEAD_DIM) + EPS) * g

    za = _dot(h, w_ref[:, 1792:2560])
    gq = gq_ref[...]
    for j in range(4):
        lo, hi = j * LANES, (j + 1) * LANES
        q = rope(rms(za[:, lo:hi], gq)) * ATT_Q_SCALE
        aq_ref[0, lo:hi, :] = q.T.astype(BF16)
    ak_ref[0] = rope(rms(za[:, 512:640], gk_ref[...])).astype(BF16)
    av_ref[0] = za[:, 640:768].astype(BF16)
    zr = _dot(h, w_ref[:, 768:1792])
    for j in range(2):
        lo, hi = j * LANES, (j + 1) * LANES
        rq_ref[0, :, lo:hi] = rope(zr[:, lo:hi])
        rk_ref[0, :, lo:hi] = rope(zr[:, 256 + lo:256 + hi] * (HEAD_DIM ** -0.5))
    rv_ref[0] = zr[:, 512:768]
    rg_ref[0] = zr[:, 768:1024]
    zc = _dot(h, w_ref[:, 0:768])
    cu_ref[0] = zc[:, 512:768] * zc[:, 0:256]
    cb_ref[0] = zc[:, 256:512]


def _inproj(x, sc, sh, w_bf, cos, sin, gq, gk, ones_bd, tm):
    b, l, d = x.shape
    n = w_bf.shape[1]
    tok = lambda w: pl.BlockSpec((1, tm, w), lambda i, j: (i, j, 0))
    vec = pl.BlockSpec((1, 1, d), lambda i, j: (i, 0, 0))
    full = lambda s: pl.BlockSpec(s, lambda i, j: (0,) * len(s))
    widths = (256, 256, 256, 256, 256, 256, 512, 128, 128)
    dtypes = (F32,) * 6 + (BF16,) * 3
    out_specs = [tok(w) for w in widths]
    out_shape = [jax.ShapeDtypeStruct((b, l, w), dt) for w, dt in zip(widths, dtypes)]
    out_specs[6] = pl.BlockSpec((1, widths[6], tm), lambda i, j: (i, 0, j))
    out_shape[6] = jax.ShapeDtypeStruct((b, widths[6], l), BF16)
    return pl.pallas_call(
        _inproj_kernel,
        grid=(b, l // tm),
        in_specs=[tok(d), vec, vec, full((d, n)),
                  pl.BlockSpec((tm, LANES), lambda i, j: (j, 0)),
                  pl.BlockSpec((tm, LANES), lambda i, j: (j, 0)),
                  full((1, LANES)), full((1, LANES)), full((LANES, LANES))],
        out_specs=out_specs,
        out_shape=out_shape,
        compiler_params=_cparams("arbitrary", "arbitrary"),
        name="inproj",
    )(x, sc, sh, w_bf, cos, sin, gq, gk, ones_bd)


def _attn_kernel(q_ref, k_ref, vt_ref, o_ref):
    tq = q_ref.shape[2]
    q4 = q_ref[0]
    qt = jnp.concatenate([q4[h * HEAD_DIM:(h + 1) * HEAD_DIM, :] for h in range(ATT_Q_PER_KV)], axis=1)
    lk = k_ref.shape[2]
    bounds = [(lo, min(lo + ATT_KV_CHUNK, lk)) for lo in range(0, lk, ATT_KV_CHUNK)]
    r = qt.shape[1] // ATT_STREAMS
    qs = [qt[:, i * r:(i + 1) * r] for i in range(ATT_STREAMS)]
    ms = [jnp.full((1, r), -jnp.inf, F32) for _ in qs]
    accs = [jnp.zeros((ATT_V_ROWS, r), F32) for _ in qs]
    nxt = [_dot(k_ref[0, 0, bounds[0][0]:bounds[0][1], :], q) for q in qs]
    for c, (lo, hi) in enumerate(bounds):
        for i, q in enumerate(qs):
            s = nxt[i]
            if c + 1 < len(bounds):
                nxt[i] = _dot(k_ref[0, 0, bounds[c + 1][0]:bounds[c + 1][1], :], q)
            m_new = jnp.maximum(ms[i], jnp.max(s, axis=0, keepdims=True))
            p = jnp.exp2(s - m_new)
            accs[i] = accs[i] * jnp.exp2(ms[i] - m_new) + _dot(vt_ref[0, 0, :, lo:hi], p.astype(BF16))
            ms[i] = m_new
    acc = jnp.concatenate(accs, axis=1)
    o = acc[:HEAD_DIM] / acc[HEAD_DIM:HEAD_DIM + 1]
    for h in range(ATT_Q_PER_KV):
        o_ref[0, h] = o[:, h * tq:(h + 1) * tq]


def _attention(q, k, vt, tq):
    b, wq, lq = q.shape
    hkv, lk = k.shape[1], k.shape[2]
    wg = ATT_Q_PER_KV * HEAD_DIM
    return pl.pallas_call(
        _attn_kernel,
        grid=(b, hkv, lq // tq),
        in_specs=[pl.BlockSpec((1, wg, tq), lambda i, g, j: (i, g, j)),
                  pl.BlockSpec((1, 1, lk, HEAD_DIM), lambda i, g, j: (i, g, 0, 0)),
                  pl.BlockSpec((1, 1, ATT_V_ROWS, lk), lambda i, g, j: (i, g, 0, 0))],
        out_specs=pl.BlockSpec((1, ATT_Q_PER_KV, HEAD_DIM, tq), lambda i, g, j: (i, g, 0, j)),
        out_shape=jax.ShapeDtypeStruct((b, wq // HEAD_DIM, HEAD_DIM, lq), F32),
        compiler_params=_cparams("arbitrary", "arbitrary", "arbitrary"),
        name="attention",
    )(q, k, vt)


def _ret_kernel(lg_ref, q_ref, k_ref, v_ref, g_ref, s0f_ref, s0b_ref, lgf_ref, lgb_ref, gn_ref,
                o_ref, sff_ref, sfb_ref, s_scr, sb_scr, dec_scr):
    i, p, j = pl.program_id(0), pl.program_id(1), pl.program_id(2)
    nc = pl.num_programs(2)
    c = k_ref.shape[1]
    w = k_ref.shape[2]
    pos = lax.broadcasted_iota(jnp.int32, (c, w), 0).astype(F32)
    lgf = lgf_ref[...]
    lgb = lgb_ref[...]

    @pl.when((i == 0) & (p == 0) & (j == 0))
    def _():
        diff = (lax.broadcasted_iota(jnp.int32, (c, c), 0) - lax.broadcasted_iota(jnp.int32, (c, c), 1)).astype(F32)
        for h in range(RET_HEADS):
            dec_scr[h] = jnp.where(diff >= 0.0, jnp.exp(lg_ref[0, h] * jnp.maximum(diff, 0.0)),
                                   jnp.exp(lg_ref[1, h] * jnp.maximum(-diff, 0.0)))

    nb = k_ref.shape[0]

    def update_state(e, kzt, vh, cdec):
        for h in range(RET_HEADS):
            lo, hi = h * HEAD_DIM, (h + 1) * HEAD_DIM
            s_scr[e, h] = cdec[:, lo:hi] * s_scr[e, h] + _dot(kzt[lo:hi, :], vh[:, lo:hi])

    @pl.when(p == 0)
    def _():
        @pl.when(j == 0)
        def _():
            s_scr[...] = s0b_ref[...]

        sb_scr[nc - 1 - j] = s_scr[...]
        for e in range(nb):
            kzt = (k_ref[e] * jnp.exp(lgb * pos)).T.astype(BF16)
            update_state(e, kzt, v_ref[e].astype(BF16), jnp.exp(lgb * float(c)))

        @pl.when(j == nc - 1)
        def _():
            sfb_ref[...] = s_scr[...]

    @pl.when(p == 1)
    def _():
        @pl.when(j == 0)
        def _():
            s_scr[...] = s0f_ref[...]

        for e in range(nb):
            q = q_ref[e]
            k = k_ref[e]
            qf = (q * jnp.exp(lgf * (pos + 1.0))).astype(BF16)
            qb = (q * jnp.exp(lgb * (float(c) - pos))).astype(BF16)
            kzt = (k * jnp.exp(lgf * (float(c) - 1.0 - pos))).T.astype(BF16)
            qh = q.astype(BF16)
            kh = k.astype(BF16)
            vh = v_ref[e].astype(BF16)
            outs = []
            for h in range(RET_HEADS):
                lo, hi = h * HEAD_DIM, (h + 1) * HEAD_DIM
                sc = _dot_nt(qh[:, lo:hi], kh[:, lo:hi])
                y = (_dot((sc * dec_scr[h]).astype(BF16), vh[:, lo:hi])
                     + _dot(qf[:, lo:hi], s_scr[e, h].astype(BF16))
                     + _dot(qb[:, lo:hi], sb_scr[j, e, h].astype(BF16)))
                mu = jnp.mean(y, axis=-1, keepdims=True)
                yc = y - mu
                var = jnp.mean(yc * yc, axis=-1, keepdims=True)
                outs.append(yc * lax.rsqrt(var + EPS))
            update_state(e, kzt, vh, jnp.exp(lgf * float(c)))
            gate = g_ref[e]
            o_ref[e] = jnp.concatenate(outs, axis=1) * gn_ref[...] * (gate * jax.nn.sigmoid(gate))

        @pl.when(j == nc - 1)
        def _():
            sff_ref[...] = s_scr[...]


def _retention(q, k, v, gate, s0_f, s0_b, lg, lgf_row, lgb_row, gn_row):
    b, l, w = q.shape
    c = min(RET_CHUNK, l)
    nc = l // c
    nb = RET_BATCH if b % RET_BATCH == 0 else 1
    st = (nb, RET_HEADS, HEAD_DIM, HEAD_DIM)
    st_spec = pl.BlockSpec(st, lambda i, p, j: (i, 0, 0, 0))
    row = pl.BlockSpec((1, w), lambda i, p, j: (0, 0))
    both = pl.BlockSpec((nb, c, w), lambda i, p, j: (i, jnp.where(p == 0, nc - 1 - j, j), 0))
    fwd_only = pl.BlockSpec((nb, c, w), lambda i, p, j: (i, p * j, 0))
    state_shape = jax.ShapeDtypeStruct((b,) + st[1:], F32)
    return pl.pallas_call(
        _ret_kernel,
        grid=(b // nb, 2, nc),
        in_specs=[pl.BlockSpec(memory_space=pltpu.SMEM), fwd_only, both, both, fwd_only,
                  st_spec, st_spec, row, row, row],
        out_specs=[fwd_only, st_spec, st_spec],
        out_shape=[jax.ShapeDtypeStruct((b, l, w), F32), state_shape, state_shape],
        scratch_shapes=[pltpu.VMEM(st, F32), pltpu.VMEM((nc,) + st, F32), pltpu.VMEM((RET_HEADS, c, c), F32)],
        compiler_params=_cparams("arbitrary", "arbitrary", "arbitrary"),
        name="retention",
    )(lg, q, k, v, gate, s0_f, s0_b, lgf_row, lgb_row, gn_row)


def _layer_norm(x, g, b):
    mu = jnp.mean(x, axis=-1, keepdims=True)
    xc = x - mu
    var = jnp.mean(xc * xc, axis=-1, keepdims=True)
    return xc * lax.rsqrt(var + EPS) * g + b


def _store_rows_packed(ref, x):
    tm, d = x.shape
    bits = pltpu.bitcast(x.astype(BF16).astype(F32), jnp.uint32)
    word = (bits[:, :d // 2] >> 16) | bits[:, d // 2:]
    for s in range(PACKED_ROWS):
        ref[pl.ds(s, tm, stride=PACKED_ROWS), :] = word[:, s * LANES:(s + 1) * LANES]


def _load_rows_packed(ref, tm):
    word = jnp.concatenate([ref[pl.ds(s, tm, stride=PACKED_ROWS), :] for s in range(PACKED_ROWS)], axis=1)
    lo = pltpu.bitcast(word << 16, F32).astype(BF16)
    hi = pltpu.bitcast(word & jnp.uint32(0xFFFF0000), F32).astype(BF16)
    return jnp.concatenate([lo, hi], axis=1)


def _outproj_kernel(alpha, cu_ref, cup_ref, cun_ref, cb_ref, cw_ref, ret_ref, att_ref, x_ref, g1_ref,
                    w_ref, lng_ref, lnb_ref, sc_ref, sh_ref, wr_ref, br_ref, cnt0_ref, before_ref,
                    x1_ref, h2_ref, ri_ref, rw_ref, cnt_ref, run_scr):
    i = pl.program_id(0)
    j = pl.program_id(1)

    @pl.when((i == 0) & (j == 0))
    def _():
        run_scr[...] = cnt0_ref[...]

    t = cu_ref[0]
    tm = t.shape[0]
    prev_row = jnp.where(j == 0, 0.0, cup_ref[0][7:8, :])
    next_row = jnp.where(j == pl.num_programs(1) - 1, 0.0, cun_ref[0][0:1, :])
    row = lax.broadcasted_iota(jnp.int32, t.shape, 0)
    t_prev = jnp.where(row == 0, prev_row, pltpu.roll(t, 1, axis=0))
    t_next = jnp.where(row == tm - 1, next_row, pltpu.roll(t, tm - 1, axis=0))
    conv = cb_ref[0] * (t_prev * cw_ref[0] + t * cw_ref[1] + t_next * cw_ref[2])
    conv = conv.astype(BF16)
    wrh = wr_ref[:, :ROUTER_PAD]
    rb = tm // OUT_STREAMS
    lane = lax.broadcasted_iota(jnp.int32, (rb, ROUTER_PAD), 1)
    lane_f = lane.astype(F32)
    routed = []
    for r0 in range(0, tm, rb):
        att_t = att_ref[0, :, :, r0:r0 + rb].reshape(att_ref.shape[1] * HEAD_DIM, rb).astype(BF16)
        y = (_dot(conv[r0:r0 + rb], w_ref[0:256, :])
             + _dot(ret_ref[0, r0:r0 + rb, :].astype(BF16), w_ref[256:512, :])
             + lax.dot_general(att_t, w_ref[512:1024, :], (((0,), (0,)), ((), ())), preferred_element_type=F32))
        x1 = _layer_norm(alpha * x_ref[0, r0:r0 + rb, :] + g1_ref[0] * y, lng_ref[...], lnb_ref[...])
        x1_ref[0, r0:r0 + rb, :] = x1
        h2 = x1 * sc_ref[0] + sh_ref[0]
        _store_rows_packed(h2_ref.at[pl.ds(r0 * PACKED_ROWS, rb * PACKED_ROWS)], h2)
        h_hi, h_lo = _split_bf16(h2)
        hw = _dot(h_hi, wr_ref[...])
        logits = hw[:, :ROUTER_PAD] + hw[:, ROUTER_PAD:] + _dot(h_lo, wrh) + br_ref[...]

        work = jnp.where(lane < N_EXPERTS, logits, -jnp.inf)
        vals, firsts, sels = [], [], []
        for _ in range(TOP_K):
            m = jnp.max(work, axis=-1, keepdims=True)
            first = jnp.min(jnp.where(work == m, lane_f, float(ROUTER_PAD)), axis=-1, keepdims=True)
            sel = lane_f == first
            vals.append(m)
            firsts.append(first)
            sels.append(sel)
            work = jnp.where(sel, -jnp.inf, work)
        exps = [jnp.exp(v - vals[0]) for v in vals]
        denom = exps[0]
        for e in exps[1:]:
            denom = denom + e
        cnt = jnp.zeros(logits.shape, F32)
        for sel in sels:
            cnt = cnt + jnp.where(sel, 1.0, 0.0)
        routed.append((firsts, sels, [e / denom for e in exps], cnt))

    cnt = jnp.concatenate([r[3] for r in routed], axis=0)
    base = _dot(before_ref[...], cnt.astype(BF16)) + run_scr[...]
    run_scr[...] = run_scr[...] + jnp.sum(cnt, axis=0, keepdims=True)
    cnt_ref[...] = run_scr[...]

    for blk, (firsts, sels, wts, _) in enumerate(routed):
        r0 = blk * rb
        ri = jnp.zeros((rb, ROUTER_PAD), F32)
        rw = jnp.zeros((rb, ROUTER_PAD), F32)
        for kk in range(TOP_K):
            rank = jnp.sum(jnp.where(sels[kk], base[r0:r0 + rb], 0.0), axis=-1, keepdims=True)
            ri = jnp.where(lane == kk, firsts[kk], ri)
            ri = jnp.where(lane == TOP_K + kk, rank, ri)
            rw = jnp.where(lane == kk, wts[kk], rw)
        ri_ref[0, r0:r0 + rb, :] = ri.astype(jnp.int32)
        rw_ref[0, r0:r0 + rb, :] = rw


def _outproj(alpha, cu, cb, cw, ret, att, x, g1, w_bf, lng, lnb, sc2, sh2, wr, br, cnt0, tm):
    b, l, d = x.shape
    nj = l // tm
    tok = lambda w: pl.BlockSpec((1, tm, w), lambda i, j: (i, j, 0))
    vec = pl.BlockSpec((1, 1, d), lambda i, j: (i, 0, 0))
    full = lambda s: pl.BlockSpec(s, lambda i, j: (0,) * len(s))
    r8 = tm // 8
    nb8 = l // 8
    tri = jnp.arange(tm)
    before = (tri[None, :] < tri[:, None]).astype(BF16)
    return pl.pallas_call(
        functools.partial(_outproj_kernel, alpha),
        grid=(b, nj),
        in_specs=[tok(256),
                  pl.BlockSpec((1, 8, 256), lambda i, j: (i, jnp.maximum(j * r8 - 1, 0), 0)),
                  pl.BlockSpec((1, 8, 256), lambda i, j: (i, jnp.minimum((j + 1) * r8, nb8 - 1), 0)),
                  tok(256), full((3, 1, 256)), tok(256),
                  pl.BlockSpec((1,) + att.shape[1:3] + (tm,), lambda i, j: (i, 0, 0, j)), tok(d), vec,
                  full((d, d)), full((1, d)), full((1, d)), vec, vec,
                  full((d, 2 * ROUTER_PAD)), full((1, ROUTER_PAD)), full((1, ROUTER_PAD)),
                  full((tm, tm))],
        out_specs=[tok(d), pl.BlockSpec((tm * PACKED_ROWS, LANES), lambda i, j: (i * nj + j, 0)),
                   tok(ROUTER_PAD), tok(ROUTER_PAD), full((1, ROUTER_PAD))],
        out_shape=[jax.ShapeDtypeStruct((b, l, d), F32),
                   jax.ShapeDtypeStruct((b * l * PACKED_ROWS, LANES), jnp.uint32),
                   jax.ShapeDtypeStruct((b, l, ROUTER_PAD), jnp.int32),
                   jax.ShapeDtypeStruct((b, l, ROUTER_PAD), F32),
                   jax.ShapeDtypeStruct((1, ROUTER_PAD), F32)],
        scratch_shapes=[pltpu.VMEM((1, ROUTER_PAD), F32)],
        compiler_params=_cparams("arbitrary", "arbitrary"),
        name="outproj_ln_router",
    )(cu, cu, cu, cb, cw, ret, att, x, g1, w_bf, lng, lnb, sc2, sh2, wr, br, cnt0, before)


def _ffn_kernel(be_ref, nv_ref, x_ref, wgu_ref, bgu_ref, wd_ref, bd_ref, o_ref, wgu_scr, wd_scr):
    j = pl.program_id(0)
    f = wd_ref.shape[2]
    tm = x_ref.shape[0] // PACKED_ROWS

    @pl.when(nv_ref[j] > 0)
    def _():
        @pl.when((j == 0) | (be_ref[j] != be_ref[jnp.maximum(j - 1, 0)]))
        def _():
            wgu_scr[...] = wgu_ref[0, 0].astype(BF16)
            wd_scr[...] = wd_ref[0, 0].astype(BF16)

        x = _load_rows_packed(x_ref, tm)
        bgu = bgu_ref[0, 0]
        acts = []
        for lo in range(0, f, FFN_BLOCK):
            hi = lo + FFN_BLOCK
            gate = jnp.minimum(_dot(x, wgu_scr[:, lo:hi]) + bgu[:, lo:hi], SWIGLU_LIMIT)
            up = jnp.clip(_dot(x, wgu_scr[:, f + lo:f + hi]) + bgu[:, f + lo:f + hi], -SWIGLU_LIMIT, SWIGLU_LIMIT)
            acts.append(((up + 1.0) * (gate * jax.nn.sigmoid(SWIGLU_ALPHA * gate))).astype(BF16))
        act = jnp.concatenate(acts, axis=1)
        _store_rows_packed(o_ref, _dot(act, wd_scr[...]) + bd_ref[0, 0])


def _expert_ffn(layer, block_e, n_valid, xs, wgu, bgu, wd, bd, tm):
    n_rows = xs.shape[0] // PACKED_ROWS
    depth, ne, d, f2 = wgu.shape
    f = f2 // 2
    rows = pl.BlockSpec((tm * PACKED_ROWS, LANES), lambda j, be, nu: (j, 0))
    grid_spec = pltpu.PrefetchScalarGridSpec(
        num_scalar_prefetch=2,
        grid=(n_rows // tm,),
        in_specs=[rows,
                  pl.BlockSpec((1, 1, d, f2), lambda j, be, nu: (layer, be[j], 0, 0)),
                  pl.BlockSpec((1, 1, 1, f2), lambda j, be, nu: (layer, be[j], 0, 0)),
                  pl.BlockSpec((1, 1, f, d), lambda j, be, nu: (layer, be[j], 0, 0)),
                  pl.BlockSpec((1, 1, 1, d), lambda j, be, nu: (layer, be[j], 0, 0))],
        out_specs=rows,
        scratch_shapes=[pltpu.VMEM((d, f2), BF16), pltpu.VMEM((f, d), BF16)],
    )
    return pl.pallas_call(
        _ffn_kernel,
        grid_spec=grid_spec,
        out_shape=jax.ShapeDtypeStruct((n_rows * PACKED_ROWS, LANES), jnp.uint32),
        compiler_params=_cparams("arbitrary"),
        name="expert_ffn",
    )(block_e, n_valid, xs, wgu, bgu.reshape(depth, ne, 1, f2), wd, bd.reshape(depth, ne, 1, d))


def _sc_worker_base(per_worker):
    return (lax.axis_index("s") * SC_CORES + lax.axis_index("c")) * per_worker


def _sc_dispatch(rows, dest, n_out):
    t = rows.shape[0]
    kk = dest.shape[0] // t
    w = SC_WINDOW
    per_worker = t // (SC_CORES * SC_SUBCORES)
    assert per_worker * SC_CORES * SC_SUBCORES == t and per_worker % w == 0
    mesh = plsc.VectorSubcoreMesh(core_axis_name="c", subcore_axis_name="s")

    n_win = per_worker // w
    row_buf = pltpu.VMEM((w,) + rows.shape[1:], rows.dtype)

    @functools.partial(
        pl.kernel, mesh=mesh,
        out_type=jax.ShapeDtypeStruct((n_out,) + rows.shape[1:], rows.dtype),
        scratch_types=[pltpu.VMEM((w,), jnp.int32)] * (2 * kk) + [row_buf] * 2 + [pltpu.SemaphoreType.DMA] * 4)
    def scatter_rows(r_hbm, d_hbm, o_hbm, *scratch):
        idx_v = (scratch[:kk], scratch[kk:2 * kk])
        rows_v = scratch[2 * kk:2 * kk + 2]
        rsem = scratch[2 * kk + 2:2 * kk + 4]
        ssem = scratch[2 * kk + 4:2 * kk + 6]
        base = _sc_worker_base(per_worker)

        def read(win, slot):
            return pltpu.make_async_copy(r_hbm.at[pl.ds(base + win * w, w)], rows_v[slot], rsem[slot])

        def scatter(slot, s):
            return pltpu.make_async_copy(rows_v[slot], o_hbm.at[idx_v[slot][s]], ssem[slot])

        def start_read(win, slot):
            read(win, slot).start()
            for s in range(kk):
                pltpu.sync_copy(d_hbm.at[pl.ds(s * t + base + win * w, w)], idx_v[slot][s])

        start_read(0, 0)

        @pl.loop(0, (n_win + 1) // 2)
        def _(pair):
            for slot in range(2):
                win = 2 * pair + slot
                other = 1 - slot

                @pl.when(win < n_win)
                def _():
                    @pl.when(win >= 1)
                    def _():
                        for s in range(kk):
                            scatter(other, s).wait()

                    @pl.when(win + 1 < n_win)
                    def _():
                        start_read(win + 1, other)

                    read(win, slot).wait()
                    for s in range(kk):
                        scatter(slot, s).start()

        for s in range(kk):
            scatter((n_win - 1) % 2, s).wait()

    return scatter_rows(rows, dest)


def _sc_gather(table, idx):
    n = idx.shape[0]
    w = SC_WINDOW
    per_worker = n // (SC_CORES * SC_SUBCORES)
    assert per_worker * SC_CORES * SC_SUBCORES == n and per_worker % w == 0
    mesh = plsc.VectorSubcoreMesh(core_axis_name="c", subcore_axis_name="s")

    n_win = per_worker // w
    assert n_win % 2 == 0
    row_buf = pltpu.VMEM((w,) + table.shape[1:], table.dtype)

    @functools.partial(
        pl.kernel, mesh=mesh,
        out_type=jax.ShapeDtypeStruct((n,) + table.shape[1:], table.dtype),
        scratch_types=[pltpu.VMEM((w,), jnp.int32)] * 2 + [row_buf] * 2 + [pltpu.SemaphoreType.DMA] * 4)
    def gather_rows(t_hbm, i_hbm, o_hbm, idx0, idx1, rows0, rows1, gsem0, gsem1, wsem0, wsem1):
        idx_v, rows_v = (idx0, idx1), (rows0, rows1)
        gsem, wsem = (gsem0, gsem1), (wsem0, wsem1)
        base = _sc_worker_base(per_worker)

        def gather(slot):
            return pltpu.make_async_copy(t_hbm.at[idx_v[slot]], rows_v[slot], gsem[slot])

        def write(win, slot):
            return pltpu.make_async_copy(rows_v[slot], o_hbm.at[pl.ds(base + win * w, w)], wsem[slot])

        def start_gather(win, slot):
            pltpu.sync_copy(i_hbm.at[pl.ds(base + win * w, w)], idx_v[slot])
            gather(slot).start()

        start_gather(0, 0)

        @pl.loop(0, n_win // 2)
        def _(pair):
            for slot in range(2):
                win = 2 * pair + slot
                other = 1 - slot

                @pl.when(win >= 1)
                def _():
                    write(win - 1, other).wait()

                @pl.when(win + 1 < n_win)
                def _():
                    start_gather(win + 1, other)

                gather(slot).wait()
                write(win, slot).start()

        write(n_win - 1, 1).wait()

    return gather_rows(table, idx)


def _combine_kernel(alpha, g_ref, w_ref, x_ref, g2_ref, lng_ref, lnb_ref, o_ref):
    tm = x_ref.shape[1]
    w = w_ref[0]
    f = _load_rows_packed(g_ref.at[0], tm).astype(F32) * w[:, 0:1]
    for kk in range(1, TOP_K):
        f = f + _load_rows_packed(g_ref.at[kk], tm).astype(F32) * w[:, kk:kk + 1]
    o_ref[0] = _layer_norm(alpha * x_ref[0] + g2_ref[0] * f, lng_ref[...], lnb_ref[...])


def _combine(alpha, gathered, tok_off, wts, x1, g2, lng, lnb, tm):
    b, l, d = x1.shape
    nj = l // tm
    assert tok_off % tm == 0
    blk_off = tok_off // tm
    tok = pl.BlockSpec((1, tm, d), lambda i, j: (i, j, 0))
    full = pl.BlockSpec((1, d), lambda i, j: (0, 0))
    return pl.pallas_call(
        functools.partial(_combine_kernel, alpha),
        grid=(b, nj),
        in_specs=[pl.BlockSpec((TOP_K, tm * PACKED_ROWS, LANES), lambda i, j: (0, blk_off + i * nj + j, 0)),
                  pl.BlockSpec((1, tm, ROUTER_PAD), lambda i, j: (i, j, 0)),
                  tok, pl.BlockSpec((1, 1, d), lambda i, j: (i, 0, 0)), full, full],
        out_specs=tok,
        out_shape=jax.ShapeDtypeStruct((b, l, d), F32),
        compiler_params=_cparams("arbitrary", "arbitrary"),
        name="combine_ln",
    )(gathered, wts, x1, g2, lng, lnb)


def _rope_tables(l):
    rows = l // GRID_W
    axis_dim = HEAD_DIM // 2
    inv_freq = ROPE_BASE ** (-jnp.arange(0, axis_dim, 2, dtype=F32) / axis_dim)
    row = jnp.repeat(jnp.arange(rows, dtype=F32), GRID_W)
    col = jnp.tile(jnp.arange(GRID_W, dtype=F32), rows)
    ang = jnp.stack([row[:, None] * inv_freq, col[:, None] * inv_freq], axis=1)
    cos, sin = jnp.cos(ang), jnp.sin(ang)
    cos64 = jnp.broadcast_to(cos[:, :, None, :], (l, 2, 2, HEAD_DIM // 4)).reshape(l, HEAD_DIM)
    sin64 = jnp.stack([-sin, sin], axis=2).reshape(l, HEAD_DIM)
    return jnp.tile(cos64, (1, LANES // HEAD_DIM)), jnp.tile(sin64, (1, LANES // HEAD_DIM))


def _route_tables(ri, counts, tm):
    t = ri.shape[0]
    top_e = ri[:, :TOP_K]
    rank = ri[:, TOP_K:2 * TOP_K]
    padded = (counts + tm - 1) // tm * tm
    pad_end = jnp.cumsum(padded)
    pad_start = pad_end - padded
    experts = jnp.arange(N_EXPERTS, dtype=jnp.int32)
    start = jnp.sum(jnp.where(top_e[:, :, None] == experts, pad_start, 0), axis=-1)
    dest = (start + rank).T.reshape(TOP_K * t)
    n_tiles = (t * TOP_K + N_EXPERTS * (tm - 1) + tm - 1) // tm
    tile_start = jnp.arange(n_tiles, dtype=jnp.int32) * tm
    block_e = jnp.minimum(jnp.sum((pad_end[None, :] <= tile_start[:, None]).astype(jnp.int32), axis=1), N_EXPERTS - 1)
    in_expert = jnp.where(block_e[:, None] == experts[None, :], (pad_start + counts)[None, :], 0).sum(axis=1)
    n_valid = jnp.clip(in_expert - tile_start, 0, tm)
    return dest, block_e.astype(jnp.int32), n_valid.astype(jnp.int32), n_tiles


def _layer_params(layer, d, w_in, conv_w, ret_decay_exp, ret_gn_g, q_norm_g, k_norm_g, w_out, ln_g, ln_b,
                  w_router, b_router):
    log_gamma = jnp.log1p(-jnp.exp2(-ret_decay_exp[layer].astype(F32)))
    lg_rows = jnp.repeat(log_gamma, HEAD_DIM, axis=1)
    wr_pad = jnp.zeros((d, ROUTER_PAD), F32).at[:, :N_EXPERTS].set(w_router[layer])
    wrh = wr_pad.astype(BF16)
    return dict(
        w_in=w_in[layer].astype(BF16), w_out=w_out[layer].astype(BF16),
        gq=jnp.tile(q_norm_g[layer], LANES // HEAD_DIM).reshape(1, LANES),
        gk=jnp.tile(k_norm_g[layer], LANES // HEAD_DIM).reshape(1, LANES),
        log_gamma=log_gamma, lgf_row=lg_rows[0:1], lgb_row=lg_rows[1:2], gn_row=ret_gn_g[layer].reshape(1, -1),
        cw=conv_w[layer].T.reshape(3, 1, -1),
        wr=jnp.concatenate([wrh, (wr_pad - wrh.astype(F32)).astype(BF16)], axis=1),
        br=jnp.zeros((1, ROUTER_PAD), F32).at[0, :N_EXPERTS].set(b_router[layer]),
        lng1=ln_g[layer, 0].reshape(1, d), lnb1=ln_b[layer, 0].reshape(1, d),
        lng2=ln_g[layer, 1].reshape(1, d), lnb2=ln_b[layer, 1].reshape(1, d))


def _layer(layer, last, alpha, x, ctx, m_lat, m_ctx, p, tables, experts):
    b, l, d = x.shape
    lc = ctx.shape[1]
    tm_lat, tm_ctx, tm_moe, tq = 512, 256, 1024, 256
    cos_l, sin_l, cos_c, sin_c, ones_bd, zero_state = tables
    sh1, sc1, g1, sh2, sc2, g2 = (m_lat[:, :, i] for i in range(N_MOD))
    sh1c, sc1c, g1c, sh2c, sc2c, g2c = (m_ctx[:, :, i] for i in range(N_MOD))

    zc = _inproj(ctx, 1.0 + sc1c, sh1c, p["w_in"], cos_c, sin_c, p["gq"], p["gk"], ones_bd, tm_ctx)
    zl = _inproj(x, 1.0 + sc1, sh1, p["w_in"], cos_l, sin_l, p["gq"], p["gk"], ones_bd, tm_lat)
    cu_c, cb_c, rq_c, rk_c, rv_c, rg_c, aq_c, ak_c, av_c = zc
    cu_l, cb_l, rq_l, rk_l, rv_l, rg_l, aq_l, ak_l, av_l = zl

    ret_args = (p["log_gamma"], p["lgf_row"], p["lgb_row"], p["gn_row"])
    ret_c, s_fwd, s_bwd = _retention(rq_c, rk_c, rv_c, rg_c, zero_state, zero_state, *ret_args)
    ret_l, _, _ = _retention(rq_l, rk_l, rv_l, rg_l, s_fwd, s_bwd, *ret_args)

    def kv_heads(a):
        return a.reshape(b, a.shape[1], -1, HEAD_DIM).transpose(0, 2, 1, 3)

    def v_heads(a):
        vt = a.reshape(b, a.shape[1], -1, HEAD_DIM).transpose(0, 2, 3, 1)
        ones = jnp.ones(vt.shape[:2] + (1, vt.shape[3]), BF16)
        pad = jnp.zeros(vt.shape[:2] + (ATT_V_ROWS - HEAD_DIM - 1, vt.shape[3]), BF16)
        return jnp.concatenate([vt, ones, pad], axis=2)

    k_all = kv_heads(jnp.concatenate([ak_c, ak_l], axis=1))
    v_all = v_heads(jnp.concatenate([av_c, av_l], axis=1))
    att_l = _attention(aq_l, k_all, v_all, tq)

    out_args = (p["w_out"], p["lng1"], p["lnb1"])
    rt_args = (p["wr"], p["br"])
    cnt0 = jnp.zeros((1, ROUTER_PAD), F32)
    if not last:
        att_c = _attention(aq_c, kv_heads(ak_c), v_heads(av_c), tq)
        ctx1, h2_c, ri_c, rw_c, cnt0 = _outproj(alpha, cu_c, cb_c, p["cw"], ret_c, att_c, ctx, g1c, *out_args,
                                                1.0 + sc2c, sh2c, *rt_args, cnt0, tm_ctx)
    x1, h2_l, ri_l, rw_l, cnt = _outproj(alpha, cu_l, cb_l, p["cw"], ret_l, att_l, x, g1, *out_args,
                                         1.0 + sc2, sh2, *rt_args, cnt0, tm_lat)
    if not last:
        n_c = b * lc
        h2 = jnp.concatenate([h2_c, h2_l], axis=0)
        ri = jnp.concatenate([ri_c.reshape(n_c, -1), ri_l.reshape(b * l, -1)], axis=0)
    else:
        n_c = 0
        h2 = h2_l
        ri = ri_l.reshape(b * l, -1)
    n_tok = n_c + b * l

    counts = cnt[0, :N_EXPERTS].astype(jnp.int32)
    dest, block_e, n_valid, n_tiles = _route_tables(ri, counts, tm_moe)
    xs = _sc_dispatch(h2.reshape(n_tok, PACKED_ROWS, LANES), dest, n_tiles * tm_moe)
    ys = _expert_ffn(layer, block_e, n_valid, xs.reshape(-1, LANES), *experts, tm_moe)
    gathered = _sc_gather(ys.reshape(-1, PACKED_ROWS, LANES), dest).reshape(TOP_K, n_tok * PACKED_ROWS, LANES)
    if not last:
        ctx = _combine(alpha, gathered, 0, rw_c, ctx1, g2c, p["lng2"], p["lnb2"], tm_ctx)
    x = _combine(alpha, gathered, n_c, rw_l, x1, g2, p["lng2"], p["lnb2"], tm_lat)
    return x, ctx


def kernel(x, c, ctx, c_ctx, w_mod, b_mod, w_in, conv_w, ret_decay_exp, ret_gn_g, q_norm_g, k_norm_g, w_out,
           ln_g, ln_b, w_router, b_router, w_gate_up, b_gate_up, w_down, b_down):
    depth = w_mod.shape[0]
    alpha = (2.0 * depth) ** 0.25
    b, l, d = x.shape
    lc = ctx.shape[1]
    groups = BATCH_GROUPS if b % BATCH_GROUPS == 0 else 1
    bg = b // groups

    n_rows = (b + 1 + 7) // 8 * 8
    c_all = jnp.zeros((n_rows, d), F32).at[:b].set(c).at[b].set(c_ctx)
    mod = _modulation(c_all, w_mod, b_mod)

    cos_l, sin_l = _rope_tables(l)
    cos_c, sin_c = jnp.ones((lc, LANES), F32), jnp.zeros((lc, LANES), F32)
    lane_head = jnp.arange(LANES) // HEAD_DIM
    ones_bd = (lane_head[:, None] == lane_head[None, :]).astype(BF16)
    zero_state = jnp.zeros((bg, RET_HEADS, HEAD_DIM, HEAD_DIM), F32)
    tables = (cos_l, sin_l, cos_c, sin_c, ones_bd, zero_state)
    experts = (w_gate_up, b_gate_up, w_down, b_down)

    xs = [x[g * bg:(g + 1) * bg] for g in range(groups)]
    cs = [ctx[g * bg:(g + 1) * bg] for g in range(groups)]
    for layer in range(depth):
        last = layer == depth - 1
        p = _layer_params(layer, d, w_in, conv_w, ret_decay_exp, ret_gn_g, q_norm_g, k_norm_g, w_out, ln_g, ln_b,
                          w_router, b_router)
        m_ctx = jnp.broadcast_to(mod[layer, b].reshape(1, 1, N_MOD, d), (bg, 1, N_MOD, d))
        for g in range(groups):
            m_lat = mod[layer, g * bg:(g + 1) * bg].reshape(bg, 1, N_MOD, d)
            xs[g], cs[g] = _layer(layer, last, alpha, xs[g], cs[g], m_lat, m_ctx, p, tables, experts)
    return jnp.concatenate(xs, axis=0) if groups > 1 else xs[0]
```

```python
import functools

import jax
import jax.numpy as jnp
from jax import lax
from jax.experimental import pallas as pl
from jax.experimental.pallas import tpu as pltpu
from jax.experimental.pallas import tpu_sc as plsc

F32 = jnp.float32
BF16 = jnp.bfloat16

HEAD_DIM = 64
GRID_W = 64
ROPE_BASE = 10000.0
N_EXPERTS = 32
TOP_K = 4
SWIGLU_ALPHA = 1.702
SWIGLU_LIMIT = 7.0
N_MOD = 6
EPS = 1e-6
RET_HEADS = 4
RET_CHUNK = 256
RET_BATCH = 8
ATT_Q_PER_KV = 4
ATT_KV_CHUNK = 256
ATT_Q_SCALE = HEAD_DIM ** -0.5 * 1.4426950408889634
ATT_STREAMS = 2
ATT_V_ROWS = 80
LANES = 128
ROUTER_PAD = 128
VMEM_LIMIT = 56 * 1024 * 1024
SUBLANES = 8
PACKED_ROWS = 4
FFN_BLOCK = 256
OUT_STREAMS = 2
SC_CORES = 2
SC_SUBCORES = 16
SC_WINDOW = 64
BATCH_GROUPS = 1


def _dot(a, b):
    return jnp.dot(a, b, preferred_element_type=F32)


def _dot_nt(a, b):
    return lax.dot_general(a, b, (((1,), (1,)), ((), ())), preferred_element_type=F32)


def _split_bf16(x):
    hi = x.astype(BF16)
    lo = (x - hi.astype(F32)).astype(BF16)
    return hi, lo


def _cparams(*sem):
    return pltpu.CompilerParams(dimension_semantics=sem, vmem_limit_bytes=VMEM_LIMIT)


def _mod_kernel(c_ref, w_ref, b_ref, o_ref):
    c = c_ref[...]
    a = c * jax.nn.sigmoid(c)
    a_hi, a_lo = _split_bf16(a)
    w_hi, w_lo = _split_bf16(w_ref[0])
    o_ref[0] = _dot(a_hi, w_hi) + _dot(a_lo, w_hi) + _dot(a_hi, w_lo) + b_ref[0]


def _modulation(c_all, w_mod, b_mod):
    depth, d, n = w_mod.shape
    r = c_all.shape[0]
    tn = 1536
    return pl.pallas_call(
        _mod_kernel,
        grid=(depth, n // tn),
        in_specs=[
            pl.BlockSpec((r, d), lambda l, j: (0, 0)),
            pl.BlockSpec((1, d, tn), lambda l, j: (l, 0, j)),
            pl.BlockSpec((1, 1, tn), lambda l, j: (l, 0, j)),
        ],
        out_specs=pl.BlockSpec((1, r, tn), lambda l, j: (l, 0, j)),
        out_shape=jax.ShapeDtypeStruct((depth, r, n), F32),
        compiler_params=_cparams("arbitrary", "arbitrary"),
        name="modulation",
    )(c_all, w_mod, b_mod.reshape(depth, 1, n))


def _inproj_kernel(x_ref, sc_ref, sh_ref, w_ref, cos_ref, sin_ref, gq_ref, gk_ref, ones_ref,
                   cu_ref, cb_ref, rq_ref, rk_ref, rv_ref, rg_ref, aq_ref, ak_ref, av_ref):
    h = (x_ref[0] * sc_ref[0] + sh_ref[0]).astype(BF16)
    cos = cos_ref[...]
    sin = sin_ref[...]
    ones = ones_ref[...]
    lane = lax.broadcasted_iota(jnp.int32, cos.shape, 1)
    first_half = (lane & 31) < 16

    def rope(xs):
        nxt = pltpu.roll(xs, LANES - 16, axis=1)
        prv = pltpu.roll(xs, 16, axis=1)
        return xs * cos + jnp.where(first_half, nxt, prv) * sin

    def rms(xs, g):
        s_hi, s_lo = _split_bf16(xs * xs)
        ssq = _dot(s_hi, ones) + _dot(s_lo, ones)
        return xs * lax.rsqrt(ssq * (1.0 / HEAD_DIM) + EPS) * g

    za = _dot(h, w_ref[:, 1792:2560])
    gq = gq_ref[...]
    for j in range(4):
        lo, hi = j * LANES, (j + 1) * LANES
        q = rope(rms(za[:, lo:hi], gq)) * ATT_Q_SCALE
        aq_ref[0, lo:hi, :] = q.T.astype(BF16)
    ak_ref[0] = rope(rms(za[:, 512:640], gk_ref[...])).astype(BF16)
    av_ref[0] = za[:, 640:768].astype(BF16)
    zr = _dot(h, w_ref[:, 768:1792])
    for j in range(2):
        lo, hi = j * LANES, (j + 1) * LANES
        rq_ref[0, lo:hi, :] = rope(zr[:, lo:hi]).T
        rk_ref[0, :, lo:hi] = rope(zr[:, 256 + lo:256 + hi] * (HEAD_DIM ** -0.5))
        rv_ref[0, lo:hi, :] = zr[:, 512 + lo:512 + hi].T
        rg_ref[0, lo:hi, :] = zr[:, 768 + lo:768 + hi].T
    zc = _dot(h, w_ref[:, 0:768])
    cu_ref[0] = zc[:, 512:768] * zc[:, 0:256]
    cb_ref[0] = zc[:, 256:512]


def _inproj(x, sc, sh, w_bf, cos, sin, gq, gk, ones_bd, tm):
    b, l, d = x.shape
    n = w_bf.shape[1]
    tok = lambda w: pl.BlockSpec((1, tm, w), lambda i, j: (i, j, 0))
    vec = pl.BlockSpec((1, 1, d), lambda i, j: (i, 0, 0))
    full = lambda s: pl.BlockSpec(s, lambda i, j: (0,) * len(s))
    widths = (256, 256, 256, 256, 256, 256, 512, 128, 128)
    dtypes = (F32,) * 6 + (BF16,) * 3
    out_specs = [tok(w) for w in widths]
    out_shape = [jax.ShapeDtypeStruct((b, l, w), dt) for w, dt in zip(widths, dtypes)]
    for o in (2, 4, 5, 6):
        out_specs[o] = pl.BlockSpec((1, widths[o], tm), lambda i, j: (i, 0, j))
        out_shape[o] = jax.ShapeDtypeStruct((b, widths[o], l), dtypes[o])
    return pl.pallas_call(
        _inproj_kernel,
        grid=(b, l // tm),
        in_specs=[tok(d), vec, vec, full((d, n)),
                  pl.BlockSpec((tm, LANES), lambda i, j: (j, 0)),
                  pl.BlockSpec((tm, LANES), lambda i, j: (j, 0)),
                  full((1, LANES)), full((1, LANES)), full((LANES, LANES))],
        out_specs=out_specs,
        out_shape=out_shape,
        compiler_params=_cparams("arbitrary", "arbitrary"),
        name="inproj",
    )(x, sc, sh, w_bf, cos, sin, gq, gk, ones_bd)


def _attn_kernel(q_ref, k_ref, vt_ref, o_ref):
    tq = q_ref.shape[2]
    q4 = q_ref[0]
    qt = jnp.concatenate([q4[h * HEAD_DIM:(h + 1) * HEAD_DIM, :] for h in range(ATT_Q_PER_KV)], axis=1)
    lk = k_ref.shape[2]
    bounds = [(lo, min(lo + ATT_KV_CHUNK, lk)) for lo in range(0, lk, ATT_KV_CHUNK)]
    r = qt.shape[1] // ATT_STREAMS
    qs = [qt[:, i * r:(i + 1) * r] for i in range(ATT_STREAMS)]
    ms = [jnp.full((1, r), -jnp.inf, F32) for _ in qs]
    accs = [jnp.zeros((ATT_V_ROWS, r), F32) for _ in qs]
    nxt = [_dot(k_ref[0, 0, bounds[0][0]:bounds[0][1], :], q) for q in qs]
    for c, (lo, hi) in enumerate(bounds):
        for i, q in enumerate(qs):
            s = nxt[i]
            if c + 1 < len(bounds):
                nxt[i] = _dot(k_ref[0, 0, bounds[c + 1][0]:bounds[c + 1][1], :], q)
            m_new = jnp.maximum(ms[i], jnp.max(s, axis=0, keepdims=True))
            p = jnp.exp2(s - m_new)
            accs[i] = accs[i] * jnp.exp2(ms[i] - m_new) + _dot(vt_ref[0, 0, :, lo:hi], p.astype(BF16))
            ms[i] = m_new
    acc = jnp.concatenate(accs, axis=1)
    o = acc[:HEAD_DIM] / acc[HEAD_DIM:HEAD_DIM + 1]
    for h in range(ATT_Q_PER_KV):
        o_ref[0, h] = o[:, h * tq:(h + 1) * tq]


def _attention(q, k, vt, tq):
    b, wq, lq = q.shape
    hkv, lk = k.shape[1], k.shape[2]
    wg = ATT_Q_PER_KV * HEAD_DIM
    return pl.pallas_call(
        _attn_kernel,
        grid=(b, hkv, lq // tq),
        in_specs=[pl.BlockSpec((1, wg, tq), lambda i, g, j: (i, g, j)),
                  pl.BlockSpec((1, 1, lk, HEAD_DIM), lambda i, g, j: (i, g, 0, 0)),
                  pl.BlockSpec((1, 1, ATT_V_ROWS, lk), lambda i, g, j: (i, g, 0, 0))],
        out_specs=pl.BlockSpec((1, ATT_Q_PER_KV, HEAD_DIM, tq), lambda i, g, j: (i, g, 0, j)),
        out_shape=jax.ShapeDtypeStruct((b, wq // HEAD_DIM, HEAD_DIM, lq), F32),
        compiler_params=_cparams("arbitrary", "arbitrary", "arbitrary"),
        name="attention",
    )(q, k, vt)


def _ret_kernel(lg_ref, q_ref, k_ref, v_ref, g_ref, s0f_ref, s0b_ref, lgf_ref, lgb_ref, lgfc_ref, lgbc_ref, gn_ref,
                o_ref, sff_ref, sfb_ref, s_scr, sb_scr, dec_scr):
    i, p, j = pl.program_id(0), pl.program_id(1), pl.program_id(2)
    nc = pl.num_programs(2)
    c = k_ref.shape[1]
    w = k_ref.shape[2]
    pos = lax.broadcasted_iota(jnp.int32, (c, w), 0).astype(F32)
    qpos = lax.broadcasted_iota(jnp.int32, (w, c), 1).astype(F32)
    lgf = lgf_ref[...]
    lgb = lgb_ref[...]
    lgf_col = lgfc_ref[...]
    lgb_col = lgbc_ref[...]

    @pl.when((i == 0) & (p == 0) & (j == 0))
    def _():
        diff = (lax.broadcasted_iota(jnp.int32, (c, c), 1) - lax.broadcasted_iota(jnp.int32, (c, c), 0)).astype(F32)
        for h in range(RET_HEADS):
            dec_scr[h] = jnp.where(diff >= 0.0, jnp.exp(lg_ref[0, h] * jnp.maximum(diff, 0.0)),
                                   jnp.exp(lg_ref[1, h] * jnp.maximum(-diff, 0.0)))

    nb = k_ref.shape[0]

    def update_state(e, vt, kz, cdec):
        for h in range(RET_HEADS):
            lo, hi = h * HEAD_DIM, (h + 1) * HEAD_DIM
            s_scr[e, h] = cdec[:, lo:hi] * s_scr[e, h] + _dot(vt[lo:hi, :], kz[:, lo:hi])

    @pl.when(p == 0)
    def _():
        @pl.when(j == 0)
        def _():
            s_scr[...] = s0b_ref[...]

        sb_scr[nc - 1 - j] = s_scr[...]
        for e in range(nb):
            kz = (k_ref[e] * jnp.exp(lgb * pos)).astype(BF16)
            update_state(e, v_ref[e].astype(BF16), kz, jnp.exp(lgb * float(c)))

        @pl.when(j == nc - 1)
        def _():
            sfb_ref[...] = s_scr[...]

    @pl.when(p == 1)
    def _():
        @pl.when(j == 0)
        def _():
            s_scr[...] = s0f_ref[...]

        for e in range(nb):
            qt = q_ref[e]
            k = k_ref[e]
            qf = (qt * jnp.exp(lgf_col * (qpos + 1.0))).astype(BF16)
            qb = (qt * jnp.exp(lgb_col * (float(c) - qpos))).astype(BF16)
            kz = (k * jnp.exp(lgf * (float(c) - 1.0 - pos))).astype(BF16)
            qh = qt.astype(BF16)
            kh = k.astype(BF16)
            vt = v_ref[e].astype(BF16)
            gate = g_ref[e]
            for h in range(RET_HEADS):
                lo, hi = h * HEAD_DIM, (h + 1) * HEAD_DIM
                sc = _dot(kh[:, lo:hi], qh[lo:hi, :])
                y = (_dot(vt[lo:hi, :], (sc * dec_scr[h]).astype(BF16))
                     + _dot(s_scr[e, h].astype(BF16), qf[lo:hi, :])
                     + _dot(sb_scr[j, e, h].astype(BF16), qb[lo:hi, :]))
                mu = jnp.mean(y, axis=0, keepdims=True)
                yc = y - mu
                var = jnp.mean(yc * yc, axis=0, keepdims=True)
                gh = gate[lo:hi, :]
                o_ref[e, lo:hi, :] = yc * lax.rsqrt(var + EPS) * gn_ref[lo:hi, :] * (gh * jax.nn.sigmoid(gh))
            update_state(e, vt, kz, jnp.exp(lgf * float(c)))

        @pl.when(j == nc - 1)
        def _():
            sff_ref[...] = s_scr[...]


def _retention(qt, k, vt, gate_t, s0_f, s0_b, lg, lgf_row, lgb_row, gn_row):
    b, l, w = k.shape
    c = min(RET_CHUNK, l)
    nc = l // c
    nb = RET_BATCH if b % RET_BATCH == 0 else 1
    st = (nb, RET_HEADS, HEAD_DIM, HEAD_DIM)
    st_spec = pl.BlockSpec(st, lambda i, p, j: (i, 0, 0, 0))
    row = pl.BlockSpec((1, w), lambda i, p, j: (0, 0))
    col = pl.BlockSpec((w, 1), lambda i, p, j: (0, 0))
    k_both = pl.BlockSpec((nb, c, w), lambda i, p, j: (i, jnp.where(p == 0, nc - 1 - j, j), 0))
    t_both = pl.BlockSpec((nb, w, c), lambda i, p, j: (i, 0, jnp.where(p == 0, nc - 1 - j, j)))
    t_fwd = pl.BlockSpec((nb, w, c), lambda i, p, j: (i, 0, p * j))
    state_shape = jax.ShapeDtypeStruct((b,) + st[1:], F32)
    return pl.pallas_call(
        _ret_kernel,
        grid=(b // nb, 2, nc),
        in_specs=[pl.BlockSpec(memory_space=pltpu.SMEM), t_fwd, k_both, t_both, t_fwd,
                  st_spec, st_spec, row, row, col, col, col],
        out_specs=[t_fwd, st_spec, st_spec],
        out_shape=[jax.ShapeDtypeStruct((b, w, l), F32), state_shape, state_shape],
        scratch_shapes=[pltpu.VMEM(st, F32), pltpu.VMEM((nc,) + st, F32), pltpu.VMEM((RET_HEADS, c, c), F32)],
        compiler_params=_cparams("arbitrary", "arbitrary", "arbitrary"),
        name="retention",
    )(lg, qt, k, vt, gate_t, s0_f, s0_b, lgf_row, lgb_row, lgf_row.reshape(w, 1), lgb_row.reshape(w, 1),
      gn_row.reshape(w, 1))


def _layer_norm(x, g, b):
    mu = jnp.mean(x, axis=-1, keepdims=True)
    xc = x - mu
    var = jnp.mean(xc * xc, axis=-1, keepdims=True)
    return xc * lax.rsqrt(var + EPS) * g + b


def _store_rows_packed(ref, x):
    tm, d = x.shape
    bits = pltpu.bitcast(x.astype(BF16).astype(F32), jnp.uint32)
    word = (bits[:, :d // 2] >> 16) | bits[:, d // 2:]
    for s in range(PACKED_ROWS):
        ref[pl.ds(s, tm, stride=PACKED_ROWS), :] = word[:, s * LANES:(s + 1) * LANES]


def _load_rows_packed(ref, tm):
    word = jnp.concatenate([ref[pl.ds(s, tm, stride=PACKED_ROWS), :] for s in range(PACKED_ROWS)], axis=1)
    lo = pltpu.bitcast(word << 16, F32).astype(BF16)
    hi = pltpu.bitcast(word & jnp.uint32(0xFFFF0000), F32).astype(BF16)
    return jnp.concatenate([lo, hi], axis=1)


def _outproj_kernel(alpha, cu_ref, cup_ref, cun_ref, cb_ref, cw_ref, ret_ref, att_ref, x_ref, g1_ref,
                    w_ref, lng_ref, lnb_ref, sc_ref, sh_ref, wr_ref, br_ref, cnt0_ref, before_ref,
                    x1_ref, h2_ref, ri_ref, rw_ref, cnt_ref, run_scr):
    i = pl.program_id(0)
    j = pl.program_id(1)

    @pl.when((i == 0) & (j == 0))
    def _():
        run_scr[...] = cnt0_ref[...]

    t = cu_ref[0]
    tm = t.shape[0]
    prev_row = jnp.where(j == 0, 0.0, cup_ref[0][7:8, :])
    next_row = jnp.where(j == pl.num_programs(1) - 1, 0.0, cun_ref[0][0:1, :])
    row = lax.broadcasted_iota(jnp.int32, t.shape, 0)
    t_prev = jnp.where(row == 0, prev_row, pltpu.roll(t, 1, axis=0))
    t_next = jnp.where(row == tm - 1, next_row, pltpu.roll(t, tm - 1, axis=0))
    conv = cb_ref[0] * (t_prev * cw_ref[0] + t * cw_ref[1] + t_next * cw_ref[2])
    conv = conv.astype(BF16)
    wrh = wr_ref[:, :ROUTER_PAD]
    rb = tm // OUT_STREAMS
    lane = lax.broadcasted_iota(jnp.int32, (rb, ROUTER_PAD), 1)
    lane_f = lane.astype(F32)
    routed = []
    for r0 in range(0, tm, rb):
        att_t = att_ref[0, :, :, r0:r0 + rb].reshape(att_ref.shape[1] * HEAD_DIM, rb).astype(BF16)
        tn = (((0,), (0,)), ((), ()))
        y = (_dot(conv[r0:r0 + rb], w_ref[0:256, :])
             + lax.dot_general(ret_ref[0, :, r0:r0 + rb].astype(BF16), w_ref[256:512, :], tn,
                               preferred_element_type=F32)
             + lax.dot_general(att_t, w_ref[512:1024, :], tn, preferred_element_type=F32))
        x1 = _layer_norm(alpha * x_ref[0, r0:r0 + rb, :] + g1_ref[0] * y, lng_ref[...], lnb_ref[...])
        x1_ref[0, r0:r0 + rb, :] = x1
        h2 = x1 * sc_ref[0] + sh_ref[0]
        _store_rows_packed(h2_ref.at[pl.ds(r0 * PACKED_ROWS, rb * PACKED_ROWS)], h2)
        h_hi, h_lo = _split_bf16(h2)
        hw = _dot(h_hi, wr_ref[...])
        logits = hw[:, :ROUTER_PAD] + hw[:, ROUTER_PAD:] + _dot(h_lo, wrh) + br_ref[...]

        work = jnp.where(lane < N_EXPERTS, logits, -jnp.inf)
        vals, firsts, sels = [], [], []
        for _ in range(TOP_K):
            m = jnp.max(work, axis=-1, keepdims=True)
            first = jnp.min(jnp.where(work == m, lane_f, float(ROUTER_PAD)), axis=-1, keepdims=True)
            sel = lane_f == first
            vals.append(m)
            firsts.append(first)
            sels.append(sel)
            work = jnp.where(sel, -jnp.inf, work)
        exps = [jnp.exp(v - vals[0]) for v in vals]
        denom = exps[0]
        for e in exps[1:]:
            denom = denom + e
        cnt = jnp.zeros(logits.shape, F32)
        for sel in sels:
            cnt = cnt + jnp.where(sel, 1.0, 0.0)
        routed.append((firsts, sels, [e / denom for e in exps], cnt))

    cnt = jnp.concatenate([r[3] for r in routed], axis=0)
    base = _dot(before_ref[...], cnt.astype(BF16)) + run_scr[...]
    run_scr[...] = run_scr[...] + jnp.sum(cnt, axis=0, keepdims=True)
    cnt_ref[...] = run_scr[...]

    for blk, (firsts, sels, wts, _) in enumerate(routed):
        r0 = blk * rb
        ri = jnp.zeros((rb, ROUTER_PAD), F32)
        rw = jnp.zeros((rb, ROUTER_PAD), F32)
        for kk in range(TOP_K):
            rank = jnp.sum(jnp.where(sels[kk], base[r0:r0 + rb], 0.0), axis=-1, keepdims=True)
            ri = jnp.where(lane == kk, firsts[kk], ri)
            ri = jnp.where(lane == TOP_K + kk, rank, ri)
            rw = jnp.where(lane == kk, wts[kk], rw)
        ri_ref[0, r0:r0 + rb, :] = ri.astype(jnp.int32)
        rw_ref[0, r0:r0 + rb, :] = rw


def _outproj(alpha, cu, cb, cw, ret, att, x, g1, w_bf, lng, lnb, sc2, sh2, wr, br, cnt0, tm):
    b, l, d = x.shape
    nj = l // tm
    tok = lambda w: pl.BlockSpec((1, tm, w), lambda i, j: (i, j, 0))
    vec = pl.BlockSpec((1, 1, d), lambda i, j: (i, 0, 0))
    full = lambda s: pl.BlockSpec(s, lambda i, j: (0,) * len(s))
    r8 = tm // 8
    nb8 = l // 8
    tri = jnp.arange(tm)
    before = (tri[None, :] < tri[:, None]).astype(BF16)
    return pl.pallas_call(
        functools.partial(_outproj_kernel, alpha),
        grid=(b, nj),
        in_specs=[tok(256),
                  pl.BlockSpec((1, 8, 256), lambda i, j: (i, jnp.maximum(j * r8 - 1, 0), 0)),
                  pl.BlockSpec((1, 8, 256), lambda i, j: (i, jnp.minimum((j + 1) * r8, nb8 - 1), 0)),
                  tok(256), full((3, 1, 256)),
                  pl.BlockSpec((1, ret.shape[1], tm), lambda i, j: (i, 0, j)),
                  pl.BlockSpec((1,) + att.shape[1:3] + (tm,), lambda i, j: (i, 0, 0, j)), tok(d), vec,
                  full((d, d)), full((1, d)), full((1, d)), vec, vec,
                  full((d, 2 * ROUTER_PAD)), full((1, ROUTER_PAD)), full((1, ROUTER_PAD)),
                  full((tm, tm))],
        out_specs=[tok(d), pl.BlockSpec((tm * PACKED_ROWS, LANES), lambda i, j: (i * nj + j, 0)),
                   tok(ROUTER_PAD), tok(ROUTER_PAD), full((1, ROUTER_PAD))],
        out_shape=[jax.ShapeDtypeStruct((b, l, d), F32),
                   jax.ShapeDtypeStruct((b * l * PACKED_ROWS, LANES), jnp.uint32),
                   jax.ShapeDtypeStruct((b, l, ROUTER_PAD), jnp.int32),
                   jax.ShapeDtypeStruct((b, l, ROUTER_PAD), F32),
                   jax.ShapeDtypeStruct((1, ROUTER_PAD), F32)],
        scratch_shapes=[pltpu.VMEM((1, ROUTER_PAD), F32)],
        compiler_params=_cparams("arbitrary", "arbitrary"),
        name="outproj_ln_router",
    )(cu, cu, cu, cb, cw, ret, att, x, g1, w_bf, lng, lnb, sc2, sh2, wr, br, cnt0, before)


def _ffn_kernel(be_ref, nv_ref, x_ref, wgu_ref, bgu_ref, wd_ref, bd_ref, o_ref, wgu_scr, wd_scr):
    j = pl.program_id(0)
    f = wd_ref.shape[2]
    tm = x_ref.shape[0] // PACKED_ROWS

    @pl.when(nv_ref[j] > 0)
    def _():
        @pl.when((j == 0) | (be_ref[j] != be_ref[jnp.maximum(j - 1, 0)]))
        def _():
            wgu_scr[...] = wgu_ref[0, 0].astype(BF16)
            wd_scr[...] = wd_ref[0, 0].astype(BF16)

        x = _load_rows_packed(x_ref, tm)
        bgu = bgu_ref[0, 0]
        acts = []
        for lo in range(0, f, FFN_BLOCK):
            hi = lo + FFN_BLOCK
            gate = jnp.minimum(_dot(x, wgu_scr[:, lo:hi]) + bgu[:, lo:hi], SWIGLU_LIMIT)
            up = jnp.clip(_dot(x, wgu_scr[:, f + lo:f + hi]) + bgu[:, f + lo:f + hi], -SWIGLU_LIMIT, SWIGLU_LIMIT)
            acts.append(((up + 1.0) * (gate * jax.nn.sigmoid(SWIGLU_ALPHA * gate))).astype(BF16))
        act = jnp.concatenate(acts, axis=1)
        _store_rows_packed(o_ref, _dot(act, wd_scr[...]) + bd_ref[0, 0])


def _expert_ffn(layer, block_e, n_valid, xs, wgu, bgu, wd, bd, tm):
    n_rows = xs.shape[0] // PACKED_ROWS
    depth, ne, d, f2 = wgu.shape
    f = f2 // 2
    rows = pl.BlockSpec((tm * PACKED_ROWS, LANES), lambda j, be, nu: (j, 0))
    grid_spec = pltpu.PrefetchScalarGridSpec(
        num_scalar_prefetch=2,
        grid=(n_rows // tm,),
        in_specs=[rows,
                  pl.BlockSpec((1, 1, d, f2), lambda j, be, nu: (layer, be[j], 0, 0)),
                  pl.BlockSpec((1, 1, 1, f2), lambda j, be, nu: (layer, be[j], 0, 0)),
                  pl.BlockSpec((1, 1, f, d), lambda j, be, nu: (layer, be[j], 0, 0)),
                  pl.BlockSpec((1, 1, 1, d), lambda j, be, nu: (layer, be[j], 0, 0))],
        out_specs=rows,
        scratch_shapes=[pltpu.VMEM((d, f2), BF16), pltpu.VMEM((f, d), BF16)],
    )
    return pl.pallas_call(
        _ffn_kernel,
        grid_spec=grid_spec,
        out_shape=jax.ShapeDtypeStruct((n_rows * PACKED_ROWS, LANES), jnp.uint32),
        compiler_params=_cparams("arbitrary"),
        name="expert_ffn",
    )(block_e, n_valid, xs, wgu, bgu.reshape(depth, ne, 1, f2), wd, bd.reshape(depth, ne, 1, d))


def _sc_worker_base(per_worker):
    return (lax.axis_index("s") * SC_CORES + lax.axis_index("c")) * per_worker


def _sc_dispatch(rows, dest, n_out):
    t = rows.shape[0]
    kk = dest.shape[0] // t
    w = SC_WINDOW
    per_worker = t // (SC_CORES * SC_SUBCORES)
    assert per_worker * SC_CORES * SC_SUBCORES == t and per_worker % w == 0
    mesh = plsc.VectorSubcoreMesh(core_axis_name="c", subcore_axis_name="s")

    n_win = per_worker // w
    row_buf = pltpu.VMEM((w,) + rows.shape[1:], rows.dtype)

    @functools.partial(
        pl.kernel, mesh=mesh,
        out_type=jax.ShapeDtypeStruct((n_out,) + rows.shape[1:], rows.dtype),
        scratch_types=[pltpu.VMEM((w,), jnp.int32)] * (2 * kk) + [row_buf] * 2 + [pltpu.SemaphoreType.DMA] * 4)
    def scatter_rows(r_hbm, d_hbm, o_hbm, *scratch):
        idx_v = (scratch[:kk], scratch[kk:2 * kk])
        rows_v = scratch[2 * kk:2 * kk + 2]
        rsem = scratch[2 * kk + 2:2 * kk + 4]
        ssem = scratch[2 * kk + 4:2 * kk + 6]
        base = _sc_worker_base(per_worker)

        def read(win, slot):
            return pltpu.make_async_copy(r_hbm.at[pl.ds(base + win * w, w)], rows_v[slot], rsem[slot])

        def scatter(slot, s):
            return pltpu.make_async_copy(rows_v[slot], o_hbm.at[idx_v[slot][s]], ssem[slot])

        def start_read(win, slot):
            read(win, slot).start()
            for s in range(kk):
                pltpu.sync_copy(d_hbm.at[pl.ds(s * t + base + win * w, w)], idx_v[slot][s])

        start_read(0, 0)

        @pl.loop(0, (n_win + 1) // 2)
        def _(pair):
            for slot in range(2):
                win = 2 * pair + slot
                other = 1 - slot

                @pl.when(win < n_win)
                def _():
                    @pl.when(win >= 1)
                    def _():
                        for s in range(kk):
                            scatter(other, s).wait()

                    @pl.when(win + 1 < n_win)
                    def _():
                        start_read(win + 1, other)

                    read(win, slot).wait()
                    for s in range(kk):
                        scatter(slot, s).start()

        for s in range(kk):
            scatter((n_win - 1) % 2, s).wait()

    return scatter_rows(rows, dest)


def _sc_gather(table, idx):
    n = idx.shape[0]
    w = SC_WINDOW
    per_worker = n // (SC_CORES * SC_SUBCORES)
    assert per_worker * SC_CORES * SC_SUBCORES == n and per_worker % w == 0
    mesh = plsc.VectorSubcoreMesh(core_axis_name="c", subcore_axis_name="s")

    n_win = per_worker // w
    assert n_win % 2 == 0
    row_buf = pltpu.VMEM((w,) + table.shape[1:], table.dtype)

    @functools.partial(
        pl.kernel, mesh=mesh,
        out_type=jax.ShapeDtypeStruct((n,) + table.shape[1:], table.dtype),
        scratch_types=[pltpu.VMEM((w,), jnp.int32)] * 2 + [row_buf] * 2 + [pltpu.SemaphoreType.DMA] * 4)
    def gather_rows(t_hbm, i_hbm, o_hbm, idx0, idx1, rows0, rows1, gsem0, gsem1, wsem0, wsem1):
        idx_v, rows_v = (idx0, idx1), (rows0, rows1)
        gsem, wsem = (gsem0, gsem1), (wsem0, wsem1)
        base = _sc_worker_base(per_worker)

        def gather(slot):
            return pltpu.make_async_copy(t_hbm.at[idx_v[slot]], rows_v[slot], gsem[slot])

        def write(win, slot):
            return pltpu.make_async_copy(rows_v[slot], o_hbm.at[pl.ds(base + win * w, w)], wsem[slot])

        def start_gather(win, slot):
            pltpu.sync_copy(i_hbm.at[pl.ds(base + win * w, w)], idx_v[slot])
            gather(slot).start()

        start_gather(0, 0)

        @pl.loop(0, n_win // 2)
        def _(pair):
            for slot in range(2):
                win = 2 * pair + slot
                other = 1 - slot

                @pl.when(win >= 1)
                def _():
                    write(win - 1, other).wait()

                @pl.when(win + 1 < n_win)
                def _():
                    start_gather(win + 1, other)

                gather(slot).wait()
                write(win, slot).start()

        write(n_win - 1, 1).wait()

    return gather_rows(table, idx)


def _combine_kernel(alpha, g_ref, w_ref, x_ref, g2_ref, lng_ref, lnb_ref, o_ref):
    tm = x_ref.shape[1]
    w = w_ref[0]
    f = _load_rows_packed(g_ref.at[0], tm).astype(F32) * w[:, 0:1]
    for kk in range(1, TOP_K):
        f = f + _load_rows_packed(g_ref.at[kk], tm).astype(F32) * w[:, kk:kk + 1]
    o_ref[0] = _layer_norm(alpha * x_ref[0] + g2_ref[0] * f, lng_ref[...], lnb_ref[...])


def _combine(alpha, gathered, tok_off, wts, x1, g2, lng, lnb, tm):
    b, l, d = x1.shape
    nj = l // tm
    assert tok_off % tm == 0
    blk_off = tok_off // tm
    tok = pl.BlockSpec((1, tm, d), lambda i, j: (i, j, 0))
    full = pl.BlockSpec((1, d), lambda i, j: (0, 0))
    return pl.pallas_call(
        functools.partial(_combine_kernel, alpha),
        grid=(b, nj),
        in_specs=[pl.BlockSpec((TOP_K, tm * PACKED_ROWS, LANES), lambda i, j: (0, blk_off + i * nj + j, 0)),
                  pl.BlockSpec((1, tm, ROUTER_PAD), lambda i, j: (i, j, 0)),
                  tok, pl.BlockSpec((1, 1, d), lambda i, j: (i, 0, 0)), full, full],
        out_specs=tok,
        out_shape=jax.ShapeDtypeStruct((b, l, d), F32),
        compiler_params=_cparams("arbitrary", "arbitrary"),
        name="combine_ln",
    )(gathered, wts, x1, g2, lng, lnb)


def _rope_tables(l):
    rows = l // GRID_W
    axis_dim = HEAD_DIM // 2
    inv_freq = ROPE_BASE ** (-jnp.arange(0, axis_dim, 2, dtype=F32) / axis_dim)
    row = jnp.repeat(jnp.arange(rows, dtype=F32), GRID_W)
    col = jnp.tile(jnp.arange(GRID_W, dtype=F32), rows)
    ang = jnp.stack([row[:, None] * inv_freq, col[:, None] * inv_freq], axis=1)
    cos, sin = jnp.cos(ang), jnp.sin(ang)
    cos64 = jnp.broadcast_to(cos[:, :, None, :], (l, 2, 2, HEAD_DIM // 4)).reshape(l, HEAD_DIM)
    sin64 = jnp.stack([-sin, sin], axis=2).reshape(l, HEAD_DIM)
    return jnp.tile(cos64, (1, LANES // HEAD_DIM)), jnp.tile(sin64, (1, LANES // HEAD_DIM))


def _route_tables(ri, counts, tm):
    t = ri.shape[0]
    top_e = ri[:, :TOP_K]
    rank = ri[:, TOP_K:2 * TOP_K]
    padded = (counts + tm - 1) // tm * tm
    pad_end = jnp.cumsum(padded)
    pad_start = pad_end - padded
    experts = jnp.arange(N_EXPERTS, dtype=jnp.int32)
    start = jnp.sum(jnp.where(top_e[:, :, None] == experts, pad_start, 0), axis=-1)
    dest = (start + rank).T.reshape(TOP_K * t)
    n_tiles = (t * TOP_K + N_EXPERTS * (tm - 1) + tm - 1) // tm
    tile_start = jnp.arange(n_tiles, dtype=jnp.int32) * tm
    block_e = jnp.minimum(jnp.sum((pad_end[None, :] <= tile_start[:, None]).astype(jnp.int32), axis=1), N_EXPERTS - 1)
    in_expert = jnp.where(block_e[:, None] == experts[None, :], (pad_start + counts)[None, :], 0).sum(axis=1)
    n_valid = jnp.clip(in_expert - tile_start, 0, tm)
    return dest, block_e.astype(jnp.int32), n_valid.astype(jnp.int32), n_tiles


def _layer_params(layer, d, w_in, conv_w, ret_decay_exp, ret_gn_g, q_norm_g, k_norm_g, w_out, ln_g, ln_b,
                  w_router, b_router):
    log_gamma = jnp.log1p(-jnp.exp2(-ret_decay_exp[layer].astype(F32)))
    lg_rows = jnp.repeat(log_gamma, HEAD_DIM, axis=1)
    wr_pad = jnp.zeros((d, ROUTER_PAD), F32).at[:, :N_EXPERTS].set(w_router[layer])
    wrh = wr_pad.astype(BF16)
    return dict(
        w_in=w_in[layer].astype(BF16), w_out=w_out[layer].astype(BF16),
        gq=jnp.tile(q_norm_g[layer], LANES // HEAD_DIM).reshape(1, LANES),
        gk=jnp.tile(k_norm_g[layer], LANES // HEAD_DIM).reshape(1, LANES),
        log_gamma=log_gamma, lgf_row=lg_rows[0:1], lgb_row=lg_rows[1:2], gn_row=ret_gn_g[layer].reshape(1, -1),
        cw=conv_w[layer].T.reshape(3, 1, -1),
        wr=jnp.concatenate([wrh, (wr_pad - wrh.astype(F32)).astype(BF16)], axis=1),
        br=jnp.zeros((1, ROUTER_PAD), F32).at[0, :N_EXPERTS].set(b_router[layer]),
        lng1=ln_g[layer, 0].reshape(1, d), lnb1=ln_b[layer, 0].reshape(1, d),
        lng2=ln_g[layer, 1].reshape(1, d), lnb2=ln_b[layer, 1].reshape(1, d))


def _layer(layer, last, alpha, x, ctx, m_lat, m_ctx, p, tables, experts):
    b, l, d = x.shape
    lc = ctx.shape[1]
    tm_lat, tm_ctx, tm_moe, tq = 512, 256, 512, 256
    cos_l, sin_l, cos_c, sin_c, ones_bd, zero_state = tables
    sh1, sc1, g1, sh2, sc2, g2 = (m_lat[:, :, i] for i in range(N_MOD))
    sh1c, sc1c, g1c, sh2c, sc2c, g2c = (m_ctx[:, :, i] for i in range(N_MOD))

    zc = _inproj(ctx, 1.0 + sc1c, sh1c, p["w_in"], cos_c, sin_c, p["gq"], p["gk"], ones_bd, tm_ctx)
    zl = _inproj(x, 1.0 + sc1, sh1, p["w_in"], cos_l, sin_l, p["gq"], p["gk"], ones_bd, tm_lat)
    cu_c, cb_c, rq_c, rk_c, rv_c, rg_c, aq_c, ak_c, av_c = zc
    cu_l, cb_l, rq_l, rk_l, rv_l, rg_l, aq_l, ak_l, av_l = zl

    ret_args = (p["log_gamma"], p["lgf_row"], p["lgb_row"], p["gn_row"])
    ret_c, s_fwd, s_bwd = _retention(rq_c, rk_c, rv_c, rg_c, zero_state, zero_state, *ret_args)
    ret_l, _, _ = _retention(rq_l, rk_l, rv_l, rg_l, s_fwd, s_bwd, *ret_args)

    def kv_heads(a):
        return a.reshape(b, a.shape[1], -1, HEAD_DIM).transpose(0, 2, 1, 3)

    def v_heads(a):
        vt = a.reshape(b, a.shape[1], -1, HEAD_DIM).transpose(0, 2, 3, 1)
        ones = jnp.ones(vt.shape[:2] + (1, vt.shape[3]), BF16)
        pad = jnp.zeros(vt.shape[:2] + (ATT_V_ROWS - HEAD_DIM - 1, vt.shape[3]), BF16)
        return jnp.concatenate([vt, ones, pad], axis=2)

    k_all = kv_heads(jnp.concatenate([ak_c, ak_l], axis=1))
    v_all = v_heads(jnp.concatenate([av_c, av_l], axis=1))
    att_l = _attention(aq_l, k_all, v_all, tq)

    out_args = (p["w_out"], p["lng1"], p["lnb1"])
    rt_args = (p["wr"], p["br"])
    cnt0 = jnp.zeros((1, ROUTER_PAD), F32)
    if not last:
        att_c = _attention(aq_c, kv_heads(ak_c), v_heads(av_c), min(tq, lc))
        ctx1, h2_c, ri_c, rw_c, cnt0 = _outproj(alpha, cu_c, cb_c, p["cw"], ret_c, att_c, ctx, g1c, *out_args,
                                                1.0 + sc2c, sh2c, *rt_args, cnt0, tm_ctx)
    x1, h2_l, ri_l, rw_l, cnt = _outproj(alpha, cu_l, cb_l, p["cw"], ret_l, att_l, x, g1, *out_args,
                                         1.0 + sc2, sh2, *rt_args, cnt0, tm_lat)
    if not last:
        n_c = b * lc
        h2 = jnp.concatenate([h2_c, h2_l], axis=0)
        ri = jnp.concatenate([ri_c.reshape(n_c, -1), ri_l.reshape(b * l, -1)], axis=0)
    else:
        n_c = 0
        h2 = h2_l
        ri = ri_l.reshape(b * l, -1)
    n_tok = n_c + b * l

    counts = cnt[0, :N_EXPERTS].astype(jnp.int32)
    dest, block_e, n_valid, n_tiles = _route_tables(ri, counts, tm_moe)
    xs = _sc_dispatch(h2.reshape(n_tok, PACKED_ROWS, LANES), dest, n_tiles * tm_moe)
    ys = _expert_ffn(layer, block_e, n_valid, xs.reshape(-1, LANES), *experts, tm_moe)
    gathered = _sc_gather(ys.reshape(-1, PACKED_ROWS, LANES), dest).reshape(TOP_K, n_tok * PACKED_ROWS, LANES)
    if not last:
        ctx = _combine(alpha, gathered, 0, rw_c, ctx1, g2c, p["lng2"], p["lnb2"], tm_ctx)
    x = _combine(alpha, gathered, n_c, rw_l, x1, g2, p["lng2"], p["lnb2"], tm_lat)
    return x, ctx


def kernel(x, c, ctx, c_ctx, w_mod, b_mod, w_in, conv_w, ret_decay_exp, ret_gn_g, q_norm_g, k_norm_g, w_out,
           ln_g, ln_b, w_router, b_router, w_gate_up, b_gate_up, w_down, b_down):
    depth = w_mod.shape[0]
    alpha = (2.0 * depth) ** 0.25
    b, l, d = x.shape
    lc = ctx.shape[1]
    groups = BATCH_GROUPS if b % BATCH_GROUPS == 0 else 1
    bg = b // groups

    n_rows = (b + 1 + 7) // 8 * 8
    c_all = jnp.zeros((n_rows, d), F32).at[:b].set(c).at[b].set(c_ctx)
    mod = _modulation(c_all, w_mod, b_mod)

    cos_l, sin_l = _rope_tables(l)
    cos_c, sin_c = jnp.ones((lc, LANES), F32), jnp.zeros((lc, LANES), F32)
    lane_head = jnp.arange(LANES) // HEAD_DIM
    ones_bd = (lane_head[:, None] == lane_head[None, :]).astype(BF16)
    zero_state = jnp.zeros((bg, RET_HEADS, HEAD_DIM, HEAD_DIM), F32)
    tables = (cos_l, sin_l, cos_c, sin_c, ones_bd, zero_state)
    experts = (w_gate_up, b_gate_up, w_down, b_down)

    xs = [x[g * bg:(g + 1) * bg] for g in range(groups)]
    cs = [ctx[g * bg:(g + 1) * bg] for g in range(groups)]
    for layer in range(depth):
        last = layer == depth - 1
        p = _layer_params(layer, d, w_in, conv_w, ret_decay_exp, ret_gn_g, q_norm_g, k_norm_g, w_out, ln_g, ln_b,
                          w_router, b_router)
        m_ctx = jnp.broadcast_to(mod[layer, b].reshape(1, 1, N_MOD, d), (bg, 1, N_MOD, d))
        for g in range(groups):
            m_lat = mod[layer, g * bg:(g + 1) * bg].reshape(bg, 1, N_MOD, d)
            xs[g], cs[g] = _layer(layer, last, alpha, xs[g], cs[g], m_lat, m_ctx, p, tables, experts)
    return jnp.concatenate(xs, axis=0) if groups > 1 else xs[0]
```

```python
import functools

import jax
import jax.numpy as jnp
from jax import lax
from jax.experimental import pallas as pl
from jax.experimental.pallas import tpu as pltpu
from jax.experimental.pallas import tpu_sc as plsc

F32 = jnp.float32
BF16 = jnp.bfloat16

HEAD_DIM = 64
GRID_W = 64
ROPE_BASE = 10000.0
N_EXPERTS = 32
TOP_K = 4
SWIGLU_ALPHA = 1.702
SWIGLU_LIMIT = 7.0
N_MOD = 6
EPS = 1e-6
RET_HEADS = 4
RET_CHUNK = 256
RET_BATCH = 8
ATT_Q_PER_KV = 4
ATT_KV_CHUNK = 256
ATT_Q_SCALE = HEAD_DIM ** -0.5 * 1.4426950408889634
ATT_STREAMS = 2
ATT_V_ROWS = 80
LANES = 128
ROUTER_PAD = 128
VMEM_LIMIT = 56 * 1024 * 1024
SUBLANES = 8
PACKED_ROWS = 4
FFN_BLOCK = 256
OUT_STREAMS = 2
COMBINE_CHUNKS = 4
SC_CORES = 2
SC_SUBCORES = 16
SC_WINDOW = 64
BATCH_GROUPS = 1


def _dot(a, b):
    return jnp.dot(a, b, preferred_element_type=F32)


def _dot_nt(a, b):
    return lax.dot_general(a, b, (((1,), (1,)), ((), ())), preferred_element_type=F32)


def _split_bf16(x):
    hi = x.astype(BF16)
    lo = (x - hi.astype(F32)).astype(BF16)
    return hi, lo


def _cparams(*sem):
    return pltpu.CompilerParams(dimension_semantics=sem, vmem_limit_bytes=VMEM_LIMIT)


def _mod_kernel(c_ref, w_ref, b_ref, o_ref):
    c = c_ref[...]
    a = c * jax.nn.sigmoid(c)
    a_hi, a_lo = _split_bf16(a)
    w_hi, w_lo = _split_bf16(w_ref[0])
    o_ref[0] = _dot(a_hi, w_hi) + _dot(a_lo, w_hi) + _dot(a_hi, w_lo) + b_ref[0]


def _modulation(c_all, w_mod, b_mod):
    depth, d, n = w_mod.shape
    r = c_all.shape[0]
    tn = 1536
    return pl.pallas_call(
        _mod_kernel,
        grid=(depth, n // tn),
        in_specs=[
            pl.BlockSpec((r, d), lambda l, j: (0, 0)),
            pl.BlockSpec((1, d, tn), lambda l, j: (l, 0, j)),
            pl.BlockSpec((1, 1, tn), lambda l, j: (l, 0, j)),
        ],
        out_specs=pl.BlockSpec((1, r, tn), lambda l, j: (l, 0, j)),
        out_shape=jax.ShapeDtypeStruct((depth, r, n), F32),
        compiler_params=_cparams("arbitrary", "arbitrary"),
        name="modulation",
    )(c_all, w_mod, b_mod.reshape(depth, 1, n))


def _inproj_kernel(x_ref, sc_ref, sh_ref, w_ref, cos_ref, sin_ref, gq_ref, gk_ref, ones_ref,
                   cu_ref, cb_ref, rq_ref, rk_ref, rv_ref, rg_ref, aq_ref, ak_ref, av_ref):
    h = (x_ref[0] * sc_ref[0] + sh_ref[0]).astype(BF16)
    cos = cos_ref[...]
    sin = sin_ref[...]
    ones = ones_ref[...]
    lane = lax.broadcasted_iota(jnp.int32, cos.shape, 1)
    first_half = (lane & 31) < 16

    def rope(xs):
        nxt = pltpu.roll(xs, LANES - 16, axis=1)
        prv = pltpu.roll(xs, 16, axis=1)
        return xs * cos + jnp.where(first_half, nxt, prv) * sin

    def rms(xs, g):
        s_hi, s_lo = _split_bf16(xs * xs)
        ssq = _dot(s_hi, ones) + _dot(s_lo, ones)
        return xs * lax.rsqrt(ssq * (1.0 / HEAD_DIM) + EPS) * g

    za = _dot(h, w_ref[:, 1792:2560])
    gq = gq_ref[...]
    for j in range(4):
        lo, hi = j * LANES, (j + 1) * LANES
        q = rope(rms(za[:, lo:hi], gq)) * ATT_Q_SCALE
        aq_ref[0, lo:hi, :] = q.T.astype(BF16)
    ak_ref[0] = rope(rms(za[:, 512:640], gk_ref[...])).astype(BF16)
    av_ref[0] = za[:, 640:768].astype(BF16)
    zr = _dot(h, w_ref[:, 768:1792])
    for j in range(2):
        lo, hi = j * LANES, (j + 1) * LANES
        rq_ref[0, lo:hi, :] = rope(zr[:, lo:hi]).T
        rk_ref[0, :, lo:hi] = rope(zr[:, 256 + lo:256 + hi] * (HEAD_DIM ** -0.5))
        rv_ref[0, lo:hi, :] = zr[:, 512 + lo:512 + hi].T
        rg_ref[0, lo:hi, :] = zr[:, 768 + lo:768 + hi].T
    zc = _dot(h, w_ref[:, 0:768])
    cu_ref[0] = zc[:, 512:768] * zc[:, 0:256]
    cb_ref[0] = zc[:, 256:512]


def _inproj(x, sc, sh, w_bf, cos, sin, gq, gk, ones_bd, tm):
    b, l, d = x.shape
    n = w_bf.shape[1]
    tok = lambda w: pl.BlockSpec((1, tm, w), lambda i, j: (i, j, 0))
    vec = pl.BlockSpec((1, 1, d), lambda i, j: (i, 0, 0))
    full = lambda s: pl.BlockSpec(s, lambda i, j: (0,) * len(s))
    widths = (256, 256, 256, 256, 256, 256, 512, 128, 128)
    dtypes = (F32,) * 6 + (BF16,) * 3
    out_specs = [tok(w) for w in widths]
    out_shape = [jax.ShapeDtypeStruct((b, l, w), dt) for w, dt in zip(widths, dtypes)]
    for o in (2, 4, 5, 6):
        out_specs[o] = pl.BlockSpec((1, widths[o], tm), lambda i, j: (i, 0, j))
        out_shape[o] = jax.ShapeDtypeStruct((b, widths[o], l), dtypes[o])
    return pl.pallas_call(
        _inproj_kernel,
        grid=(b, l // tm),
        in_specs=[tok(d), vec, vec, full((d, n)),
                  pl.BlockSpec((tm, LANES), lambda i, j: (j, 0)),
                  pl.BlockSpec((tm, LANES), lambda i, j: (j, 0)),
                  full((1, LANES)), full((1, LANES)), full((LANES, LANES))],
        out_specs=out_specs,
        out_shape=out_shape,
        compiler_params=_cparams("arbitrary", "arbitrary"),
        name="inproj",
    )(x, sc, sh, w_bf, cos, sin, gq, gk, ones_bd)


def _attn_kernel(q_ref, k_ref, vt_ref, o_ref):
    tq = q_ref.shape[2]
    q4 = q_ref[0]
    qt = jnp.concatenate([q4[h * HEAD_DIM:(h + 1) * HEAD_DIM, :] for h in range(ATT_Q_PER_KV)], axis=1)
    lk = k_ref.shape[2]
    bounds = [(lo, min(lo + ATT_KV_CHUNK, lk)) for lo in range(0, lk, ATT_KV_CHUNK)]
    r = qt.shape[1] // ATT_STREAMS
    qs = [qt[:, i * r:(i + 1) * r] for i in range(ATT_STREAMS)]
    ms = [jnp.full((1, r), -jnp.inf, F32) for _ in qs]
    accs = [jnp.zeros((ATT_V_ROWS, r), F32) for _ in qs]
    nxt = [_dot(k_ref[0, 0, bounds[0][0]:bounds[0][1], :], q) for q in qs]
    for c, (lo, hi) in enumerate(bounds):
        for i, q in enumerate(qs):
            s = nxt[i]
            if c + 1 < len(bounds):
                nxt[i] = _dot(k_ref[0, 0, bounds[c + 1][0]:bounds[c + 1][1], :], q)
            m_new = jnp.maximum(ms[i], jnp.max(s, axis=0, keepdims=True))
            p = jnp.exp2(s - m_new)
            accs[i] = accs[i] * jnp.exp2(ms[i] - m_new) + _dot(vt_ref[0, 0, :, lo:hi], p.astype(BF16))
            ms[i] = m_new
    acc = jnp.concatenate(accs, axis=1)
    o = acc[:HEAD_DIM] / acc[HEAD_DIM:HEAD_DIM + 1]
    for h in range(ATT_Q_PER_KV):
        o_ref[0, h] = o[:, h * tq:(h + 1) * tq]


def _attention(q, k, vt, tq):
    b, wq, lq = q.shape
    hkv, lk = k.shape[1], k.shape[2]
    wg = ATT_Q_PER_KV * HEAD_DIM
    return pl.pallas_call(
        _attn_kernel,
        grid=(b, hkv, lq // tq),
        in_specs=[pl.BlockSpec((1, wg, tq), lambda i, g, j: (i, g, j)),
                  pl.BlockSpec((1, 1, lk, HEAD_DIM), lambda i, g, j: (i, g, 0, 0)),
                  pl.BlockSpec((1, 1, ATT_V_ROWS, lk), lambda i, g, j: (i, g, 0, 0))],
        out_specs=pl.BlockSpec((1, ATT_Q_PER_KV, HEAD_DIM, tq), lambda i, g, j: (i, g, 0, j)),
        out_shape=jax.ShapeDtypeStruct((b, wq // HEAD_DIM, HEAD_DIM, lq), F32),
        compiler_params=_cparams("arbitrary", "arbitrary", "arbitrary"),
        name="attention",
    )(q, k, vt)


def _ret_kernel(lg_ref, q_ref, k_ref, v_ref, g_ref, s0f_ref, s0b_ref, lgf_ref, lgb_ref, lgfc_ref, lgbc_ref, gn_ref,
                o_ref, sff_ref, sfb_ref, s_scr, sb_scr, dec_scr):
    i, p, j = pl.program_id(0), pl.program_id(1), pl.program_id(2)
    nc = pl.num_programs(2)
    c = k_ref.shape[1]
    w = k_ref.shape[2]
    pos = lax.broadcasted_iota(jnp.int32, (c, w), 0).astype(F32)
    qpos = lax.broadcasted_iota(jnp.int32, (w, c), 1).astype(F32)
    lgf = lgf_ref[...]
    lgb = lgb_ref[...]
    lgf_col = lgfc_ref[...]
    lgb_col = lgbc_ref[...]

    @pl.when((i == 0) & (p == 0) & (j == 0))
    def _():
        diff = (lax.broadcasted_iota(jnp.int32, (c, c), 1) - lax.broadcasted_iota(jnp.int32, (c, c), 0)).astype(F32)
        for h in range(RET_HEADS):
            dec_scr[h] = jnp.where(diff >= 0.0, jnp.exp(lg_ref[0, h] * jnp.maximum(diff, 0.0)),
                                   jnp.exp(lg_ref[1, h] * jnp.maximum(-diff, 0.0)))

    nb = k_ref.shape[0]

    def update_state(e, vt, kz, cdec):
        for h in range(RET_HEADS):
            lo, hi = h * HEAD_DIM, (h + 1) * HEAD_DIM
            s_scr[e, h] = cdec[:, lo:hi] * s_scr[e, h] + _dot(vt[lo:hi, :], kz[:, lo:hi])

    @pl.when(p == 0)
    def _():
        @pl.when(j == 0)
        def _():
            s_scr[...] = s0b_ref[...]

        sb_scr[nc - 1 - j] = s_scr[...]
        for e in range(nb):
            kz = (k_ref[e] * jnp.exp(lgb * pos)).astype(BF16)
            update_state(e, v_ref[e].astype(BF16), kz, jnp.exp(lgb * float(c)))

        @pl.when(j == nc - 1)
        def _():
            sfb_ref[...] = s_scr[...]

    @pl.when(p == 1)
    def _():
        @pl.when(j == 0)
        def _():
            s_scr[...] = s0f_ref[...]

        for e in range(nb):
            qt = q_ref[e]
            k = k_ref[e]
            qf = (qt * jnp.exp(lgf_col * (qpos + 1.0))).astype(BF16)
            qb = (qt * jnp.exp(lgb_col * (float(c) - qpos))).astype(BF16)
            kz = (k * jnp.exp(lgf * (float(c) - 1.0 - pos))).astype(BF16)
            qh = qt.astype(BF16)
            kh = k.astype(BF16)
            vt = v_ref[e].astype(BF16)
            gate = g_ref[e]
            for h in range(RET_HEADS):
                lo, hi = h * HEAD_DIM, (h + 1) * HEAD_DIM
                sc = _dot(kh[:, lo:hi], qh[lo:hi, :])
                y = (_dot(vt[lo:hi, :], (sc * dec_scr[h]).astype(BF16))
                     + _dot(s_scr[e, h].astype(BF16), qf[lo:hi, :])
                     + _dot(sb_scr[j, e, h].astype(BF16), qb[lo:hi, :]))
                mu = jnp.mean(y, axis=0, keepdims=True)
                yc = y - mu
                var = jnp.mean(yc * yc, axis=0, keepdims=True)
                gh = gate[lo:hi, :]
                o_ref[e, lo:hi, :] = yc * lax.rsqrt(var + EPS) * gn_ref[lo:hi, :] * (gh * jax.nn.sigmoid(gh))
            update_state(e, vt, kz, jnp.exp(lgf * float(c)))

        @pl.when(j == nc - 1)
        def _():
            sff_ref[...] = s_scr[...]


def _retention(qt, k, vt, gate_t, s0_f, s0_b, lg, lgf_row, lgb_row, gn_row):
    b, l, w = k.shape
    c = min(RET_CHUNK, l)
    nc = l // c
    nb = RET_BATCH if b % RET_BATCH == 0 else 1
    st = (nb, RET_HEADS, HEAD_DIM, HEAD_DIM)
    st_spec = pl.BlockSpec(st, lambda i, p, j: (i, 0, 0, 0))
    row = pl.BlockSpec((1, w), lambda i, p, j: (0, 0))
    col = pl.BlockSpec((w, 1), lambda i, p, j: (0, 0))
    k_both = pl.BlockSpec((nb, c, w), lambda i, p, j: (i, jnp.where(p == 0, nc - 1 - j, j), 0))
    t_both = pl.BlockSpec((nb, w, c), lambda i, p, j: (i, 0, jnp.where(p == 0, nc - 1 - j, j)))
    t_fwd = pl.BlockSpec((nb, w, c), lambda i, p, j: (i, 0, p * j))
    state_shape = jax.ShapeDtypeStruct((b,) + st[1:], F32)
    return pl.pallas_call(
        _ret_kernel,
        grid=(b // nb, 2, nc),
        in_specs=[pl.BlockSpec(memory_space=pltpu.SMEM), t_fwd, k_both, t_both, t_fwd,
                  st_spec, st_spec, row, row, col, col, col],
        out_specs=[t_fwd, st_spec, st_spec],
        out_shape=[jax.ShapeDtypeStruct((b, w, l), F32), state_shape, state_shape],
        scratch_shapes=[pltpu.VMEM(st, F32), pltpu.VMEM((nc,) + st, F32), pltpu.VMEM((RET_HEADS, c, c), F32)],
        compiler_params=_cparams("arbitrary", "arbitrary", "arbitrary"),
        name="retention",
    )(lg, qt, k, vt, gate_t, s0_f, s0_b, lgf_row, lgb_row, lgf_row.reshape(w, 1), lgb_row.reshape(w, 1),
      gn_row.reshape(w, 1))


def _layer_norm(x, g, b):
    mu = jnp.mean(x, axis=-1, keepdims=True)
    xc = x - mu
    var = jnp.mean(xc * xc, axis=-1, keepdims=True)
    return xc * lax.rsqrt(var + EPS) * g + b


def _store_rows_packed(ref, x):
    tm, d = x.shape
    bits = pltpu.bitcast(x.astype(BF16).astype(F32), jnp.uint32)
    word = (bits[:, :d // 2] >> 16) | bits[:, d // 2:]
    for s in range(PACKED_ROWS):
        ref[pl.ds(s, tm, stride=PACKED_ROWS), :] = word[:, s * LANES:(s + 1) * LANES]


def _load_rows_packed(ref, tm):
    word = jnp.concatenate([ref[pl.ds(s, tm, stride=PACKED_ROWS), :] for s in range(PACKED_ROWS)], axis=1)
    lo = pltpu.bitcast(word << 16, F32).astype(BF16)
    hi = pltpu.bitcast(word & jnp.uint32(0xFFFF0000), F32).astype(BF16)
    return jnp.concatenate([lo, hi], axis=1)


def _outproj_kernel(alpha, cu_ref, cup_ref, cun_ref, cb_ref, cw_ref, ret_ref, att_ref, x_ref, g1_ref,
                    w_ref, lng_ref, lnb_ref, sc_ref, sh_ref, wr_ref, br_ref, cnt0_ref, before_ref,
                    x1_ref, h2_ref, ri_ref, rw_ref, cnt_ref, run_scr):
    i = pl.program_id(0)
    j = pl.program_id(1)

    @pl.when((i == 0) & (j == 0))
    def _():
        run_scr[...] = cnt0_ref[...]

    t = cu_ref[0]
    tm = t.shape[0]
    prev_row = jnp.where(j == 0, 0.0, cup_ref[0][7:8, :])
    next_row = jnp.where(j == pl.num_programs(1) - 1, 0.0, cun_ref[0][0:1, :])
    row = lax.broadcasted_iota(jnp.int32, t.shape, 0)
    t_prev = jnp.where(row == 0, prev_row, pltpu.roll(t, 1, axis=0))
    t_next = jnp.where(row == tm - 1, next_row, pltpu.roll(t, tm - 1, axis=0))
    conv = cb_ref[0] * (t_prev * cw_ref[0] + t * cw_ref[1] + t_next * cw_ref[2])
    conv = conv.astype(BF16)
    wrh = wr_ref[:, :ROUTER_PAD]
    rb = tm // OUT_STREAMS
    lane = lax.broadcasted_iota(jnp.int32, (rb, ROUTER_PAD), 1)
    lane_f = lane.astype(F32)
    routed = []
    for r0 in range(0, tm, rb):
        att_t = att_ref[0, :, :, r0:r0 + rb].reshape(att_ref.shape[1] * HEAD_DIM, rb).astype(BF16)
        tn = (((0,), (0,)), ((), ()))
        y = (_dot(conv[r0:r0 + rb], w_ref[0:256, :])
             + lax.dot_general(ret_ref[0, :, r0:r0 + rb].astype(BF16), w_ref[256:512, :], tn,
                               preferred_element_type=F32)
             + lax.dot_general(att_t, w_ref[512:1024, :], tn, preferred_element_type=F32))
        x1 = _layer_norm(alpha * x_ref[0, r0:r0 + rb, :] + g1_ref[0] * y, lng_ref[...], lnb_ref[...])
        x1_ref[0, r0:r0 + rb, :] = x1
        h2 = x1 * sc_ref[0] + sh_ref[0]
        _store_rows_packed(h2_ref.at[pl.ds(r0 * PACKED_ROWS, rb * PACKED_ROWS)], h2)
        h_hi, h_lo = _split_bf16(h2)
        hw = _dot(h_hi, wr_ref[...])
        logits = hw[:, :ROUTER_PAD] + hw[:, ROUTER_PAD:] + _dot(h_lo, wrh) + br_ref[...]

        work = jnp.where(lane < N_EXPERTS, logits, -jnp.inf)
        vals, firsts, sels = [], [], []
        for _ in range(TOP_K):
            m = jnp.max(work, axis=-1, keepdims=True)
            first = jnp.min(jnp.where(work == m, lane_f, float(ROUTER_PAD)), axis=-1, keepdims=True)
            sel = lane_f == first
            vals.append(m)
            firsts.append(first)
            sels.append(sel)
            work = jnp.where(sel, -jnp.inf, work)
        exps = [jnp.exp(v - vals[0]) for v in vals]
        denom = exps[0]
        for e in exps[1:]:
            denom = denom + e
        cnt = jnp.zeros(logits.shape, F32)
        for sel in sels:
            cnt = cnt + jnp.where(sel, 1.0, 0.0)
        routed.append((firsts, sels, [e / denom for e in exps], cnt))

    cnt = jnp.concatenate([r[3] for r in routed], axis=0)
    base = _dot(before_ref[...], cnt.astype(BF16)) + run_scr[...]
    run_scr[...] = run_scr[...] + jnp.sum(cnt, axis=0, keepdims=True)
    cnt_ref[...] = run_scr[...]

    for blk, (firsts, sels, wts, _) in enumerate(routed):
        r0 = blk * rb
        ri = jnp.zeros((rb, ROUTER_PAD), F32)
        rw = jnp.zeros((rb, ROUTER_PAD), F32)
        for kk in range(TOP_K):
            rank = jnp.sum(jnp.where(sels[kk], base[r0:r0 + rb], 0.0), axis=-1, keepdims=True)
            ri = jnp.where(lane == kk, firsts[kk], ri)
            ri = jnp.where(lane == TOP_K + kk, rank, ri)
            rw = jnp.where(lane == kk, wts[kk], rw)
        ri_ref[0, r0:r0 + rb, :] = ri.astype(jnp.int32)
        rw_ref[0, r0:r0 + rb, :] = rw


def _outproj(alpha, cu, cb, cw, ret, att, x, g1, w_bf, lng, lnb, sc2, sh2, wr, br, cnt0, tm):
    b, l, d = x.shape
    nj = l // tm
    tok = lambda w: pl.BlockSpec((1, tm, w), lambda i, j: (i, j, 0))
    vec = pl.BlockSpec((1, 1, d), lambda i, j: (i, 0, 0))
    full = lambda s: pl.BlockSpec(s, lambda i, j: (0,) * len(s))
    r8 = tm // 8
    nb8 = l // 8
    tri = jnp.arange(tm)
    before = (tri[None, :] < tri[:, None]).astype(BF16)
    return pl.pallas_call(
        functools.partial(_outproj_kernel, alpha),
        grid=(b, nj),
        in_specs=[tok(256),
                  pl.BlockSpec((1, 8, 256), lambda i, j: (i, jnp.maximum(j * r8 - 1, 0), 0)),
                  pl.BlockSpec((1, 8, 256), lambda i, j: (i, jnp.minimum((j + 1) * r8, nb8 - 1), 0)),
                  tok(256), full((3, 1, 256)),
                  pl.BlockSpec((1, ret.shape[1], tm), lambda i, j: (i, 0, j)),
                  pl.BlockSpec((1,) + att.shape[1:3] + (tm,), lambda i, j: (i, 0, 0, j)), tok(d), vec,
                  full((d, d)), full((1, d)), full((1, d)), vec, vec,
                  full((d, 2 * ROUTER_PAD)), full((1, ROUTER_PAD)), full((1, ROUTER_PAD)),
                  full((tm, tm))],
        out_specs=[tok(d), pl.BlockSpec((tm * PACKED_ROWS, LANES), lambda i, j: (i * nj + j, 0)),
                   tok(ROUTER_PAD), tok(ROUTER_PAD), full((1, ROUTER_PAD))],
        out_shape=[jax.ShapeDtypeStruct((b, l, d), F32),
                   jax.ShapeDtypeStruct((b * l * PACKED_ROWS, LANES), jnp.uint32),
                   jax.ShapeDtypeStruct((b, l, ROUTER_PAD), jnp.int32),
                   jax.ShapeDtypeStruct((b, l, ROUTER_PAD), F32),
                   jax.ShapeDtypeStruct((1, ROUTER_PAD), F32)],
        scratch_shapes=[pltpu.VMEM((1, ROUTER_PAD), F32)],
        compiler_params=_cparams("arbitrary", "arbitrary"),
        name="outproj_ln_router",
    )(cu, cu, cu, cb, cw, ret, att, x, g1, w_bf, lng, lnb, sc2, sh2, wr, br, cnt0, before)


def _ffn_kernel(be_ref, nv_ref, x_ref, wgu_ref, bgu_ref, wd_ref, bd_ref, o_ref, wgu_scr, wd_scr):
    j = pl.program_id(0)
    f = wd_ref.shape[2]
    tm = x_ref.shape[0] // PACKED_ROWS

    @pl.when(nv_ref[j] > 0)
    def _():
        @pl.when((j == 0) | (be_ref[j] != be_ref[jnp.maximum(j - 1, 0)]))
        def _():
            wgu_scr[...] = wgu_ref[0, 0].astype(BF16)
            wd_scr[...] = wd_ref[0, 0].astype(BF16)

        x = _load_rows_packed(x_ref, tm)
        bgu = bgu_ref[0, 0]
        acts = []
        for lo in range(0, f, FFN_BLOCK):
            hi = lo + FFN_BLOCK
            gate = jnp.minimum(_dot(x, wgu_scr[:, lo:hi]) + bgu[:, lo:hi], SWIGLU_LIMIT)
            up = jnp.clip(_dot(x, wgu_scr[:, f + lo:f + hi]) + bgu[:, f + lo:f + hi], -SWIGLU_LIMIT, SWIGLU_LIMIT)
            acts.append(((up + 1.0) * (gate * jax.nn.sigmoid(SWIGLU_ALPHA * gate))).astype(BF16))
        act = jnp.concatenate(acts, axis=1)
        _store_rows_packed(o_ref, _dot(act, wd_scr[...]) + bd_ref[0, 0])


def _expert_ffn(layer, block_e, n_valid, xs, wgu, bgu, wd, bd, tm):
    n_rows = xs.shape[0] // PACKED_ROWS
    depth, ne, d, f2 = wgu.shape
    f = f2 // 2
    rows = pl.BlockSpec((tm * PACKED_ROWS, LANES), lambda j, be, nu: (j, 0))
    grid_spec = pltpu.PrefetchScalarGridSpec(
        num_scalar_prefetch=2,
        grid=(n_rows // tm,),
        in_specs=[rows,
                  pl.BlockSpec((1, 1, d, f2), lambda j, be, nu: (layer, be[j], 0, 0)),
                  pl.BlockSpec((1, 1, 1, f2), lambda j, be, nu: (layer, be[j], 0, 0)),
                  pl.BlockSpec((1, 1, f, d), lambda j, be, nu: (layer, be[j], 0, 0)),
                  pl.BlockSpec((1, 1, 1, d), lambda j, be, nu: (layer, be[j], 0, 0))],
        out_specs=rows,
        scratch_shapes=[pltpu.VMEM((d, f2), BF16), pltpu.VMEM((f, d), BF16)],
    )
    return pl.pallas_call(
        _ffn_kernel,
        grid_spec=grid_spec,
        out_shape=jax.ShapeDtypeStruct((n_rows * PACKED_ROWS, LANES), jnp.uint32),
        compiler_params=_cparams("arbitrary"),
        name="expert_ffn",
    )(block_e, n_valid, xs, wgu, bgu.reshape(depth, ne, 1, f2), wd, bd.reshape(depth, ne, 1, d))


def _sc_worker_base(per_worker):
    return (lax.axis_index("s") * SC_CORES + lax.axis_index("c")) * per_worker


def _sc_dispatch(rows, dest, n_out):
    t = rows.shape[0]
    kk = dest.shape[0] // t
    w = SC_WINDOW
    per_worker = t // (SC_CORES * SC_SUBCORES)
    assert per_worker * SC_CORES * SC_SUBCORES == t and per_worker % w == 0
    mesh = plsc.VectorSubcoreMesh(core_axis_name="c", subcore_axis_name="s")

    n_win = per_worker // w
    row_buf = pltpu.VMEM((w,) + rows.shape[1:], rows.dtype)

    @functools.partial(
        pl.kernel, mesh=mesh,
        out_type=jax.ShapeDtypeStruct((n_out,) + rows.shape[1:], rows.dtype),
        scratch_types=[pltpu.VMEM((w,), jnp.int32)] * (2 * kk) + [row_buf] * 2 + [pltpu.SemaphoreType.DMA] * 4)
    def scatter_rows(r_hbm, d_hbm, o_hbm, *scratch):
        idx_v = (scratch[:kk], scratch[kk:2 * kk])
        rows_v = scratch[2 * kk:2 * kk + 2]
        rsem = scratch[2 * kk + 2:2 * kk + 4]
        ssem = scratch[2 * kk + 4:2 * kk + 6]
        base = _sc_worker_base(per_worker)

        def read(win, slot):
            return pltpu.make_async_copy(r_hbm.at[pl.ds(base + win * w, w)], rows_v[slot], rsem[slot])

        def scatter(slot, s):
            return pltpu.make_async_copy(rows_v[slot], o_hbm.at[idx_v[slot][s]], ssem[slot])

        def start_read(win, slot):
            read(win, slot).start()
            for s in range(kk):
                pltpu.sync_copy(d_hbm.at[pl.ds(s * t + base + win * w, w)], idx_v[slot][s])

        start_read(0, 0)

        @pl.loop(0, (n_win + 1) // 2)
        def _(pair):
            for slot in range(2):
                win = 2 * pair + slot
                other = 1 - slot

                @pl.when(win < n_win)
                def _():
                    @pl.when(win >= 1)
                    def _():
                        for s in range(kk):
                            scatter(other, s).wait()

                    @pl.when(win + 1 < n_win)
                    def _():
                        start_read(win + 1, other)

                    read(win, slot).wait()
                    for s in range(kk):
                        scatter(slot, s).start()

        for s in range(kk):
            scatter((n_win - 1) % 2, s).wait()

    return scatter_rows(rows, dest)


def _sc_gather(table, idx):
    n = idx.shape[0]
    w = SC_WINDOW
    per_worker = n // (SC_CORES * SC_SUBCORES)
    assert per_worker * SC_CORES * SC_SUBCORES == n and per_worker % w == 0
    mesh = plsc.VectorSubcoreMesh(core_axis_name="c", subcore_axis_name="s")

    n_win = per_worker // w
    assert n_win % 2 == 0
    row_buf = pltpu.VMEM((w,) + table.shape[1:], table.dtype)

    @functools.partial(
        pl.kernel, mesh=mesh,
        out_type=jax.ShapeDtypeStruct((n,) + table.shape[1:], table.dtype),
        scratch_types=[pltpu.VMEM((w,), jnp.int32)] * 2 + [row_buf] * 2 + [pltpu.SemaphoreType.DMA] * 4)
    def gather_rows(t_hbm, i_hbm, o_hbm, idx0, idx1, rows0, rows1, gsem0, gsem1, wsem0, wsem1):
        idx_v, rows_v = (idx0, idx1), (rows0, rows1)
        gsem, wsem = (gsem0, gsem1), (wsem0, wsem1)
        base = _sc_worker_base(per_worker)

        def gather(slot):
            return pltpu.make_async_copy(t_hbm.at[idx_v[slot]], rows_v[slot], gsem[slot])

        def write(win, slot):
            return pltpu.make_async_copy(rows_v[slot], o_hbm.at[pl.ds(base + win * w, w)], wsem[slot])

        def start_gather(win, slot):
            pltpu.sync_copy(i_hbm.at[pl.ds(base + win * w, w)], idx_v[slot])
            gather(slot).start()

        start_gather(0, 0)

        @pl.loop(0, n_win // 2)
        def _(pair):
            for slot in range(2):
                win = 2 * pair + slot
                other = 1 - slot

                @pl.when(win >= 1)
                def _():
                    write(win - 1, other).wait()

                @pl.when(win + 1 < n_win)
                def _():
                    start_gather(win + 1, other)

                gather(slot).wait()
                write(win, slot).start()

        write(n_win - 1, 1).wait()

    return gather_rows(table, idx)


def _combine_kernel(alpha, g_ref, w_ref, x_ref, g2_ref, lng_ref, lnb_ref, *rest):
    o_ref = rest[-1]
    tm = x_ref.shape[1]
    w = w_ref[0]
    f = _load_rows_packed(g_ref.at[0], tm).astype(F32) * w[:, 0:1]
    for kk in range(1, TOP_K):
        f = f + _load_rows_packed(g_ref.at[kk], tm).astype(F32) * w[:, kk:kk + 1]
    o_ref[0] = _layer_norm(alpha * x_ref[0] + g2_ref[0] * f, lng_ref[...], lnb_ref[...])


def _combine(alpha, gathered, b0, nb, wts, x1, g2, lng, lnb, tm, prev=None):
    b, l, d = x1.shape
    nj = l // tm
    tok = pl.BlockSpec((1, tm, d), lambda i, j: (b0 + i, j, 0))
    full = pl.BlockSpec((1, d), lambda i, j: (0, 0))
    in_specs = [pl.BlockSpec((TOP_K, tm * PACKED_ROWS, LANES), lambda i, j: (0, i * nj + j, 0)),
                pl.BlockSpec((1, tm, ROUTER_PAD), lambda i, j: (b0 + i, j, 0)),
                tok, pl.BlockSpec((1, 1, d), lambda i, j: (b0 + i, 0, 0)), full, full]
    args = [gathered, wts, x1, g2, lng, lnb]
    aliases = {}
    if prev is not None:
        in_specs.append(pl.BlockSpec(memory_space=pl.ANY))
        args.append(prev)
        aliases = {len(args) - 1: 0}
    return pl.pallas_call(
        functools.partial(_combine_kernel, alpha),
        grid=(nb, nj),
        in_specs=in_specs,
        out_specs=tok,
        out_shape=jax.ShapeDtypeStruct((b, l, d), F32),
        input_output_aliases=aliases,
        compiler_params=_cparams("arbitrary", "arbitrary"),
        name="combine_ln",
    )(*args)


def _rope_tables(l):
    rows = l // GRID_W
    axis_dim = HEAD_DIM // 2
    inv_freq = ROPE_BASE ** (-jnp.arange(0, axis_dim, 2, dtype=F32) / axis_dim)
    row = jnp.repeat(jnp.arange(rows, dtype=F32), GRID_W)
    col = jnp.tile(jnp.arange(GRID_W, dtype=F32), rows)
    ang = jnp.stack([row[:, None] * inv_freq, col[:, None] * inv_freq], axis=1)
    cos, sin = jnp.cos(ang), jnp.sin(ang)
    cos64 = jnp.broadcast_to(cos[:, :, None, :], (l, 2, 2, HEAD_DIM // 4)).reshape(l, HEAD_DIM)
    sin64 = jnp.stack([-sin, sin], axis=2).reshape(l, HEAD_DIM)
    return jnp.tile(cos64, (1, LANES // HEAD_DIM)), jnp.tile(sin64, (1, LANES // HEAD_DIM))


def _route_tables(ri, counts, tm):
    t = ri.shape[0]
    top_e = ri[:, :TOP_K]
    rank = ri[:, TOP_K:2 * TOP_K]
    padded = (counts + tm - 1) // tm * tm
    pad_end = jnp.cumsum(padded)
    pad_start = pad_end - padded
    experts = jnp.arange(N_EXPERTS, dtype=jnp.int32)
    start = jnp.sum(jnp.where(top_e[:, :, None] == experts, pad_start, 0), axis=-1)
    dest = (start + rank).T.reshape(TOP_K * t)
    n_tiles = (t * TOP_K + N_EXPERTS * (tm - 1) + tm - 1) // tm
    tile_start = jnp.arange(n_tiles, dtype=jnp.int32) * tm
    block_e = jnp.minimum(jnp.sum((pad_end[None, :] <= tile_start[:, None]).astype(jnp.int32), axis=1), N_EXPERTS - 1)
    in_expert = jnp.where(block_e[:, None] == experts[None, :], (pad_start + counts)[None, :], 0).sum(axis=1)
    n_valid = jnp.clip(in_expert - tile_start, 0, tm)
    return dest, block_e.astype(jnp.int32), n_valid.astype(jnp.int32), n_tiles


def _layer_params(layer, d, w_in, conv_w, ret_decay_exp, ret_gn_g, q_norm_g, k_norm_g, w_out, ln_g, ln_b,
                  w_router, b_router):
    log_gamma = jnp.log1p(-jnp.exp2(-ret_decay_exp[layer].astype(F32)))
    lg_rows = jnp.repeat(log_gamma, HEAD_DIM, axis=1)
    wr_pad = jnp.zeros((d, ROUTER_PAD), F32).at[:, :N_EXPERTS].set(w_router[layer])
    wrh = wr_pad.astype(BF16)
    return dict(
        w_in=w_in[layer].astype(BF16), w_out=w_out[layer].astype(BF16),
        gq=jnp.tile(q_norm_g[layer], LANES // HEAD_DIM).reshape(1, LANES),
        gk=jnp.tile(k_norm_g[layer], LANES // HEAD_DIM).reshape(1, LANES),
        log_gamma=log_gamma, lgf_row=lg_rows[0:1], lgb_row=lg_rows[1:2], gn_row=ret_gn_g[layer].reshape(1, -1),
        cw=conv_w[layer].T.reshape(3, 1, -1),
        wr=jnp.concatenate([wrh, (wr_pad - wrh.astype(F32)).astype(BF16)], axis=1),
        br=jnp.zeros((1, ROUTER_PAD), F32).at[0, :N_EXPERTS].set(b_router[layer]),
        lng1=ln_g[layer, 0].reshape(1, d), lnb1=ln_b[layer, 0].reshape(1, d),
        lng2=ln_g[layer, 1].reshape(1, d), lnb2=ln_b[layer, 1].reshape(1, d))


def _layer(layer, last, alpha, x, ctx, m_lat, m_ctx, p, tables, experts):
    b, l, d = x.shape
    lc = ctx.shape[1]
    tm_lat, tm_ctx, tm_moe, tq = 512, 256, 1024, 256
    cos_l, sin_l, cos_c, sin_c, ones_bd, zero_state = tables
    sh1, sc1, g1, sh2, sc2, g2 = (m_lat[:, :, i] for i in range(N_MOD))
    sh1c, sc1c, g1c, sh2c, sc2c, g2c = (m_ctx[:, :, i] for i in range(N_MOD))

    zc = _inproj(ctx, 1.0 + sc1c, sh1c, p["w_in"], cos_c, sin_c, p["gq"], p["gk"], ones_bd, tm_ctx)
    zl = _inproj(x, 1.0 + sc1, sh1, p["w_in"], cos_l, sin_l, p["gq"], p["gk"], ones_bd, tm_lat)
    cu_c, cb_c, rq_c, rk_c, rv_c, rg_c, aq_c, ak_c, av_c = zc
    cu_l, cb_l, rq_l, rk_l, rv_l, rg_l, aq_l, ak_l, av_l = zl

    ret_args = (p["log_gamma"], p["lgf_row"], p["lgb_row"], p["gn_row"])
    ret_c, s_fwd, s_bwd = _retention(rq_c, rk_c, rv_c, rg_c, zero_state, zero_state, *ret_args)
    ret_l, _, _ = _retention(rq_l, rk_l, rv_l, rg_l, s_fwd, s_bwd, *ret_args)

    def kv_heads(a):
        return a.reshape(b, a.shape[1], -1, HEAD_DIM).transpose(0, 2, 1, 3)

    def v_heads(a):
        vt = a.reshape(b, a.shape[1], -1, HEAD_DIM).transpose(0, 2, 3, 1)
        ones = jnp.ones(vt.shape[:2] + (1, vt.shape[3]), BF16)
        pad = jnp.zeros(vt.shape[:2] + (ATT_V_ROWS - HEAD_DIM - 1, vt.shape[3]), BF16)
        return jnp.concatenate([vt, ones, pad], axis=2)

    k_all = kv_heads(jnp.concatenate([ak_c, ak_l], axis=1))
    v_all = v_heads(jnp.concatenate([av_c, av_l], axis=1))
    att_l = _attention(aq_l, k_all, v_all, tq)

    out_args = (p["w_out"], p["lng1"], p["lnb1"])
    rt_args = (p["wr"], p["br"])
    cnt0 = jnp.zeros((1, ROUTER_PAD), F32)
    if not last:
        att_c = _attention(aq_c, kv_heads(ak_c), v_heads(av_c), min(tq, lc))
        ctx1, h2_c, ri_c, rw_c, cnt0 = _outproj(alpha, cu_c, cb_c, p["cw"], ret_c, att_c, ctx, g1c, *out_args,
                                                1.0 + sc2c, sh2c, *rt_args, cnt0, tm_ctx)
    x1, h2_l, ri_l, rw_l, cnt = _outproj(alpha, cu_l, cb_l, p["cw"], ret_l, att_l, x, g1, *out_args,
                                         1.0 + sc2, sh2, *rt_args, cnt0, tm_lat)
    if not last:
        n_c = b * lc
        h2 = jnp.concatenate([h2_c, h2_l], axis=0)
        ri = jnp.concatenate([ri_c.reshape(n_c, -1), ri_l.reshape(b * l, -1)], axis=0)
    else:
        n_c = 0
        h2 = h2_l
        ri = ri_l.reshape(b * l, -1)
    n_tok = n_c + b * l

    counts = cnt[0, :N_EXPERTS].astype(jnp.int32)
    dest, block_e, n_valid, n_tiles = _route_tables(ri, counts, tm_moe)
    xs = _sc_dispatch(h2.reshape(n_tok, PACKED_ROWS, LANES), dest, n_tiles * tm_moe)
    ys = _expert_ffn(layer, block_e, n_valid, xs.reshape(-1, LANES), *experts, tm_moe)
    ys = ys.reshape(-1, PACKED_ROWS, LANES)
    dest = dest.reshape(TOP_K, n_tok)

    def gather(lo, hi):
        rows = _sc_gather(ys, dest[:, lo:hi].reshape(-1))
        return rows.reshape(TOP_K, (hi - lo) * PACKED_ROWS, LANES)

    if not last:
        ctx = _combine(alpha, gather(0, n_c), 0, b, rw_c, ctx1, g2c, p["lng2"], p["lnb2"], tm_ctx)
    nb = b // COMBINE_CHUNKS if b % COMBINE_CHUNKS == 0 else b
    x_new = None
    for b0 in range(0, b, nb):
        g = gather(n_c + b0 * l, n_c + (b0 + nb) * l)
        x_new = _combine(alpha, g, b0, nb, rw_l, x1, g2, p["lng2"], p["lnb2"], tm_lat, prev=x_new)
    return x_new, ctx


def kernel(x, c, ctx, c_ctx, w_mod, b_mod, w_in, conv_w, ret_decay_exp, ret_gn_g, q_norm_g, k_norm_g, w_out,
           ln_g, ln_b, w_router, b_router, w_gate_up, b_gate_up, w_down, b_down):
    depth = w_mod.shape[0]
    alpha = (2.0 * depth) ** 0.25
    b, l, d = x.shape
    lc = ctx.shape[1]
    groups = BATCH_GROUPS if b % BATCH_GROUPS == 0 else 1
    bg = b // groups

    n_rows = (b + 1 + 7) // 8 * 8
    c_all = jnp.zeros((n_rows, d), F32).at[:b].set(c).at[b].set(c_ctx)
    mod = _modulation(c_all, w_mod, b_mod)

    cos_l, sin_l = _rope_tables(l)
    cos_c, sin_c = jnp.ones((lc, LANES), F32), jnp.zeros((lc, LANES), F32)
    lane_head = jnp.arange(LANES) // HEAD_DIM
    ones_bd = (lane_head[:, None] == lane_head[None, :]).astype(BF16)
    zero_state = jnp.zeros((bg, RET_HEADS, HEAD_DIM, HEAD_DIM), F32)
    tables = (cos_l, sin_l, cos_c, sin_c, ones_bd, zero_state)
    experts = (w_gate_up, b_gate_up, w_down, b_down)

    xs = [x[g * bg:(g + 1) * bg] for g in range(groups)]
    cs = [ctx[g * bg:(g + 1) * bg] for g in range(groups)]
    for layer in range(depth):
        last = layer == depth - 1
        p = _layer_params(layer, d, w_in, conv_w, ret_decay_exp, ret_gn_g, q_norm_g, k_norm_g, w_out, ln_g, ln_b,
                          w_router, b_router)
        m_ctx = jnp.broadcast_to(mod[layer, b].reshape(1, 1, N_MOD, d), (bg, 1, N_MOD, d))
        for g in range(groups):
            m_lat = mod[layer, g * bg:(g + 1) * bg].reshape(bg, 1, N_MOD, d)
            xs[g], cs[g] = _layer(layer, last, alpha, xs[g], cs[g], m_lat, m_ctx, p, tables, experts)
    return jnp.concatenate(xs, axis=0) if groups > 1 else xs[0]
```

```python
import functools

import jax
import jax.numpy as jnp
from jax import lax
from jax.experimental import pallas as pl
from jax.experimental.pallas import tpu as pltpu
from jax.experimental.pallas import tpu_sc as plsc

F32 = jnp.float32
BF16 = jnp.bfloat16

HEAD_DIM = 64
GRID_W = 64
ROPE_BASE = 10000.0
N_EXPERTS = 32
TOP_K = 4
SWIGLU_ALPHA = 1.702
SWIGLU_LIMIT = 7.0
N_MOD = 6
EPS = 1e-6
RET_HEADS = 4
RET_CHUNK = 256
RET_BATCH = 8
ATT_Q_PER_KV = 4
ATT_KV_CHUNK = 256
ATT_Q_SCALE = HEAD_DIM ** -0.5 * 1.4426950408889634
ATT_STREAMS = 2
ATT_V_ROWS = 80
LANES = 128
ROUTER_PAD = 128
VMEM_LIMIT = 56 * 1024 * 1024
SUBLANES = 8
PACKED_ROWS = 4
FFN_BLOCK = 256
OUT_STREAMS = 2
COMBINE_CHUNKS = 4
SC_CORES = 2
SC_SUBCORES = 16
SC_WINDOW = 64
BATCH_GROUPS = 1


def _dot(a, b):
    return jnp.dot(a, b, preferred_element_type=F32)


def _dot_nt(a, b):
    return lax.dot_general(a, b, (((1,), (1,)), ((), ())), preferred_element_type=F32)


def _split_bf16(x):
    hi = x.astype(BF16)
    lo = (x - hi.astype(F32)).astype(BF16)
    return hi, lo


def _cparams(*sem):
    return pltpu.CompilerParams(dimension_semantics=sem, vmem_limit_bytes=VMEM_LIMIT)


def _mod_kernel(c_ref, w_ref, b_ref, o_ref):
    c = c_ref[...]
    a = c * jax.nn.sigmoid(c)
    a_hi, a_lo = _split_bf16(a)
    w_hi, w_lo = _split_bf16(w_ref[0])
    o_ref[0] = _dot(a_hi, w_hi) + _dot(a_lo, w_hi) + _dot(a_hi, w_lo) + b_ref[0]


def _modulation(c_all, w_mod, b_mod):
    depth, d, n = w_mod.shape
    r = c_all.shape[0]
    tn = 1536
    return pl.pallas_call(
        _mod_kernel,
        grid=(depth, n // tn),
        in_specs=[
            pl.BlockSpec((r, d), lambda l, j: (0, 0)),
            pl.BlockSpec((1, d, tn), lambda l, j: (l, 0, j)),
            pl.BlockSpec((1, 1, tn), lambda l, j: (l, 0, j)),
        ],
        out_specs=pl.BlockSpec((1, r, tn), lambda l, j: (l, 0, j)),
        out_shape=jax.ShapeDtypeStruct((depth, r, n), F32),
        compiler_params=_cparams("arbitrary", "arbitrary"),
        name="modulation",
    )(c_all, w_mod, b_mod.reshape(depth, 1, n))


def _inproj_kernel(x_ref, sc_ref, sh_ref, w_ref, cos_ref, sin_ref, gq_ref, gk_ref, ones_ref,
                   cu_ref, cb_ref, rq_ref, rk_ref, rv_ref, rg_ref, aq_ref, ak_ref, av_ref):
    h = (x_ref[0] * sc_ref[0] + sh_ref[0]).astype(BF16)
    cos = cos_ref[...]
    sin = sin_ref[...]
    ones = ones_ref[...]
    lane = lax.broadcasted_iota(jnp.int32, cos.shape, 1)
    first_half = (lane & 31) < 16

    def rope(xs):
        nxt = pltpu.roll(xs, LANES - 16, axis=1)
        prv = pltpu.roll(xs, 16, axis=1)
        return xs * cos + jnp.where(first_half, nxt, prv) * sin

    def rms(xs, g):
        s_hi, s_lo = _split_bf16(xs * xs)
        ssq = _dot(s_hi, ones) + _dot(s_lo, ones)
        return xs * lax.rsqrt(ssq * (1.0 / HEAD_DIM) + EPS) * g

    za = _dot(h, w_ref[:, 1792:2560])
    gq = gq_ref[...]
    for j in range(4):
        lo, hi = j * LANES, (j + 1) * LANES
        q = rope(rms(za[:, lo:hi], gq)) * ATT_Q_SCALE
        aq_ref[0, lo:hi, :] = q.T.astype(BF16)
    ak_ref[0] = rope(rms(za[:, 512:640], gk_ref[...])).astype(BF16)
    av_ref[0] = za[:, 640:768].astype(BF16)
    zr = _dot(h, w_ref[:, 768:1792])
    for j in range(2):
        lo, hi = j * LANES, (j + 1) * LANES
        rq_ref[0, lo:hi, :] = rope(zr[:, lo:hi]).T
        rk_ref[0, :, lo:hi] = rope(zr[:, 256 + lo:256 + hi] * (HEAD_DIM ** -0.5))
        rv_ref[0, lo:hi, :] = zr[:, 512 + lo:512 + hi].T
        rg_ref[0, lo:hi, :] = zr[:, 768 + lo:768 + hi].T
    zc = _dot(h, w_ref[:, 0:768])
    cu_ref[0] = zc[:, 512:768] * zc[:, 0:256]
    cb_ref[0] = zc[:, 256:512]


def _inproj(x, sc, sh, w_bf, cos, sin, gq, gk, ones_bd, tm):
    b, l, d = x.shape
    n = w_bf.shape[1]
    tok = lambda w: pl.BlockSpec((1, tm, w), lambda i, j: (i, j, 0))
    vec = pl.BlockSpec((1, 1, d), lambda i, j: (i, 0, 0))
    full = lambda s: pl.BlockSpec(s, lambda i, j: (0,) * len(s))
    widths = (256, 256, 256, 256, 256, 256, 512, 128, 128)
    dtypes = (F32,) * 6 + (BF16,) * 3
    out_specs = [tok(w) for w in widths]
    out_shape = [jax.ShapeDtypeStruct((b, l, w), dt) for w, dt in zip(widths, dtypes)]
    for o in (2, 4, 5, 6):
        out_specs[o] = pl.BlockSpec((1, widths[o], tm), lambda i, j: (i, 0, j))
        out_shape[o] = jax.ShapeDtypeStruct((b, widths[o], l), dtypes[o])
    return pl.pallas_call(
        _inproj_kernel,
        grid=(b, l // tm),
        in_specs=[tok(d), vec, vec, full((d, n)),
                  pl.BlockSpec((tm, LANES), lambda i, j: (j, 0)),
                  pl.BlockSpec((tm, LANES), lambda i, j: (j, 0)),
                  full((1, LANES)), full((1, LANES)), full((LANES, LANES))],
        out_specs=out_specs,
        out_shape=out_shape,
        compiler_params=_cparams("arbitrary", "arbitrary"),
        name="inproj",
    )(x, sc, sh, w_bf, cos, sin, gq, gk, ones_bd)


def _attn_kernel(q_ref, k_ref, vt_ref, o_ref):
    tq = q_ref.shape[2]
    q4 = q_ref[0]
    qt = jnp.concatenate([q4[h * HEAD_DIM:(h + 1) * HEAD_DIM, :] for h in range(ATT_Q_PER_KV)], axis=1)
    lk = k_ref.shape[2]
    bounds = [(lo, min(lo + ATT_KV_CHUNK, lk)) for lo in range(0, lk, ATT_KV_CHUNK)]
    r = qt.shape[1] // ATT_STREAMS
    qs = [qt[:, i * r:(i + 1) * r] for i in range(ATT_STREAMS)]
    ms = [jnp.full((1, r), -jnp.inf, F32) for _ in qs]
    accs = [jnp.zeros((ATT_V_ROWS, r), F32) for _ in qs]
    nxt = [_dot(k_ref[0, 0, bounds[0][0]:bounds[0][1], :], q) for q in qs]
    for c, (lo, hi) in enumerate(bounds):
        for i, q in enumerate(qs):
            s = nxt[i]
            if c + 1 < len(bounds):
                nxt[i] = _dot(k_ref[0, 0, bounds[c + 1][0]:bounds[c + 1][1], :], q)
            m_new = jnp.maximum(ms[i], jnp.max(s, axis=0, keepdims=True))
            p = jnp.exp2(s - m_new)
            accs[i] = accs[i] * jnp.exp2(ms[i] - m_new) + _dot(vt_ref[0, 0, :, lo:hi], p.astype(BF16))
            ms[i] = m_new
    acc = jnp.concatenate(accs, axis=1)
    o = acc[:HEAD_DIM] / acc[HEAD_DIM:HEAD_DIM + 1]
    for h in range(ATT_Q_PER_KV):
        o_ref[0, h] = o[:, h * tq:(h + 1) * tq]


def _attention(q, k, vt, tq):
    b, wq, lq = q.shape
    hkv, lk = k.shape[1], k.shape[2]
    wg = ATT_Q_PER_KV * HEAD_DIM
    return pl.pallas_call(
        _attn_kernel,
        grid=(b, hkv, lq // tq),
        in_specs=[pl.BlockSpec((1, wg, tq), lambda i, g, j: (i, g, j)),
                  pl.BlockSpec((1, 1, lk, HEAD_DIM), lambda i, g, j: (i, g, 0, 0)),
                  pl.BlockSpec((1, 1, ATT_V_ROWS, lk), lambda i, g, j: (i, g, 0, 0))],
        out_specs=pl.BlockSpec((1, ATT_Q_PER_KV, HEAD_DIM, tq), lambda i, g, j: (i, g, 0, j)),
        out_shape=jax.ShapeDtypeStruct((b, wq // HEAD_DIM, HEAD_DIM, lq), F32),
        compiler_params=_cparams("arbitrary", "arbitrary", "arbitrary"),
        name="attention",
    )(q, k, vt)


def _ret_kernel(lg_ref, q_ref, k_ref, v_ref, g_ref, s0f_ref, s0b_ref, lgf_ref, lgb_ref, lgfc_ref, lgbc_ref, gn_ref,
                o_ref, sff_ref, sfb_ref, s_scr, sb_scr, dec_scr):
    i, p, j = pl.program_id(0), pl.program_id(1), pl.program_id(2)
    nc = pl.num_programs(2)
    c = k_ref.shape[1]
    w = k_ref.shape[2]
    pos = lax.broadcasted_iota(jnp.int32, (c, w), 0).astype(F32)
    qpos = lax.broadcasted_iota(jnp.int32, (w, c), 1).astype(F32)
    lgf = lgf_ref[...]
    lgb = lgb_ref[...]
    lgf_col = lgfc_ref[...]
    lgb_col = lgbc_ref[...]

    @pl.when((i == 0) & (p == 0) & (j == 0))
    def _():
        diff = (lax.broadcasted_iota(jnp.int32, (c, c), 1) - lax.broadcasted_iota(jnp.int32, (c, c), 0)).astype(F32)
        for h in range(RET_HEADS):
            dec_scr[h] = jnp.where(diff >= 0.0, jnp.exp(lg_ref[0, h] * jnp.maximum(diff, 0.0)),
                                   jnp.exp(lg_ref[1, h] * jnp.maximum(-diff, 0.0)))

    nb = k_ref.shape[0]

    def update_state(e, vt, kz, cdec):
        for h in range(RET_HEADS):
            lo, hi = h * HEAD_DIM, (h + 1) * HEAD_DIM
            s_scr[e, h] = cdec[:, lo:hi] * s_scr[e, h] + _dot(vt[lo:hi, :], kz[:, lo:hi])

    @pl.when(p == 0)
    def _():
        @pl.when(j == 0)
        def _():
            s_scr[...] = s0b_ref[...]

        sb_scr[nc - 1 - j] = s_scr[...]
        for e in range(nb):
            kz = (k_ref[e] * jnp.exp(lgb * pos)).astype(BF16)
            update_state(e, v_ref[e].astype(BF16), kz, jnp.exp(lgb * float(c)))

        @pl.when(j == nc - 1)
        def _():
            sfb_ref[...] = s_scr[...]

    @pl.when(p == 1)
    def _():
        @pl.when(j == 0)
        def _():
            s_scr[...] = s0f_ref[...]

        for e in range(nb):
            qt = q_ref[e]
            k = k_ref[e]
            qf = (qt * jnp.exp(lgf_col * (qpos + 1.0))).astype(BF16)
            qb = (qt * jnp.exp(lgb_col * (float(c) - qpos))).astype(BF16)
            kz = (k * jnp.exp(lgf * (float(c) - 1.0 - pos))).astype(BF16)
            qh = qt.astype(BF16)
            kh = k.astype(BF16)
            vt = v_ref[e].astype(BF16)
            gate = g_ref[e]
            for h in range(RET_HEADS):
                lo, hi = h * HEAD_DIM, (h + 1) * HEAD_DIM
                sc = _dot(kh[:, lo:hi], qh[lo:hi, :])
                y = (_dot(vt[lo:hi, :], (sc * dec_scr[h]).astype(BF16))
                     + _dot(s_scr[e, h].astype(BF16), qf[lo:hi, :])
                     + _dot(sb_scr[j, e, h].astype(BF16), qb[lo:hi, :]))
                mu = jnp.mean(y, axis=0, keepdims=True)
                yc = y - mu
                var = jnp.mean(yc * yc, axis=0, keepdims=True)
                gh = gate[lo:hi, :]
                o_ref[e, lo:hi, :] = yc * lax.rsqrt(var + EPS) * gn_ref[lo:hi, :] * (gh * jax.nn.sigmoid(gh))
            update_state(e, vt, kz, jnp.exp(lgf * float(c)))

        @pl.when(j == nc - 1)
        def _():
            sff_ref[...] = s_scr[...]


def _retention(qt, k, vt, gate_t, s0_f, s0_b, lg, lgf_row, lgb_row, gn_row):
    b, l, w = k.shape
    c = min(RET_CHUNK, l)
    nc = l // c
    nb = RET_BATCH if b % RET_BATCH == 0 else 1
    st = (nb, RET_HEADS, HEAD_DIM, HEAD_DIM)
    st_spec = pl.BlockSpec(st, lambda i, p, j: (i, 0, 0, 0))
    row = pl.BlockSpec((1, w), lambda i, p, j: (0, 0))
    col = pl.BlockSpec((w, 1), lambda i, p, j: (0, 0))
    k_both = pl.BlockSpec((nb, c, w), lambda i, p, j: (i, jnp.where(p == 0, nc - 1 - j, j), 0))
    t_both = pl.BlockSpec((nb, w, c), lambda i, p, j: (i, 0, jnp.where(p == 0, nc - 1 - j, j)))
    t_fwd = pl.BlockSpec((nb, w, c), lambda i, p, j: (i, 0, p * j))
    state_shape = jax.ShapeDtypeStruct((b,) + st[1:], F32)
    return pl.pallas_call(
        _ret_kernel,
        grid=(b // nb, 2, nc),
        in_specs=[pl.BlockSpec(memory_space=pltpu.SMEM), t_fwd, k_both, t_both, t_fwd,
                  st_spec, st_spec, row, row, col, col, col],
        out_specs=[t_fwd, st_spec, st_spec],
        out_shape=[jax.ShapeDtypeStruct((b, w, l), F32), state_shape, state_shape],
        scratch_shapes=[pltpu.VMEM(st, F32), pltpu.VMEM((nc,) + st, F32), pltpu.VMEM((RET_HEADS, c, c), F32)],
        compiler_params=_cparams("arbitrary", "arbitrary", "arbitrary"),
        name="retention",
    )(lg, qt, k, vt, gate_t, s0_f, s0_b, lgf_row, lgb_row, lgf_row.reshape(w, 1), lgb_row.reshape(w, 1),
      gn_row.reshape(w, 1))


def _layer_norm(x, g, b):
    mu = jnp.mean(x, axis=-1, keepdims=True)
    xc = x - mu
    var = jnp.mean(xc * xc, axis=-1, keepdims=True)
    return xc * lax.rsqrt(var + EPS) * g + b


def _store_rows_packed(ref, x):
    tm, d = x.shape
    bits = pltpu.bitcast(x.astype(BF16).astype(F32), jnp.uint32)
    word = (bits[:, :d // 2] >> 16) | bits[:, d // 2:]
    for s in range(PACKED_ROWS):
        ref[pl.ds(s, tm, stride=PACKED_ROWS), :] = word[:, s * LANES:(s + 1) * LANES]


def _load_rows_packed(ref, tm):
    word = jnp.concatenate([ref[pl.ds(s, tm, stride=PACKED_ROWS), :] for s in range(PACKED_ROWS)], axis=1)
    lo = pltpu.bitcast(word << 16, F32).astype(BF16)
    hi = pltpu.bitcast(word & jnp.uint32(0xFFFF0000), F32).astype(BF16)
    return jnp.concatenate([lo, hi], axis=1)


def _outproj_kernel(alpha, cu_ref, cup_ref, cun_ref, cb_ref, cw_ref, ret_ref, att_ref, x_ref, g1_ref,
                    w_ref, lng_ref, lnb_ref, sc_ref, sh_ref, wr_ref, br_ref, cnt0_ref, before_ref,
                    x1_ref, h2_ref, ri_ref, rw_ref, cnt_ref, run_scr):
    i = pl.program_id(0)
    j = pl.program_id(1)

    @pl.when((i == 0) & (j == 0))
    def _():
        run_scr[...] = cnt0_ref[...]

    t = cu_ref[0]
    tm = t.shape[0]
    prev_row = jnp.where(j == 0, 0.0, cup_ref[0][7:8, :])
    next_row = jnp.where(j == pl.num_programs(1) - 1, 0.0, cun_ref[0][0:1, :])
    row = lax.broadcasted_iota(jnp.int32, t.shape, 0)
    t_prev = jnp.where(row == 0, prev_row, pltpu.roll(t, 1, axis=0))
    t_next = jnp.where(row == tm - 1, next_row, pltpu.roll(t, tm - 1, axis=0))
    conv = cb_ref[0] * (t_prev * cw_ref[0] + t * cw_ref[1] + t_next * cw_ref[2])
    conv = conv.astype(BF16)
    wrh = wr_ref[:, :ROUTER_PAD]
    rb = tm // OUT_STREAMS
    lane = lax.broadcasted_iota(jnp.int32, (rb, ROUTER_PAD), 1)
    lane_f = lane.astype(F32)
    routed = []
    for r0 in range(0, tm, rb):
        att_t = att_ref[0, :, :, r0:r0 + rb].reshape(att_ref.shape[1] * HEAD_DIM, rb).astype(BF16)
        tn = (((0,), (0,)), ((), ()))
        y = (_dot(conv[r0:r0 + rb], w_ref[0:256, :])
             + lax.dot_general(ret_ref[0, :, r0:r0 + rb].astype(BF16), w_ref[256:512, :], tn,
                               preferred_element_type=F32)
             + lax.dot_general(att_t, w_ref[512:1024, :], tn, preferred_element_type=F32))
        x1 = _layer_norm(alpha * x_ref[0, r0:r0 + rb, :] + g1_ref[0] * y, lng_ref[...], lnb_ref[...])
        x1_ref[0, r0:r0 + rb, :] = x1
        h2 = x1 * sc_ref[0] + sh_ref[0]
        _store_rows_packed(h2_ref.at[pl.ds(r0 * PACKED_ROWS, rb * PACKED_ROWS)], h2)
        h_hi, h_lo = _split_bf16(h2)
        hw = _dot(h_hi, wr_ref[...])
        logits = hw[:, :ROUTER_PAD] + hw[:, ROUTER_PAD:] + _dot(h_lo, wrh) + br_ref[...]

        work = jnp.where(lane < N_EXPERTS, logits, -jnp.inf)
        vals, firsts, sels = [], [], []
        for _ in range(TOP_K):
            m = jnp.max(work, axis=-1, keepdims=True)
            first = jnp.min(jnp.where(work == m, lane_f, float(ROUTER_PAD)), axis=-1, keepdims=True)
            sel = lane_f == first
            vals.append(m)
            firsts.append(first)
            sels.append(sel)
            work = jnp.where(sel, -jnp.inf, work)
        exps = [jnp.exp(v - vals[0]) for v in vals]
        denom = exps[0]
        for e in exps[1:]:
            denom = denom + e
        cnt = jnp.zeros(logits.shape, F32)
        for sel in sels:
            cnt = cnt + jnp.where(sel, 1.0, 0.0)
        routed.append((firsts, sels, [e / denom for e in exps], cnt))

    cnt = jnp.concatenate([r[3] for r in routed], axis=0)
    base = _dot(before_ref[...], cnt.astype(BF16)) + run_scr[...]
    run_scr[...] = run_scr[...] + jnp.sum(cnt, axis=0, keepdims=True)
    cnt_ref[...] = run_scr[...]

    for blk, (firsts, sels, wts, _) in enumerate(routed):
        r0 = blk * rb
        ri = jnp.zeros((rb, ROUTER_PAD), F32)
        rw = jnp.zeros((rb, ROUTER_PAD), F32)
        for kk in range(TOP_K):
            rank = jnp.sum(jnp.where(sels[kk], base[r0:r0 + rb], 0.0), axis=-1, keepdims=True)
            ri = jnp.where(lane == kk, firsts[kk], ri)
            ri = jnp.where(lane == TOP_K + kk, rank, ri)
            rw = jnp.where(lane == kk, wts[kk], rw)
        ri_ref[0, r0:r0 + rb, :] = ri.astype(jnp.int32)
        rw_ref[0, r0:r0 + rb, :] = rw


def _outproj(alpha, cu, cb, cw, ret, att, x, g1, w_bf, lng, lnb, sc2, sh2, wr, br, cnt0, tm):
    b, l, d = x.shape
    nj = l // tm
    tok = lambda w: pl.BlockSpec((1, tm, w), lambda i, j: (i, j, 0))
    vec = pl.BlockSpec((1, 1, d), lambda i, j: (i, 0, 0))
    full = lambda s: pl.BlockSpec(s, lambda i, j: (0,) * len(s))
    r8 = tm // 8
    nb8 = l // 8
    tri = jnp.arange(tm)
    before = (tri[None, :] < tri[:, None]).astype(BF16)
    return pl.pallas_call(
        functools.partial(_outproj_kernel, alpha),
        grid=(b, nj),
        in_specs=[tok(256),
                  pl.BlockSpec((1, 8, 256), lambda i, j: (i, jnp.maximum(j * r8 - 1, 0), 0)),
                  pl.BlockSpec((1, 8, 256), lambda i, j: (i, jnp.minimum((j + 1) * r8, nb8 - 1), 0)),
                  tok(256), full((3, 1, 256)),
                  pl.BlockSpec((1, ret.shape[1], tm), lambda i, j: (i, 0, j)),
                  pl.BlockSpec((1,) + att.shape[1:3] + (tm,), lambda i, j: (i, 0, 0, j)), tok(d), vec,
                  full((d, d)), full((1, d)), full((1, d)), vec, vec,
                  full((d, 2 * ROUTER_PAD)), full((1, ROUTER_PAD)), full((1, ROUTER_PAD)),
                  full((tm, tm))],
        out_specs=[tok(d), pl.BlockSpec((tm * PACKED_ROWS, LANES), lambda i, j: (i * nj + j, 0)),
                   tok(ROUTER_PAD), tok(ROUTER_PAD), full((1, ROUTER_PAD))],
        out_shape=[jax.ShapeDtypeStruct((b, l, d), F32),
                   jax.ShapeDtypeStruct((b * l * PACKED_ROWS, LANES), jnp.uint32),
                   jax.ShapeDtypeStruct((b, l, ROUTER_PAD), jnp.int32),
                   jax.ShapeDtypeStruct((b, l, ROUTER_PAD), F32),
                   jax.ShapeDtypeStruct((1, ROUTER_PAD), F32)],
        scratch_shapes=[pltpu.VMEM((1, ROUTER_PAD), F32)],
        compiler_params=_cparams("arbitrary", "arbitrary"),
        name="outproj_ln_router",
    )(cu, cu, cu, cb, cw, ret, att, x, g1, w_bf, lng, lnb, sc2, sh2, wr, br, cnt0, before)


def _ffn_kernel(be_ref, nv_ref, x_ref, wgu_ref, bgu_ref, wd_ref, bd_ref, o_ref, wgu_scr, wd_scr):
    j = pl.program_id(0)
    f = wd_ref.shape[2]
    tm = x_ref.shape[0] // PACKED_ROWS
    nv = nv_ref[j]

    @pl.when(nv > 0)
    def _():
        @pl.when((j == 0) | (be_ref[j] != be_ref[jnp.maximum(j - 1, 0)]))
        def _():
            wgu_scr[...] = wgu_ref[0, 0].astype(BF16)
            wd_scr[...] = wd_ref[0, 0].astype(BF16)

    def compute(rows):
        part = pl.ds(0, rows * PACKED_ROWS)
        x = _load_rows_packed(x_ref.at[part], rows)
        bgu = bgu_ref[0, 0]
        acts = []
        for lo in range(0, f, FFN_BLOCK):
            hi = lo + FFN_BLOCK
            gate = jnp.minimum(_dot(x, wgu_scr[:, lo:hi]) + bgu[:, lo:hi], SWIGLU_LIMIT)
            up = jnp.clip(_dot(x, wgu_scr[:, f + lo:f + hi]) + bgu[:, f + lo:f + hi], -SWIGLU_LIMIT, SWIGLU_LIMIT)
            acts.append(((up + 1.0) * (gate * jax.nn.sigmoid(SWIGLU_ALPHA * gate))).astype(BF16))
        act = jnp.concatenate(acts, axis=1)
        _store_rows_packed(o_ref.at[part], _dot(act, wd_scr[...]) + bd_ref[0, 0])

    @pl.when(nv > tm // 2)
    def _():
        compute(tm)

    @pl.when((nv > 0) & (nv <= tm // 2))
    def _():
        compute(tm // 2)


def _expert_ffn(layer, block_e, n_valid, xs, wgu, bgu, wd, bd, tm):
    n_rows = xs.shape[0] // PACKED_ROWS
    depth, ne, d, f2 = wgu.shape
    f = f2 // 2
    rows = pl.BlockSpec((tm * PACKED_ROWS, LANES), lambda j, be, nu: (j, 0))
    grid_spec = pltpu.PrefetchScalarGridSpec(
        num_scalar_prefetch=2,
        grid=(n_rows // tm,),
        in_specs=[rows,
                  pl.BlockSpec((1, 1, d, f2), lambda j, be, nu: (layer, be[j], 0, 0)),
                  pl.BlockSpec((1, 1, 1, f2), lambda j, be, nu: (layer, be[j], 0, 0)),
                  pl.BlockSpec((1, 1, f, d), lambda j, be, nu: (layer, be[j], 0, 0)),
                  pl.BlockSpec((1, 1, 1, d), lambda j, be, nu: (layer, be[j], 0, 0))],
        out_specs=rows,
        scratch_shapes=[pltpu.VMEM((d, f2), BF16), pltpu.VMEM((f, d), BF16)],
    )
    return pl.pallas_call(
        _ffn_kernel,
        grid_spec=grid_spec,
        out_shape=jax.ShapeDtypeStruct((n_rows * PACKED_ROWS, LANES), jnp.uint32),
        compiler_params=_cparams("arbitrary"),
        name="expert_ffn",
    )(block_e, n_valid, xs, wgu, bgu.reshape(depth, ne, 1, f2), wd, bd.reshape(depth, ne, 1, d))


def _sc_worker_base(per_worker):
    return (lax.axis_index("s") * SC_CORES + lax.axis_index("c")) * per_worker


def _sc_dispatch(rows, dest, n_out):
    t = rows.shape[0]
    kk = dest.shape[0] // t
    w = SC_WINDOW
    per_worker = t // (SC_CORES * SC_SUBCORES)
    assert per_worker * SC_CORES * SC_SUBCORES == t and per_worker % w == 0
    mesh = plsc.VectorSubcoreMesh(core_axis_name="c", subcore_axis_name="s")

    n_win = per_worker // w
    row_buf = pltpu.VMEM((w,) + rows.shape[1:], rows.dtype)

    @functools.partial(
        pl.kernel, mesh=mesh,
        out_type=jax.ShapeDtypeStruct((n_out,) + rows.shape[1:], rows.dtype),
        scratch_types=[pltpu.VMEM((w,), jnp.int32)] * (2 * kk) + [row_buf] * 2 + [pltpu.SemaphoreType.DMA] * 4)
    def scatter_rows(r_hbm, d_hbm, o_hbm, *scratch):
        idx_v = (scratch[:kk], scratch[kk:2 * kk])
        rows_v = scratch[2 * kk:2 * kk + 2]
        rsem = scratch[2 * kk + 2:2 * kk + 4]
        ssem = scratch[2 * kk + 4:2 * kk + 6]
        base = _sc_worker_base(per_worker)

        def read(win, slot):
            return pltpu.make_async_copy(r_hbm.at[pl.ds(base + win * w, w)], rows_v[slot], rsem[slot])

        def scatter(slot, s):
            return pltpu.make_async_copy(rows_v[slot], o_hbm.at[idx_v[slot][s]], ssem[slot])

        def start_read(win, slot):
            read(win, slot).start()
            for s in range(kk):
                pltpu.sync_copy(d_hbm.at[pl.ds(s * t + base + win * w, w)], idx_v[slot][s])

        start_read(0, 0)

        @pl.loop(0, (n_win + 1) // 2)
        def _(pair):
            for slot in range(2):
                win = 2 * pair + slot
                other = 1 - slot

                @pl.when(win < n_win)
                def _():
                    @pl.when(win >= 1)
                    def _():
                        for s in range(kk):
                            scatter(other, s).wait()

                    @pl.when(win + 1 < n_win)
                    def _():
                        start_read(win + 1, other)

                    read(win, slot).wait()
                    for s in range(kk):
                        scatter(slot, s).start()

        for s in range(kk):
            scatter((n_win - 1) % 2, s).wait()

    return scatter_rows(rows, dest)


def _sc_gather(table, idx):
    n = idx.shape[0]
    w = SC_WINDOW
    per_worker = n // (SC_CORES * SC_SUBCORES)
    assert per_worker * SC_CORES * SC_SUBCORES == n and per_worker % w == 0
    mesh = plsc.VectorSubcoreMesh(core_axis_name="c", subcore_axis_name="s")

    n_win = per_worker // w
    assert n_win % 2 == 0
    row_buf = pltpu.VMEM((w,) + table.shape[1:], table.dtype)

    @functools.partial(
        pl.kernel, mesh=mesh,
        out_type=jax.ShapeDtypeStruct((n,) + table.shape[1:], table.dtype),
        scratch_types=[pltpu.VMEM((w,), jnp.int32)] * 2 + [row_buf] * 2 + [pltpu.SemaphoreType.DMA] * 4)
    def gather_rows(t_hbm, i_hbm, o_hbm, idx0, idx1, rows0, rows1, gsem0, gsem1, wsem0, wsem1):
        idx_v, rows_v = (idx0, idx1), (rows0, rows1)
        gsem, wsem = (gsem0, gsem1), (wsem0, wsem1)
        base = _sc_worker_base(per_worker)

        def gather(slot):
            return pltpu.make_async_copy(t_hbm.at[idx_v[slot]], rows_v[slot], gsem[slot])

        def write(win, slot):
            return pltpu.make_async_copy(rows_v[slot], o_hbm.at[pl.ds(base + win * w, w)], wsem[slot])

        def start_gather(win, slot):
            pltpu.sync_copy(i_hbm.at[pl.ds(base + win * w, w)], idx_v[slot])
            gather(slot).start()

        start_gather(0, 0)

        @pl.loop(0, n_win // 2)
        def _(pair):
            for slot in range(2):
                win = 2 * pair + slot
                other = 1 - slot

                @pl.when(win >= 1)
                def _():
                    write(win - 1, other).wait()

                @pl.when(win + 1 < n_win)
                def _():
                    start_gather(win + 1, other)

                gather(slot).wait()
                write(win, slot).start()

        write(n_win - 1, 1).wait()

    return gather_rows(table, idx)


def _combine_kernel(alpha, g_ref, w_ref, x_ref, g2_ref, lng_ref, lnb_ref, *rest):
    o_ref = rest[-1]
    tm = x_ref.shape[1]
    w = w_ref[0]
    f = _load_rows_packed(g_ref.at[0], tm).astype(F32) * w[:, 0:1]
    for kk in range(1, TOP_K):
        f = f + _load_rows_packed(g_ref.at[kk], tm).astype(F32) * w[:, kk:kk + 1]
    o_ref[0] = _layer_norm(alpha * x_ref[0] + g2_ref[0] * f, lng_ref[...], lnb_ref[...])


def _combine(alpha, gathered, b0, nb, wts, x1, g2, lng, lnb, tm, prev=None):
    b, l, d = x1.shape
    nj = l // tm
    tok = pl.BlockSpec((1, tm, d), lambda i, j: (b0 + i, j, 0))
    full = pl.BlockSpec((1, d), lambda i, j: (0, 0))
    in_specs = [pl.BlockSpec((TOP_K, tm * PACKED_ROWS, LANES), lambda i, j: (0, i * nj + j, 0)),
                pl.BlockSpec((1, tm, ROUTER_PAD), lambda i, j: (b0 + i, j, 0)),
                tok, pl.BlockSpec((1, 1, d), lambda i, j: (b0 + i, 0, 0)), full, full]
    args = [gathered, wts, x1, g2, lng, lnb]
    aliases = {}
    if prev is not None:
        in_specs.append(pl.BlockSpec(memory_space=pl.ANY))
        args.append(prev)
        aliases = {len(args) - 1: 0}
    return pl.pallas_call(
        functools.partial(_combine_kernel, alpha),
        grid=(nb, nj),
        in_specs=in_specs,
        out_specs=tok,
        out_shape=jax.ShapeDtypeStruct((b, l, d), F32),
        input_output_aliases=aliases,
        compiler_params=_cparams("arbitrary", "arbitrary"),
        name="combine_ln",
    )(*args)


def _rope_tables(l):
    rows = l // GRID_W
    axis_dim = HEAD_DIM // 2
    inv_freq = ROPE_BASE ** (-jnp.arange(0, axis_dim, 2, dtype=F32) / axis_dim)
    row = jnp.repeat(jnp.arange(rows, dtype=F32), GRID_W)
    col = jnp.tile(jnp.arange(GRID_W, dtype=F32), rows)
    ang = jnp.stack([row[:, None] * inv_freq, col[:, None] * inv_freq], axis=1)
    cos, sin = jnp.cos(ang), jnp.sin(ang)
    cos64 = jnp.broadcast_to(cos[:, :, None, :], (l, 2, 2, HEAD_DIM // 4)).reshape(l, HEAD_DIM)
    sin64 = jnp.stack([-sin, sin], axis=2).reshape(l, HEAD_DIM)
    return jnp.tile(cos64, (1, LANES // HEAD_DIM)), jnp.tile(sin64, (1, LANES // HEAD_DIM))


def _route_tables(ri, counts, tm):
    t = ri.shape[0]
    top_e = ri[:, :TOP_K]
    rank = ri[:, TOP_K:2 * TOP_K]
    padded = (counts + tm - 1) // tm * tm
    pad_end = jnp.cumsum(padded)
    pad_start = pad_end - padded
    experts = jnp.arange(N_EXPERTS, dtype=jnp.int32)
    start = jnp.sum(jnp.where(top_e[:, :, None] == experts, pad_start, 0), axis=-1)
    dest = (start + rank).T.reshape(TOP_K * t)
    n_tiles = (t * TOP_K + N_EXPERTS * (tm - 1) + tm - 1) // tm
    tile_start = jnp.arange(n_tiles, dtype=jnp.int32) * tm
    block_e = jnp.minimum(jnp.sum((pad_end[None, :] <= tile_start[:, None]).astype(jnp.int32), axis=1), N_EXPERTS - 1)
    in_expert = jnp.where(block_e[:, None] == experts[None, :], (pad_start + counts)[None, :], 0).sum(axis=1)
    n_valid = jnp.clip(in_expert - tile_start, 0, tm)
    return dest, block_e.astype(jnp.int32), n_valid.astype(jnp.int32), n_tiles


def _layer_params(layer, d, w_in, conv_w, ret_decay_exp, ret_gn_g, q_norm_g, k_norm_g, w_out, ln_g, ln_b,
                  w_router, b_router):
    log_gamma = jnp.log1p(-jnp.exp2(-ret_decay_exp[layer].astype(F32)))
    lg_rows = jnp.repeat(log_gamma, HEAD_DIM, axis=1)
    wr_pad = jnp.zeros((d, ROUTER_PAD), F32).at[:, :N_EXPERTS].set(w_router[layer])
    wrh = wr_pad.astype(BF16)
    return dict(
        w_in=w_in[layer].astype(BF16), w_out=w_out[layer].astype(BF16),
        gq=jnp.tile(q_norm_g[layer], LANES // HEAD_DIM).reshape(1, LANES),
        gk=jnp.tile(k_norm_g[layer], LANES // HEAD_DIM).reshape(1, LANES),
        log_gamma=log_gamma, lgf_row=lg_rows[0:1], lgb_row=lg_rows[1:2], gn_row=ret_gn_g[layer].reshape(1, -1),
        cw=conv_w[layer].T.reshape(3, 1, -1),
        wr=jnp.concatenate([wrh, (wr_pad - wrh.astype(F32)).astype(BF16)], axis=1),
        br=jnp.zeros((1, ROUTER_PAD), F32).at[0, :N_EXPERTS].set(b_router[layer]),
        lng1=ln_g[layer, 0].reshape(1, d), lnb1=ln_b[layer, 0].reshape(1, d),
        lng2=ln_g[layer, 1].reshape(1, d), lnb2=ln_b[layer, 1].reshape(1, d))


def _layer(layer, last, alpha, x, ctx, m_lat, m_ctx, p, tables, experts):
    b, l, d = x.shape
    lc = ctx.shape[1]
    tm_lat, tm_ctx, tm_moe, tq = 512, 256, 1024, 256
    cos_l, sin_l, cos_c, sin_c, ones_bd, zero_state = tables
    sh1, sc1, g1, sh2, sc2, g2 = (m_lat[:, :, i] for i in range(N_MOD))
    sh1c, sc1c, g1c, sh2c, sc2c, g2c = (m_ctx[:, :, i] for i in range(N_MOD))

    zc = _inproj(ctx, 1.0 + sc1c, sh1c, p["w_in"], cos_c, sin_c, p["gq"], p["gk"], ones_bd, tm_ctx)
    zl = _inproj(x, 1.0 + sc1, sh1, p["w_in"], cos_l, sin_l, p["gq"], p["gk"], ones_bd, tm_lat)
    cu_c, cb_c, rq_c, rk_c, rv_c, rg_c, aq_c, ak_c, av_c = zc
    cu_l, cb_l, rq_l, rk_l, rv_l, rg_l, aq_l, ak_l, av_l = zl

    ret_args = (p["log_gamma"], p["lgf_row"], p["lgb_row"], p["gn_row"])
    ret_c, s_fwd, s_bwd = _retention(rq_c, rk_c, rv_c, rg_c, zero_state, zero_state, *ret_args)
    ret_l, _, _ = _retention(rq_l, rk_l, rv_l, rg_l, s_fwd, s_bwd, *ret_args)

    def kv_heads(a):
        return a.reshape(b, a.shape[1], -1, HEAD_DIM).transpose(0, 2, 1, 3)

    def v_heads(a):
        vt = a.reshape(b, a.shape[1], -1, HEAD_DIM).transpose(0, 2, 3, 1)
        ones = jnp.ones(vt.shape[:2] + (1, vt.shape[3]), BF16)
        pad = jnp.zeros(vt.shape[:2] + (ATT_V_ROWS - HEAD_DIM - 1, vt.shape[3]), BF16)
        return jnp.concatenate([vt, ones, pad], axis=2)

    k_all = kv_heads(jnp.concatenate([ak_c, ak_l], axis=1))
    v_all = v_heads(jnp.concatenate([av_c, av_l], axis=1))
    att_l = _attention(aq_l, k_all, v_all, tq)

    out_args = (p["w_out"], p["lng1"], p["lnb1"])
    rt_args = (p["wr"], p["br"])
    cnt0 = jnp.zeros((1, ROUTER_PAD), F32)
    if not last:
        att_c = _attention(aq_c, kv_heads(ak_c), v_heads(av_c), min(tq, lc))
        ctx1, h2_c, ri_c, rw_c, cnt0 = _outproj(alpha, cu_c, cb_c, p["cw"], ret_c, att_c, ctx, g1c, *out_args,
                                                1.0 + sc2c, sh2c, *rt_args, cnt0, tm_ctx)
    x1, h2_l, ri_l, rw_l, cnt = _outproj(alpha, cu_l, cb_l, p["cw"], ret_l, att_l, x, g1, *out_args,
                                         1.0 + sc2, sh2, *rt_args, cnt0, tm_lat)
    if not last:
        n_c = b * lc
        h2 = jnp.concatenate([h2_c, h2_l], axis=0)
        ri = jnp.concatenate([ri_c.reshape(n_c, -1), ri_l.reshape(b * l, -1)], axis=0)
    else:
        n_c = 0
        h2 = h2_l
        ri = ri_l.reshape(b * l, -1)
    n_tok = n_c + b * l

    counts = cnt[0, :N_EXPERTS].astype(jnp.int32)
    dest, block_e, n_valid, n_tiles = _route_tables(ri, counts, tm_moe)
    xs = _sc_dispatch(h2.reshape(n_tok, PACKED_ROWS, LANES), dest, n_tiles * tm_moe)
    ys = _expert_ffn(layer, block_e, n_valid, xs.reshape(-1, LANES), *experts, tm_moe)
    ys = ys.reshape(-1, PACKED_ROWS, LANES)
    dest = dest.reshape(TOP_K, n_tok)

    def gather(lo, hi):
        rows = _sc_gather(ys, dest[:, lo:hi].reshape(-1))
        return rows.reshape(TOP_K, (hi - lo) * PACKED_ROWS, LANES)

    if not last:
        ctx = _combine(alpha, gather(0, n_c), 0, b, rw_c, ctx1, g2c, p["lng2"], p["lnb2"], tm_ctx)
    nb = b // COMBINE_CHUNKS if b % COMBINE_CHUNKS == 0 else b
    x_new = None
    for b0 in range(0, b, nb):
        g = gather(n_c + b0 * l, n_c + (b0 + nb) * l)
        x_new = _combine(alpha, g, b0, nb, rw_l, x1, g2, p["lng2"], p["lnb2"], tm_lat, prev=x_new)
    return x_new, ctx


def kernel(x, c, ctx, c_ctx, w_mod, b_mod, w_in, conv_w, ret_decay_exp, ret_gn_g, q_norm_g, k_norm_g, w_out,
           ln_g, ln_b, w_router, b_router, w_gate_up, b_gate_up, w_down, b_down):
    depth = w_mod.shape[0]
    alpha = (2.0 * depth) ** 0.25
    b, l, d = x.shape
    lc = ctx.shape[1]
    groups = BATCH_GROUPS if b % BATCH_GROUPS == 0 else 1
    bg = b // groups

    n_rows = (b + 1 + 7) // 8 * 8
    c_all = jnp.zeros((n_rows, d), F32).at[:b].set(c).at[b].set(c_ctx)
    mod = _modulation(c_all, w_mod, b_mod)

    cos_l, sin_l = _rope_tables(l)
    cos_c, sin_c = jnp.ones((lc, LANES), F32), jnp.zeros((lc, LANES), F32)
    lane_head = jnp.arange(LANES) // HEAD_DIM
    ones_bd = (lane_head[:, None] == lane_head[None, :]).astype(BF16)
    zero_state = jnp.zeros((bg, RET_HEADS, HEAD_DIM, HEAD_DIM), F32)
    tables = (cos_l, sin_l, cos_c, sin_c, ones_bd, zero_state)
    experts = (w_gate_up, b_gate_up, w_down, b_down)

    xs = [x[g * bg:(g + 1) * bg] for g in range(groups)]
    cs = [ctx[g * bg:(g + 1) * bg] for g in range(groups)]
    for layer in range(depth):
        last = layer == depth - 1
        p = _layer_params(layer, d, w_in, conv_w, ret_decay_exp, ret_gn_g, q_norm_g, k_norm_g, w_out, ln_g, ln_b,
                          w_router, b_router)
        m_ctx = jnp.broadcast_to(mod[layer, b].reshape(1, 1, N_MOD, d), (bg, 1, N_MOD, d))
        for g in range(groups):
            m_lat = mod[layer, g * bg:(g + 1) * bg].reshape(bg, 1, N_MOD, d)
            xs[g], cs[g] = _layer(layer, last, alpha, xs[g], cs[g], m_lat, m_ctx, p, tables, experts)
    return jnp.concatenate(xs, axis=0) if groups > 1 else xs[0]
```

```python
import functools

import jax
import jax.numpy as jnp
from jax import lax
from jax.experimental import pallas as pl
from jax.experimental.pallas import tpu as pltpu
from jax.experimental.pallas import tpu_sc as plsc

F32 = jnp.float32
BF16 = jnp.bfloat16

HEAD_DIM = 64
GRID_W = 64
ROPE_BASE = 10000.0
N_EXPERTS = 32
TOP_K = 4
SWIGLU_ALPHA = 1.702
SWIGLU_LIMIT = 7.0
N_MOD = 6
EPS = 1e-6
RET_HEADS = 4
RET_CHUNK = 256
RET_BATCH = 8
ATT_Q_PER_KV = 4
ATT_KV_CHUNK = 256
ATT_Q_SCALE = HEAD_DIM ** -0.5 * 1.4426950408889634
ATT_STREAMS = 2
ATT_V_ROWS = 80
LANES = 128
ROUTER_PAD = 128
VMEM_LIMIT = 56 * 1024 * 1024
SUBLANES = 8
PACKED_ROWS = 4
FFN_BLOCK = 256
OUT_STREAMS = 2
COMBINE_CHUNKS = 4
SC_CORES = 2
SC_SUBCORES = 16
SC_WINDOW = 64
BATCH_GROUPS = 1


def _dot(a, b):
    return jnp.dot(a, b, preferred_element_type=F32)


def _dot_nt(a, b):
    return lax.dot_general(a, b, (((1,), (1,)), ((), ())), preferred_element_type=F32)


def _split_bf16(x):
    hi = x.astype(BF16)
    lo = (x - hi.astype(F32)).astype(BF16)
    return hi, lo


def _cparams(*sem):
    return pltpu.CompilerParams(dimension_semantics=sem, vmem_limit_bytes=VMEM_LIMIT)


def _mod_kernel(c_ref, w_ref, b_ref, o_ref):
    c = c_ref[...]
    a = c * jax.nn.sigmoid(c)
    a_hi, a_lo = _split_bf16(a)
    w_hi, w_lo = _split_bf16(w_ref[0])
    o_ref[0] = _dot(a_hi, w_hi) + _dot(a_lo, w_hi) + _dot(a_hi, w_lo) + b_ref[0]


def _modulation(c_all, w_mod, b_mod):
    depth, d, n = w_mod.shape
    r = c_all.shape[0]
    tn = 1536
    return pl.pallas_call(
        _mod_kernel,
        grid=(depth, n // tn),
        in_specs=[
            pl.BlockSpec((r, d), lambda l, j: (0, 0)),
            pl.BlockSpec((1, d, tn), lambda l, j: (l, 0, j)),
            pl.BlockSpec((1, 1, tn), lambda l, j: (l, 0, j)),
        ],
        out_specs=pl.BlockSpec((1, r, tn), lambda l, j: (l, 0, j)),
        out_shape=jax.ShapeDtypeStruct((depth, r, n), F32),
        compiler_params=_cparams("arbitrary", "arbitrary"),
        name="modulation",
    )(c_all, w_mod, b_mod.reshape(depth, 1, n))


def _inproj_kernel(x_ref, sc_ref, sh_ref, w_ref, cos_ref, sin_ref, gq_ref, gk_ref, ones_ref,
                   cu_ref, cb_ref, rq_ref, rk_ref, rv_ref, rg_ref, aq_ref, ak_ref, av_ref):
    h = (x_ref[0] * sc_ref[0] + sh_ref[0]).astype(BF16)
    cos = cos_ref[...]
    sin = sin_ref[...]
    ones = ones_ref[...]
    lane = lax.broadcasted_iota(jnp.int32, cos.shape, 1)
    first_half = (lane & 31) < 16

    def rope(xs):
        nxt = pltpu.roll(xs, LANES - 16, axis=1)
        prv = pltpu.roll(xs, 16, axis=1)
        return xs * cos + jnp.where(first_half, nxt, prv) * sin

    def rms(xs, g):
        s_hi, s_lo = _split_bf16(xs * xs)
        ssq = _dot(s_hi, ones) + _dot(s_lo, ones)
        return xs * lax.rsqrt(ssq * (1.0 / HEAD_DIM) + EPS) * g

    za = _dot(h, w_ref[:, 1792:2560])
    gq = gq_ref[...]
    for j in range(4):
        lo, hi = j * LANES, (j + 1) * LANES
        q = rope(rms(za[:, lo:hi], gq)) * ATT_Q_SCALE
        aq_ref[0, lo:hi, :] = q.T.astype(BF16)
    ak_ref[0] = rope(rms(za[:, 512:640], gk_ref[...])).astype(BF16)
    av_ref[0] = za[:, 640:768].astype(BF16)
    zr = _dot(h, w_ref[:, 768:1792])
    for j in range(2):
        lo, hi = j * LANES, (j + 1) * LANES
        rq_ref[0, lo:hi, :] = rope(zr[:, lo:hi]).T
        rk_ref[0, :, lo:hi] = rope(zr[:, 256 + lo:256 + hi] * (HEAD_DIM ** -0.5))
        rv_ref[0, lo:hi, :] = zr[:, 512 + lo:512 + hi].T
        rg_ref[0, lo:hi, :] = zr[:, 768 + lo:768 + hi].T
    zc = _dot(h, w_ref[:, 0:768])
    cu_ref[0] = zc[:, 512:768] * zc[:, 0:256]
    cb_ref[0] = zc[:, 256:512]


def _inproj(x, sc, sh, w_bf, cos, sin, gq, gk, ones_bd, tm):
    b, l, d = x.shape
    n = w_bf.shape[1]
    tok = lambda w: pl.BlockSpec((1, tm, w), lambda i, j: (i, j, 0))
    vec = pl.BlockSpec((1, 1, d), lambda i, j: (i, 0, 0))
    full = lambda s: pl.BlockSpec(s, lambda i, j: (0,) * len(s))
    widths = (256, 256, 256, 256, 256, 256, 512, 128, 128)
    dtypes = (F32,) * 6 + (BF16,) * 3
    out_specs = [tok(w) for w in widths]
    out_shape = [jax.ShapeDtypeStruct((b, l, w), dt) for w, dt in zip(widths, dtypes)]
    for o in (2, 4, 5, 6):
        out_specs[o] = pl.BlockSpec((1, widths[o], tm), lambda i, j: (i, 0, j))
        out_shape[o] = jax.ShapeDtypeStruct((b, widths[o], l), dtypes[o])
    return pl.pallas_call(
        _inproj_kernel,
        grid=(b, l // tm),
        in_specs=[tok(d), vec, vec, full((d, n)),
                  pl.BlockSpec((tm, LANES), lambda i, j: (j, 0)),
                  pl.BlockSpec((tm, LANES), lambda i, j: (j, 0)),
                  full((1, LANES)), full((1, LANES)), full((LANES, LANES))],
        out_specs=out_specs,
        out_shape=out_shape,
        compiler_params=_cparams("arbitrary", "arbitrary"),
        name="inproj",
    )(x, sc, sh, w_bf, cos, sin, gq, gk, ones_bd)


def _attn_kernel(q_ref, k_ref, vt_ref, o_ref):
    tq = q_ref.shape[2]
    q4 = q_ref[0]
    qt = jnp.concatenate([q4[h * HEAD_DIM:(h + 1) * HEAD_DIM, :] for h in range(ATT_Q_PER_KV)], axis=1)
    lk = k_ref.shape[2]
    bounds = [(lo, min(lo + ATT_KV_CHUNK, lk)) for lo in range(0, lk, ATT_KV_CHUNK)]
    r = qt.shape[1] // ATT_STREAMS
    qs = [qt[:, i * r:(i + 1) * r] for i in range(ATT_STREAMS)]
    ms = [jnp.full((1, r), -jnp.inf, F32) for _ in qs]
    accs = [jnp.zeros((ATT_V_ROWS, r), F32) for _ in qs]
    nxt = [_dot(k_ref[0, 0, bounds[0][0]:bounds[0][1], :], q) for q in qs]
    for c, (lo, hi) in enumerate(bounds):
        for i, q in enumerate(qs):
            s = nxt[i]
            if c + 1 < len(bounds):
                nxt[i] = _dot(k_ref[0, 0, bounds[c + 1][0]:bounds[c + 1][1], :], q)
            m_new = jnp.maximum(ms[i], jnp.max(s, axis=0, keepdims=True))
            p = jnp.exp2(s - m_new)
            accs[i] = accs[i] * jnp.exp2(ms[i] - m_new) + _dot(vt_ref[0, 0, :, lo:hi], p.astype(BF16))
            ms[i] = m_new
    acc = jnp.concatenate(accs, axis=1)
    o = acc[:HEAD_DIM] / acc[HEAD_DIM:HEAD_DIM + 1]
    for h in range(ATT_Q_PER_KV):
        o_ref[0, h] = o[:, h * tq:(h + 1) * tq]


def _attention(q, k, vt, tq):
    b, wq, lq = q.shape
    hkv, lk = k.shape[1], k.shape[2]
    wg = ATT_Q_PER_KV * HEAD_DIM
    return pl.pallas_call(
        _attn_kernel,
        grid=(b, hkv, lq // tq),
        in_specs=[pl.BlockSpec((1, wg, tq), lambda i, g, j: (i, g, j)),
                  pl.BlockSpec((1, 1, lk, HEAD_DIM), lambda i, g, j: (i, g, 0, 0)),
                  pl.BlockSpec((1, 1, ATT_V_ROWS, lk), lambda i, g, j: (i, g, 0, 0))],
        out_specs=pl.BlockSpec((1, ATT_Q_PER_KV, HEAD_DIM, tq), lambda i, g, j: (i, g, 0, j)),
        out_shape=jax.ShapeDtypeStruct((b, wq // HEAD_DIM, HEAD_DIM, lq), F32),
        compiler_params=_cparams("arbitrary", "arbitrary", "arbitrary"),
        name="attention",
    )(q, k, vt)


def _ret_kernel(lg_ref, q_ref, k_ref, v_ref, g_ref, s0f_ref, s0b_ref, lgf_ref, lgb_ref, lgfc_ref, lgbc_ref, gn_ref,
                o_ref, sff_ref, sfb_ref, s_scr, sb_scr, dec_scr):
    i, p, j = pl.program_id(0), pl.program_id(1), pl.program_id(2)
    nc = pl.num_programs(2)
    c = k_ref.shape[1]
    w = k_ref.shape[2]
    pos = lax.broadcasted_iota(jnp.int32, (c, w), 0).astype(F32)
    qpos = lax.broadcasted_iota(jnp.int32, (w, c), 1).astype(F32)
    lgf = lgf_ref[...]
    lgb = lgb_ref[...]
    lgf_col = lgfc_ref[...]
    lgb_col = lgbc_ref[...]

    @pl.when((i == 0) & (p == 0) & (j == 0))
    def _():
        diff = (lax.broadcasted_iota(jnp.int32, (c, c), 1) - lax.broadcasted_iota(jnp.int32, (c, c), 0)).astype(F32)
        for h in range(RET_HEADS):
            dec_scr[h] = jnp.where(diff >= 0.0, jnp.exp(lg_ref[0, h] * jnp.maximum(diff, 0.0)),
                                   jnp.exp(lg_ref[1, h] * jnp.maximum(-diff, 0.0)))

    nb = k_ref.shape[0]

    def update_state(e, vt, kz, cdec):
        for h in range(RET_HEADS):
            lo, hi = h * HEAD_DIM, (h + 1) * HEAD_DIM
            s_scr[e, h] = cdec[:, lo:hi] * s_scr[e, h] + _dot(vt[lo:hi, :], kz[:, lo:hi])

    @pl.when(p == 0)
    def _():
        @pl.when(j == 0)
        def _():
            s_scr[...] = s0b_ref[...]

        sb_scr[nc - 1 - j] = s_scr[...]
        for e in range(nb):
            kz = (k_ref[e] * jnp.exp(lgb * pos)).astype(BF16)
            update_state(e, v_ref[e].astype(BF16), kz, jnp.exp(lgb * float(c)))

        @pl.when(j == nc - 1)
        def _():
            sfb_ref[...] = s_scr[...]

    @pl.when(p == 1)
    def _():
        @pl.when(j == 0)
        def _():
            s_scr[...] = s0f_ref[...]

        for e in range(nb):
            qt = q_ref[e]
            k = k_ref[e]
            qf = (qt * jnp.exp(lgf_col * (qpos + 1.0))).astype(BF16)
            qb = (qt * jnp.exp(lgb_col * (float(c) - qpos))).astype(BF16)
            kz = (k * jnp.exp(lgf * (float(c) - 1.0 - pos))).astype(BF16)
            qh = qt.astype(BF16)
            kh = k.astype(BF16)
            vt = v_ref[e].astype(BF16)
            gate = g_ref[e]
            for h in range(RET_HEADS):
                lo, hi = h * HEAD_DIM, (h + 1) * HEAD_DIM
                sc = _dot(kh[:, lo:hi], qh[lo:hi, :])
                y = (_dot(vt[lo:hi, :], (sc * dec_scr[h]).astype(BF16))
                     + _dot(s_scr[e, h].astype(BF16), qf[lo:hi, :])
                     + _dot(sb_scr[j, e, h].astype(BF16), qb[lo:hi, :]))
                mu = jnp.mean(y, axis=0, keepdims=True)
                yc = y - mu
                var = jnp.mean(yc * yc, axis=0, keepdims=True)
                gh = gate[lo:hi, :]
                o_ref[e, lo:hi, :] = yc * lax.rsqrt(var + EPS) * gn_ref[lo:hi, :] * (gh * jax.nn.sigmoid(gh))
            update_state(e, vt, kz, jnp.exp(lgf * float(c)))

        @pl.when(j == nc - 1)
        def _():
            sff_ref[...] = s_scr[...]


def _retention(qt, k, vt, gate_t, s0_f, s0_b, lg, lgf_row, lgb_row, gn_row):
    b, l, w = k.shape
    c = min(RET_CHUNK, l)
    nc = l // c
    nb = RET_BATCH if b % RET_BATCH == 0 else 1
    st = (nb, RET_HEADS, HEAD_DIM, HEAD_DIM)
    st_spec = pl.BlockSpec(st, lambda i, p, j: (i, 0, 0, 0))
    row = pl.BlockSpec((1, w), lambda i, p, j: (0, 0))
    col = pl.BlockSpec((w, 1), lambda i, p, j: (0, 0))
    k_both = pl.BlockSpec((nb, c, w), lambda i, p, j: (i, jnp.where(p == 0, nc - 1 - j, j), 0))
    t_both = pl.BlockSpec((nb, w, c), lambda i, p, j: (i, 0, jnp.where(p == 0, nc - 1 - j, j)))
    t_fwd = pl.BlockSpec((nb, w, c), lambda i, p, j: (i, 0, p * j))
    state_shape = jax.ShapeDtypeStruct((b,) + st[1:], F32)
    return pl.pallas_call(
        _ret_kernel,
        grid=(b // nb, 2, nc),
        in_specs=[pl.BlockSpec(memory_space=pltpu.SMEM), t_fwd, k_both, t_both, t_fwd,
                  st_spec, st_spec, row, row, col, col, col],
        out_specs=[t_fwd, st_spec, st_spec],
        out_shape=[jax.ShapeDtypeStruct((b, w, l), F32), state_shape, state_shape],
        scratch_shapes=[pltpu.VMEM(st, F32), pltpu.VMEM((nc,) + st, F32), pltpu.VMEM((RET_HEADS, c, c), F32)],
        compiler_params=_cparams("arbitrary", "arbitrary", "arbitrary"),
        name="retention",
    )(lg, qt, k, vt, gate_t, s0_f, s0_b, lgf_row, lgb_row, lgf_row.reshape(w, 1), lgb_row.reshape(w, 1),
      gn_row.reshape(w, 1))


def _layer_norm(x, g, b):
    mu = jnp.mean(x, axis=-1, keepdims=True)
    xc = x - mu
    var = jnp.mean(xc * xc, axis=-1, keepdims=True)
    return xc * lax.rsqrt(var + EPS) * g + b


def _store_rows_packed(ref, x):
    tm, d = x.shape
    bits = pltpu.bitcast(x.astype(BF16).astype(F32), jnp.uint32)
    word = (bits[:, :d // 2] >> 16) | bits[:, d // 2:]
    for s in range(PACKED_ROWS):
        ref[pl.ds(s, tm, stride=PACKED_ROWS), :] = word[:, s * LANES:(s + 1) * LANES]


def _load_rows_packed(ref, tm):
    word = jnp.concatenate([ref[pl.ds(s, tm, stride=PACKED_ROWS), :] for s in range(PACKED_ROWS)], axis=1)
    lo = pltpu.bitcast(word << 16, F32).astype(BF16)
    hi = pltpu.bitcast(word & jnp.uint32(0xFFFF0000), F32).astype(BF16)
    return jnp.concatenate([lo, hi], axis=1)


def _outproj_kernel(alpha, cu_ref, cup_ref, cun_ref, cb_ref, cw_ref, ret_ref, att_ref, x_ref, g1_ref,
                    w_ref, lng_ref, lnb_ref, sc_ref, sh_ref, wr_ref, br_ref, cnt0_ref, before_ref, *rest):
    x1_ref, h2_ref, ri_ref, rw_ref, cnt_ref, run_scr = rest[-6:]
    i = pl.program_id(0)
    j = pl.program_id(1)

    @pl.when((i == 0) & (j == 0))
    def _():
        run_scr[...] = cnt0_ref[...]

    t = cu_ref[0]
    tm = t.shape[0]
    prev_row = jnp.where(j == 0, 0.0, cup_ref[0][7:8, :])
    next_row = jnp.where(j == pl.num_programs(1) - 1, 0.0, cun_ref[0][0:1, :])
    row = lax.broadcasted_iota(jnp.int32, t.shape, 0)
    t_prev = jnp.where(row == 0, prev_row, pltpu.roll(t, 1, axis=0))
    t_next = jnp.where(row == tm - 1, next_row, pltpu.roll(t, tm - 1, axis=0))
    conv = cb_ref[0] * (t_prev * cw_ref[0] + t * cw_ref[1] + t_next * cw_ref[2])
    conv = conv.astype(BF16)
    wrh = wr_ref[:, :ROUTER_PAD]
    rb = tm // OUT_STREAMS
    lane = lax.broadcasted_iota(jnp.int32, (rb, ROUTER_PAD), 1)
    lane_f = lane.astype(F32)
    routed = []
    for r0 in range(0, tm, rb):
        att_t = att_ref[0, :, :, r0:r0 + rb].reshape(att_ref.shape[1] * HEAD_DIM, rb).astype(BF16)
        tn = (((0,), (0,)), ((), ()))
        y = (_dot(conv[r0:r0 + rb], w_ref[0:256, :])
             + lax.dot_general(ret_ref[0, :, r0:r0 + rb].astype(BF16), w_ref[256:512, :], tn,
                               preferred_element_type=F32)
             + lax.dot_general(att_t, w_ref[512:1024, :], tn, preferred_element_type=F32))
        x1 = _layer_norm(alpha * x_ref[0, r0:r0 + rb, :] + g1_ref[0] * y, lng_ref[...], lnb_ref[...])
        x1_ref[0, r0:r0 + rb, :] = x1
        h2 = x1 * sc_ref[0] + sh_ref[0]
        _store_rows_packed(h2_ref.at[pl.ds(r0 * PACKED_ROWS, rb * PACKED_ROWS)], h2)
        h_hi, h_lo = _split_bf16(h2)
        hw = _dot(h_hi, wr_ref[...])
        logits = hw[:, :ROUTER_PAD] + hw[:, ROUTER_PAD:] + _dot(h_lo, wrh) + br_ref[...]

        work = jnp.where(lane < N_EXPERTS, logits, -jnp.inf)
        vals, firsts, sels = [], [], []
        for _ in range(TOP_K):
            m = jnp.max(work, axis=-1, keepdims=True)
            first = jnp.min(jnp.where(work == m, lane_f, float(ROUTER_PAD)), axis=-1, keepdims=True)
            sel = lane_f == first
            vals.append(m)
            firsts.append(first)
            sels.append(sel)
            work = jnp.where(sel, -jnp.inf, work)
        exps = [jnp.exp(v - vals[0]) for v in vals]
        denom = exps[0]
        for e in exps[1:]:
            denom = denom + e
        cnt = jnp.zeros(logits.shape, F32)
        for sel in sels:
            cnt = cnt + jnp.where(sel, 1.0, 0.0)
        routed.append((firsts, sels, [e / denom for e in exps], cnt))

    cnt = jnp.concatenate([r[3] for r in routed], axis=0)
    base = _dot(before_ref[...], cnt.astype(BF16)) + run_scr[...]
    run_scr[...] = run_scr[...] + jnp.sum(cnt, axis=0, keepdims=True)
    cnt_ref[...] = run_scr[...]

    for blk, (firsts, sels, wts, _) in enumerate(routed):
        r0 = blk * rb
        ri = jnp.zeros((rb, ROUTER_PAD), F32)
        rw = jnp.zeros((rb, ROUTER_PAD), F32)
        for kk in range(TOP_K):
            rank = jnp.sum(jnp.where(sels[kk], base[r0:r0 + rb], 0.0), axis=-1, keepdims=True)
            ri = jnp.where(lane == kk, firsts[kk], ri)
            ri = jnp.where(lane == TOP_K + kk, rank, ri)
            rw = jnp.where(lane == kk, wts[kk], rw)
        ri_ref[0, r0:r0 + rb, :] = ri.astype(jnp.int32)
        rw_ref[0, r0:r0 + rb, :] = rw


def _outproj(alpha, cu, cb, cw, ret, att, x, g1, w_bf, lng, lnb, sc2, sh2, wr, br, cnt0, tm,
             h2_tokens=None, h2_offset=0, h2_prev=None):
    b, l, d = x.shape
    nj = l // tm
    h2_tokens = b * l if h2_tokens is None else h2_tokens
    assert h2_offset % tm == 0
    h2_blk = h2_offset // tm
    extra_specs, extra_args, aliases = [], [], {}
    if h2_prev is not None:
        extra_specs, extra_args, aliases = [pl.BlockSpec(memory_space=pl.ANY)], [h2_prev], {18: 1}
    tok = lambda w: pl.BlockSpec((1, tm, w), lambda i, j: (i, j, 0))
    vec = pl.BlockSpec((1, 1, d), lambda i, j: (i, 0, 0))
    full = lambda s: pl.BlockSpec(s, lambda i, j: (0,) * len(s))
    r8 = tm // 8
    nb8 = l // 8
    tri = jnp.arange(tm)
    before = (tri[None, :] < tri[:, None]).astype(BF16)
    return pl.pallas_call(
        functools.partial(_outproj_kernel, alpha),
        grid=(b, nj),
        in_specs=[tok(256),
                  pl.BlockSpec((1, 8, 256), lambda i, j: (i, jnp.maximum(j * r8 - 1, 0), 0)),
                  pl.BlockSpec((1, 8, 256), lambda i, j: (i, jnp.minimum((j + 1) * r8, nb8 - 1), 0)),
                  tok(256), full((3, 1, 256)),
                  pl.BlockSpec((1, ret.shape[1], tm), lambda i, j: (i, 0, j)),
                  pl.BlockSpec((1,) + att.shape[1:3] + (tm,), lambda i, j: (i, 0, 0, j)), tok(d), vec,
                  full((d, d)), full((1, d)), full((1, d)), vec, vec,
                  full((d, 2 * ROUTER_PAD)), full((1, ROUTER_PAD)), full((1, ROUTER_PAD)),
                  full((tm, tm))] + extra_specs,
        out_specs=[tok(d), pl.BlockSpec((tm * PACKED_ROWS, LANES), lambda i, j: (h2_blk + i * nj + j, 0)),
                   tok(ROUTER_PAD), tok(ROUTER_PAD), full((1, ROUTER_PAD))],
        out_shape=[jax.ShapeDtypeStruct((b, l, d), F32),
                   jax.ShapeDtypeStruct((h2_tokens * PACKED_ROWS, LANES), jnp.uint32),
                   jax.ShapeDtypeStruct((b, l, ROUTER_PAD), jnp.int32),
                   jax.ShapeDtypeStruct((b, l, ROUTER_PAD), F32),
                   jax.ShapeDtypeStruct((1, ROUTER_PAD), F32)],
        scratch_shapes=[pltpu.VMEM((1, ROUTER_PAD), F32)],
        input_output_aliases=aliases,
        compiler_params=_cparams("arbitrary", "arbitrary"),
        name="outproj_ln_router",
    )(cu, cu, cu, cb, cw, ret, att, x, g1, w_bf, lng, lnb, sc2, sh2, wr, br, cnt0, before, *extra_args)


def _ffn_kernel(be_ref, nv_ref, x_ref, wgu_ref, bgu_ref, wd_ref, bd_ref, o_ref, wgu_scr, wd_scr):
    j = pl.program_id(0)
    f = wd_ref.shape[2]
    tm = x_ref.shape[0] // PACKED_ROWS
    nv = nv_ref[j]

    @pl.when(nv > 0)
    def _():
        @pl.when((j == 0) | (be_ref[j] != be_ref[jnp.maximum(j - 1, 0)]))
        def _():
            wgu_scr[...] = wgu_ref[0, 0].astype(BF16)
            wd_scr[...] = wd_ref[0, 0].astype(BF16)

    def compute(rows):
        part = pl.ds(0, rows * PACKED_ROWS)
        x = _load_rows_packed(x_ref.at[part], rows)
        bgu = bgu_ref[0, 0]
        acts = []
        for lo in range(0, f, FFN_BLOCK):
            hi = lo + FFN_BLOCK
            gate = jnp.minimum(_dot(x, wgu_scr[:, lo:hi]) + bgu[:, lo:hi], SWIGLU_LIMIT)
            up = jnp.clip(_dot(x, wgu_scr[:, f + lo:f + hi]) + bgu[:, f + lo:f + hi], -SWIGLU_LIMIT, SWIGLU_LIMIT)
            acts.append(((up + 1.0) * (gate * jax.nn.sigmoid(SWIGLU_ALPHA * gate))).astype(BF16))
        act = jnp.concatenate(acts, axis=1)
        _store_rows_packed(o_ref.at[part], _dot(act, wd_scr[...]) + bd_ref[0, 0])

    @pl.when(nv > tm // 2)
    def _():
        compute(tm)

    @pl.when((nv > 0) & (nv <= tm // 2))
    def _():
        compute(tm // 2)


def _expert_ffn(layer, block_e, n_valid, xs, wgu, bgu, wd, bd, tm):
    n_rows = xs.shape[0] // PACKED_ROWS
    depth, ne, d, f2 = wgu.shape
    f = f2 // 2
    rows = pl.BlockSpec((tm * PACKED_ROWS, LANES), lambda j, be, nu: (j, 0))
    grid_spec = pltpu.PrefetchScalarGridSpec(
        num_scalar_prefetch=2,
        grid=(n_rows // tm,),
        in_specs=[rows,
                  pl.BlockSpec((1, 1, d, f2), lambda j, be, nu: (layer, be[j], 0, 0)),
                  pl.BlockSpec((1, 1, 1, f2), lambda j, be, nu: (layer, be[j], 0, 0)),
                  pl.BlockSpec((1, 1, f, d), lambda j, be, nu: (layer, be[j], 0, 0)),
                  pl.BlockSpec((1, 1, 1, d), lambda j, be, nu: (layer, be[j], 0, 0))],
        out_specs=rows,
        scratch_shapes=[pltpu.VMEM((d, f2), BF16), pltpu.VMEM((f, d), BF16)],
    )
    return pl.pallas_call(
        _ffn_kernel,
        grid_spec=grid_spec,
        out_shape=jax.ShapeDtypeStruct((n_rows * PACKED_ROWS, LANES), jnp.uint32),
        compiler_params=_cparams("arbitrary"),
        name="expert_ffn",
    )(block_e, n_valid, xs, wgu, bgu.reshape(depth, ne, 1, f2), wd, bd.reshape(depth, ne, 1, d))


def _sc_worker_base(per_worker):
    return (lax.axis_index("s") * SC_CORES + lax.axis_index("c")) * per_worker


def _sc_dispatch(rows, dest, n_out):
    t = rows.shape[0]
    kk = dest.shape[0] // t
    w = SC_WINDOW
    per_worker = t // (SC_CORES * SC_SUBCORES)
    assert per_worker * SC_CORES * SC_SUBCORES == t and per_worker % w == 0
    mesh = plsc.VectorSubcoreMesh(core_axis_name="c", subcore_axis_name="s")

    n_win = per_worker // w
    row_buf = pltpu.VMEM((w,) + rows.shape[1:], rows.dtype)

    @functools.partial(
        pl.kernel, mesh=mesh,
        out_type=jax.ShapeDtypeStruct((n_out,) + rows.shape[1:], rows.dtype),
        scratch_types=[pltpu.VMEM((w,), jnp.int32)] * (2 * kk) + [row_buf] * 2 + [pltpu.SemaphoreType.DMA] * 4)
    def scatter_rows(r_hbm, d_hbm, o_hbm, *scratch):
        idx_v = (scratch[:kk], scratch[kk:2 * kk])
        rows_v = scratch[2 * kk:2 * kk + 2]
        rsem = scratch[2 * kk + 2:2 * kk + 4]
        ssem = scratch[2 * kk + 4:2 * kk + 6]
        base = _sc_worker_base(per_worker)

        def read(win, slot):
            return pltpu.make_async_copy(r_hbm.at[pl.ds(base + win * w, w)], rows_v[slot], rsem[slot])

        def scatter(slot, s):
            return pltpu.make_async_copy(rows_v[slot], o_hbm.at[idx_v[slot][s]], ssem[slot])

        def start_read(win, slot):
            read(win, slot).start()
            for s in range(kk):
                pltpu.sync_copy(d_hbm.at[pl.ds(s * t + base + win * w, w)], idx_v[slot][s])

        start_read(0, 0)

        @pl.loop(0, (n_win + 1) // 2)
        def _(pair):
            for slot in range(2):
                win = 2 * pair + slot
                other = 1 - slot

                @pl.when(win < n_win)
                def _():
                    @pl.when(win >= 1)
                    def _():
                        for s in range(kk):
                            scatter(other, s).wait()

                    @pl.when(win + 1 < n_win)
                    def _():
                        start_read(win + 1, other)

                    read(win, slot).wait()
                    for s in range(kk):
                        scatter(slot, s).start()

        for s in range(kk):
            scatter((n_win - 1) % 2, s).wait()

    return scatter_rows(rows, dest)


def _sc_gather(table, idx):
    n = idx.shape[0]
    w = SC_WINDOW
    per_worker = n // (SC_CORES * SC_SUBCORES)
    assert per_worker * SC_CORES * SC_SUBCORES == n and per_worker % w == 0
    mesh = plsc.VectorSubcoreMesh(core_axis_name="c", subcore_axis_name="s")

    n_win = per_worker // w
    assert n_win % 2 == 0
    row_buf = pltpu.VMEM((w,) + table.shape[1:], table.dtype)

    @functools.partial(
        pl.kernel, mesh=mesh,
        out_type=jax.ShapeDtypeStruct((n,) + table.shape[1:], table.dtype),
        scratch_types=[pltpu.VMEM((w,), jnp.int32)] * 2 + [row_buf] * 2 + [pltpu.SemaphoreType.DMA] * 4)
    def gather_rows(t_hbm, i_hbm, o_hbm, idx0, idx1, rows0, rows1, gsem0, gsem1, wsem0, wsem1):
        idx_v, rows_v = (idx0, idx1), (rows0, rows1)
        gsem, wsem = (gsem0, gsem1), (wsem0, wsem1)
        base = _sc_worker_base(per_worker)

        def gather(slot):
            return pltpu.make_async_copy(t_hbm.at[idx_v[slot]], rows_v[slot], gsem[slot])

        def write(win, slot):
            return pltpu.make_async_copy(rows_v[slot], o_hbm.at[pl.ds(base + win * w, w)], wsem[slot])

        def start_gather(win, slot):
            pltpu.sync_copy(i_hbm.at[pl.ds(base + win * w, w)], idx_v[slot])
            gather(slot).start()

        start_gather(0, 0)

        @pl.loop(0, n_win // 2)
        def _(pair):
            for slot in range(2):
                win = 2 * pair + slot
                other = 1 - slot

                @pl.when(win >= 1)
                def _():
                    write(win - 1, other).wait()

                @pl.when(win + 1 < n_win)
                def _():
                    start_gather(win + 1, other)

                gather(slot).wait()
                write(win, slot).start()

        write(n_win - 1, 1).wait()

    return gather_rows(table, idx)


def _combine_kernel(alpha, g_ref, w_ref, x_ref, g2_ref, lng_ref, lnb_ref, *rest):
    o_ref = rest[-1]
    tm = x_ref.shape[1]
    w = w_ref[0]
    f = _load_rows_packed(g_ref.at[0], tm).astype(F32) * w[:, 0:1]
    for kk in range(1, TOP_K):
        f = f + _load_rows_packed(g_ref.at[kk], tm).astype(F32) * w[:, kk:kk + 1]
    o_ref[0] = _layer_norm(alpha * x_ref[0] + g2_ref[0] * f, lng_ref[...], lnb_ref[...])


def _combine(alpha, gathered, b0, nb, wts, x1, g2, lng, lnb, tm, prev=None):
    b, l, d = x1.shape
    nj = l // tm
    tok = pl.BlockSpec((1, tm, d), lambda i, j: (b0 + i, j, 0))
    full = pl.BlockSpec((1, d), lambda i, j: (0, 0))
    in_specs = [pl.BlockSpec((TOP_K, tm * PACKED_ROWS, LANES), lambda i, j: (0, i * nj + j, 0)),
                pl.BlockSpec((1, tm, ROUTER_PAD), lambda i, j: (b0 + i, j, 0)),
                tok, pl.BlockSpec((1, 1, d), lambda i, j: (b0 + i, 0, 0)), full, full]
    args = [gathered, wts, x1, g2, lng, lnb]
    aliases = {}
    if prev is not None:
        in_specs.append(pl.BlockSpec(memory_space=pl.ANY))
        args.append(prev)
        aliases = {len(args) - 1: 0}
    return pl.pallas_call(
        functools.partial(_combine_kernel, alpha),
        grid=(nb, nj),
        in_specs=in_specs,
        out_specs=tok,
        out_shape=jax.ShapeDtypeStruct((b, l, d), F32),
        input_output_aliases=aliases,
        compiler_params=_cparams("arbitrary", "arbitrary"),
        name="combine_ln",
    )(*args)


def _rope_tables(l):
    rows = l // GRID_W
    axis_dim = HEAD_DIM // 2
    inv_freq = ROPE_BASE ** (-jnp.arange(0, axis_dim, 2, dtype=F32) / axis_dim)
    row = jnp.repeat(jnp.arange(rows, dtype=F32), GRID_W)
    col = jnp.tile(jnp.arange(GRID_W, dtype=F32), rows)
    ang = jnp.stack([row[:, None] * inv_freq, col[:, None] * inv_freq], axis=1)
    cos, sin = jnp.cos(ang), jnp.sin(ang)
    cos64 = jnp.broadcast_to(cos[:, :, None, :], (l, 2, 2, HEAD_DIM // 4)).reshape(l, HEAD_DIM)
    sin64 = jnp.stack([-sin, sin], axis=2).reshape(l, HEAD_DIM)
    return jnp.tile(cos64, (1, LANES // HEAD_DIM)), jnp.tile(sin64, (1, LANES // HEAD_DIM))


def _route_tables(ri, counts, tm):
    t = ri.shape[0]
    top_e = ri[:, :TOP_K]
    rank = ri[:, TOP_K:2 * TOP_K]
    padded = (counts + tm - 1) // tm * tm
    pad_end = jnp.cumsum(padded)
    pad_start = pad_end - padded
    experts = jnp.arange(N_EXPERTS, dtype=jnp.int32)
    start = jnp.sum(jnp.where(top_e[:, :, None] == experts, pad_start, 0), axis=-1)
    dest = (start + rank).T.reshape(TOP_K * t)
    n_tiles = (t * TOP_K + N_EXPERTS * (tm - 1) + tm - 1) // tm
    tile_start = jnp.arange(n_tiles, dtype=jnp.int32) * tm
    block_e = jnp.minimum(jnp.sum((pad_end[None, :] <= tile_start[:, None]).astype(jnp.int32), axis=1), N_EXPERTS - 1)
    in_expert = jnp.where(block_e[:, None] == experts[None, :], (pad_start + counts)[None, :], 0).sum(axis=1)
    n_valid = jnp.clip(in_expert - tile_start, 0, tm)
    return dest, block_e.astype(jnp.int32), n_valid.astype(jnp.int32), n_tiles


def _layer_params(layer, d, w_in, conv_w, ret_decay_exp, ret_gn_g, q_norm_g, k_norm_g, w_out, ln_g, ln_b,
                  w_router, b_router):
    log_gamma = jnp.log1p(-jnp.exp2(-ret_decay_exp[layer].astype(F32)))
    lg_rows = jnp.repeat(log_gamma, HEAD_DIM, axis=1)
    wr_pad = jnp.zeros((d, ROUTER_PAD), F32).at[:, :N_EXPERTS].set(w_router[layer])
    wrh = wr_pad.astype(BF16)
    return dict(
        w_in=w_in[layer].astype(BF16), w_out=w_out[layer].astype(BF16),
        gq=jnp.tile(q_norm_g[layer], LANES // HEAD_DIM).reshape(1, LANES),
        gk=jnp.tile(k_norm_g[layer], LANES // HEAD_DIM).reshape(1, LANES),
        log_gamma=log_gamma, lgf_row=lg_rows[0:1], lgb_row=lg_rows[1:2], gn_row=ret_gn_g[layer].reshape(1, -1),
        cw=conv_w[layer].T.reshape(3, 1, -1),
        wr=jnp.concatenate([wrh, (wr_pad - wrh.astype(F32)).astype(BF16)], axis=1),
        br=jnp.zeros((1, ROUTER_PAD), F32).at[0, :N_EXPERTS].set(b_router[layer]),
        lng1=ln_g[layer, 0].reshape(1, d), lnb1=ln_b[layer, 0].reshape(1, d),
        lng2=ln_g[layer, 1].reshape(1, d), lnb2=ln_b[layer, 1].reshape(1, d))


def _layer(layer, last, alpha, x, ctx, m_lat, m_ctx, p, tables, experts):
    b, l, d = x.shape
    lc = ctx.shape[1]
    tm_lat, tm_ctx, tm_moe, tq = 512, 256, 1024, 256
    cos_l, sin_l, cos_c, sin_c, ones_bd, zero_state = tables
    sh1, sc1, g1, sh2, sc2, g2 = (m_lat[:, :, i] for i in range(N_MOD))
    sh1c, sc1c, g1c, sh2c, sc2c, g2c = (m_ctx[:, :, i] for i in range(N_MOD))

    zc = _inproj(ctx, 1.0 + sc1c, sh1c, p["w_in"], cos_c, sin_c, p["gq"], p["gk"], ones_bd, tm_ctx)
    zl = _inproj(x, 1.0 + sc1, sh1, p["w_in"], cos_l, sin_l, p["gq"], p["gk"], ones_bd, tm_lat)
    cu_c, cb_c, rq_c, rk_c, rv_c, rg_c, aq_c, ak_c, av_c = zc
    cu_l, cb_l, rq_l, rk_l, rv_l, rg_l, aq_l, ak_l, av_l = zl

    ret_args = (p["log_gamma"], p["lgf_row"], p["lgb_row"], p["gn_row"])
    ret_c, s_fwd, s_bwd = _retention(rq_c, rk_c, rv_c, rg_c, zero_state, zero_state, *ret_args)
    ret_l, _, _ = _retention(rq_l, rk_l, rv_l, rg_l, s_fwd, s_bwd, *ret_args)

    def kv_heads(a):
        return a.reshape(b, a.shape[1], -1, HEAD_DIM).transpose(0, 2, 1, 3)

    def v_heads(a):
        vt = a.reshape(b, a.shape[1], -1, HEAD_DIM).transpose(0, 2, 3, 1)
        ones = jnp.ones(vt.shape[:2] + (1, vt.shape[3]), BF16)
        pad = jnp.zeros(vt.shape[:2] + (ATT_V_ROWS - HEAD_DIM - 1, vt.shape[3]), BF16)
        return jnp.concatenate([vt, ones, pad], axis=2)

    k_all = kv_heads(jnp.concatenate([ak_c, ak_l], axis=1))
    v_all = v_heads(jnp.concatenate([av_c, av_l], axis=1))
    att_l = _attention(aq_l, k_all, v_all, tq)

    out_args = (p["w_out"], p["lng1"], p["lnb1"])
    rt_args = (p["wr"], p["br"])
    cnt0 = jnp.zeros((1, ROUTER_PAD), F32)
    n_c = 0 if last else b * lc
    n_tok = n_c + b * l
    h2_c = None
    if not last:
        att_c = _attention(aq_c, kv_heads(ak_c), v_heads(av_c), min(tq, lc))
        ctx1, h2_c, ri_c, rw_c, cnt0 = _outproj(alpha, cu_c, cb_c, p["cw"], ret_c, att_c, ctx, g1c, *out_args,
                                                1.0 + sc2c, sh2c, *rt_args, cnt0, tm_ctx, h2_tokens=n_tok)
    x1, h2, ri_l, rw_l, cnt = _outproj(alpha, cu_l, cb_l, p["cw"], ret_l, att_l, x, g1, *out_args,
                                       1.0 + sc2, sh2, *rt_args, cnt0, tm_lat,
                                       h2_tokens=n_tok, h2_offset=n_c, h2_prev=h2_c)
    if not last:
        ri = jnp.concatenate([ri_c.reshape(n_c, -1), ri_l.reshape(b * l, -1)], axis=0)
    else:
        ri = ri_l.reshape(b * l, -1)

    counts = cnt[0, :N_EXPERTS].astype(jnp.int32)
    dest, block_e, n_valid, n_tiles = _route_tables(ri, counts, tm_moe)
    xs = _sc_dispatch(h2.reshape(n_tok, PACKED_ROWS, LANES), dest, n_tiles * tm_moe)
    ys = _expert_ffn(layer, block_e, n_valid, xs.reshape(-1, LANES), *experts, tm_moe)
    ys = ys.reshape(-1, PACKED_ROWS, LANES)
    dest = dest.reshape(TOP_K, n_tok)

    def gather(lo, hi):
        rows = _sc_gather(ys, dest[:, lo:hi].reshape(-1))
        return rows.reshape(TOP_K, (hi - lo) * PACKED_ROWS, LANES)

    if not last:
        ctx = _combine(alpha, gather(0, n_c), 0, b, rw_c, ctx1, g2c, p["lng2"], p["lnb2"], tm_ctx)
    nb = b // COMBINE_CHUNKS if b % COMBINE_CHUNKS == 0 else b
    x_new = None
    for b0 in range(0, b, nb):
        g = gather(n_c + b0 * l, n_c + (b0 + nb) * l)
        x_new = _combine(alpha, g, b0, nb, rw_l, x1, g2, p["lng2"], p["lnb2"], tm_lat, prev=x_new)
    return x_new, ctx


def kernel(x, c, ctx, c_ctx, w_mod, b_mod, w_in, conv_w, ret_decay_exp, ret_gn_g, q_norm_g, k_norm_g, w_out,
           ln_g, ln_b, w_router, b_router, w_gate_up, b_gate_up, w_down, b_down):
    depth = w_mod.shape[0]
    alpha = (2.0 * depth) ** 0.25
    b, l, d = x.shape
    lc = ctx.shape[1]
    groups = BATCH_GROUPS if b % BATCH_GROUPS == 0 else 1
    bg = b // groups

    n_rows = (b + 1 + 7) // 8 * 8
    c_all = jnp.zeros((n_rows, d), F32).at[:b].set(c).at[b].set(c_ctx)
    mod = _modulation(c_all, w_mod, b_mod)

    cos_l, sin_l = _rope_tables(l)
    cos_c, sin_c = jnp.ones((lc, LANES), F32), jnp.zeros((lc, LANES), F32)
    lane_head = jnp.arange(LANES) // HEAD_DIM
    ones_bd = (lane_head[:, None] == lane_head[None, :]).astype(BF16)
    zero_state = jnp.zeros((bg, RET_HEADS, HEAD_DIM, HEAD_DIM), F32)
    tables = (cos_l, sin_l, cos_c, sin_c, ones_bd, zero_state)
    experts = (w_gate_up, b_gate_up, w_down, b_down)

    xs = [x[g * bg:(g + 1) * bg] for g in range(groups)]
    cs = [ctx[g * bg:(g + 1) * bg] for g in range(groups)]
    for layer in range(depth):
        last = layer == depth - 1
        p = _layer_params(layer, d, w_in, conv_w, ret_decay_exp, ret_gn_g, q_norm_g, k_norm_g, w_out, ln_g, ln_b,
                          w_router, b_router)
        m_ctx = jnp.broadcast_to(mod[layer, b].reshape(1, 1, N_MOD, d), (bg, 1, N_MOD, d))
        for g in range(groups):
            m_lat = mod[layer, g * bg:(g + 1) * bg].reshape(bg, 1, N_MOD, d)
            xs[g], cs[g] = _layer(layer, last, alpha, xs[g], cs[g], m_lat, m_ctx, p, tables, experts)
    return jnp.concatenate(xs, axis=0) if groups > 1 else xs[0]
```

```python
import functools

import jax
import jax.numpy as jnp
from jax import lax
from jax.experimental import pallas as pl
from jax.experimental.pallas import tpu as pltpu
from jax.experimental.pallas import tpu_sc as plsc

F32 = jnp.float32
BF16 = jnp.bfloat16

HEAD_DIM = 64
GRID_W = 64
ROPE_BASE = 10000.0
N_EXPERTS = 32
TOP_K = 4
SWIGLU_ALPHA = 1.702
SWIGLU_LIMIT = 7.0
N_MOD = 6
EPS = 1e-6
RET_HEADS = 4
RET_CHUNK = 256
RET_BATCH = 8
ATT_Q_PER_KV = 4
ATT_KV_CHUNK = 256
ATT_Q_SCALE = HEAD_DIM ** -0.5 * 1.4426950408889634
ATT_STREAMS = 2
ATT_V_ROWS = 80
LANES = 128
ROUTER_PAD = 128
VMEM_LIMIT = 56 * 1024 * 1024
PACKED_ROWS = 4
FFN_BLOCK = 256
OUT_STREAMS = 2
COMBINE_CHUNKS = 4
SC_CORES = 2
SC_SUBCORES = 16
SC_WINDOW = 64


def _dot(a, b):
    return jnp.dot(a, b, preferred_element_type=F32)


def _split_bf16(x):
    hi = x.astype(BF16)
    lo = (x - hi.astype(F32)).astype(BF16)
    return hi, lo


def _cparams(*sem):
    return pltpu.CompilerParams(dimension_semantics=sem, vmem_limit_bytes=VMEM_LIMIT)


def _mod_kernel(c_ref, w_ref, b_ref, o_ref):
    c = c_ref[...]
    a = c * jax.nn.sigmoid(c)
    a_hi, a_lo = _split_bf16(a)
    w_hi, w_lo = _split_bf16(w_ref[0])
    o_ref[0] = _dot(a_hi, w_hi) + _dot(a_lo, w_hi) + _dot(a_hi, w_lo) + b_ref[0]


def _modulation(c_all, w_mod, b_mod):
    depth, d, n = w_mod.shape
    r = c_all.shape[0]
    tn = 1536
    return pl.pallas_call(
        _mod_kernel,
        grid=(depth, n // tn),
        in_specs=[
            pl.BlockSpec((r, d), lambda l, j: (0, 0)),
            pl.BlockSpec((1, d, tn), lambda l, j: (l, 0, j)),
            pl.BlockSpec((1, 1, tn), lambda l, j: (l, 0, j)),
        ],
        out_specs=pl.BlockSpec((1, r, tn), lambda l, j: (l, 0, j)),
        out_shape=jax.ShapeDtypeStruct((depth, r, n), F32),
        compiler_params=_cparams("arbitrary", "arbitrary"),
        name="modulation",
    )(c_all, w_mod, b_mod.reshape(depth, 1, n))


def _inproj_kernel(x_ref, sc_ref, sh_ref, w_ref, cos_ref, sin_ref, gq_ref, gk_ref, ones_ref,
                   cu_ref, cb_ref, rq_ref, rk_ref, rv_ref, rg_ref, aq_ref, ak_ref, av_ref):
    h = (x_ref[0] * sc_ref[0] + sh_ref[0]).astype(BF16)
    cos = cos_ref[...]
    sin = sin_ref[...]
    ones = ones_ref[...]
    lane = lax.broadcasted_iota(jnp.int32, cos.shape, 1)
    first_half = (lane & 31) < 16

    def rope(xs):
        nxt = pltpu.roll(xs, LANES - 16, axis=1)
        prv = pltpu.roll(xs, 16, axis=1)
        return xs * cos + jnp.where(first_half, nxt, prv) * sin

    def rms(xs, g):
        s_hi, s_lo = _split_bf16(xs * xs)
        ssq = _dot(s_hi, ones) + _dot(s_lo, ones)
        return xs * lax.rsqrt(ssq * (1.0 / HEAD_DIM) + EPS) * g

    za = _dot(h, w_ref[:, 1792:2560])
    gq = gq_ref[...]
    for j in range(4):
        lo, hi = j * LANES, (j + 1) * LANES
        q = rope(rms(za[:, lo:hi], gq)) * ATT_Q_SCALE
        aq_ref[0, lo:hi, :] = q.T.astype(BF16)
    ak_ref[0] = rope(rms(za[:, 512:640], gk_ref[...])).astype(BF16)
    av_ref[0] = za[:, 640:768].astype(BF16)
    zr = _dot(h, w_ref[:, 768:1792])
    for j in range(2):
        lo, hi = j * LANES, (j + 1) * LANES
        rq_ref[0, lo:hi, :] = rope(zr[:, lo:hi]).T
        rk_ref[0, :, lo:hi] = rope(zr[:, 256 + lo:256 + hi] * (HEAD_DIM ** -0.5))
        rv_ref[0, lo:hi, :] = zr[:, 512 + lo:512 + hi].T
        rg_ref[0, lo:hi, :] = zr[:, 768 + lo:768 + hi].T
    zc = _dot(h, w_ref[:, 0:768])
    cu_ref[0] = zc[:, 512:768] * zc[:, 0:256]
    cb_ref[0] = zc[:, 256:512]


def _inproj(x, sc, sh, w_bf, cos, sin, gq, gk, ones_bd, tm):
    b, l, d = x.shape
    n = w_bf.shape[1]
    tok = lambda w: pl.BlockSpec((1, tm, w), lambda i, j: (i, j, 0))
    vec = pl.BlockSpec((1, 1, d), lambda i, j: (i, 0, 0))
    full = lambda s: pl.BlockSpec(s, lambda i, j: (0,) * len(s))
    widths = (256, 256, 256, 256, 256, 256, 512, 128, 128)
    dtypes = (F32,) * 6 + (BF16,) * 3
    out_specs = [tok(w) for w in widths]
    out_shape = [jax.ShapeDtypeStruct((b, l, w), dt) for w, dt in zip(widths, dtypes)]
    for o in (2, 4, 5, 6):
        out_specs[o] = pl.BlockSpec((1, widths[o], tm), lambda i, j: (i, 0, j))
        out_shape[o] = jax.ShapeDtypeStruct((b, widths[o], l), dtypes[o])
    return pl.pallas_call(
        _inproj_kernel,
        grid=(b, l // tm),
        in_specs=[tok(d), vec, vec, full((d, n)),
                  pl.BlockSpec((tm, LANES), lambda i, j: (j, 0)),
                  pl.BlockSpec((tm, LANES), lambda i, j: (j, 0)),
                  full((1, LANES)), full((1, LANES)), full((LANES, LANES))],
        out_specs=out_specs,
        out_shape=out_shape,
        compiler_params=_cparams("arbitrary", "arbitrary"),
        name="inproj",
    )(x, sc, sh, w_bf, cos, sin, gq, gk, ones_bd)


def _attn_kernel(q_ref, k_ref, vt_ref, o_ref):
    tq = q_ref.shape[2]
    q4 = q_ref[0]
    qt = jnp.concatenate([q4[h * HEAD_DIM:(h + 1) * HEAD_DIM, :] for h in range(ATT_Q_PER_KV)], axis=1)
    lk = k_ref.shape[2]
    bounds = [(lo, min(lo + ATT_KV_CHUNK, lk)) for lo in range(0, lk, ATT_KV_CHUNK)]
    r = qt.shape[1] // ATT_STREAMS
    qs = [qt[:, i * r:(i + 1) * r] for i in range(ATT_STREAMS)]
    ms = [jnp.full((1, r), -jnp.inf, F32) for _ in qs]
    accs = [jnp.zeros((ATT_V_ROWS, r), F32) for _ in qs]
    nxt = [_dot(k_ref[0, 0, bounds[0][0]:bounds[0][1], :], q) for q in qs]
    for c, (lo, hi) in enumerate(bounds):
        for i, q in enumerate(qs):
            s = nxt[i]
            if c + 1 < len(bounds):
                nxt[i] = _dot(k_ref[0, 0, bounds[c + 1][0]:bounds[c + 1][1], :], q)
            m_new = jnp.maximum(ms[i], jnp.max(s, axis=0, keepdims=True))
            p = jnp.exp2(s - m_new)
            accs[i] = accs[i] * jnp.exp2(ms[i] - m_new) + _dot(vt_ref[0, 0, :, lo:hi], p.astype(BF16))
            ms[i] = m_new
    acc = jnp.concatenate(accs, axis=1)
    o = acc[:HEAD_DIM] / acc[HEAD_DIM:HEAD_DIM + 1]
    for h in range(ATT_Q_PER_KV):
        o_ref[0, h] = o[:, h * tq:(h + 1) * tq]


def _attention(q, k, vt, tq):
    b, wq, lq = q.shape
    hkv, lk = k.shape[1], k.shape[2]
    wg = ATT_Q_PER_KV * HEAD_DIM
    return pl.pallas_call(
        _attn_kernel,
        grid=(b, hkv, lq // tq),
        in_specs=[pl.BlockSpec((1, wg, tq), lambda i, g, j: (i, g, j)),
                  pl.BlockSpec((1, 1, lk, HEAD_DIM), lambda i, g, j: (i, g, 0, 0)),
                  pl.BlockSpec((1, 1, ATT_V_ROWS, lk), lambda i, g, j: (i, g, 0, 0))],
        out_specs=pl.BlockSpec((1, ATT_Q_PER_KV, HEAD_DIM, tq), lambda i, g, j: (i, g, 0, j)),
        out_shape=jax.ShapeDtypeStruct((b, wq // HEAD_DIM, HEAD_DIM, lq), F32),
        compiler_params=_cparams("arbitrary", "arbitrary", "arbitrary"),
        name="attention",
    )(q, k, vt)


def _ret_kernel(lg_ref, q_ref, k_ref, v_ref, g_ref, s0f_ref, s0b_ref, lgf_ref, lgb_ref, lgfc_ref, lgbc_ref, gn_ref,
                o_ref, sff_ref, sfb_ref, s_scr, sb_scr, dec_scr):
    i, p, j = pl.program_id(0), pl.program_id(1), pl.program_id(2)
    nc = pl.num_programs(2)
    c = k_ref.shape[1]
    w = k_ref.shape[2]
    pos = lax.broadcasted_iota(jnp.int32, (c, w), 0).astype(F32)
    qpos = lax.broadcasted_iota(jnp.int32, (w, c), 1).astype(F32)
    lgf = lgf_ref[...]
    lgb = lgb_ref[...]
    lgf_col = lgfc_ref[...]
    lgb_col = lgbc_ref[...]

    @pl.when((i == 0) & (p == 0) & (j == 0))
    def _():
        diff = (lax.broadcasted_iota(jnp.int32, (c, c), 1) - lax.broadcasted_iota(jnp.int32, (c, c), 0)).astype(F32)
        for h in range(RET_HEADS):
            dec_scr[h] = jnp.where(diff >= 0.0, jnp.exp(lg_ref[0, h] * jnp.maximum(diff, 0.0)),
                                   jnp.exp(lg_ref[1, h] * jnp.maximum(-diff, 0.0)))

    nb = k_ref.shape[0]

    def update_state(e, vt, kz, cdec):
        for h in range(RET_HEADS):
            lo, hi = h * HEAD_DIM, (h + 1) * HEAD_DIM
            s_scr[e, h] = cdec[:, lo:hi] * s_scr[e, h] + _dot(vt[lo:hi, :], kz[:, lo:hi])

    @pl.when(p == 0)
    def _():
        @pl.when(j == 0)
        def _():
            s_scr[...] = s0b_ref[...]

        sb_scr[nc - 1 - j] = s_scr[...]
        for e in range(nb):
            kz = (k_ref[e] * jnp.exp(lgb * pos)).astype(BF16)
            update_state(e, v_ref[e].astype(BF16), kz, jnp.exp(lgb * float(c)))

        @pl.when(j == nc - 1)
        def _():
            sfb_ref[...] = s_scr[...]

    @pl.when(p == 1)
    def _():
        @pl.when(j == 0)
        def _():
            s_scr[...] = s0f_ref[...]

        for e in range(nb):
            qt = q_ref[e]
            k = k_ref[e]
            qf = (qt * jnp.exp(lgf_col * (qpos + 1.0))).astype(BF16)
            qb = (qt * jnp.exp(lgb_col * (float(c) - qpos))).astype(BF16)
            kz = (k * jnp.exp(lgf * (float(c) - 1.0 - pos))).astype(BF16)
            qh = qt.astype(BF16)
            kh = k.astype(BF16)
            vt = v_ref[e].astype(BF16)
            gate = g_ref[e]
            for h in range(RET_HEADS):
                lo, hi = h * HEAD_DIM, (h + 1) * HEAD_DIM
                sc = _dot(kh[:, lo:hi], qh[lo:hi, :])
                y = (_dot(vt[lo:hi, :], (sc * dec_scr[h]).astype(BF16))
                     + _dot(s_scr[e, h].astype(BF16), qf[lo:hi, :])
                     + _dot(sb_scr[j, e, h].astype(BF16), qb[lo:hi, :]))
                mu = jnp.mean(y, axis=0, keepdims=True)
                yc = y - mu
                var = jnp.mean(yc * yc, axis=0, keepdims=True)
                gh = gate[lo:hi, :]
                o_ref[e, lo:hi, :] = yc * lax.rsqrt(var + EPS) * gn_ref[lo:hi, :] * (gh * jax.nn.sigmoid(gh))
            update_state(e, vt, kz, jnp.exp(lgf * float(c)))

        @pl.when(j == nc - 1)
        def _():
            sff_ref[...] = s_scr[...]


def _retention(qt, k, vt, gate_t, s0_f, s0_b, lg, lgf_row, lgb_row, gn_row):
    b, l, w = k.shape
    c = min(RET_CHUNK, l)
    nc = l // c
    nb = RET_BATCH if b % RET_BATCH == 0 else 1
    st = (nb, RET_HEADS, HEAD_DIM, HEAD_DIM)
    st_spec = pl.BlockSpec(st, lambda i, p, j: (i, 0, 0, 0))
    row = pl.BlockSpec((1, w), lambda i, p, j: (0, 0))
    col = pl.BlockSpec((w, 1), lambda i, p, j: (0, 0))
    k_both = pl.BlockSpec((nb, c, w), lambda i, p, j: (i, jnp.where(p == 0, nc - 1 - j, j), 0))
    t_both = pl.BlockSpec((nb, w, c), lambda i, p, j: (i, 0, jnp.where(p == 0, nc - 1 - j, j)))
    t_fwd = pl.BlockSpec((nb, w, c), lambda i, p, j: (i, 0, p * j))
    state_shape = jax.ShapeDtypeStruct((b,) + st[1:], F32)
    return pl.pallas_call(
        _ret_kernel,
        grid=(b // nb, 2, nc),
        in_specs=[pl.BlockSpec(memory_space=pltpu.SMEM), t_fwd, k_both, t_both, t_fwd,
                  st_spec, st_spec, row, row, col, col, col],
        out_specs=[t_fwd, st_spec, st_spec],
        out_shape=[jax.ShapeDtypeStruct((b, w, l), F32), state_shape, state_shape],
        scratch_shapes=[pltpu.VMEM(st, F32), pltpu.VMEM((nc,) + st, F32), pltpu.VMEM((RET_HEADS, c, c), F32)],
        compiler_params=_cparams("arbitrary", "arbitrary", "arbitrary"),
        name="retention",
    )(lg, qt, k, vt, gate_t, s0_f, s0_b, lgf_row, lgb_row, lgf_row.reshape(w, 1), lgb_row.reshape(w, 1),
      gn_row.reshape(w, 1))


def _layer_norm(x, g, b):
    mu = jnp.mean(x, axis=-1, keepdims=True)
    xc = x - mu
    var = jnp.mean(xc * xc, axis=-1, keepdims=True)
    return xc * lax.rsqrt(var + EPS) * g + b


def _store_rows_packed(ref, x):
    tm, d = x.shape
    bits = pltpu.bitcast(x.astype(BF16).astype(F32), jnp.uint32)
    word = (bits[:, :d // 2] >> 16) | bits[:, d // 2:]
    for s in range(PACKED_ROWS):
        ref[pl.ds(s, tm, stride=PACKED_ROWS), :] = word[:, s * LANES:(s + 1) * LANES]


def _load_rows_packed(ref, tm):
    word = jnp.concatenate([ref[pl.ds(s, tm, stride=PACKED_ROWS), :] for s in range(PACKED_ROWS)], axis=1)
    lo = pltpu.bitcast(word << 16, F32).astype(BF16)
    hi = pltpu.bitcast(word & jnp.uint32(0xFFFF0000), F32).astype(BF16)
    return jnp.concatenate([lo, hi], axis=1)


def _outproj_kernel(alpha, cu_ref, cup_ref, cun_ref, cb_ref, cw_ref, ret_ref, att_ref, x_ref, g1_ref,
                    w_ref, lng_ref, lnb_ref, sc_ref, sh_ref, wr_ref, br_ref, cnt0_ref, before_ref, *rest):
    x1_ref, h2_ref, ri_ref, rw_ref, cnt_ref, run_scr = rest[-6:]
    i = pl.program_id(0)
    j = pl.program_id(1)

    @pl.when((i == 0) & (j == 0))
    def _():
        run_scr[...] = cnt0_ref[...]

    t = cu_ref[0]
    tm = t.shape[0]
    prev_row = jnp.where(j == 0, 0.0, cup_ref[0][7:8, :])
    next_row = jnp.where(j == pl.num_programs(1) - 1, 0.0, cun_ref[0][0:1, :])
    row = lax.broadcasted_iota(jnp.int32, t.shape, 0)
    t_prev = jnp.where(row == 0, prev_row, pltpu.roll(t, 1, axis=0))
    t_next = jnp.where(row == tm - 1, next_row, pltpu.roll(t, tm - 1, axis=0))
    conv = cb_ref[0] * (t_prev * cw_ref[0] + t * cw_ref[1] + t_next * cw_ref[2])
    conv = conv.astype(BF16)
    wrh = wr_ref[:, :ROUTER_PAD]
    rb = tm // OUT_STREAMS
    lane = lax.broadcasted_iota(jnp.int32, (rb, ROUTER_PAD), 1)
    lane_f = lane.astype(F32)
    routed = []
    for r0 in range(0, tm, rb):
        att_t = att_ref[0, :, :, r0:r0 + rb].reshape(att_ref.shape[1] * HEAD_DIM, rb).astype(BF16)
        tn = (((0,), (0,)), ((), ()))
        y = (_dot(conv[r0:r0 + rb], w_ref[0:256, :])
             + lax.dot_general(ret_ref[0, :, r0:r0 + rb].astype(BF16), w_ref[256:512, :], tn,
                               preferred_element_type=F32)
             + lax.dot_general(att_t, w_ref[512:1024, :], tn, preferred_element_type=F32))
        x1 = _layer_norm(alpha * x_ref[0, r0:r0 + rb, :] + g1_ref[0] * y, lng_ref[...], lnb_ref[...])
        x1_ref[0, r0:r0 + rb, :] = x1
        h2 = x1 * sc_ref[0] + sh_ref[0]
        _store_rows_packed(h2_ref.at[pl.ds(r0 * PACKED_ROWS, rb * PACKED_ROWS)], h2)
        h_hi, h_lo = _split_bf16(h2)
        hw = _dot(h_hi, wr_ref[...])
        logits = hw[:, :ROUTER_PAD] + hw[:, ROUTER_PAD:] + _dot(h_lo, wrh) + br_ref[...]

        work = jnp.where(lane < N_EXPERTS, logits, -jnp.inf)
        vals, firsts, sels = [], [], []
        for _ in range(TOP_K):
            m = jnp.max(work, axis=-1, keepdims=True)
            first = jnp.min(jnp.where(work == m, lane_f, float(ROUTER_PAD)), axis=-1, keepdims=True)
            sel = lane_f == first
            vals.append(m)
            firsts.append(first)
            sels.append(sel)
            work = jnp.where(sel, -jnp.inf, work)
        exps = [jnp.exp(v - vals[0]) for v in vals]
        denom = exps[0]
        for e in exps[1:]:
            denom = denom + e
        cnt = jnp.zeros(logits.shape, F32)
        for sel in sels:
            cnt = cnt + jnp.where(sel, 1.0, 0.0)
        routed.append((firsts, sels, [e / denom for e in exps], cnt))

    cnt = jnp.concatenate([r[3] for r in routed], axis=0)
    base = _dot(before_ref[...], cnt.astype(BF16)) + run_scr[...]
    run_scr[...] = run_scr[...] + jnp.sum(cnt, axis=0, keepdims=True)
    cnt_ref[...] = run_scr[...]

    for blk, (firsts, sels, wts, _) in enumerate(routed):
        r0 = blk * rb
        ri = jnp.zeros((rb, ROUTER_PAD), F32)
        rw = jnp.zeros((rb, ROUTER_PAD), F32)
        for kk in range(TOP_K):
            rank = jnp.sum(jnp.where(sels[kk], base[r0:r0 + rb], 0.0), axis=-1, keepdims=True)
            ri = jnp.where(lane == kk, firsts[kk], ri)
            ri = jnp.where(lane == TOP_K + kk, rank, ri)
            rw = jnp.where(lane == kk, wts[kk], rw)
        ri_ref[0, r0:r0 + rb, :] = ri.astype(jnp.int32)
        rw_ref[0, r0:r0 + rb, :] = rw


def _outproj(alpha, cu, cb, cw, ret, att, x, g1, w_bf, lng, lnb, sc2, sh2, wr, br, cnt0, tm,
             h2_tokens=None, h2_offset=0, h2_prev=None):
    b, l, d = x.shape
    nj = l // tm
    h2_tokens = b * l if h2_tokens is None else h2_tokens
    assert h2_offset % tm == 0
    h2_blk = h2_offset // tm
    extra_specs, extra_args, aliases = [], [], {}
    if h2_prev is not None:
        extra_specs, extra_args, aliases = [pl.BlockSpec(memory_space=pl.ANY)], [h2_prev], {18: 1}
    tok = lambda w: pl.BlockSpec((1, tm, w), lambda i, j: (i, j, 0))
    vec = pl.BlockSpec((1, 1, d), lambda i, j: (i, 0, 0))
    full = lambda s: pl.BlockSpec(s, lambda i, j: (0,) * len(s))
    r8 = tm // 8
    nb8 = l // 8
    tri = jnp.arange(tm)
    before = (tri[None, :] < tri[:, None]).astype(BF16)
    return pl.pallas_call(
        functools.partial(_outproj_kernel, alpha),
        grid=(b, nj),
        in_specs=[tok(256),
                  pl.BlockSpec((1, 8, 256), lambda i, j: (i, jnp.maximum(j * r8 - 1, 0), 0)),
                  pl.BlockSpec((1, 8, 256), lambda i, j: (i, jnp.minimum((j + 1) * r8, nb8 - 1), 0)),
                  tok(256), full((3, 1, 256)),
                  pl.BlockSpec((1, ret.shape[1], tm), lambda i, j: (i, 0, j)),
                  pl.BlockSpec((1,) + att.shape[1:3] + (tm,), lambda i, j: (i, 0, 0, j)), tok(d), vec,
                  full((d, d)), full((1, d)), full((1, d)), vec, vec,
                  full((d, 2 * ROUTER_PAD)), full((1, ROUTER_PAD)), full((1, ROUTER_PAD)),
                  full((tm, tm))] + extra_specs,
        out_specs=[tok(d), pl.BlockSpec((tm * PACKED_ROWS, LANES), lambda i, j: (h2_blk + i * nj + j, 0)),
                   tok(ROUTER_PAD), tok(ROUTER_PAD), full((1, ROUTER_PAD))],
        out_shape=[jax.ShapeDtypeStruct((b, l, d), F32),
                   jax.ShapeDtypeStruct((h2_tokens * PACKED_ROWS, LANES), jnp.uint32),
                   jax.ShapeDtypeStruct((b, l, ROUTER_PAD), jnp.int32),
                   jax.ShapeDtypeStruct((b, l, ROUTER_PAD), F32),
                   jax.ShapeDtypeStruct((1, ROUTER_PAD), F32)],
        scratch_shapes=[pltpu.VMEM((1, ROUTER_PAD), F32)],
        input_output_aliases=aliases,
        compiler_params=_cparams("arbitrary", "arbitrary"),
        name="outproj_ln_router",
    )(cu, cu, cu, cb, cw, ret, att, x, g1, w_bf, lng, lnb, sc2, sh2, wr, br, cnt0, before, *extra_args)


def _ffn_kernel(be_ref, nv_ref, x_ref, wgu_ref, bgu_ref, wd_ref, bd_ref, o_ref, wgu_scr, wd_scr):
    j = pl.program_id(0)
    f = wd_ref.shape[2]
    tm = x_ref.shape[0] // PACKED_ROWS
    nv = nv_ref[j]

    @pl.when(nv > 0)
    def _():
        @pl.when((j == 0) | (be_ref[j] != be_ref[jnp.maximum(j - 1, 0)]))
        def _():
            wgu_scr[...] = wgu_ref[0, 0].astype(BF16)
            wd_scr[...] = wd_ref[0, 0].astype(BF16)

    def compute(rows):
        part = pl.ds(0, rows * PACKED_ROWS)
        x = _load_rows_packed(x_ref.at[part], rows)
        bgu = bgu_ref[0, 0]
        acts = []
        for lo in range(0, f, FFN_BLOCK):
            hi = lo + FFN_BLOCK
            gate = jnp.minimum(_dot(x, wgu_scr[:, lo:hi]) + bgu[:, lo:hi], SWIGLU_LIMIT)
            up = jnp.clip(_dot(x, wgu_scr[:, f + lo:f + hi]) + bgu[:, f + lo:f + hi], -SWIGLU_LIMIT, SWIGLU_LIMIT)
            acts.append(((up + 1.0) * (gate * jax.nn.sigmoid(SWIGLU_ALPHA * gate))).astype(BF16))
        act = jnp.concatenate(acts, axis=1)
        _store_rows_packed(o_ref.at[part], _dot(act, wd_scr[...]) + bd_ref[0, 0])

    @pl.when(nv > tm // 2)
    def _():
        compute(tm)

    @pl.when((nv > 0) & (nv <= tm // 2))
    def _():
        compute(tm // 2)


def _expert_ffn(layer, block_e, n_valid, xs, wgu, bgu, wd, bd, tm):
    n_rows = xs.shape[0] // PACKED_ROWS
    depth, ne, d, f2 = wgu.shape
    f = f2 // 2
    rows = pl.BlockSpec((tm * PACKED_ROWS, LANES), lambda j, be, nu: (j, 0))
    grid_spec = pltpu.PrefetchScalarGridSpec(
        num_scalar_prefetch=2,
        grid=(n_rows // tm,),
        in_specs=[rows,
                  pl.BlockSpec((1, 1, d, f2), lambda j, be, nu: (layer, be[j], 0, 0)),
                  pl.BlockSpec((1, 1, 1, f2), lambda j, be, nu: (layer, be[j], 0, 0)),
                  pl.BlockSpec((1, 1, f, d), lambda j, be, nu: (layer, be[j], 0, 0)),
                  pl.BlockSpec((1, 1, 1, d), lambda j, be, nu: (layer, be[j], 0, 0))],
        out_specs=rows,
        scratch_shapes=[pltpu.VMEM((d, f2), BF16), pltpu.VMEM((f, d), BF16)],
    )
    return pl.pallas_call(
        _ffn_kernel,
        grid_spec=grid_spec,
        out_shape=jax.ShapeDtypeStruct((n_rows * PACKED_ROWS, LANES), jnp.uint32),
        compiler_params=_cparams("arbitrary"),
        name="expert_ffn",
    )(block_e, n_valid, xs, wgu, bgu.reshape(depth, ne, 1, f2), wd, bd.reshape(depth, ne, 1, d))


def _sc_worker_base(per_worker):
    return (lax.axis_index("s") * SC_CORES + lax.axis_index("c")) * per_worker


def _sc_dispatch(rows, dest, n_out):
    t = rows.shape[0]
    kk = dest.shape[0] // t
    w = SC_WINDOW
    per_worker = t // (SC_CORES * SC_SUBCORES)
    assert per_worker * SC_CORES * SC_SUBCORES == t and per_worker % w == 0
    mesh = plsc.VectorSubcoreMesh(core_axis_name="c", subcore_axis_name="s")

    n_win = per_worker // w
    row_buf = pltpu.VMEM((w,) + rows.shape[1:], rows.dtype)

    @functools.partial(
        pl.kernel, mesh=mesh,
        out_type=jax.ShapeDtypeStruct((n_out,) + rows.shape[1:], rows.dtype),
        scratch_types=[pltpu.VMEM((w,), jnp.int32)] * (2 * kk) + [row_buf] * 2 + [pltpu.SemaphoreType.DMA] * 4)
    def scatter_rows(r_hbm, d_hbm, o_hbm, *scratch):
        idx_v = (scratch[:kk], scratch[kk:2 * kk])
        rows_v = scratch[2 * kk:2 * kk + 2]
        rsem = scratch[2 * kk + 2:2 * kk + 4]
        ssem = scratch[2 * kk + 4:2 * kk + 6]
        base = _sc_worker_base(per_worker)

        def read(win, slot):
            return pltpu.make_async_copy(r_hbm.at[pl.ds(base + win * w, w)], rows_v[slot], rsem[slot])

        def scatter(slot, s):
            return pltpu.make_async_copy(rows_v[slot], o_hbm.at[idx_v[slot][s]], ssem[slot])

        def start_read(win, slot):
            read(win, slot).start()
            for s in range(kk):
                pltpu.sync_copy(d_hbm.at[pl.ds(s * t + base + win * w, w)], idx_v[slot][s])

        start_read(0, 0)

        @pl.loop(0, (n_win + 1) // 2)
        def _(pair):
            for slot in range(2):
                win = 2 * pair + slot
                other = 1 - slot

                @pl.when(win < n_win)
                def _():
                    @pl.when(win >= 1)
                    def _():
                        for s in range(kk):
                            scatter(other, s).wait()

                    @pl.when(win + 1 < n_win)
                    def _():
                        start_read(win + 1, other)

                    read(win, slot).wait()
                    for s in range(kk):
                        scatter(slot, s).start()

        for s in range(kk):
            scatter((n_win - 1) % 2, s).wait()

    return scatter_rows(rows, dest)


def _sc_gather(table, idx):
    n = idx.shape[0]
    w = SC_WINDOW
    per_worker = n // (SC_CORES * SC_SUBCORES)
    assert per_worker * SC_CORES * SC_SUBCORES == n and per_worker % w == 0
    mesh = plsc.VectorSubcoreMesh(core_axis_name="c", subcore_axis_name="s")

    n_win = per_worker // w
    assert n_win % 2 == 0
    row_buf = pltpu.VMEM((w,) + table.shape[1:], table.dtype)

    @functools.partial(
        pl.kernel, mesh=mesh,
        out_type=jax.ShapeDtypeStruct((n,) + table.shape[1:], table.dtype),
        scratch_types=[pltpu.VMEM((w,), jnp.int32)] * 2 + [row_buf] * 2 + [pltpu.SemaphoreType.DMA] * 4)
    def gather_rows(t_hbm, i_hbm, o_hbm, idx0, idx1, rows0, rows1, gsem0, gsem1, wsem0, wsem1):
        idx_v, rows_v = (idx0, idx1), (rows0, rows1)
        gsem, wsem = (gsem0, gsem1), (wsem0, wsem1)
        base = _sc_worker_base(per_worker)

        def gather(slot):
            return pltpu.make_async_copy(t_hbm.at[idx_v[slot]], rows_v[slot], gsem[slot])

        def write(win, slot):
            return pltpu.make_async_copy(rows_v[slot], o_hbm.at[pl.ds(base + win * w, w)], wsem[slot])

        def start_gather(win, slot):
            pltpu.sync_copy(i_hbm.at[pl.ds(base + win * w, w)], idx_v[slot])
            gather(slot).start()

        start_gather(0, 0)

        @pl.loop(0, n_win // 2)
        def _(pair):
            for slot in range(2):
                win = 2 * pair + slot
                other = 1 - slot

                @pl.when(win >= 1)
                def _():
                    write(win - 1, other).wait()

                @pl.when(win + 1 < n_win)
                def _():
                    start_gather(win + 1, other)

                gather(slot).wait()
                write(win, slot).start()

        write(n_win - 1, 1).wait()

    return gather_rows(table, idx)


def _combine_kernel(alpha, g_ref, w_ref, x_ref, g2_ref, lng_ref, lnb_ref, *rest):
    o_ref = rest[-1]
    tm = x_ref.shape[1]
    w = w_ref[0]
    f = _load_rows_packed(g_ref.at[0], tm).astype(F32) * w[:, 0:1]
    for kk in range(1, TOP_K):
        f = f + _load_rows_packed(g_ref.at[kk], tm).astype(F32) * w[:, kk:kk + 1]
    o_ref[0] = _layer_norm(alpha * x_ref[0] + g2_ref[0] * f, lng_ref[...], lnb_ref[...])


def _combine(alpha, gathered, b0, nb, wts, x1, g2, lng, lnb, tm, prev=None):
    b, l, d = x1.shape
    nj = l // tm
    tok = pl.BlockSpec((1, tm, d), lambda i, j: (b0 + i, j, 0))
    full = pl.BlockSpec((1, d), lambda i, j: (0, 0))
    in_specs = [pl.BlockSpec((TOP_K, tm * PACKED_ROWS, LANES), lambda i, j: (0, i * nj + j, 0)),
                pl.BlockSpec((1, tm, ROUTER_PAD), lambda i, j: (b0 + i, j, 0)),
                tok, pl.BlockSpec((1, 1, d), lambda i, j: (b0 + i, 0, 0)), full, full]
    args = [gathered, wts, x1, g2, lng, lnb]
    aliases = {}
    if prev is not None:
        in_specs.append(pl.BlockSpec(memory_space=pl.ANY))
        args.append(prev)
        aliases = {len(args) - 1: 0}
    return pl.pallas_call(
        functools.partial(_combine_kernel, alpha),
        grid=(nb, nj),
        in_specs=in_specs,
        out_specs=tok,
        out_shape=jax.ShapeDtypeStruct((b, l, d), F32),
        input_output_aliases=aliases,
        compiler_params=_cparams("arbitrary", "arbitrary"),
        name="combine_ln",
    )(*args)


def _rope_tables(l):
    rows = l // GRID_W
    axis_dim = HEAD_DIM // 2
    inv_freq = ROPE_BASE ** (-jnp.arange(0, axis_dim, 2, dtype=F32) / axis_dim)
    row = jnp.repeat(jnp.arange(rows, dtype=F32), GRID_W)
    col = jnp.tile(jnp.arange(GRID_W, dtype=F32), rows)
    ang = jnp.stack([row[:, None] * inv_freq, col[:, None] * inv_freq], axis=1)
    cos, sin = jnp.cos(ang), jnp.sin(ang)
    cos64 = jnp.broadcast_to(cos[:, :, None, :], (l, 2, 2, HEAD_DIM // 4)).reshape(l, HEAD_DIM)
    sin64 = jnp.stack([-sin, sin], axis=2).reshape(l, HEAD_DIM)
    return jnp.tile(cos64, (1, LANES // HEAD_DIM)), jnp.tile(sin64, (1, LANES // HEAD_DIM))


def _route_tables(ri, counts, tm):
    t = ri.shape[0]
    top_e = ri[:, :TOP_K]
    rank = ri[:, TOP_K:2 * TOP_K]
    padded = (counts + tm - 1) // tm * tm
    pad_end = jnp.cumsum(padded)
    pad_start = pad_end - padded
    experts = jnp.arange(N_EXPERTS, dtype=jnp.int32)
    start = jnp.sum(jnp.where(top_e[:, :, None] == experts, pad_start, 0), axis=-1)
    dest = (start + rank).T.reshape(TOP_K * t)
    n_tiles = (t * TOP_K + N_EXPERTS * (tm - 1) + tm - 1) // tm
    tile_start = jnp.arange(n_tiles, dtype=jnp.int32) * tm
    block_e = jnp.minimum(jnp.sum((pad_end[None, :] <= tile_start[:, None]).astype(jnp.int32), axis=1), N_EXPERTS - 1)
    in_expert = jnp.where(block_e[:, None] == experts[None, :], (pad_start + counts)[None, :], 0).sum(axis=1)
    n_valid = jnp.clip(in_expert - tile_start, 0, tm)
    return dest, block_e.astype(jnp.int32), n_valid.astype(jnp.int32), n_tiles


def _layer_params(layer, d, w_in, conv_w, ret_decay_exp, ret_gn_g, q_norm_g, k_norm_g, w_out, ln_g, ln_b,
                  w_router, b_router):
    log_gamma = jnp.log1p(-jnp.exp2(-ret_decay_exp[layer].astype(F32)))
    lg_rows = jnp.repeat(log_gamma, HEAD_DIM, axis=1)
    wr_pad = jnp.zeros((d, ROUTER_PAD), F32).at[:, :N_EXPERTS].set(w_router[layer])
    wrh = wr_pad.astype(BF16)
    return dict(
        w_in=w_in[layer].astype(BF16), w_out=w_out[layer].astype(BF16),
        gq=jnp.tile(q_norm_g[layer], LANES // HEAD_DIM).reshape(1, LANES),
        gk=jnp.tile(k_norm_g[layer], LANES // HEAD_DIM).reshape(1, LANES),
        log_gamma=log_gamma, lgf_row=lg_rows[0:1], lgb_row=lg_rows[1:2], gn_row=ret_gn_g[layer].reshape(1, -1),
        cw=conv_w[layer].T.reshape(3, 1, -1),
        wr=jnp.concatenate([wrh, (wr_pad - wrh.astype(F32)).astype(BF16)], axis=1),
        br=jnp.zeros((1, ROUTER_PAD), F32).at[0, :N_EXPERTS].set(b_router[layer]),
        lng1=ln_g[layer, 0].reshape(1, d), lnb1=ln_b[layer, 0].reshape(1, d),
        lng2=ln_g[layer, 1].reshape(1, d), lnb2=ln_b[layer, 1].reshape(1, d))


def _layer(layer, last, alpha, x, ctx, m_lat, m_ctx, p, tables, experts):
    b, l, d = x.shape
    lc = ctx.shape[1]
    tm_lat, tm_ctx, tm_moe, tq = 512, 256, 1024, 256
    cos_l, sin_l, cos_c, sin_c, ones_bd, zero_state = tables
    sh1, sc1, g1, sh2, sc2, g2 = (m_lat[:, :, i] for i in range(N_MOD))
    sh1c, sc1c, g1c, sh2c, sc2c, g2c = (m_ctx[:, :, i] for i in range(N_MOD))

    zc = _inproj(ctx, 1.0 + sc1c, sh1c, p["w_in"], cos_c, sin_c, p["gq"], p["gk"], ones_bd, tm_ctx)
    zl = _inproj(x, 1.0 + sc1, sh1, p["w_in"], cos_l, sin_l, p["gq"], p["gk"], ones_bd, 2 * tm_lat)
    cu_c, cb_c, rq_c, rk_c, rv_c, rg_c, aq_c, ak_c, av_c = zc
    cu_l, cb_l, rq_l, rk_l, rv_l, rg_l, aq_l, ak_l, av_l = zl

    ret_args = (p["log_gamma"], p["lgf_row"], p["lgb_row"], p["gn_row"])
    ret_c, s_fwd, s_bwd = _retention(rq_c, rk_c, rv_c, rg_c, zero_state, zero_state, *ret_args)
    ret_l, _, _ = _retention(rq_l, rk_l, rv_l, rg_l, s_fwd, s_bwd, *ret_args)

    def kv_heads(a):
        return a.reshape(b, a.shape[1], -1, HEAD_DIM).transpose(0, 2, 1, 3)

    def v_heads(a):
        vt = a.reshape(b, a.shape[1], -1, HEAD_DIM).transpose(0, 2, 3, 1)
        ones = jnp.ones(vt.shape[:2] + (1, vt.shape[3]), BF16)
        pad = jnp.zeros(vt.shape[:2] + (ATT_V_ROWS - HEAD_DIM - 1, vt.shape[3]), BF16)
        return jnp.concatenate([vt, ones, pad], axis=2)

    k_all = kv_heads(jnp.concatenate([ak_c, ak_l], axis=1))
    v_all = v_heads(jnp.concatenate([av_c, av_l], axis=1))
    att_l = _attention(aq_l, k_all, v_all, tq)

    out_args = (p["w_out"], p["lng1"], p["lnb1"])
    rt_args = (p["wr"], p["br"])
    cnt0 = jnp.zeros((1, ROUTER_PAD), F32)
    n_c = 0 if last else b * lc
    n_tok = n_c + b * l
    h2_c = None
    if not last:
        att_c = _attention(aq_c, kv_heads(ak_c), v_heads(av_c), min(tq, lc))
        ctx1, h2_c, ri_c, rw_c, cnt0 = _outproj(alpha, cu_c, cb_c, p["cw"], ret_c, att_c, ctx, g1c, *out_args,
                                                1.0 + sc2c, sh2c, *rt_args, cnt0, tm_ctx, h2_tokens=n_tok)
    x1, h2, ri_l, rw_l, cnt = _outproj(alpha, cu_l, cb_l, p["cw"], ret_l, att_l, x, g1, *out_args,
                                       1.0 + sc2, sh2, *rt_args, cnt0, tm_lat,
                                       h2_tokens=n_tok, h2_offset=n_c, h2_prev=h2_c)
    if not last:
        ri = jnp.concatenate([ri_c.reshape(n_c, -1), ri_l.reshape(b * l, -1)], axis=0)
    else:
        ri = ri_l.reshape(b * l, -1)

    counts = cnt[0, :N_EXPERTS].astype(jnp.int32)
    dest, block_e, n_valid, n_tiles = _route_tables(ri, counts, tm_moe)
    xs = _sc_dispatch(h2.reshape(n_tok, PACKED_ROWS, LANES), dest, n_tiles * tm_moe)
    ys = _expert_ffn(layer, block_e, n_valid, xs.reshape(-1, LANES), *experts, tm_moe)
    ys = ys.reshape(-1, PACKED_ROWS, LANES)
    dest = dest.reshape(TOP_K, n_tok)

    def gather(lo, hi):
        rows = _sc_gather(ys, dest[:, lo:hi].reshape(-1))
        return rows.reshape(TOP_K, (hi - lo) * PACKED_ROWS, LANES)

    if not last:
        ctx = _combine(alpha, gather(0, n_c), 0, b, rw_c, ctx1, g2c, p["lng2"], p["lnb2"], tm_ctx)
    nb = b // COMBINE_CHUNKS if b % COMBINE_CHUNKS == 0 else b
    x_new = None
    for b0 in range(0, b, nb):
        g = gather(n_c + b0 * l, n_c + (b0 + nb) * l)
        x_new = _combine(alpha, g, b0, nb, rw_l, x1, g2, p["lng2"], p["lnb2"], tm_lat, prev=x_new)
    return x_new, ctx


def kernel(x, c, ctx, c_ctx, w_mod, b_mod, w_in, conv_w, ret_decay_exp, ret_gn_g, q_norm_g, k_norm_g, w_out,
           ln_g, ln_b, w_router, b_router, w_gate_up, b_gate_up, w_down, b_down):
    depth = w_mod.shape[0]
    alpha = (2.0 * depth) ** 0.25
    b, l, d = x.shape
    lc = ctx.shape[1]

    n_rows = (b + 1 + 7) // 8 * 8
    c_all = jnp.zeros((n_rows, d), F32).at[:b].set(c).at[b].set(c_ctx)
    mod = _modulation(c_all, w_mod, b_mod)

    cos_l, sin_l = _rope_tables(l)
    cos_c, sin_c = jnp.ones((lc, LANES), F32), jnp.zeros((lc, LANES), F32)
    lane_head = jnp.arange(LANES) // HEAD_DIM
    ones_bd = (lane_head[:, None] == lane_head[None, :]).astype(BF16)
    zero_state = jnp.zeros((b, RET_HEADS, HEAD_DIM, HEAD_DIM), F32)
    tables = (cos_l, sin_l, cos_c, sin_c, ones_bd, zero_state)
    experts = (w_gate_up, b_gate_up, w_down, b_down)

    for layer in range(depth):
        last = layer == depth - 1
        p = _layer_params(layer, d, w_in, conv_w, ret_decay_exp, ret_gn_g, q_norm_g, k_norm_g, w_out, ln_g, ln_b,
                          w_router, b_router)
        m_lat = mod[layer, :b].reshape(b, 1, N_MOD, d)
        m_ctx = jnp.broadcast_to(mod[layer, b].reshape(1, 1, N_MOD, d), (b, 1, N_MOD, d))
        x, ctx = _layer(layer, last, alpha, x, ctx, m_lat, m_ctx, p, tables, experts)
    return x
```

```python
import functools

import jax
import jax.numpy as jnp
from jax import lax
from jax.experimental import pallas as pl
from jax.experimental.pallas import tpu as pltpu
from jax.experimental.pallas import tpu_sc as plsc

F32 = jnp.float32
BF16 = jnp.bfloat16

HEAD_DIM = 64
GRID_W = 64
ROPE_BASE = 10000.0
N_EXPERTS = 32
TOP_K = 4
SWIGLU_ALPHA = 1.702
SWIGLU_LIMIT = 7.0
N_MOD = 6
EPS = 1e-6
RET_HEADS = 4
RET_CHUNK = 256
RET_BATCH = 8
ATT_Q_PER_KV = 4
ATT_KV_CHUNK = 256
ATT_Q_SCALE = HEAD_DIM ** -0.5 * 1.4426950408889634
ATT_STREAMS = 2
ATT_V_ROWS = 80
LANES = 128
ROUTER_PAD = 128
VMEM_LIMIT = 56 * 1024 * 1024
PACKED_ROWS = 4
FFN_BLOCK = 256
FFN_ROW_STEPS = 4
OUT_STREAMS = 2
COMBINE_CHUNKS = 4
SC_CORES = 2
SC_SUBCORES = 16
SC_WINDOW = 64


def _dot(a, b):
    return jnp.dot(a, b, preferred_element_type=F32)


def _split_bf16(x):
    hi = x.astype(BF16)
    lo = (x - hi.astype(F32)).astype(BF16)
    return hi, lo


def _cparams(*sem):
    return pltpu.CompilerParams(dimension_semantics=sem, vmem_limit_bytes=VMEM_LIMIT)


def _mod_kernel(c_ref, w_ref, b_ref, o_ref):
    c = c_ref[...]
    a = c * jax.nn.sigmoid(c)
    a_hi, a_lo = _split_bf16(a)
    w_hi, w_lo = _split_bf16(w_ref[0])
    o_ref[0] = _dot(a_hi, w_hi) + _dot(a_lo, w_hi) + _dot(a_hi, w_lo) + b_ref[0]


def _modulation(c_all, w_mod, b_mod):
    depth, d, n = w_mod.shape
    r = c_all.shape[0]
    tn = 1536
    return pl.pallas_call(
        _mod_kernel,
        grid=(depth, n // tn),
        in_specs=[
            pl.BlockSpec((r, d), lambda l, j: (0, 0)),
            pl.BlockSpec((1, d, tn), lambda l, j: (l, 0, j)),
            pl.BlockSpec((1, 1, tn), lambda l, j: (l, 0, j)),
        ],
        out_specs=pl.BlockSpec((1, r, tn), lambda l, j: (l, 0, j)),
        out_shape=jax.ShapeDtypeStruct((depth, r, n), F32),
        compiler_params=_cparams("arbitrary", "arbitrary"),
        name="modulation",
    )(c_all, w_mod, b_mod.reshape(depth, 1, n))


def _inproj_kernel(x_ref, sc_ref, sh_ref, w_ref, cos_ref, sin_ref, gq_ref, gk_ref, ones_ref,
                   cu_ref, cb_ref, rq_ref, rk_ref, rv_ref, rg_ref, aq_ref, ak_ref, av_ref):
    h = (x_ref[0] * sc_ref[0] + sh_ref[0]).astype(BF16)
    cos = cos_ref[...]
    sin = sin_ref[...]
    ones = ones_ref[...]
    lane = lax.broadcasted_iota(jnp.int32, cos.shape, 1)
    first_half = (lane & 31) < 16

    def rope(xs):
        nxt = pltpu.roll(xs, LANES - 16, axis=1)
        prv = pltpu.roll(xs, 16, axis=1)
        return xs * cos + jnp.where(first_half, nxt, prv) * sin

    def rms(xs, g):
        s_hi, s_lo = _split_bf16(xs * xs)
        ssq = _dot(s_hi, ones) + _dot(s_lo, ones)
        return xs * lax.rsqrt(ssq * (1.0 / HEAD_DIM) + EPS) * g

    za = _dot(h, w_ref[:, 1792:2560])
    gq = gq_ref[...]
    for j in range(4):
        lo, hi = j * LANES, (j + 1) * LANES
        q = rope(rms(za[:, lo:hi], gq)) * ATT_Q_SCALE
        aq_ref[0, lo:hi, :] = q.T.astype(BF16)
    ak_ref[0] = rope(rms(za[:, 512:640], gk_ref[...])).astype(BF16)
    av_ref[0] = za[:, 640:768].astype(BF16)
    zr = _dot(h, w_ref[:, 768:1792])
    for j in range(2):
        lo, hi = j * LANES, (j + 1) * LANES
        rq_ref[0, lo:hi, :] = rope(zr[:, lo:hi]).T
        rk_ref[0, :, lo:hi] = rope(zr[:, 256 + lo:256 + hi] * (HEAD_DIM ** -0.5))
        rv_ref[0, lo:hi, :] = zr[:, 512 + lo:512 + hi].T
        rg_ref[0, lo:hi, :] = zr[:, 768 + lo:768 + hi].T
    zc = _dot(h, w_ref[:, 0:768])
    cu_ref[0] = zc[:, 512:768] * zc[:, 0:256]
    cb_ref[0] = zc[:, 256:512]


def _inproj(x, sc, sh, w_bf, cos, sin, gq, gk, ones_bd, tm):
    b, l, d = x.shape
    n = w_bf.shape[1]
    tok = lambda w: pl.BlockSpec((1, tm, w), lambda i, j: (i, j, 0))
    vec = pl.BlockSpec((1, 1, d), lambda i, j: (i, 0, 0))
    full = lambda s: pl.BlockSpec(s, lambda i, j: (0,) * len(s))
    widths = (256, 256, 256, 256, 256, 256, 512, 128, 128)
    dtypes = (F32,) * 6 + (BF16,) * 3
    out_specs = [tok(w) for w in widths]
    out_shape = [jax.ShapeDtypeStruct((b, l, w), dt) for w, dt in zip(widths, dtypes)]
    for o in (2, 4, 5, 6):
        out_specs[o] = pl.BlockSpec((1, widths[o], tm), lambda i, j: (i, 0, j))
        out_shape[o] = jax.ShapeDtypeStruct((b, widths[o], l), dtypes[o])
    return pl.pallas_call(
        _inproj_kernel,
        grid=(b, l // tm),
        in_specs=[tok(d), vec, vec, full((d, n)),
                  pl.BlockSpec((tm, LANES), lambda i, j: (j, 0)),
                  pl.BlockSpec((tm, LANES), lambda i, j: (j, 0)),
                  full((1, LANES)), full((1, LANES)), full((LANES, LANES))],
        out_specs=out_specs,
        out_shape=out_shape,
        compiler_params=_cparams("arbitrary", "arbitrary"),
        name="inproj",
    )(x, sc, sh, w_bf, cos, sin, gq, gk, ones_bd)


def _attn_kernel(q_ref, k_ref, vt_ref, o_ref):
    tq = q_ref.shape[2]
    q4 = q_ref[0]
    qt = jnp.concatenate([q4[h * HEAD_DIM:(h + 1) * HEAD_DIM, :] for h in range(ATT_Q_PER_KV)], axis=1)
    lk = k_ref.shape[2]
    bounds = [(lo, min(lo + ATT_KV_CHUNK, lk)) for lo in range(0, lk, ATT_KV_CHUNK)]
    r = qt.shape[1] // ATT_STREAMS
    qs = [qt[:, i * r:(i + 1) * r] for i in range(ATT_STREAMS)]
    ms = [jnp.full((1, r), -jnp.inf, F32) for _ in qs]
    accs = [jnp.zeros((ATT_V_ROWS, r), F32) for _ in qs]
    nxt = [_dot(k_ref[0, 0, bounds[0][0]:bounds[0][1], :], q) for q in qs]
    for c, (lo, hi) in enumerate(bounds):
        for i, q in enumerate(qs):
            s = nxt[i]
            if c + 1 < len(bounds):
                nxt[i] = _dot(k_ref[0, 0, bounds[c + 1][0]:bounds[c + 1][1], :], q)
            m_new = jnp.maximum(ms[i], jnp.max(s, axis=0, keepdims=True))
            p = jnp.exp2(s - m_new)
            accs[i] = accs[i] * jnp.exp2(ms[i] - m_new) + _dot(vt_ref[0, 0, :, lo:hi], p.astype(BF16))
            ms[i] = m_new
    acc = jnp.concatenate(accs, axis=1)
    o = acc[:HEAD_DIM] / acc[HEAD_DIM:HEAD_DIM + 1]
    for h in range(ATT_Q_PER_KV):
        o_ref[0, h] = o[:, h * tq:(h + 1) * tq]


def _attention(q, k, vt, tq):
    b, wq, lq = q.shape
    hkv, lk = k.shape[1], k.shape[2]
    wg = ATT_Q_PER_KV * HEAD_DIM
    return pl.pallas_call(
        _attn_kernel,
        grid=(b, hkv, lq // tq),
        in_specs=[pl.BlockSpec((1, wg, tq), lambda i, g, j: (i, g, j)),
                  pl.BlockSpec((1, 1, lk, HEAD_DIM), lambda i, g, j: (i, g, 0, 0)),
                  pl.BlockSpec((1, 1, ATT_V_ROWS, lk), lambda i, g, j: (i, g, 0, 0))],
        out_specs=pl.BlockSpec((1, ATT_Q_PER_KV, HEAD_DIM, tq), lambda i, g, j: (i, g, 0, j)),
        out_shape=jax.ShapeDtypeStruct((b, wq // HEAD_DIM, HEAD_DIM, lq), F32),
        compiler_params=_cparams("arbitrary", "arbitrary", "arbitrary"),
        name="attention",
    )(q, k, vt)


def _ret_kernel(lg_ref, q_ref, k_ref, v_ref, g_ref, s0f_ref, s0b_ref, lgf_ref, lgb_ref, lgfc_ref, lgbc_ref, gn_ref,
                o_ref, sff_ref, sfb_ref, s_scr, sb_scr, dec_scr):
    i, p, j = pl.program_id(0), pl.program_id(1), pl.program_id(2)
    nc = pl.num_programs(2)
    c = k_ref.shape[1]
    w = k_ref.shape[2]
    pos = lax.broadcasted_iota(jnp.int32, (c, w), 0).astype(F32)
    qpos = lax.broadcasted_iota(jnp.int32, (w, c), 1).astype(F32)
    lgf = lgf_ref[...]
    lgb = lgb_ref[...]
    lgf_col = lgfc_ref[...]
    lgb_col = lgbc_ref[...]

    @pl.when((i == 0) & (p == 0) & (j == 0))
    def _():
        diff = (lax.broadcasted_iota(jnp.int32, (c, c), 1) - lax.broadcasted_iota(jnp.int32, (c, c), 0)).astype(F32)
        for h in range(RET_HEADS):
            dec_scr[h] = jnp.where(diff >= 0.0, jnp.exp(lg_ref[0, h] * jnp.maximum(diff, 0.0)),
                                   jnp.exp(lg_ref[1, h] * jnp.maximum(-diff, 0.0)))

    nb = k_ref.shape[0]

    def update_state(e, vt, kz, cdec):
        for h in range(RET_HEADS):
            lo, hi = h * HEAD_DIM, (h + 1) * HEAD_DIM
            s_scr[e, h] = cdec[:, lo:hi] * s_scr[e, h] + _dot(vt[lo:hi, :], kz[:, lo:hi])

    @pl.when(p == 0)
    def _():
        @pl.when(j == 0)
        def _():
            s_scr[...] = s0b_ref[...]

        sb_scr[nc - 1 - j] = s_scr[...]
        for e in range(nb):
            kz = (k_ref[e] * jnp.exp(lgb * pos)).astype(BF16)
            update_state(e, v_ref[e].astype(BF16), kz, jnp.exp(lgb * float(c)))

        @pl.when(j == nc - 1)
        def _():
            sfb_ref[...] = s_scr[...]

    @pl.when(p == 1)
    def _():
        @pl.when(j == 0)
        def _():
            s_scr[...] = s0f_ref[...]

        for e in range(nb):
            qt = q_ref[e]
            k = k_ref[e]
            qf = (qt * jnp.exp(lgf_col * (qpos + 1.0))).astype(BF16)
            qb = (qt * jnp.exp(lgb_col * (float(c) - qpos))).astype(BF16)
            kz = (k * jnp.exp(lgf * (float(c) - 1.0 - pos))).astype(BF16)
            qh = qt.astype(BF16)
            kh = k.astype(BF16)
            vt = v_ref[e].astype(BF16)
            gate = g_ref[e]
            for h in range(RET_HEADS):
                lo, hi = h * HEAD_DIM, (h + 1) * HEAD_DIM
                sc = _dot(kh[:, lo:hi], qh[lo:hi, :])
                y = (_dot(vt[lo:hi, :], (sc * dec_scr[h]).astype(BF16))
                     + _dot(s_scr[e, h].astype(BF16), qf[lo:hi, :])
                     + _dot(sb_scr[j, e, h].astype(BF16), qb[lo:hi, :]))
                mu = jnp.mean(y, axis=0, keepdims=True)
                yc = y - mu
                var = jnp.mean(yc * yc, axis=0, keepdims=True)
                gh = gate[lo:hi, :]
                o_ref[e, lo:hi, :] = yc * lax.rsqrt(var + EPS) * gn_ref[lo:hi, :] * (gh * jax.nn.sigmoid(gh))
            update_state(e, vt, kz, jnp.exp(lgf * float(c)))

        @pl.when(j == nc - 1)
        def _():
            sff_ref[...] = s_scr[...]


def _retention(qt, k, vt, gate_t, s0_f, s0_b, lg, lgf_row, lgb_row, gn_row):
    b, l, w = k.shape
    c = min(RET_CHUNK, l)
    nc = l // c
    nb = RET_BATCH if b % RET_BATCH == 0 else 1
    st = (nb, RET_HEADS, HEAD_DIM, HEAD_DIM)
    st_spec = pl.BlockSpec(st, lambda i, p, j: (i, 0, 0, 0))
    row = pl.BlockSpec((1, w), lambda i, p, j: (0, 0))
    col = pl.BlockSpec((w, 1), lambda i, p, j: (0, 0))
    k_both = pl.BlockSpec((nb, c, w), lambda i, p, j: (i, jnp.where(p == 0, nc - 1 - j, j), 0))
    t_both = pl.BlockSpec((nb, w, c), lambda i, p, j: (i, 0, jnp.where(p == 0, nc - 1 - j, j)))
    t_fwd = pl.BlockSpec((nb, w, c), lambda i, p, j: (i, 0, p * j))
    state_shape = jax.ShapeDtypeStruct((b,) + st[1:], F32)
    return pl.pallas_call(
        _ret_kernel,
        grid=(b // nb, 2, nc),
        in_specs=[pl.BlockSpec(memory_space=pltpu.SMEM), t_fwd, k_both, t_both, t_fwd,
                  st_spec, st_spec, row, row, col, col, col],
        out_specs=[t_fwd, st_spec, st_spec],
        out_shape=[jax.ShapeDtypeStruct((b, w, l), F32), state_shape, state_shape],
        scratch_shapes=[pltpu.VMEM(st, F32), pltpu.VMEM((nc,) + st, F32), pltpu.VMEM((RET_HEADS, c, c), F32)],
        compiler_params=_cparams("arbitrary", "arbitrary", "arbitrary"),
        name="retention",
    )(lg, qt, k, vt, gate_t, s0_f, s0_b, lgf_row, lgb_row, lgf_row.reshape(w, 1), lgb_row.reshape(w, 1),
      gn_row.reshape(w, 1))


def _layer_norm(x, g, b):
    mu = jnp.mean(x, axis=-1, keepdims=True)
    xc = x - mu
    var = jnp.mean(xc * xc, axis=-1, keepdims=True)
    return xc * lax.rsqrt(var + EPS) * g + b


def _store_rows_packed(ref, x):
    tm, d = x.shape
    bits = pltpu.bitcast(x.astype(BF16).astype(F32), jnp.uint32)
    word = (bits[:, :d // 2] >> 16) | bits[:, d // 2:]
    for s in range(PACKED_ROWS):
        ref[pl.ds(s, tm, stride=PACKED_ROWS), :] = word[:, s * LANES:(s + 1) * LANES]


def _load_rows_packed(ref, tm):
    word = jnp.concatenate([ref[pl.ds(s, tm, stride=PACKED_ROWS), :] for s in range(PACKED_ROWS)], axis=1)
    lo = pltpu.bitcast(word << 16, F32).astype(BF16)
    hi = pltpu.bitcast(word & jnp.uint32(0xFFFF0000), F32).astype(BF16)
    return jnp.concatenate([lo, hi], axis=1)


def _outproj_kernel(alpha, cu_ref, cup_ref, cun_ref, cb_ref, cw_ref, ret_ref, att_ref, x_ref, g1_ref,
                    w_ref, lng_ref, lnb_ref, sc_ref, sh_ref, wr_ref, br_ref, cnt0_ref, before_ref, *rest):
    x1_ref, h2_ref, ri_ref, rw_ref, cnt_ref, run_scr = rest[-6:]
    i = pl.program_id(0)
    j = pl.program_id(1)

    @pl.when((i == 0) & (j == 0))
    def _():
        run_scr[...] = cnt0_ref[...]

    t = cu_ref[0]
    tm = t.shape[0]
    prev_row = jnp.where(j == 0, 0.0, cup_ref[0][7:8, :])
    next_row = jnp.where(j == pl.num_programs(1) - 1, 0.0, cun_ref[0][0:1, :])
    row = lax.broadcasted_iota(jnp.int32, t.shape, 0)
    t_prev = jnp.where(row == 0, prev_row, pltpu.roll(t, 1, axis=0))
    t_next = jnp.where(row == tm - 1, next_row, pltpu.roll(t, tm - 1, axis=0))
    conv = cb_ref[0] * (t_prev * cw_ref[0] + t * cw_ref[1] + t_next * cw_ref[2])
    conv = conv.astype(BF16)
    wrh = wr_ref[:, :ROUTER_PAD]
    rb = tm // OUT_STREAMS
    lane = lax.broadcasted_iota(jnp.int32, (rb, ROUTER_PAD), 1)
    lane_f = lane.astype(F32)
    routed = []
    for r0 in range(0, tm, rb):
        att_t = att_ref[0, :, :, r0:r0 + rb].reshape(att_ref.shape[1] * HEAD_DIM, rb).astype(BF16)
        tn = (((0,), (0,)), ((), ()))
        y = (_dot(conv[r0:r0 + rb], w_ref[0:256, :])
             + lax.dot_general(ret_ref[0, :, r0:r0 + rb].astype(BF16), w_ref[256:512, :], tn,
                               preferred_element_type=F32)
             + lax.dot_general(att_t, w_ref[512:1024, :], tn, preferred_element_type=F32))
        x1 = _layer_norm(alpha * x_ref[0, r0:r0 + rb, :] + g1_ref[0] * y, lng_ref[...], lnb_ref[...])
        x1_ref[0, r0:r0 + rb, :] = x1
        h2 = x1 * sc_ref[0] + sh_ref[0]
        _store_rows_packed(h2_ref.at[pl.ds(r0 * PACKED_ROWS, rb * PACKED_ROWS)], h2)
        h_hi, h_lo = _split_bf16(h2)
        hw = _dot(h_hi, wr_ref[...])
        logits = hw[:, :ROUTER_PAD] + hw[:, ROUTER_PAD:] + _dot(h_lo, wrh) + br_ref[...]

        work = jnp.where(lane < N_EXPERTS, logits, -jnp.inf)
        vals, firsts, sels = [], [], []
        for _ in range(TOP_K):
            m = jnp.max(work, axis=-1, keepdims=True)
            first = jnp.min(jnp.where(work == m, lane_f, float(ROUTER_PAD)), axis=-1, keepdims=True)
            sel = lane_f == first
            vals.append(m)
            firsts.append(first)
            sels.append(sel)
            work = jnp.where(sel, -jnp.inf, work)
        exps = [jnp.exp(v - vals[0]) for v in vals]
        denom = exps[0]
        for e in exps[1:]:
            denom = denom + e
        cnt = jnp.zeros(logits.shape, F32)
        for sel in sels:
            cnt = cnt + jnp.where(sel, 1.0, 0.0)
        routed.append((firsts, sels, [e / denom for e in exps], cnt))

    cnt = jnp.concatenate([r[3] for r in routed], axis=0)
    base = _dot(before_ref[...], cnt.astype(BF16)) + run_scr[...]
    run_scr[...] = run_scr[...] + jnp.sum(cnt, axis=0, keepdims=True)
    cnt_ref[...] = run_scr[...]

    for blk, (firsts, sels, wts, _) in enumerate(routed):
        r0 = blk * rb
        ri = jnp.zeros((rb, ROUTER_PAD), F32)
        rw = jnp.zeros((rb, ROUTER_PAD), F32)
        for kk in range(TOP_K):
            rank = jnp.sum(jnp.where(sels[kk], base[r0:r0 + rb], 0.0), axis=-1, keepdims=True)
            ri = jnp.where(lane == kk, firsts[kk], ri)
            ri = jnp.where(lane == TOP_K + kk, rank, ri)
            rw = jnp.where(lane == kk, wts[kk], rw)
        ri_ref[0, r0:r0 + rb, :] = ri.astype(jnp.int32)
        rw_ref[0, r0:r0 + rb, :] = rw


def _outproj(alpha, cu, cb, cw, ret, att, x, g1, w_bf, lng, lnb, sc2, sh2, wr, br, cnt0, tm,
             h2_tokens=None, h2_offset=0, h2_prev=None):
    b, l, d = x.shape
    nj = l // tm
    h2_tokens = b * l if h2_tokens is None else h2_tokens
    assert h2_offset % tm == 0
    h2_blk = h2_offset // tm
    extra_specs, extra_args, aliases = [], [], {}
    if h2_prev is not None:
        extra_specs, extra_args, aliases = [pl.BlockSpec(memory_space=pl.ANY)], [h2_prev], {18: 1}
    tok = lambda w: pl.BlockSpec((1, tm, w), lambda i, j: (i, j, 0))
    vec = pl.BlockSpec((1, 1, d), lambda i, j: (i, 0, 0))
    full = lambda s: pl.BlockSpec(s, lambda i, j: (0,) * len(s))
    r8 = tm // 8
    nb8 = l // 8
    tri = jnp.arange(tm)
    before = (tri[None, :] < tri[:, None]).astype(BF16)
    return pl.pallas_call(
        functools.partial(_outproj_kernel, alpha),
        grid=(b, nj),
        in_specs=[tok(256),
                  pl.BlockSpec((1, 8, 256), lambda i, j: (i, jnp.maximum(j * r8 - 1, 0), 0)),
                  pl.BlockSpec((1, 8, 256), lambda i, j: (i, jnp.minimum((j + 1) * r8, nb8 - 1), 0)),
                  tok(256), full((3, 1, 256)),
                  pl.BlockSpec((1, ret.shape[1], tm), lambda i, j: (i, 0, j)),
                  pl.BlockSpec((1,) + att.shape[1:3] + (tm,), lambda i, j: (i, 0, 0, j)), tok(d), vec,
                  full((d, d)), full((1, d)), full((1, d)), vec, vec,
                  full((d, 2 * ROUTER_PAD)), full((1, ROUTER_PAD)), full((1, ROUTER_PAD)),
                  full((tm, tm))] + extra_specs,
        out_specs=[tok(d), pl.BlockSpec((tm * PACKED_ROWS, LANES), lambda i, j: (h2_blk + i * nj + j, 0)),
                   tok(ROUTER_PAD), tok(ROUTER_PAD), full((1, ROUTER_PAD))],
        out_shape=[jax.ShapeDtypeStruct((b, l, d), F32),
                   jax.ShapeDtypeStruct((h2_tokens * PACKED_ROWS, LANES), jnp.uint32),
                   jax.ShapeDtypeStruct((b, l, ROUTER_PAD), jnp.int32),
                   jax.ShapeDtypeStruct((b, l, ROUTER_PAD), F32),
                   jax.ShapeDtypeStruct((1, ROUTER_PAD), F32)],
        scratch_shapes=[pltpu.VMEM((1, ROUTER_PAD), F32)],
        input_output_aliases=aliases,
        compiler_params=_cparams("arbitrary", "arbitrary"),
        name="outproj_ln_router",
    )(cu, cu, cu, cb, cw, ret, att, x, g1, w_bf, lng, lnb, sc2, sh2, wr, br, cnt0, before, *extra_args)


def _ffn_kernel(be_ref, nv_ref, x_ref, wgu_ref, bgu_ref, wd_ref, bd_ref, o_ref, wgu_scr, wd_scr):
    j = pl.program_id(0)
    f = wd_ref.shape[2]
    tm = x_ref.shape[0] // PACKED_ROWS
    nv = nv_ref[j]

    @pl.when(nv > 0)
    def _():
        @pl.when((j == 0) | (be_ref[j] != be_ref[jnp.maximum(j - 1, 0)]))
        def _():
            wgu_scr[...] = wgu_ref[0, 0].astype(BF16)
            wd_scr[...] = wd_ref[0, 0].astype(BF16)

    def compute(rows):
        part = pl.ds(0, rows * PACKED_ROWS)
        x = _load_rows_packed(x_ref.at[part], rows)
        bgu = bgu_ref[0, 0]
        acts = []
        for lo in range(0, f, FFN_BLOCK):
            hi = lo + FFN_BLOCK
            gate = jnp.minimum(_dot(x, wgu_scr[:, lo:hi]) + bgu[:, lo:hi], SWIGLU_LIMIT)
            up = jnp.clip(_dot(x, wgu_scr[:, f + lo:f + hi]) + bgu[:, f + lo:f + hi], -SWIGLU_LIMIT, SWIGLU_LIMIT)
            acts.append(((up + 1.0) * (gate * jax.nn.sigmoid(SWIGLU_ALPHA * gate))).astype(BF16))
        act = jnp.concatenate(acts, axis=1)
        _store_rows_packed(o_ref.at[part], _dot(act, wd_scr[...]) + bd_ref[0, 0])

    step = tm // FFN_ROW_STEPS
    for k in range(1, FFN_ROW_STEPS + 1):
        @pl.when((nv > (k - 1) * step) & (nv <= k * step))
        def _():
            compute(k * step)


def _expert_ffn(layer, block_e, n_valid, xs, wgu, bgu, wd, bd, tm):
    n_rows = xs.shape[0] // PACKED_ROWS
    depth, ne, d, f2 = wgu.shape
    f = f2 // 2
    rows = pl.BlockSpec((tm * PACKED_ROWS, LANES), lambda j, be, nu: (j, 0))
    grid_spec = pltpu.PrefetchScalarGridSpec(
        num_scalar_prefetch=2,
        grid=(n_rows // tm,),
        in_specs=[rows,
                  pl.BlockSpec((1, 1, d, f2), lambda j, be, nu: (layer, be[j], 0, 0)),
                  pl.BlockSpec((1, 1, 1, f2), lambda j, be, nu: (layer, be[j], 0, 0)),
                  pl.BlockSpec((1, 1, f, d), lambda j, be, nu: (layer, be[j], 0, 0)),
                  pl.BlockSpec((1, 1, 1, d), lambda j, be, nu: (layer, be[j], 0, 0))],
        out_specs=rows,
        scratch_shapes=[pltpu.VMEM((d, f2), BF16), pltpu.VMEM((f, d), BF16)],
    )
    return pl.pallas_call(
        _ffn_kernel,
        grid_spec=grid_spec,
        out_shape=jax.ShapeDtypeStruct((n_rows * PACKED_ROWS, LANES), jnp.uint32),
        compiler_params=_cparams("arbitrary"),
        name="expert_ffn",
    )(block_e, n_valid, xs, wgu, bgu.reshape(depth, ne, 1, f2), wd, bd.reshape(depth, ne, 1, d))


def _sc_worker_base(per_worker):
    return (lax.axis_index("s") * SC_CORES + lax.axis_index("c")) * per_worker


def _sc_dispatch(rows, dest, n_out):
    t = rows.shape[0]
    kk = dest.shape[0] // t
    w = SC_WINDOW
    per_worker = t // (SC_CORES * SC_SUBCORES)
    assert per_worker * SC_CORES * SC_SUBCORES == t and per_worker % w == 0
    mesh = plsc.VectorSubcoreMesh(core_axis_name="c", subcore_axis_name="s")

    n_win = per_worker // w
    row_buf = pltpu.VMEM((w,) + rows.shape[1:], rows.dtype)

    @functools.partial(
        pl.kernel, mesh=mesh,
        out_type=jax.ShapeDtypeStruct((n_out,) + rows.shape[1:], rows.dtype),
        scratch_types=[pltpu.VMEM((w,), jnp.int32)] * (2 * kk) + [row_buf] * 2 + [pltpu.SemaphoreType.DMA] * 4)
    def scatter_rows(r_hbm, d_hbm, o_hbm, *scratch):
        idx_v = (scratch[:kk], scratch[kk:2 * kk])
        rows_v = scratch[2 * kk:2 * kk + 2]
        rsem = scratch[2 * kk + 2:2 * kk + 4]
        ssem = scratch[2 * kk + 4:2 * kk + 6]
        base = _sc_worker_base(per_worker)

        def read(win, slot):
            return pltpu.make_async_copy(r_hbm.at[pl.ds(base + win * w, w)], rows_v[slot], rsem[slot])

        def scatter(slot, s):
            return pltpu.make_async_copy(rows_v[slot], o_hbm.at[idx_v[slot][s]], ssem[slot])

        def start_read(win, slot):
            read(win, slot).start()
            for s in range(kk):
                pltpu.sync_copy(d_hbm.at[pl.ds(s * t + base + win * w, w)], idx_v[slot][s])

        start_read(0, 0)

        @pl.loop(0, (n_win + 1) // 2)
        def _(pair):
            for slot in range(2):
                win = 2 * pair + slot
                other = 1 - slot

                @pl.when(win < n_win)
                def _():
                    @pl.when(win >= 1)
                    def _():
                        for s in range(kk):
                            scatter(other, s).wait()

                    @pl.when(win + 1 < n_win)
                    def _():
                        start_read(win + 1, other)

                    read(win, slot).wait()
                    for s in range(kk):
                        scatter(slot, s).start()

        for s in range(kk):
            scatter((n_win - 1) % 2, s).wait()

    return scatter_rows(rows, dest)


def _sc_gather(table, idx):
    n = idx.shape[0]
    w = SC_WINDOW
    per_worker = n // (SC_CORES * SC_SUBCORES)
    assert per_worker * SC_CORES * SC_SUBCORES == n and per_worker % w == 0
    mesh = plsc.VectorSubcoreMesh(core_axis_name="c", subcore_axis_name="s")

    n_win = per_worker // w
    assert n_win % 2 == 0
    row_buf = pltpu.VMEM((w,) + table.shape[1:], table.dtype)

    @functools.partial(
        pl.kernel, mesh=mesh,
        out_type=jax.ShapeDtypeStruct((n,) + table.shape[1:], table.dtype),
        scratch_types=[pltpu.VMEM((w,), jnp.int32)] * 2 + [row_buf] * 2 + [pltpu.SemaphoreType.DMA] * 4)
    def gather_rows(t_hbm, i_hbm, o_hbm, idx0, idx1, rows0, rows1, gsem0, gsem1, wsem0, wsem1):
        idx_v, rows_v = (idx0, idx1), (rows0, rows1)
        gsem, wsem = (gsem0, gsem1), (wsem0, wsem1)
        base = _sc_worker_base(per_worker)

        def gather(slot):
            return pltpu.make_async_copy(t_hbm.at[idx_v[slot]], rows_v[slot], gsem[slot])

        def write(win, slot):
            return pltpu.make_async_copy(rows_v[slot], o_hbm.at[pl.ds(base + win * w, w)], wsem[slot])

        def start_gather(win, slot):
            pltpu.sync_copy(i_hbm.at[pl.ds(base + win * w, w)], idx_v[slot])
            gather(slot).start()

        start_gather(0, 0)

        @pl.loop(0, n_win // 2)
        def _(pair):
            for slot in range(2):
                win = 2 * pair + slot
                other = 1 - slot

                @pl.when(win >= 1)
                def _():
                    write(win - 1, other).wait()

                @pl.when(win + 1 < n_win)
                def _():
                    start_gather(win + 1, other)

                gather(slot).wait()
                write(win, slot).start()

        write(n_win - 1, 1).wait()

    return gather_rows(table, idx)


def _combine_kernel(alpha, g_ref, w_ref, x_ref, g2_ref, lng_ref, lnb_ref, *rest):
    o_ref = rest[-1]
    tm = x_ref.shape[1]
    w = w_ref[0]
    f = _load_rows_packed(g_ref.at[0], tm).astype(F32) * w[:, 0:1]
    for kk in range(1, TOP_K):
        f = f + _load_rows_packed(g_ref.at[kk], tm).astype(F32) * w[:, kk:kk + 1]
    o_ref[0] = _layer_norm(alpha * x_ref[0] + g2_ref[0] * f, lng_ref[...], lnb_ref[...])


def _combine(alpha, gathered, b0, nb, wts, x1, g2, lng, lnb, tm, prev=None):
    b, l, d = x1.shape
    nj = l // tm
    tok = pl.BlockSpec((1, tm, d), lambda i, j: (b0 + i, j, 0))
    full = pl.BlockSpec((1, d), lambda i, j: (0, 0))
    in_specs = [pl.BlockSpec((TOP_K, tm * PACKED_ROWS, LANES), lambda i, j: (0, i * nj + j, 0)),
                pl.BlockSpec((1, tm, ROUTER_PAD), lambda i, j: (b0 + i, j, 0)),
                tok, pl.BlockSpec((1, 1, d), lambda i, j: (b0 + i, 0, 0)), full, full]
    args = [gathered, wts, x1, g2, lng, lnb]
    aliases = {}
    if prev is not None:
        in_specs.append(pl.BlockSpec(memory_space=pl.ANY))
        args.append(prev)
        aliases = {len(args) - 1: 0}
    return pl.pallas_call(
        functools.partial(_combine_kernel, alpha),
        grid=(nb, nj),
        in_specs=in_specs,
        out_specs=tok,
        out_shape=jax.ShapeDtypeStruct((b, l, d), F32),
        input_output_aliases=aliases,
        compiler_params=_cparams("arbitrary", "arbitrary"),
        name="combine_ln",
    )(*args)


def _rope_tables(l):
    rows = l // GRID_W
    axis_dim = HEAD_DIM // 2
    inv_freq = ROPE_BASE ** (-jnp.arange(0, axis_dim, 2, dtype=F32) / axis_dim)
    row = jnp.repeat(jnp.arange(rows, dtype=F32), GRID_W)
    col = jnp.tile(jnp.arange(GRID_W, dtype=F32), rows)
    ang = jnp.stack([row[:, None] * inv_freq, col[:, None] * inv_freq], axis=1)
    cos, sin = jnp.cos(ang), jnp.sin(ang)
    cos64 = jnp.broadcast_to(cos[:, :, None, :], (l, 2, 2, HEAD_DIM // 4)).reshape(l, HEAD_DIM)
    sin64 = jnp.stack([-sin, sin], axis=2).reshape(l, HEAD_DIM)
    return jnp.tile(cos64, (1, LANES // HEAD_DIM)), jnp.tile(sin64, (1, LANES // HEAD_DIM))


def _route_tables(ri, counts, tm):
    t = ri.shape[0]
    top_e = ri[:, :TOP_K]
    rank = ri[:, TOP_K:2 * TOP_K]
    padded = (counts + tm - 1) // tm * tm
    pad_end = jnp.cumsum(padded)
    pad_start = pad_end - padded
    experts = jnp.arange(N_EXPERTS, dtype=jnp.int32)
    start = jnp.sum(jnp.where(top_e[:, :, None] == experts, pad_start, 0), axis=-1)
    dest = (start + rank).T.reshape(TOP_K * t)
    n_tiles = (t * TOP_K + N_EXPERTS * (tm - 1) + tm - 1) // tm
    tile_start = jnp.arange(n_tiles, dtype=jnp.int32) * tm
    block_e = jnp.minimum(jnp.sum((pad_end[None, :] <= tile_start[:, None]).astype(jnp.int32), axis=1), N_EXPERTS - 1)
    in_expert = jnp.where(block_e[:, None] == experts[None, :], (pad_start + counts)[None, :], 0).sum(axis=1)
    n_valid = jnp.clip(in_expert - tile_start, 0, tm)
    return dest, block_e.astype(jnp.int32), n_valid.astype(jnp.int32), n_tiles


def _layer_params(layer, d, w_in, conv_w, ret_decay_exp, ret_gn_g, q_norm_g, k_norm_g, w_out, ln_g, ln_b,
                  w_router, b_router):
    log_gamma = jnp.log1p(-jnp.exp2(-ret_decay_exp[layer].astype(F32)))
    lg_rows = jnp.repeat(log_gamma, HEAD_DIM, axis=1)
    wr_pad = jnp.zeros((d, ROUTER_PAD), F32).at[:, :N_EXPERTS].set(w_router[layer])
    wrh = wr_pad.astype(BF16)
    return dict(
        w_in=w_in[layer].astype(BF16), w_out=w_out[layer].astype(BF16),
        gq=jnp.tile(q_norm_g[layer], LANES // HEAD_DIM).reshape(1, LANES),
        gk=jnp.tile(k_norm_g[layer], LANES // HEAD_DIM).reshape(1, LANES),
        log_gamma=log_gamma, lgf_row=lg_rows[0:1], lgb_row=lg_rows[1:2], gn_row=ret_gn_g[layer].reshape(1, -1),
        cw=conv_w[layer].T.reshape(3, 1, -1),
        wr=jnp.concatenate([wrh, (wr_pad - wrh.astype(F32)).astype(BF16)], axis=1),
        br=jnp.zeros((1, ROUTER_PAD), F32).at[0, :N_EXPERTS].set(b_router[layer]),
        lng1=ln_g[layer, 0].reshape(1, d), lnb1=ln_b[layer, 0].reshape(1, d),
        lng2=ln_g[layer, 1].reshape(1, d), lnb2=ln_b[layer, 1].reshape(1, d))


def _layer(layer, last, alpha, x, ctx, m_lat, m_ctx, p, tables, experts):
    b, l, d = x.shape
    lc = ctx.shape[1]
    tm_in, tm_lat, tm_ctx, tm_moe, tq = 1024, 512, 256, 1024, 256
    assert l % tm_in == 0 and l % tm_lat == 0 and lc % tm_ctx == 0 and l % tq == 0
    cos_l, sin_l, cos_c, sin_c, ones_bd, zero_state = tables
    sh1, sc1, g1, sh2, sc2, g2 = (m_lat[:, :, i] for i in range(N_MOD))
    sh1c, sc1c, g1c, sh2c, sc2c, g2c = (m_ctx[:, :, i] for i in range(N_MOD))

    zc = _inproj(ctx, 1.0 + sc1c, sh1c, p["w_in"], cos_c, sin_c, p["gq"], p["gk"], ones_bd, tm_ctx)
    zl = _inproj(x, 1.0 + sc1, sh1, p["w_in"], cos_l, sin_l, p["gq"], p["gk"], ones_bd, tm_in)
    cu_c, cb_c, rq_c, rk_c, rv_c, rg_c, aq_c, ak_c, av_c = zc
    cu_l, cb_l, rq_l, rk_l, rv_l, rg_l, aq_l, ak_l, av_l = zl

    ret_args = (p["log_gamma"], p["lgf_row"], p["lgb_row"], p["gn_row"])
    ret_c, s_fwd, s_bwd = _retention(rq_c, rk_c, rv_c, rg_c, zero_state, zero_state, *ret_args)
    ret_l, _, _ = _retention(rq_l, rk_l, rv_l, rg_l, s_fwd, s_bwd, *ret_args)

    def kv_heads(a):
        return a.reshape(b, a.shape[1], -1, HEAD_DIM).transpose(0, 2, 1, 3)

    def v_heads(a):
        vt = a.reshape(b, a.shape[1], -1, HEAD_DIM).transpose(0, 2, 3, 1)
        ones = jnp.ones(vt.shape[:2] + (1, vt.shape[3]), BF16)
        pad = jnp.zeros(vt.shape[:2] + (ATT_V_ROWS - HEAD_DIM - 1, vt.shape[3]), BF16)
        return jnp.concatenate([vt, ones, pad], axis=2)

    k_all = kv_heads(jnp.concatenate([ak_c, ak_l], axis=1))
    v_all = v_heads(jnp.concatenate([av_c, av_l], axis=1))
    att_l = _attention(aq_l, k_all, v_all, tq)

    out_args = (p["w_out"], p["lng1"], p["lnb1"])
    rt_args = (p["wr"], p["br"])
    cnt0 = jnp.zeros((1, ROUTER_PAD), F32)
    n_c = 0 if last else b * lc
    n_tok = n_c + b * l
    h2_c = None
    if not last:
        att_c = _attention(aq_c, kv_heads(ak_c), v_heads(av_c), min(tq, lc))
        ctx1, h2_c, ri_c, rw_c, cnt0 = _outproj(alpha, cu_c, cb_c, p["cw"], ret_c, att_c, ctx, g1c, *out_args,
                                                1.0 + sc2c, sh2c, *rt_args, cnt0, tm_ctx, h2_tokens=n_tok)
    x1, h2, ri_l, rw_l, cnt = _outproj(alpha, cu_l, cb_l, p["cw"], ret_l, att_l, x, g1, *out_args,
                                       1.0 + sc2, sh2, *rt_args, cnt0, tm_lat,
                                       h2_tokens=n_tok, h2_offset=n_c, h2_prev=h2_c)
    if not last:
        ri = jnp.concatenate([ri_c.reshape(n_c, -1), ri_l.reshape(b * l, -1)], axis=0)
    else:
        ri = ri_l.reshape(b * l, -1)

    counts = cnt[0, :N_EXPERTS].astype(jnp.int32)
    dest, block_e, n_valid, n_tiles = _route_tables(ri, counts, tm_moe)
    xs = _sc_dispatch(h2.reshape(n_tok, PACKED_ROWS, LANES), dest, n_tiles * tm_moe)
    ys = _expert_ffn(layer, block_e, n_valid, xs.reshape(-1, LANES), *experts, tm_moe)
    ys = ys.reshape(-1, PACKED_ROWS, LANES)
    dest = dest.reshape(TOP_K, n_tok)

    def gather(lo, hi):
        rows = _sc_gather(ys, dest[:, lo:hi].reshape(-1))
        return rows.reshape(TOP_K, (hi - lo) * PACKED_ROWS, LANES)

    if not last:
        ctx = _combine(alpha, gather(0, n_c), 0, b, rw_c, ctx1, g2c, p["lng2"], p["lnb2"], tm_ctx)
    nb = b // COMBINE_CHUNKS if b % COMBINE_CHUNKS == 0 else b
    x_new = None
    for b0 in range(0, b, nb):
        g = gather(n_c + b0 * l, n_c + (b0 + nb) * l)
        x_new = _combine(alpha, g, b0, nb, rw_l, x1, g2, p["lng2"], p["lnb2"], tm_lat, prev=x_new)
    return x_new, ctx


def kernel(x, c, ctx, c_ctx, w_mod, b_mod, w_in, conv_w, ret_decay_exp, ret_gn_g, q_norm_g, k_norm_g, w_out,
           ln_g, ln_b, w_router, b_router, w_gate_up, b_gate_up, w_down, b_down):
    depth = w_mod.shape[0]
    alpha = (2.0 * depth) ** 0.25
    b, l, d = x.shape
    lc = ctx.shape[1]

    n_rows = (b + 1 + 7) // 8 * 8
    c_all = jnp.zeros((n_rows, d), F32).at[:b].set(c).at[b].set(c_ctx)
    mod = _modulation(c_all, w_mod, b_mod)

    cos_l, sin_l = _rope_tables(l)
    cos_c, sin_c = jnp.ones((lc, LANES), F32), jnp.zeros((lc, LANES), F32)
    lane_head = jnp.arange(LANES) // HEAD_DIM
    ones_bd = (lane_head[:, None] == lane_head[None, :]).astype(BF16)
    zero_state = jnp.zeros((b, RET_HEADS, HEAD_DIM, HEAD_DIM), F32)
    tables = (cos_l, sin_l, cos_c, sin_c, ones_bd, zero_state)
    experts = (w_gate_up, b_gate_up, w_down, b_down)

    for layer in range(depth):
        last = layer == depth - 1
        p = _layer_params(layer, d, w_in, conv_w, ret_decay_exp, ret_gn_g, q_norm_g, k_norm_g, w_out, ln_g, ln_b,
                          w_router, b_router)
        m_lat = mod[layer, :b].reshape(b, 1, N_MOD, d)
        m_ctx = jnp.broadcast_to(mod[layer, b].reshape(1, 1, N_MOD, d), (b, 1, N_MOD, d))
        x, ctx = _layer(layer, last, alpha, x, ctx, m_lat, m_ctx, p, tables, experts)
    return x
```

```python
import functools

import jax
import jax.numpy as jnp
from jax import lax
from jax.experimental import pallas as pl
from jax.experimental.pallas import tpu as pltpu
from jax.experimental.pallas import tpu_sc as plsc

F32 = jnp.float32
BF16 = jnp.bfloat16

HEAD_DIM = 64
GRID_W = 64
ROPE_BASE = 10000.0
N_EXPERTS = 32
TOP_K = 4
SWIGLU_ALPHA = 1.702
SWIGLU_LIMIT = 7.0
N_MOD = 6
EPS = 1e-6
RET_HEADS = 4
RET_CHUNK = 256
RET_BATCH = 8
ATT_Q_PER_KV = 4
ATT_KV_CHUNK = 256
ATT_Q_SCALE = HEAD_DIM ** -0.5 * 1.4426950408889634
ATT_STREAMS = 2
ATT_V_ROWS = 80
LANES = 128
ROUTER_PAD = 128
VMEM_LIMIT = 56 * 1024 * 1024
PACKED_ROWS = 4
FFN_BLOCK = 256
FFN_ROW_STEPS = 4
OUT_STREAMS = 2
COMBINE_CHUNKS = 8
SC_CORES = 2
SC_SUBCORES = 16
SC_WINDOW = 64


def _dot(a, b):
    return jnp.dot(a, b, preferred_element_type=F32)


def _split_bf16(x):
    hi = x.astype(BF16)
    lo = (x - hi.astype(F32)).astype(BF16)
    return hi, lo


def _cparams(*sem):
    return pltpu.CompilerParams(dimension_semantics=sem, vmem_limit_bytes=VMEM_LIMIT)


def _mod_kernel(c_ref, w_ref, b_ref, o_ref):
    c = c_ref[...]
    a = c * jax.nn.sigmoid(c)
    a_hi, a_lo = _split_bf16(a)
    w_hi, w_lo = _split_bf16(w_ref[0])
    o_ref[0] = _dot(a_hi, w_hi) + _dot(a_lo, w_hi) + _dot(a_hi, w_lo) + b_ref[0]


def _modulation(c_all, w_mod, b_mod):
    depth, d, n = w_mod.shape
    r = c_all.shape[0]
    tn = 1536
    return pl.pallas_call(
        _mod_kernel,
        grid=(depth, n // tn),
        in_specs=[
            pl.BlockSpec((r, d), lambda l, j: (0, 0)),
            pl.BlockSpec((1, d, tn), lambda l, j: (l, 0, j)),
            pl.BlockSpec((1, 1, tn), lambda l, j: (l, 0, j)),
        ],
        out_specs=pl.BlockSpec((1, r, tn), lambda l, j: (l, 0, j)),
        out_shape=jax.ShapeDtypeStruct((depth, r, n), F32),
        compiler_params=_cparams("arbitrary", "arbitrary"),
        name="modulation",
    )(c_all, w_mod, b_mod.reshape(depth, 1, n))


def _inproj_kernel(x_ref, sc_ref, sh_ref, w_ref, cos_ref, sin_ref, gq_ref, gk_ref, ones_ref,
                   cu_ref, cb_ref, rq_ref, rk_ref, rv_ref, rg_ref, aq_ref, ak_ref, av_ref):
    h = (x_ref[0] * sc_ref[0] + sh_ref[0]).astype(BF16)
    cos = cos_ref[...]
    sin = sin_ref[...]
    ones = ones_ref[...]
    lane = lax.broadcasted_iota(jnp.int32, cos.shape, 1)
    first_half = (lane & 31) < 16

    def rope(xs):
        nxt = pltpu.roll(xs, LANES - 16, axis=1)
        prv = pltpu.roll(xs, 16, axis=1)
        return xs * cos + jnp.where(first_half, nxt, prv) * sin

    def rms(xs, g):
        s_hi, s_lo = _split_bf16(xs * xs)
        ssq = _dot(s_hi, ones) + _dot(s_lo, ones)
        return xs * lax.rsqrt(ssq * (1.0 / HEAD_DIM) + EPS) * g

    za = _dot(h, w_ref[:, 1792:2560])
    gq = gq_ref[...]
    for j in range(4):
        lo, hi = j * LANES, (j + 1) * LANES
        q = rope(rms(za[:, lo:hi], gq)) * ATT_Q_SCALE
        aq_ref[0, lo:hi, :] = q.T.astype(BF16)
    ak_ref[0] = rope(rms(za[:, 512:640], gk_ref[...])).astype(BF16)
    av_ref[0] = za[:, 640:768].astype(BF16)
    zr = _dot(h, w_ref[:, 768:1792])
    for j in range(2):
        lo, hi = j * LANES, (j + 1) * LANES
        rq_ref[0, lo:hi, :] = rope(zr[:, lo:hi]).T
        rk_ref[0, :, lo:hi] = rope(zr[:, 256 + lo:256 + hi] * (HEAD_DIM ** -0.5))
        rv_ref[0, lo:hi, :] = zr[:, 512 + lo:512 + hi].T
        rg_ref[0, lo:hi, :] = zr[:, 768 + lo:768 + hi].T
    zc = _dot(h, w_ref[:, 0:768])
    cu_ref[0] = zc[:, 512:768] * zc[:, 0:256]
    cb_ref[0] = zc[:, 256:512]


def _inproj(x, sc, sh, w_bf, cos, sin, gq, gk, ones_bd, tm):
    b, l, d = x.shape
    n = w_bf.shape[1]
    tok = lambda w: pl.BlockSpec((1, tm, w), lambda i, j: (i, j, 0))
    vec = pl.BlockSpec((1, 1, d), lambda i, j: (i, 0, 0))
    full = lambda s: pl.BlockSpec(s, lambda i, j: (0,) * len(s))
    widths = (256, 256, 256, 256, 256, 256, 512, 128, 128)
    dtypes = (F32,) * 6 + (BF16,) * 3
    out_specs = [tok(w) for w in widths]
    out_shape = [jax.ShapeDtypeStruct((b, l, w), dt) for w, dt in zip(widths, dtypes)]
    for o in (2, 4, 5, 6):
        out_specs[o] = pl.BlockSpec((1, widths[o], tm), lambda i, j: (i, 0, j))
        out_shape[o] = jax.ShapeDtypeStruct((b, widths[o], l), dtypes[o])
    return pl.pallas_call(
        _inproj_kernel,
        grid=(b, l // tm),
        in_specs=[tok(d), vec, vec, full((d, n)),
                  pl.BlockSpec((tm, LANES), lambda i, j: (j, 0)),
                  pl.BlockSpec((tm, LANES), lambda i, j: (j, 0)),
                  full((1, LANES)), full((1, LANES)), full((LANES, LANES))],
        out_specs=out_specs,
        out_shape=out_shape,
        compiler_params=_cparams("arbitrary", "arbitrary"),
        name="inproj",
    )(x, sc, sh, w_bf, cos, sin, gq, gk, ones_bd)


def _attn_kernel(q_ref, k_ref, vt_ref, o_ref):
    tq = q_ref.shape[2]
    q4 = q_ref[0]
    qt = jnp.concatenate([q4[h * HEAD_DIM:(h + 1) * HEAD_DIM, :] for h in range(ATT_Q_PER_KV)], axis=1)
    lk = k_ref.shape[2]
    bounds = [(lo, min(lo + ATT_KV_CHUNK, lk)) for lo in range(0, lk, ATT_KV_CHUNK)]
    r = qt.shape[1] // ATT_STREAMS
    qs = [qt[:, i * r:(i + 1) * r] for i in range(ATT_STREAMS)]
    ms = [jnp.full((1, r), -jnp.inf, F32) for _ in qs]
    accs = [jnp.zeros((ATT_V_ROWS, r), F32) for _ in qs]
    nxt = [_dot(k_ref[0, 0, bounds[0][0]:bounds[0][1], :], q) for q in qs]
    for c, (lo, hi) in enumerate(bounds):
        for i, q in enumerate(qs):
            s = nxt[i]
            if c + 1 < len(bounds):
                nxt[i] = _dot(k_ref[0, 0, bounds[c + 1][0]:bounds[c + 1][1], :], q)
            m_new = jnp.maximum(ms[i], jnp.max(s, axis=0, keepdims=True))
            p = jnp.exp2(s - m_new)
            accs[i] = accs[i] * jnp.exp2(ms[i] - m_new) + _dot(vt_ref[0, 0, :, lo:hi], p.astype(BF16))
            ms[i] = m_new
    acc = jnp.concatenate(accs, axis=1)
    o = acc[:HEAD_DIM] / acc[HEAD_DIM:HEAD_DIM + 1]
    for h in range(ATT_Q_PER_KV):
        o_ref[0, h] = o[:, h * tq:(h + 1) * tq]


def _attention(q, k, vt, tq):
    b, wq, lq = q.shape
    hkv, lk = k.shape[1], k.shape[2]
    wg = ATT_Q_PER_KV * HEAD_DIM
    return pl.pallas_call(
        _attn_kernel,
        grid=(b, hkv, lq // tq),
        in_specs=[pl.BlockSpec((1, wg, tq), lambda i, g, j: (i, g, j)),
                  pl.BlockSpec((1, 1, lk, HEAD_DIM), lambda i, g, j: (i, g, 0, 0)),
                  pl.BlockSpec((1, 1, ATT_V_ROWS, lk), lambda i, g, j: (i, g, 0, 0))],
        out_specs=pl.BlockSpec((1, ATT_Q_PER_KV, HEAD_DIM, tq), lambda i, g, j: (i, g, 0, j)),
        out_shape=jax.ShapeDtypeStruct((b, wq // HEAD_DIM, HEAD_DIM, lq), F32),
        compiler_params=_cparams("arbitrary", "arbitrary", "arbitrary"),
        name="attention",
    )(q, k, vt)


def _ret_kernel(lg_ref, q_ref, k_ref, v_ref, g_ref, s0f_ref, s0b_ref, lgf_ref, lgb_ref, lgfc_ref, lgbc_ref, gn_ref,
                o_ref, sff_ref, sfb_ref, s_scr, sb_scr, dec_scr):
    i, p, j = pl.program_id(0), pl.program_id(1), pl.program_id(2)
    nc = pl.num_programs(2)
    c = k_ref.shape[1]
    w = k_ref.shape[2]
    pos = lax.broadcasted_iota(jnp.int32, (c, w), 0).astype(F32)
    qpos = lax.broadcasted_iota(jnp.int32, (w, c), 1).astype(F32)
    lgf = lgf_ref[...]
    lgb = lgb_ref[...]
    lgf_col = lgfc_ref[...]
    lgb_col = lgbc_ref[...]

    @pl.when((i == 0) & (p == 0) & (j == 0))
    def _():
        diff = (lax.broadcasted_iota(jnp.int32, (c, c), 1) - lax.broadcasted_iota(jnp.int32, (c, c), 0)).astype(F32)
        for h in range(RET_HEADS):
            dec_scr[h] = jnp.where(diff >= 0.0, jnp.exp(lg_ref[0, h] * jnp.maximum(diff, 0.0)),
                                   jnp.exp(lg_ref[1, h] * jnp.maximum(-diff, 0.0)))

    nb = k_ref.shape[0]

    def update_state(e, vt, kz, cdec):
        for h in range(RET_HEADS):
            lo, hi = h * HEAD_DIM, (h + 1) * HEAD_DIM
            s_scr[e, h] = cdec[:, lo:hi] * s_scr[e, h] + _dot(vt[lo:hi, :], kz[:, lo:hi])

    @pl.when(p == 0)
    def _():
        @pl.when(j == 0)
        def _():
            s_scr[...] = s0b_ref[...]

        sb_scr[nc - 1 - j] = s_scr[...]
        for e in range(nb):
            kz = (k_ref[e] * jnp.exp(lgb * pos)).astype(BF16)
            update_state(e, v_ref[e].astype(BF16), kz, jnp.exp(lgb * float(c)))

        @pl.when(j == nc - 1)
        def _():
            sfb_ref[...] = s_scr[...]

    @pl.when(p == 1)
    def _():
        @pl.when(j == 0)
        def _():
            s_scr[...] = s0f_ref[...]

        for e in range(nb):
            qt = q_ref[e]
            k = k_ref[e]
            qf = (qt * jnp.exp(lgf_col * (qpos + 1.0))).astype(BF16)
            qb = (qt * jnp.exp(lgb_col * (float(c) - qpos))).astype(BF16)
            kz = (k * jnp.exp(lgf * (float(c) - 1.0 - pos))).astype(BF16)
            qh = qt.astype(BF16)
            kh = k.astype(BF16)
            vt = v_ref[e].astype(BF16)
            gate = g_ref[e]
            for h in range(RET_HEADS):
                lo, hi = h * HEAD_DIM, (h + 1) * HEAD_DIM
                sc = _dot(kh[:, lo:hi], qh[lo:hi, :])
                y = (_dot(vt[lo:hi, :], (sc * dec_scr[h]).astype(BF16))
                     + _dot(s_scr[e, h].astype(BF16), qf[lo:hi, :])
                     + _dot(sb_scr[j, e, h].astype(BF16), qb[lo:hi, :]))
                mu = jnp.mean(y, axis=0, keepdims=True)
                yc = y - mu
                var = jnp.mean(yc * yc, axis=0, keepdims=True)
                gh = gate[lo:hi, :]
                o_ref[e, lo:hi, :] = yc * lax.rsqrt(var + EPS) * gn_ref[lo:hi, :] * (gh * jax.nn.sigmoid(gh))
            update_state(e, vt, kz, jnp.exp(lgf * float(c)))

        @pl.when(j == nc - 1)
        def _():
            sff_ref[...] = s_scr[...]


def _retention(qt, k, vt, gate_t, s0_f, s0_b, lg, lgf_row, lgb_row, gn_row):
    b, l, w = k.shape
    c = min(RET_CHUNK, l)
    nc = l // c
    nb = RET_BATCH if b % RET_BATCH == 0 else 1
    st = (nb, RET_HEADS, HEAD_DIM, HEAD_DIM)
    st_spec = pl.BlockSpec(st, lambda i, p, j: (i, 0, 0, 0))
    row = pl.BlockSpec((1, w), lambda i, p, j: (0, 0))
    col = pl.BlockSpec((w, 1), lambda i, p, j: (0, 0))
    k_both = pl.BlockSpec((nb, c, w), lambda i, p, j: (i, jnp.where(p == 0, nc - 1 - j, j), 0))
    t_both = pl.BlockSpec((nb, w, c), lambda i, p, j: (i, 0, jnp.where(p == 0, nc - 1 - j, j)))
    t_fwd = pl.BlockSpec((nb, w, c), lambda i, p, j: (i, 0, p * j))
    state_shape = jax.ShapeDtypeStruct((b,) + st[1:], F32)
    return pl.pallas_call(
        _ret_kernel,
        grid=(b // nb, 2, nc),
        in_specs=[pl.BlockSpec(memory_space=pltpu.SMEM), t_fwd, k_both, t_both, t_fwd,
                  st_spec, st_spec, row, row, col, col, col],
        out_specs=[t_fwd, st_spec, st_spec],
        out_shape=[jax.ShapeDtypeStruct((b, w, l), F32), state_shape, state_shape],
        scratch_shapes=[pltpu.VMEM(st, F32), pltpu.VMEM((nc,) + st, F32), pltpu.VMEM((RET_HEADS, c, c), F32)],
        compiler_params=_cparams("arbitrary", "arbitrary", "arbitrary"),
        name="retention",
    )(lg, qt, k, vt, gate_t, s0_f, s0_b, lgf_row, lgb_row, lgf_row.reshape(w, 1), lgb_row.reshape(w, 1),
      gn_row.reshape(w, 1))


def _layer_norm(x, g, b):
    mu = jnp.mean(x, axis=-1, keepdims=True)
    xc = x - mu
    var = jnp.mean(xc * xc, axis=-1, keepdims=True)
    return xc * lax.rsqrt(var + EPS) * g + b


def _store_rows_packed(ref, x):
    tm, d = x.shape
    bits = pltpu.bitcast(x.astype(BF16).astype(F32), jnp.uint32)
    word = (bits[:, :d // 2] >> 16) | bits[:, d // 2:]
    for s in range(PACKED_ROWS):
        ref[pl.ds(s, tm, stride=PACKED_ROWS), :] = word[:, s * LANES:(s + 1) * LANES]


def _load_rows_packed(ref, tm):
    word = jnp.concatenate([ref[pl.ds(s, tm, stride=PACKED_ROWS), :] for s in range(PACKED_ROWS)], axis=1)
    lo = pltpu.bitcast(word << 16, F32).astype(BF16)
    hi = pltpu.bitcast(word & jnp.uint32(0xFFFF0000), F32).astype(BF16)
    return jnp.concatenate([lo, hi], axis=1)


def _outproj_kernel(alpha, cu_ref, cup_ref, cun_ref, cb_ref, cw_ref, ret_ref, att_ref, x_ref, g1_ref,
                    w_ref, lng_ref, lnb_ref, sc_ref, sh_ref, wr_ref, br_ref, cnt0_ref, before_ref, *rest):
    x1_ref, h2_ref, ri_ref, rw_ref, cnt_ref, run_scr = rest[-6:]
    i = pl.program_id(0)
    j = pl.program_id(1)

    @pl.when((i == 0) & (j == 0))
    def _():
        run_scr[...] = cnt0_ref[...]

    t = cu_ref[0]
    tm = t.shape[0]
    prev_row = jnp.where(j == 0, 0.0, cup_ref[0][7:8, :])
    next_row = jnp.where(j == pl.num_programs(1) - 1, 0.0, cun_ref[0][0:1, :])
    row = lax.broadcasted_iota(jnp.int32, t.shape, 0)
    t_prev = jnp.where(row == 0, prev_row, pltpu.roll(t, 1, axis=0))
    t_next = jnp.where(row == tm - 1, next_row, pltpu.roll(t, tm - 1, axis=0))
    conv = cb_ref[0] * (t_prev * cw_ref[0] + t * cw_ref[1] + t_next * cw_ref[2])
    conv = conv.astype(BF16)
    wrh = wr_ref[:, :ROUTER_PAD]
    rb = tm // OUT_STREAMS
    lane = lax.broadcasted_iota(jnp.int32, (rb, ROUTER_PAD), 1)
    lane_f = lane.astype(F32)
    routed = []
    for r0 in range(0, tm, rb):
        att_t = att_ref[0, :, :, r0:r0 + rb].reshape(att_ref.shape[1] * HEAD_DIM, rb).astype(BF16)
        tn = (((0,), (0,)), ((), ()))
        y = (_dot(conv[r0:r0 + rb], w_ref[0:256, :])
             + lax.dot_general(ret_ref[0, :, r0:r0 + rb].astype(BF16), w_ref[256:512, :], tn,
                               preferred_element_type=F32)
             + lax.dot_general(att_t, w_ref[512:1024, :], tn, preferred_element_type=F32))
        x1 = _layer_norm(alpha * x_ref[0, r0:r0 + rb, :] + g1_ref[0] * y, lng_ref[...], lnb_ref[...])
        x1_ref[0, r0:r0 + rb, :] = x1
        h2 = x1 * sc_ref[0] + sh_ref[0]
        _store_rows_packed(h2_ref.at[pl.ds(r0 * PACKED_ROWS, rb * PACKED_ROWS)], h2)
        h_hi, h_lo = _split_bf16(h2)
        hw = _dot(h_hi, wr_ref[...])
        logits = hw[:, :ROUTER_PAD] + hw[:, ROUTER_PAD:] + _dot(h_lo, wrh) + br_ref[...]

        work = jnp.where(lane < N_EXPERTS, logits, -jnp.inf)
        vals, firsts, sels = [], [], []
        for _ in range(TOP_K):
            m = jnp.max(work, axis=-1, keepdims=True)
            first = jnp.min(jnp.where(work == m, lane_f, float(ROUTER_PAD)), axis=-1, keepdims=True)
            sel = lane_f == first
            vals.append(m)
            firsts.append(first)
            sels.append(sel)
            work = jnp.where(sel, -jnp.inf, work)
        exps = [jnp.exp(v - vals[0]) for v in vals]
        denom = exps[0]
        for e in exps[1:]:
            denom = denom + e
        cnt = jnp.zeros(logits.shape, F32)
        for sel in sels:
            cnt = cnt + jnp.where(sel, 1.0, 0.0)
        routed.append((firsts, sels, [e / denom for e in exps], cnt))

    cnt = jnp.concatenate([r[3] for r in routed], axis=0)
    base = _dot(before_ref[...], cnt.astype(BF16)) + run_scr[...]
    run_scr[...] = run_scr[...] + jnp.sum(cnt, axis=0, keepdims=True)
    cnt_ref[...] = run_scr[...]

    for blk, (firsts, sels, wts, _) in enumerate(routed):
        r0 = blk * rb
        ri = jnp.zeros((rb, ROUTER_PAD), F32)
        rw = jnp.zeros((rb, ROUTER_PAD), F32)
        for kk in range(TOP_K):
            rank = jnp.sum(jnp.where(sels[kk], base[r0:r0 + rb], 0.0), axis=-1, keepdims=True)
            ri = jnp.where(lane == kk, firsts[kk], ri)
            ri = jnp.where(lane == TOP_K + kk, rank, ri)
            rw = jnp.where(lane == kk, wts[kk], rw)
        ri_ref[0, r0:r0 + rb, :] = ri.astype(jnp.int32)
        rw_ref[0, r0:r0 + rb, :] = rw


def _outproj(alpha, cu, cb, cw, ret, att, x, g1, w_bf, lng, lnb, sc2, sh2, wr, br, cnt0, tm,
             h2_tokens=None, h2_offset=0, h2_prev=None):
    b, l, d = x.shape
    nj = l // tm
    h2_tokens = b * l if h2_tokens is None else h2_tokens
    assert h2_offset % tm == 0
    h2_blk = h2_offset // tm
    extra_specs, extra_args, aliases = [], [], {}
    if h2_prev is not None:
        extra_specs, extra_args, aliases = [pl.BlockSpec(memory_space=pl.ANY)], [h2_prev], {18: 1}
    tok = lambda w: pl.BlockSpec((1, tm, w), lambda i, j: (i, j, 0))
    vec = pl.BlockSpec((1, 1, d), lambda i, j: (i, 0, 0))
    full = lambda s: pl.BlockSpec(s, lambda i, j: (0,) * len(s))
    r8 = tm // 8
    nb8 = l // 8
    tri = jnp.arange(tm)
    before = (tri[None, :] < tri[:, None]).astype(BF16)
    return pl.pallas_call(
        functools.partial(_outproj_kernel, alpha),
        grid=(b, nj),
        in_specs=[tok(256),
                  pl.BlockSpec((1, 8, 256), lambda i, j: (i, jnp.maximum(j * r8 - 1, 0), 0)),
                  pl.BlockSpec((1, 8, 256), lambda i, j: (i, jnp.minimum((j + 1) * r8, nb8 - 1), 0)),
                  tok(256), full((3, 1, 256)),
                  pl.BlockSpec((1, ret.shape[1], tm), lambda i, j: (i, 0, j)),
                  pl.BlockSpec((1,) + att.shape[1:3] + (tm,), lambda i, j: (i, 0, 0, j)), tok(d), vec,
                  full((d, d)), full((1, d)), full((1, d)), vec, vec,
                  full((d, 2 * ROUTER_PAD)), full((1, ROUTER_PAD)), full((1, ROUTER_PAD)),
                  full((tm, tm))] + extra_specs,
        out_specs=[tok(d), pl.BlockSpec((tm * PACKED_ROWS, LANES), lambda i, j: (h2_blk + i * nj + j, 0)),
                   tok(ROUTER_PAD), tok(ROUTER_PAD), full((1, ROUTER_PAD))],
        out_shape=[jax.ShapeDtypeStruct((b, l, d), F32),
                   jax.ShapeDtypeStruct((h2_tokens * PACKED_ROWS, LANES), jnp.uint32),
                   jax.ShapeDtypeStruct((b, l, ROUTER_PAD), jnp.int32),
                   jax.ShapeDtypeStruct((b, l, ROUTER_PAD), F32),
                   jax.ShapeDtypeStruct((1, ROUTER_PAD), F32)],
        scratch_shapes=[pltpu.VMEM((1, ROUTER_PAD), F32)],
        input_output_aliases=aliases,
        compiler_params=_cparams("arbitrary", "arbitrary"),
        name="outproj_ln_router",
    )(cu, cu, cu, cb, cw, ret, att, x, g1, w_bf, lng, lnb, sc2, sh2, wr, br, cnt0, before, *extra_args)


def _ffn_kernel(be_ref, nv_ref, x_ref, wgu_ref, bgu_ref, wd_ref, bd_ref, o_ref, wgu_scr, wd_scr):
    j = pl.program_id(0)
    f = wd_ref.shape[2]
    tm = x_ref.shape[0] // PACKED_ROWS
    nv = nv_ref[j]

    @pl.when(nv > 0)
    def _():
        @pl.when((j == 0) | (be_ref[j] != be_ref[jnp.maximum(j - 1, 0)]))
        def _():
            wgu_scr[...] = wgu_ref[0, 0].astype(BF16)
            wd_scr[...] = wd_ref[0, 0].astype(BF16)

    def compute(rows):
        part = pl.ds(0, rows * PACKED_ROWS)
        x = _load_rows_packed(x_ref.at[part], rows)
        bgu = bgu_ref[0, 0]
        acts = []
        for lo in range(0, f, FFN_BLOCK):
            hi = lo + FFN_BLOCK
            gate = jnp.minimum(_dot(x, wgu_scr[:, lo:hi]) + bgu[:, lo:hi], SWIGLU_LIMIT)
            up = jnp.clip(_dot(x, wgu_scr[:, f + lo:f + hi]) + bgu[:, f + lo:f + hi], -SWIGLU_LIMIT, SWIGLU_LIMIT)
            acts.append(((up + 1.0) * (gate * jax.nn.sigmoid(SWIGLU_ALPHA * gate))).astype(BF16))
        act = jnp.concatenate(acts, axis=1)
        _store_rows_packed(o_ref.at[part], _dot(act, wd_scr[...]) + bd_ref[0, 0])

    step = tm // FFN_ROW_STEPS
    for k in range(1, FFN_ROW_STEPS + 1):
        @pl.when((nv > (k - 1) * step) & (nv <= k * step))
        def _():
            compute(k * step)


def _expert_ffn(layer, block_e, n_valid, xs, wgu, bgu, wd, bd, tm):
    n_rows = xs.shape[0] // PACKED_ROWS
    depth, ne, d, f2 = wgu.shape
    f = f2 // 2
    rows = pl.BlockSpec((tm * PACKED_ROWS, LANES), lambda j, be, nu: (j, 0))
    grid_spec = pltpu.PrefetchScalarGridSpec(
        num_scalar_prefetch=2,
        grid=(n_rows // tm,),
        in_specs=[rows,
                  pl.BlockSpec((1, 1, d, f2), lambda j, be, nu: (layer, be[j], 0, 0)),
                  pl.BlockSpec((1, 1, 1, f2), lambda j, be, nu: (layer, be[j], 0, 0)),
                  pl.BlockSpec((1, 1, f, d), lambda j, be, nu: (layer, be[j], 0, 0)),
                  pl.BlockSpec((1, 1, 1, d), lambda j, be, nu: (layer, be[j], 0, 0))],
        out_specs=rows,
        scratch_shapes=[pltpu.VMEM((d, f2), BF16), pltpu.VMEM((f, d), BF16)],
    )
    return pl.pallas_call(
        _ffn_kernel,
        grid_spec=grid_spec,
        out_shape=jax.ShapeDtypeStruct((n_rows * PACKED_ROWS, LANES), jnp.uint32),
        compiler_params=_cparams("arbitrary"),
        name="expert_ffn",
    )(block_e, n_valid, xs, wgu, bgu.reshape(depth, ne, 1, f2), wd, bd.reshape(depth, ne, 1, d))


def _sc_worker_base(per_worker):
    return (lax.axis_index("s") * SC_CORES + lax.axis_index("c")) * per_worker


def _sc_dispatch(rows, dest, n_out):
    t = rows.shape[0]
    kk = dest.shape[0] // t
    w = SC_WINDOW
    per_worker = t // (SC_CORES * SC_SUBCORES)
    assert per_worker * SC_CORES * SC_SUBCORES == t and per_worker % w == 0
    mesh = plsc.VectorSubcoreMesh(core_axis_name="c", subcore_axis_name="s")

    n_win = per_worker // w
    row_buf = pltpu.VMEM((w,) + rows.shape[1:], rows.dtype)

    @functools.partial(
        pl.kernel, mesh=mesh,
        out_type=jax.ShapeDtypeStruct((n_out,) + rows.shape[1:], rows.dtype),
        scratch_types=[pltpu.VMEM((w,), jnp.int32)] * (2 * kk) + [row_buf] * 2 + [pltpu.SemaphoreType.DMA] * 4)
    def scatter_rows(r_hbm, d_hbm, o_hbm, *scratch):
        idx_v = (scratch[:kk], scratch[kk:2 * kk])
        rows_v = scratch[2 * kk:2 * kk + 2]
        rsem = scratch[2 * kk + 2:2 * kk + 4]
        ssem = scratch[2 * kk + 4:2 * kk + 6]
        base = _sc_worker_base(per_worker)

        def read(win, slot):
            return pltpu.make_async_copy(r_hbm.at[pl.ds(base + win * w, w)], rows_v[slot], rsem[slot])

        def scatter(slot, s):
            return pltpu.make_async_copy(rows_v[slot], o_hbm.at[idx_v[slot][s]], ssem[slot])

        def start_read(win, slot):
            read(win, slot).start()
            for s in range(kk):
                pltpu.sync_copy(d_hbm.at[pl.ds(s * t + base + win * w, w)], idx_v[slot][s])

        start_read(0, 0)

        @pl.loop(0, (n_win + 1) // 2)
        def _(pair):
            for slot in range(2):
                win = 2 * pair + slot
                other = 1 - slot

                @pl.when(win < n_win)
                def _():
                    @pl.when(win >= 1)
                    def _():
                        for s in range(kk):
                            scatter(other, s).wait()

                    @pl.when(win + 1 < n_win)
                    def _():
                        start_read(win + 1, other)

                    read(win, slot).wait()
                    for s in range(kk):
                        scatter(slot, s).start()

        for s in range(kk):
            scatter((n_win - 1) % 2, s).wait()

    return scatter_rows(rows, dest)


def _sc_gather(table, idx):
    n = idx.shape[0]
    w = SC_WINDOW
    per_worker = n // (SC_CORES * SC_SUBCORES)
    assert per_worker * SC_CORES * SC_SUBCORES == n and per_worker % w == 0
    mesh = plsc.VectorSubcoreMesh(core_axis_name="c", subcore_axis_name="s")

    n_win = per_worker // w
    assert n_win % 2 == 0
    row_buf = pltpu.VMEM((w,) + table.shape[1:], table.dtype)

    @functools.partial(
        pl.kernel, mesh=mesh,
        out_type=jax.ShapeDtypeStruct((n,) + table.shape[1:], table.dtype),
        scratch_types=[pltpu.VMEM((w,), jnp.int32)] * 2 + [row_buf] * 2 + [pltpu.SemaphoreType.DMA] * 4)
    def gather_rows(t_hbm, i_hbm, o_hbm, idx0, idx1, rows0, rows1, gsem0, gsem1, wsem0, wsem1):
        idx_v, rows_v = (idx0, idx1), (rows0, rows1)
        gsem, wsem = (gsem0, gsem1), (wsem0, wsem1)
        base = _sc_worker_base(per_worker)

        def gather(slot):
            return pltpu.make_async_copy(t_hbm.at[idx_v[slot]], rows_v[slot], gsem[slot])

        def write(win, slot):
            return pltpu.make_async_copy(rows_v[slot], o_hbm.at[pl.ds(base + win * w, w)], wsem[slot])

        def start_gather(win, slot):
            pltpu.sync_copy(i_hbm.at[pl.ds(base + win * w, w)], idx_v[slot])
            gather(slot).start()

        start_gather(0, 0)

        @pl.loop(0, n_win // 2)
        def _(pair):
            for slot in range(2):
                win = 2 * pair + slot
                other = 1 - slot

                @pl.when(win >= 1)
                def _():
                    write(win - 1, other).wait()

                @pl.when(win + 1 < n_win)
                def _():
                    start_gather(win + 1, other)

                gather(slot).wait()
                write(win, slot).start()

        write(n_win - 1, 1).wait()

    return gather_rows(table, idx)


def _combine_kernel(alpha, g_ref, w_ref, x_ref, g2_ref, lng_ref, lnb_ref, *rest):
    o_ref = rest[-1]
    tm = x_ref.shape[1]
    w = w_ref[0]
    f = _load_rows_packed(g_ref.at[0], tm).astype(F32) * w[:, 0:1]
    for kk in range(1, TOP_K):
        f = f + _load_rows_packed(g_ref.at[kk], tm).astype(F32) * w[:, kk:kk + 1]
    o_ref[0] = _layer_norm(alpha * x_ref[0] + g2_ref[0] * f, lng_ref[...], lnb_ref[...])


def _combine(alpha, gathered, b0, nb, wts, x1, g2, lng, lnb, tm, prev=None):
    b, l, d = x1.shape
    nj = l // tm
    tok = pl.BlockSpec((1, tm, d), lambda i, j: (b0 + i, j, 0))
    full = pl.BlockSpec((1, d), lambda i, j: (0, 0))
    in_specs = [pl.BlockSpec((TOP_K, tm * PACKED_ROWS, LANES), lambda i, j: (0, i * nj + j, 0)),
                pl.BlockSpec((1, tm, ROUTER_PAD), lambda i, j: (b0 + i, j, 0)),
                tok, pl.BlockSpec((1, 1, d), lambda i, j: (b0 + i, 0, 0)), full, full]
    args = [gathered, wts, x1, g2, lng, lnb]
    aliases = {}
    if prev is not None:
        in_specs.append(pl.BlockSpec(memory_space=pl.ANY))
        args.append(prev)
        aliases = {len(args) - 1: 0}
    return pl.pallas_call(
        functools.partial(_combine_kernel, alpha),
        grid=(nb, nj),
        in_specs=in_specs,
        out_specs=tok,
        out_shape=jax.ShapeDtypeStruct((b, l, d), F32),
        input_output_aliases=aliases,
        compiler_params=_cparams("arbitrary", "arbitrary"),
        name="combine_ln",
    )(*args)


def _rope_tables(l):
    rows = l // GRID_W
    axis_dim = HEAD_DIM // 2
    inv_freq = ROPE_BASE ** (-jnp.arange(0, axis_dim, 2, dtype=F32) / axis_dim)
    row = jnp.repeat(jnp.arange(rows, dtype=F32), GRID_W)
    col = jnp.tile(jnp.arange(GRID_W, dtype=F32), rows)
    ang = jnp.stack([row[:, None] * inv_freq, col[:, None] * inv_freq], axis=1)
    cos, sin = jnp.cos(ang), jnp.sin(ang)
    cos64 = jnp.broadcast_to(cos[:, :, None, :], (l, 2, 2, HEAD_DIM // 4)).reshape(l, HEAD_DIM)
    sin64 = jnp.stack([-sin, sin], axis=2).reshape(l, HEAD_DIM)
    return jnp.tile(cos64, (1, LANES // HEAD_DIM)), jnp.tile(sin64, (1, LANES // HEAD_DIM))


def _route_tables(ri, counts, tm):
    t = ri.shape[0]
    top_e = ri[:, :TOP_K]
    rank = ri[:, TOP_K:2 * TOP_K]
    padded = (counts + tm - 1) // tm * tm
    pad_end = jnp.cumsum(padded)
    pad_start = pad_end - padded
    experts = jnp.arange(N_EXPERTS, dtype=jnp.int32)
    start = jnp.sum(jnp.where(top_e[:, :, None] == experts, pad_start, 0), axis=-1)
    dest = (start + rank).T.reshape(TOP_K * t)
    n_tiles = (t * TOP_K + N_EXPERTS * (tm - 1) + tm - 1) // tm
    tile_start = jnp.arange(n_tiles, dtype=jnp.int32) * tm
    block_e = jnp.minimum(jnp.sum((pad_end[None, :] <= tile_start[:, None]).astype(jnp.int32), axis=1), N_EXPERTS - 1)
    in_expert = jnp.where(block_e[:, None] == experts[None, :], (pad_start + counts)[None, :], 0).sum(axis=1)
    n_valid = jnp.clip(in_expert - tile_start, 0, tm)
    return dest, block_e.astype(jnp.int32), n_valid.astype(jnp.int32), n_tiles


def _layer_params(layer, d, w_in, conv_w, ret_decay_exp, ret_gn_g, q_norm_g, k_norm_g, w_out, ln_g, ln_b,
                  w_router, b_router):
    log_gamma = jnp.log1p(-jnp.exp2(-ret_decay_exp[layer].astype(F32)))
    lg_rows = jnp.repeat(log_gamma, HEAD_DIM, axis=1)
    wr_pad = jnp.zeros((d, ROUTER_PAD), F32).at[:, :N_EXPERTS].set(w_router[layer])
    wrh = wr_pad.astype(BF16)
    return dict(
        w_in=w_in[layer].astype(BF16), w_out=w_out[layer].astype(BF16),
        gq=jnp.tile(q_norm_g[layer], LANES // HEAD_DIM).reshape(1, LANES),
        gk=jnp.tile(k_norm_g[layer], LANES // HEAD_DIM).reshape(1, LANES),
        log_gamma=log_gamma, lgf_row=lg_rows[0:1], lgb_row=lg_rows[1:2], gn_row=ret_gn_g[layer].reshape(1, -1),
        cw=conv_w[layer].T.reshape(3, 1, -1),
        wr=jnp.concatenate([wrh, (wr_pad - wrh.astype(F32)).astype(BF16)], axis=1),
        br=jnp.zeros((1, ROUTER_PAD), F32).at[0, :N_EXPERTS].set(b_router[layer]),
        lng1=ln_g[layer, 0].reshape(1, d), lnb1=ln_b[layer, 0].reshape(1, d),
        lng2=ln_g[layer, 1].reshape(1, d), lnb2=ln_b[layer, 1].reshape(1, d))


def _layer(layer, last, alpha, x, ctx, m_lat, m_ctx, p, tables, experts):
    b, l, d = x.shape
    lc = ctx.shape[1]
    tm_in, tm_lat, tm_ctx, tm_moe, tq = 1024, 512, 256, 1024, 256
    assert l % tm_in == 0 and l % tm_lat == 0 and lc % tm_ctx == 0 and l % tq == 0
    cos_l, sin_l, cos_c, sin_c, ones_bd, zero_state = tables
    sh1, sc1, g1, sh2, sc2, g2 = (m_lat[:, :, i] for i in range(N_MOD))
    sh1c, sc1c, g1c, sh2c, sc2c, g2c = (m_ctx[:, :, i] for i in range(N_MOD))

    zc = _inproj(ctx, 1.0 + sc1c, sh1c, p["w_in"], cos_c, sin_c, p["gq"], p["gk"], ones_bd, tm_ctx)
    zl = _inproj(x, 1.0 + sc1, sh1, p["w_in"], cos_l, sin_l, p["gq"], p["gk"], ones_bd, tm_in)
    cu_c, cb_c, rq_c, rk_c, rv_c, rg_c, aq_c, ak_c, av_c = zc
    cu_l, cb_l, rq_l, rk_l, rv_l, rg_l, aq_l, ak_l, av_l = zl

    ret_args = (p["log_gamma"], p["lgf_row"], p["lgb_row"], p["gn_row"])
    ret_c, s_fwd, s_bwd = _retention(rq_c, rk_c, rv_c, rg_c, zero_state, zero_state, *ret_args)
    ret_l, _, _ = _retention(rq_l, rk_l, rv_l, rg_l, s_fwd, s_bwd, *ret_args)

    def kv_heads(a):
        return a.reshape(b, a.shape[1], -1, HEAD_DIM).transpose(0, 2, 1, 3)

    def v_heads(a):
        vt = a.reshape(b, a.shape[1], -1, HEAD_DIM).transpose(0, 2, 3, 1)
        ones = jnp.ones(vt.shape[:2] + (1, vt.shape[3]), BF16)
        pad = jnp.zeros(vt.shape[:2] + (ATT_V_ROWS - HEAD_DIM - 1, vt.shape[3]), BF16)
        return jnp.concatenate([vt, ones, pad], axis=2)

    k_all = kv_heads(jnp.concatenate([ak_c, ak_l], axis=1))
    v_all = v_heads(jnp.concatenate([av_c, av_l], axis=1))
    att_l = _attention(aq_l, k_all, v_all, tq)

    out_args = (p["w_out"], p["lng1"], p["lnb1"])
    rt_args = (p["wr"], p["br"])
    cnt0 = jnp.zeros((1, ROUTER_PAD), F32)
    n_c = 0 if last else b * lc
    n_tok = n_c + b * l
    h2_c = None
    if not last:
        att_c = _attention(aq_c, kv_heads(ak_c), v_heads(av_c), min(tq, lc))
        ctx1, h2_c, ri_c, rw_c, cnt0 = _outproj(alpha, cu_c, cb_c, p["cw"], ret_c, att_c, ctx, g1c, *out_args,
                                                1.0 + sc2c, sh2c, *rt_args, cnt0, tm_ctx, h2_tokens=n_tok)
    x1, h2, ri_l, rw_l, cnt = _outproj(alpha, cu_l, cb_l, p["cw"], ret_l, att_l, x, g1, *out_args,
                                       1.0 + sc2, sh2, *rt_args, cnt0, tm_lat,
                                       h2_tokens=n_tok, h2_offset=n_c, h2_prev=h2_c)
    if not last:
        ri = jnp.concatenate([ri_c.reshape(n_c, -1), ri_l.reshape(b * l, -1)], axis=0)
    else:
        ri = ri_l.reshape(b * l, -1)

    counts = cnt[0, :N_EXPERTS].astype(jnp.int32)
    dest, block_e, n_valid, n_tiles = _route_tables(ri, counts, tm_moe)
    xs = _sc_dispatch(h2.reshape(n_tok, PACKED_ROWS, LANES), dest, n_tiles * tm_moe)
    ys = _expert_ffn(layer, block_e, n_valid, xs.reshape(-1, LANES), *experts, tm_moe)
    ys = ys.reshape(-1, PACKED_ROWS, LANES)
    dest = dest.reshape(TOP_K, n_tok)

    def gather(lo, hi):
        rows = _sc_gather(ys, dest[:, lo:hi].reshape(-1))
        return rows.reshape(TOP_K, (hi - lo) * PACKED_ROWS, LANES)

    if not last:
        ctx = _combine(alpha, gather(0, n_c), 0, b, rw_c, ctx1, g2c, p["lng2"], p["lnb2"], tm_ctx)
    nb = b // COMBINE_CHUNKS if b % COMBINE_CHUNKS == 0 else b
    x_new = None
    for b0 in range(0, b, nb):
        g = gather(n_c + b0 * l, n_c + (b0 + nb) * l)
        x_new = _combine(alpha, g, b0, nb, rw_l, x1, g2, p["lng2"], p["lnb2"], tm_lat, prev=x_new)
    return x_new, ctx


def kernel(x, c, ctx, c_ctx, w_mod, b_mod, w_in, conv_w, ret_decay_exp, ret_gn_g, q_norm_g, k_norm_g, w_out,
           ln_g, ln_b, w_router, b_router, w_gate_up, b_gate_up, w_down, b_down):
    depth = w_mod.shape[0]
    alpha = (2.0 * depth) ** 0.25
    b, l, d = x.shape
    lc = ctx.shape[1]

    n_rows = (b + 1 + 7) // 8 * 8
    c_all = jnp.zeros((n_rows, d), F32).at[:b].set(c).at[b].set(c_ctx)
    mod = _modulation(c_all, w_mod, b_mod)

    cos_l, sin_l = _rope_tables(l)
    cos_c, sin_c = jnp.ones((lc, LANES), F32), jnp.zeros((lc, LANES), F32)
    lane_head = jnp.arange(LANES) // HEAD_DIM
    ones_bd = (lane_head[:, None] == lane_head[None, :]).astype(BF16)
    zero_state = jnp.zeros((b, RET_HEADS, HEAD_DIM, HEAD_DIM), F32)
    tables = (cos_l, sin_l, cos_c, sin_c, ones_bd, zero_state)
    experts = (w_gate_up, b_gate_up, w_down, b_down)

    for layer in range(depth):
        last = layer == depth - 1
        p = _layer_params(layer, d, w_in, conv_w, ret_decay_exp, ret_gn_g, q_norm_g, k_norm_g, w_out, ln_g, ln_b,
                          w_router, b_router)
        m_lat = mod[layer, :b].reshape(b, 1, N_MOD, d)
        m_ctx = jnp.broadcast_to(mod[layer, b].reshape(1, 1, N_MOD, d), (b, 1, N_MOD, d))
        x, ctx = _layer(layer, last, alpha, x, ctx, m_lat, m_ctx, p, tables, experts)
    return x
```

```python
import functools

import jax
import jax.numpy as jnp
from jax import lax
from jax.experimental import pallas as pl
from jax.experimental.pallas import tpu as pltpu
from jax.experimental.pallas import tpu_sc as plsc

F32 = jnp.float32
BF16 = jnp.bfloat16

HEAD_DIM = 64
GRID_W = 64
ROPE_BASE = 10000.0
N_EXPERTS = 32
TOP_K = 4
SWIGLU_ALPHA = 1.702
SWIGLU_LIMIT = 7.0
N_MOD = 6
EPS = 1e-6
IN_CONV, IN_RET, IN_ATT = (0, 768), (768, 1792), (1792, 2560)
RET_HEADS = 4
RET_CHUNK = 256
RET_BATCH = 8
ATT_Q_PER_KV = 4
ATT_KV_CHUNK = 256
ATT_Q_SCALE = HEAD_DIM ** -0.5 * 1.4426950408889634
ATT_STREAMS = 2
ATT_V_ROWS = 80
LANES = 128
ROUTER_PAD = 128
VMEM_LIMIT = 56 * 1024 * 1024
PACKED_ROWS = 4
FFN_BLOCK = 256
FFN_ROW_STEPS = 4
OUT_STREAMS = 2
COMBINE_CHUNKS = 4
SC_CORES = 2
SC_SUBCORES = 16
SC_WINDOW = 64


def _dot(a, b):
    return jnp.dot(a, b, preferred_element_type=F32)


def _split_bf16(x):
    hi = x.astype(BF16)
    lo = (x - hi.astype(F32)).astype(BF16)
    return hi, lo


def _cparams(*sem):
    return pltpu.CompilerParams(dimension_semantics=sem, vmem_limit_bytes=VMEM_LIMIT)


def _mod_kernel(c_ref, w_ref, b_ref, o_ref):
    c = c_ref[...]
    a = c * jax.nn.sigmoid(c)
    a_hi, a_lo = _split_bf16(a)
    w_hi, w_lo = _split_bf16(w_ref[0])
    o_ref[0] = _dot(a_hi, w_hi) + _dot(a_lo, w_hi) + _dot(a_hi, w_lo) + b_ref[0]


def _modulation(c_all, w_mod, b_mod):
    depth, d, n = w_mod.shape
    r = c_all.shape[0]
    tn = 1536
    return pl.pallas_call(
        _mod_kernel,
        grid=(depth, n // tn),
        in_specs=[
            pl.BlockSpec((r, d), lambda l, j: (0, 0)),
            pl.BlockSpec((1, d, tn), lambda l, j: (l, 0, j)),
            pl.BlockSpec((1, 1, tn), lambda l, j: (l, 0, j)),
        ],
        out_specs=pl.BlockSpec((1, r, tn), lambda l, j: (l, 0, j)),
        out_shape=jax.ShapeDtypeStruct((depth, r, n), F32),
        compiler_params=_cparams("arbitrary", "arbitrary"),
        name="modulation",
    )(c_all, w_mod, b_mod.reshape(depth, 1, n))


def _inproj_kernel(x_ref, sc_ref, sh_ref, w_ref, cos_ref, sin_ref, gq_ref, gk_ref, ones_ref,
                   cu_ref, cb_ref, rq_ref, rk_ref, rv_ref, rg_ref, aq_ref, ak_ref, av_ref):
    h = (x_ref[0] * sc_ref[0] + sh_ref[0]).astype(BF16)
    cos = cos_ref[...]
    sin = sin_ref[...]
    ones = ones_ref[...]
    lane = lax.broadcasted_iota(jnp.int32, cos.shape, 1)
    first_half = (lane & 31) < 16

    def rope(xs):
        nxt = pltpu.roll(xs, LANES - 16, axis=1)
        prv = pltpu.roll(xs, 16, axis=1)
        return xs * cos + jnp.where(first_half, nxt, prv) * sin

    def rms(xs, g):
        s_hi, s_lo = _split_bf16(xs * xs)
        ssq = _dot(s_hi, ones) + _dot(s_lo, ones)
        return xs * lax.rsqrt(ssq * (1.0 / HEAD_DIM) + EPS) * g

    za = _dot(h, w_ref[:, IN_ATT[0]:IN_ATT[1]])
    gq = gq_ref[...]
    for j in range(4):
        lo, hi = j * LANES, (j + 1) * LANES
        q = rope(rms(za[:, lo:hi], gq)) * ATT_Q_SCALE
        aq_ref[0, lo:hi, :] = q.T.astype(BF16)
    ak_ref[0] = rope(rms(za[:, 512:640], gk_ref[...])).astype(BF16)
    av_ref[0] = za[:, 640:768].astype(BF16)
    zr = _dot(h, w_ref[:, IN_RET[0]:IN_RET[1]])
    for j in range(2):
        lo, hi = j * LANES, (j + 1) * LANES
        rq_ref[0, lo:hi, :] = rope(zr[:, lo:hi]).T
        rk_ref[0, :, lo:hi] = rope(zr[:, 256 + lo:256 + hi] * (HEAD_DIM ** -0.5))
        rv_ref[0, lo:hi, :] = zr[:, 512 + lo:512 + hi].T
        rg_ref[0, lo:hi, :] = zr[:, 768 + lo:768 + hi].T
    zc = _dot(h, w_ref[:, IN_CONV[0]:IN_CONV[1]])
    cu_ref[0] = zc[:, 512:768] * zc[:, 0:256]
    cb_ref[0] = zc[:, 256:512]


def _inproj(x, sc, sh, w_bf, cos, sin, gq, gk, ones_bd, tm):
    b, l, d = x.shape
    n = w_bf.shape[1]
    tok = lambda w: pl.BlockSpec((1, tm, w), lambda i, j: (i, j, 0))
    vec = pl.BlockSpec((1, 1, d), lambda i, j: (i, 0, 0))
    full = lambda s: pl.BlockSpec(s, lambda i, j: (0,) * len(s))
    widths = (256, 256, 256, 256, 256, 256, 512, 128, 128)
    dtypes = (F32,) * 6 + (BF16,) * 3
    out_specs = [tok(w) for w in widths]
    out_shape = [jax.ShapeDtypeStruct((b, l, w), dt) for w, dt in zip(widths, dtypes)]
    for o in (2, 4, 5, 6):
        out_specs[o] = pl.BlockSpec((1, widths[o], tm), lambda i, j: (i, 0, j))
        out_shape[o] = jax.ShapeDtypeStruct((b, widths[o], l), dtypes[o])
    return pl.pallas_call(
        _inproj_kernel,
        grid=(b, l // tm),
        in_specs=[tok(d), vec, vec, full((d, n)),
                  pl.BlockSpec((tm, LANES), lambda i, j: (j, 0)),
                  pl.BlockSpec((tm, LANES), lambda i, j: (j, 0)),
                  full((1, LANES)), full((1, LANES)), full((LANES, LANES))],
        out_specs=out_specs,
        out_shape=out_shape,
        compiler_params=_cparams("arbitrary", "arbitrary"),
        name="inproj",
    )(x, sc, sh, w_bf, cos, sin, gq, gk, ones_bd)


def _attn_kernel(q_ref, k_ref, vt_ref, o_ref):
    tq = q_ref.shape[2]
    q4 = q_ref[0]
    qt = jnp.concatenate([q4[h * HEAD_DIM:(h + 1) * HEAD_DIM, :] for h in range(ATT_Q_PER_KV)], axis=1)
    lk = k_ref.shape[2]
    bounds = [(lo, min(lo + ATT_KV_CHUNK, lk)) for lo in range(0, lk, ATT_KV_CHUNK)]
    r = qt.shape[1] // ATT_STREAMS
    qs = [qt[:, i * r:(i + 1) * r] for i in range(ATT_STREAMS)]
    ms = [jnp.full((1, r), -jnp.inf, F32) for _ in qs]
    accs = [jnp.zeros((ATT_V_ROWS, r), F32) for _ in qs]
    nxt = [_dot(k_ref[0, 0, bounds[0][0]:bounds[0][1], :], q) for q in qs]
    for c, (lo, hi) in enumerate(bounds):
        for i, q in enumerate(qs):
            s = nxt[i]
            if c + 1 < len(bounds):
                nxt[i] = _dot(k_ref[0, 0, bounds[c + 1][0]:bounds[c + 1][1], :], q)
            m_new = jnp.maximum(ms[i], jnp.max(s, axis=0, keepdims=True))
            p = jnp.exp2(s - m_new)
            accs[i] = accs[i] * jnp.exp2(ms[i] - m_new) + _dot(vt_ref[0, 0, :, lo:hi], p.astype(BF16))
            ms[i] = m_new
    acc = jnp.concatenate(accs, axis=1)
    o = acc[:HEAD_DIM] / acc[HEAD_DIM:HEAD_DIM + 1]
    for h in range(ATT_Q_PER_KV):
        o_ref[0, h] = o[:, h * tq:(h + 1) * tq]


def _attention(q, k, vt, tq):
    b, wq, lq = q.shape
    hkv, lk = k.shape[1], k.shape[2]
    wg = ATT_Q_PER_KV * HEAD_DIM
    return pl.pallas_call(
        _attn_kernel,
        grid=(b, hkv, lq // tq),
        in_specs=[pl.BlockSpec((1, wg, tq), lambda i, g, j: (i, g, j)),
                  pl.BlockSpec((1, 1, lk, HEAD_DIM), lambda i, g, j: (i, g, 0, 0)),
                  pl.BlockSpec((1, 1, ATT_V_ROWS, lk), lambda i, g, j: (i, g, 0, 0))],
        out_specs=pl.BlockSpec((1, ATT_Q_PER_KV, HEAD_DIM, tq), lambda i, g, j: (i, g, 0, j)),
        out_shape=jax.ShapeDtypeStruct((b, wq // HEAD_DIM, HEAD_DIM, lq), F32),
        compiler_params=_cparams("arbitrary", "arbitrary", "arbitrary"),
        name="attention",
    )(q, k, vt)


def _ret_kernel(lg_ref, q_ref, k_ref, v_ref, g_ref, s0f_ref, s0b_ref, lgf_ref, lgb_ref, lgfc_ref, lgbc_ref, gn_ref,
                o_ref, sff_ref, sfb_ref, s_scr, sb_scr, dec_scr):
    i, p, j = pl.program_id(0), pl.program_id(1), pl.program_id(2)
    nc = pl.num_programs(2)
    c = k_ref.shape[1]
    w = k_ref.shape[2]
    pos = lax.broadcasted_iota(jnp.int32, (c, w), 0).astype(F32)
    qpos = lax.broadcasted_iota(jnp.int32, (w, c), 1).astype(F32)
    lgf = lgf_ref[...]
    lgb = lgb_ref[...]
    lgf_col = lgfc_ref[...]
    lgb_col = lgbc_ref[...]

    @pl.when((i == 0) & (p == 0) & (j == 0))
    def _():
        diff = (lax.broadcasted_iota(jnp.int32, (c, c), 1) - lax.broadcasted_iota(jnp.int32, (c, c), 0)).astype(F32)
        for h in range(RET_HEADS):
            dec_scr[h] = jnp.where(diff >= 0.0, jnp.exp(lg_ref[0, h] * jnp.maximum(diff, 0.0)),
                                   jnp.exp(lg_ref[1, h] * jnp.maximum(-diff, 0.0)))

    nb = k_ref.shape[0]

    def update_state(e, vt, kz, cdec):
        for h in range(RET_HEADS):
            lo, hi = h * HEAD_DIM, (h + 1) * HEAD_DIM
            s_scr[e, h] = cdec[:, lo:hi] * s_scr[e, h] + _dot(vt[lo:hi, :], kz[:, lo:hi])

    @pl.when(p == 0)
    def _():
        @pl.when(j == 0)
        def _():
            s_scr[...] = s0b_ref[...]

        sb_scr[nc - 1 - j] = s_scr[...]
        for e in range(nb):
            kz = (k_ref[e] * jnp.exp(lgb * pos)).astype(BF16)
            update_state(e, v_ref[e].astype(BF16), kz, jnp.exp(lgb * float(c)))

        @pl.when(j == nc - 1)
        def _():
            sfb_ref[...] = s_scr[...]

    @pl.when(p == 1)
    def _():
        @pl.when(j == 0)
        def _():
            s_scr[...] = s0f_ref[...]

        for e in range(nb):
            qt = q_ref[e]
            k = k_ref[e]
            qf = (qt * jnp.exp(lgf_col * (qpos + 1.0))).astype(BF16)
            qb = (qt * jnp.exp(lgb_col * (float(c) - qpos))).astype(BF16)
            kz = (k * jnp.exp(lgf * (float(c) - 1.0 - pos))).astype(BF16)
            qh = qt.astype(BF16)
            kh = k.astype(BF16)
            vt = v_ref[e].astype(BF16)
            gate = g_ref[e]
            for h in range(RET_HEADS):
                lo, hi = h * HEAD_DIM, (h + 1) * HEAD_DIM
                sc = _dot(kh[:, lo:hi], qh[lo:hi, :])
                y = (_dot(vt[lo:hi, :], (sc * dec_scr[h]).astype(BF16))
                     + _dot(s_scr[e, h].astype(BF16), qf[lo:hi, :])
                     + _dot(sb_scr[j, e, h].astype(BF16), qb[lo:hi, :]))
                mu = jnp.mean(y, axis=0, keepdims=True)
                yc = y - mu
                var = jnp.mean(yc * yc, axis=0, keepdims=True)
                gh = gate[lo:hi, :]
                o_ref[e, lo:hi, :] = yc * lax.rsqrt(var + EPS) * gn_ref[lo:hi, :] * (gh * jax.nn.sigmoid(gh))
            update_state(e, vt, kz, jnp.exp(lgf * float(c)))

        @pl.when(j == nc - 1)
        def _():
            sff_ref[...] = s_scr[...]


def _retention(qt, k, vt, gate_t, s0_f, s0_b, lg, lgf_row, lgb_row, gn_row):
    b, l, w = k.shape
    c = min(RET_CHUNK, l)
    nc = l // c
    nb = RET_BATCH if b % RET_BATCH == 0 else 1
    st = (nb, RET_HEADS, HEAD_DIM, HEAD_DIM)
    st_spec = pl.BlockSpec(st, lambda i, p, j: (i, 0, 0, 0))
    row = pl.BlockSpec((1, w), lambda i, p, j: (0, 0))
    col = pl.BlockSpec((w, 1), lambda i, p, j: (0, 0))
    k_both = pl.BlockSpec((nb, c, w), lambda i, p, j: (i, jnp.where(p == 0, nc - 1 - j, j), 0))
    t_both = pl.BlockSpec((nb, w, c), lambda i, p, j: (i, 0, jnp.where(p == 0, nc - 1 - j, j)))
    t_fwd = pl.BlockSpec((nb, w, c), lambda i, p, j: (i, 0, p * j))
    state_shape = jax.ShapeDtypeStruct((b,) + st[1:], F32)
    return pl.pallas_call(
        _ret_kernel,
        grid=(b // nb, 2, nc),
        in_specs=[pl.BlockSpec(memory_space=pltpu.SMEM), t_fwd, k_both, t_both, t_fwd,
                  st_spec, st_spec, row, row, col, col, col],
        out_specs=[t_fwd, st_spec, st_spec],
        out_shape=[jax.ShapeDtypeStruct((b, w, l), F32), state_shape, state_shape],
        scratch_shapes=[pltpu.VMEM(st, F32), pltpu.VMEM((nc,) + st, F32), pltpu.VMEM((RET_HEADS, c, c), F32)],
        compiler_params=_cparams("arbitrary", "arbitrary", "arbitrary"),
        name="retention",
    )(lg, qt, k, vt, gate_t, s0_f, s0_b, lgf_row, lgb_row, lgf_row.reshape(w, 1), lgb_row.reshape(w, 1),
      gn_row.reshape(w, 1))


def _layer_norm(x, g, b):
    mu = jnp.mean(x, axis=-1, keepdims=True)
    xc = x - mu
    var = jnp.mean(xc * xc, axis=-1, keepdims=True)
    return xc * lax.rsqrt(var + EPS) * g + b


def _store_rows_packed(ref, x):
    tm, d = x.shape
    bits = pltpu.bitcast(x.astype(BF16).astype(F32), jnp.uint32)
    word = (bits[:, :d // 2] >> 16) | bits[:, d // 2:]
    for s in range(PACKED_ROWS):
        ref[pl.ds(s, tm, stride=PACKED_ROWS), :] = word[:, s * LANES:(s + 1) * LANES]


def _load_rows_packed(ref, tm):
    word = jnp.concatenate([ref[pl.ds(s, tm, stride=PACKED_ROWS), :] for s in range(PACKED_ROWS)], axis=1)
    lo = pltpu.bitcast(word << 16, F32).astype(BF16)
    hi = pltpu.bitcast(word & jnp.uint32(0xFFFF0000), F32).astype(BF16)
    return jnp.concatenate([lo, hi], axis=1)


def _outproj_kernel(alpha, cu_ref, cup_ref, cun_ref, cb_ref, cw_ref, ret_ref, att_ref, x_ref, g1_ref,
                    w_ref, lng_ref, lnb_ref, sc_ref, sh_ref, wr_ref, br_ref, cnt0_ref, before_ref, *rest):
    x1_ref, h2_ref, ri_ref, rw_ref, cnt_ref, run_scr = rest[-6:]
    i = pl.program_id(0)
    j = pl.program_id(1)

    @pl.when((i == 0) & (j == 0))
    def _():
        run_scr[...] = cnt0_ref[...]

    t = cu_ref[0]
    tm = t.shape[0]
    prev_row = jnp.where(j == 0, 0.0, cup_ref[0][7:8, :])
    next_row = jnp.where(j == pl.num_programs(1) - 1, 0.0, cun_ref[0][0:1, :])
    row = lax.broadcasted_iota(jnp.int32, t.shape, 0)
    t_prev = jnp.where(row == 0, prev_row, pltpu.roll(t, 1, axis=0))
    t_next = jnp.where(row == tm - 1, next_row, pltpu.roll(t, tm - 1, axis=0))
    conv = cb_ref[0] * (t_prev * cw_ref[0] + t * cw_ref[1] + t_next * cw_ref[2])
    conv = conv.astype(BF16)
    wrh = wr_ref[:, :ROUTER_PAD]
    rb = tm // OUT_STREAMS
    lane = lax.broadcasted_iota(jnp.int32, (rb, ROUTER_PAD), 1)
    lane_f = lane.astype(F32)
    routed = []
    for r0 in range(0, tm, rb):
        att_t = att_ref[0, :, :, r0:r0 + rb].reshape(att_ref.shape[1] * HEAD_DIM, rb).astype(BF16)
        tn = (((0,), (0,)), ((), ()))
        y = (_dot(conv[r0:r0 + rb], w_ref[0:256, :])
             + lax.dot_general(ret_ref[0, :, r0:r0 + rb].astype(BF16), w_ref[256:512, :], tn,
                               preferred_element_type=F32)
             + lax.dot_general(att_t, w_ref[512:1024, :], tn, preferred_element_type=F32))
        x1 = _layer_norm(alpha * x_ref[0, r0:r0 + rb, :] + g1_ref[0] * y, lng_ref[...], lnb_ref[...])
        x1_ref[0, r0:r0 + rb, :] = x1
        h2 = x1 * sc_ref[0] + sh_ref[0]
        _store_rows_packed(h2_ref.at[pl.ds(r0 * PACKED_ROWS, rb * PACKED_ROWS)], h2)
        h_hi, h_lo = _split_bf16(h2)
        hw = _dot(h_hi, wr_ref[...])
        logits = hw[:, :ROUTER_PAD] + hw[:, ROUTER_PAD:] + _dot(h_lo, wrh) + br_ref[...]

        work = jnp.where(lane < N_EXPERTS, logits, -jnp.inf)
        vals, firsts, sels = [], [], []
        for _ in range(TOP_K):
            m = jnp.max(work, axis=-1, keepdims=True)
            first = jnp.min(jnp.where(work == m, lane_f, float(ROUTER_PAD)), axis=-1, keepdims=True)
            sel = lane_f == first
            vals.append(m)
            firsts.append(first)
            sels.append(sel)
            work = jnp.where(sel, -jnp.inf, work)
        exps = [jnp.exp(v - vals[0]) for v in vals]
        denom = exps[0]
        for e in exps[1:]:
            denom = denom + e
        cnt = jnp.zeros(logits.shape, F32)
        for sel in sels:
            cnt = cnt + jnp.where(sel, 1.0, 0.0)
        routed.append((firsts, sels, [e / denom for e in exps], cnt))

    cnt = jnp.concatenate([r[3] for r in routed], axis=0)
    base = _dot(before_ref[...], cnt.astype(BF16)) + run_scr[...]
    run_scr[...] = run_scr[...] + jnp.sum(cnt, axis=0, keepdims=True)
    cnt_ref[...] = run_scr[...]

    for blk, (firsts, sels, wts, _) in enumerate(routed):
        r0 = blk * rb
        ri = jnp.zeros((rb, ROUTER_PAD), F32)
        rw = jnp.zeros((rb, ROUTER_PAD), F32)
        for kk in range(TOP_K):
            rank = jnp.sum(jnp.where(sels[kk], base[r0:r0 + rb], 0.0), axis=-1, keepdims=True)
            ri = jnp.where(lane == kk, firsts[kk], ri)
            ri = jnp.where(lane == TOP_K + kk, rank, ri)
            rw = jnp.where(lane == kk, wts[kk], rw)
        ri_ref[0, r0:r0 + rb, :] = ri.astype(jnp.int32)
        rw_ref[0, r0:r0 + rb, :] = rw


def _outproj(alpha, cu, cb, cw, ret, att, x, g1, w_bf, lng, lnb, sc2, sh2, wr, br, cnt0, tm,
             h2_tokens=None, h2_offset=0, h2_prev=None):
    b, l, d = x.shape
    nj = l // tm
    h2_tokens = b * l if h2_tokens is None else h2_tokens
    assert h2_offset % tm == 0
    h2_blk = h2_offset // tm
    extra_specs, extra_args, aliases = [], [], {}
    if h2_prev is not None:
        extra_specs, extra_args, aliases = [pl.BlockSpec(memory_space=pl.ANY)], [h2_prev], {18: 1}
    tok = lambda w: pl.BlockSpec((1, tm, w), lambda i, j: (i, j, 0))
    vec = pl.BlockSpec((1, 1, d), lambda i, j: (i, 0, 0))
    full = lambda s: pl.BlockSpec(s, lambda i, j: (0,) * len(s))
    r8 = tm // 8
    nb8 = l // 8
    tri = jnp.arange(tm)
    before = (tri[None, :] < tri[:, None]).astype(BF16)
    return pl.pallas_call(
        functools.partial(_outproj_kernel, alpha),
        grid=(b, nj),
        in_specs=[tok(256),
                  pl.BlockSpec((1, 8, 256), lambda i, j: (i, jnp.maximum(j * r8 - 1, 0), 0)),
                  pl.BlockSpec((1, 8, 256), lambda i, j: (i, jnp.minimum((j + 1) * r8, nb8 - 1), 0)),
                  tok(256), full((3, 1, 256)),
                  pl.BlockSpec((1, ret.shape[1], tm), lambda i, j: (i, 0, j)),
                  pl.BlockSpec((1,) + att.shape[1:3] + (tm,), lambda i, j: (i, 0, 0, j)), tok(d), vec,
                  full((d, d)), full((1, d)), full((1, d)), vec, vec,
                  full((d, 2 * ROUTER_PAD)), full((1, ROUTER_PAD)), full((1, ROUTER_PAD)),
                  full((tm, tm))] + extra_specs,
        out_specs=[tok(d), pl.BlockSpec((tm * PACKED_ROWS, LANES), lambda i, j: (h2_blk + i * nj + j, 0)),
                   tok(ROUTER_PAD), tok(ROUTER_PAD), full((1, ROUTER_PAD))],
        out_shape=[jax.ShapeDtypeStruct((b, l, d), F32),
                   jax.ShapeDtypeStruct((h2_tokens * PACKED_ROWS, LANES), jnp.uint32),
                   jax.ShapeDtypeStruct((b, l, ROUTER_PAD), jnp.int32),
                   jax.ShapeDtypeStruct((b, l, ROUTER_PAD), F32),
                   jax.ShapeDtypeStruct((1, ROUTER_PAD), F32)],
        scratch_shapes=[pltpu.VMEM((1, ROUTER_PAD), F32)],
        input_output_aliases=aliases,
        compiler_params=_cparams("arbitrary", "arbitrary"),
        name="outproj_ln_router",
    )(cu, cu, cu, cb, cw, ret, att, x, g1, w_bf, lng, lnb, sc2, sh2, wr, br, cnt0, before, *extra_args)


def _ffn_kernel(be_ref, nv_ref, x_ref, wgu_ref, bgu_ref, wd_ref, bd_ref, o_ref, wgu_scr, wd_scr):
    j = pl.program_id(0)
    f = wd_ref.shape[2]
    tm = x_ref.shape[0] // PACKED_ROWS
    nv = nv_ref[j]

    @pl.when(nv > 0)
    def _():
        @pl.when((j == 0) | (be_ref[j] != be_ref[jnp.maximum(j - 1, 0)]))
        def _():
            wgu_scr[...] = wgu_ref[0, 0].astype(BF16)
            wd_scr[...] = wd_ref[0, 0].astype(BF16)

    def compute(rows):
        part = pl.ds(0, rows * PACKED_ROWS)
        x = _load_rows_packed(x_ref.at[part], rows)
        bgu = bgu_ref[0, 0]
        acts = []
        for lo in range(0, f, FFN_BLOCK):
            hi = lo + FFN_BLOCK
            gate = jnp.minimum(_dot(x, wgu_scr[:, lo:hi]) + bgu[:, lo:hi], SWIGLU_LIMIT)
            up = jnp.clip(_dot(x, wgu_scr[:, f + lo:f + hi]) + bgu[:, f + lo:f + hi], -SWIGLU_LIMIT, SWIGLU_LIMIT)
            acts.append(((up + 1.0) * (gate * jax.nn.sigmoid(SWIGLU_ALPHA * gate))).astype(BF16))
        act = jnp.concatenate(acts, axis=1)
        _store_rows_packed(o_ref.at[part], _dot(act, wd_scr[...]) + bd_ref[0, 0])

    step = tm // FFN_ROW_STEPS
    for k in range(1, FFN_ROW_STEPS + 1):
        @pl.when((nv > (k - 1) * step) & (nv <= k * step))
        def _():
            compute(k * step)


def _expert_ffn(layer, block_e, n_valid, xs, wgu, bgu, wd, bd, tm):
    n_rows = xs.shape[0] // PACKED_ROWS
    depth, ne, d, f2 = wgu.shape
    f = f2 // 2
    rows = pl.BlockSpec((tm * PACKED_ROWS, LANES), lambda j, be, nu: (j, 0))
    grid_spec = pltpu.PrefetchScalarGridSpec(
        num_scalar_prefetch=2,
        grid=(n_rows // tm,),
        in_specs=[rows,
                  pl.BlockSpec((1, 1, d, f2), lambda j, be, nu: (layer, be[j], 0, 0)),
                  pl.BlockSpec((1, 1, 1, f2), lambda j, be, nu: (layer, be[j], 0, 0)),
                  pl.BlockSpec((1, 1, f, d), lambda j, be, nu: (layer, be[j], 0, 0)),
                  pl.BlockSpec((1, 1, 1, d), lambda j, be, nu: (layer, be[j], 0, 0))],
        out_specs=rows,
        scratch_shapes=[pltpu.VMEM((d, f2), BF16), pltpu.VMEM((f, d), BF16)],
    )
    return pl.pallas_call(
        _ffn_kernel,
        grid_spec=grid_spec,
        out_shape=jax.ShapeDtypeStruct((n_rows * PACKED_ROWS, LANES), jnp.uint32),
        compiler_params=_cparams("arbitrary"),
        name="expert_ffn",
    )(block_e, n_valid, xs, wgu, bgu.reshape(depth, ne, 1, f2), wd, bd.reshape(depth, ne, 1, d))


def _sc_worker_base(per_worker):
    return (lax.axis_index("s") * SC_CORES + lax.axis_index("c")) * per_worker


def _sc_dispatch(rows, dest, n_out):
    t = rows.shape[0]
    kk = dest.shape[0] // t
    w = SC_WINDOW
    per_worker = t // (SC_CORES * SC_SUBCORES)
    assert per_worker * SC_CORES * SC_SUBCORES == t and per_worker % w == 0
    mesh = plsc.VectorSubcoreMesh(core_axis_name="c", subcore_axis_name="s")

    n_win = per_worker // w
    row_buf = pltpu.VMEM((w,) + rows.shape[1:], rows.dtype)

    @functools.partial(
        pl.kernel, mesh=mesh,
        out_type=jax.ShapeDtypeStruct((n_out,) + rows.shape[1:], rows.dtype),
        scratch_types=[pltpu.VMEM((w,), jnp.int32)] * (2 * kk) + [row_buf] * 2 + [pltpu.SemaphoreType.DMA] * 4)
    def scatter_rows(r_hbm, d_hbm, o_hbm, *scratch):
        idx_v = (scratch[:kk], scratch[kk:2 * kk])
        rows_v = scratch[2 * kk:2 * kk + 2]
        rsem = scratch[2 * kk + 2:2 * kk + 4]
        ssem = scratch[2 * kk + 4:2 * kk + 6]
        base = _sc_worker_base(per_worker)

        def read(win, slot):
            return pltpu.make_async_copy(r_hbm.at[pl.ds(base + win * w, w)], rows_v[slot], rsem[slot])

        def scatter(slot, s):
            return pltpu.make_async_copy(rows_v[slot], o_hbm.at[idx_v[slot][s]], ssem[slot])

        def start_read(win, slot):
            read(win, slot).start()
            for s in range(kk):
                pltpu.sync_copy(d_hbm.at[pl.ds(s * t + base + win * w, w)], idx_v[slot][s])

        start_read(0, 0)

        @pl.loop(0, (n_win + 1) // 2)
        def _(pair):
            for slot in range(2):
                win = 2 * pair + slot
                other = 1 - slot

                @pl.when(win < n_win)
                def _():
                    @pl.when(win >= 1)
                    def _():
                        for s in range(kk):
                            scatter(other, s).wait()

                    @pl.when(win + 1 < n_win)
                    def _():
                        start_read(win + 1, other)

                    read(win, slot).wait()
                    for s in range(kk):
                        scatter(slot, s).start()

        for s in range(kk):
            scatter((n_win - 1) % 2, s).wait()

    return scatter_rows(rows, dest)


def _sc_gather(table, idx):
    n = idx.shape[0]
    w = SC_WINDOW
    per_worker = n // (SC_CORES * SC_SUBCORES)
    assert per_worker * SC_CORES * SC_SUBCORES == n and per_worker % w == 0
    mesh = plsc.VectorSubcoreMesh(core_axis_name="c", subcore_axis_name="s")

    n_win = per_worker // w
    assert n_win % 2 == 0
    row_buf = pltpu.VMEM((w,) + table.shape[1:], table.dtype)

    @functools.partial(
        pl.kernel, mesh=mesh,
        out_type=jax.ShapeDtypeStruct((n,) + table.shape[1:], table.dtype),
        scratch_types=[pltpu.VMEM((w,), jnp.int32)] * 2 + [row_buf] * 2 + [pltpu.SemaphoreType.DMA] * 4)
    def gather_rows(t_hbm, i_hbm, o_hbm, idx0, idx1, rows0, rows1, gsem0, gsem1, wsem0, wsem1):
        idx_v, rows_v = (idx0, idx1), (rows0, rows1)
        gsem, wsem = (gsem0, gsem1), (wsem0, wsem1)
        base = _sc_worker_base(per_worker)

        def gather(slot):
            return pltpu.make_async_copy(t_hbm.at[idx_v[slot]], rows_v[slot], gsem[slot])

        def write(win, slot):
            return pltpu.make_async_copy(rows_v[slot], o_hbm.at[pl.ds(base + win * w, w)], wsem[slot])

        def start_gather(win, slot):
            pltpu.sync_copy(i_hbm.at[pl.ds(base + win * w, w)], idx_v[slot])
            gather(slot).start()

        start_gather(0, 0)

        @pl.loop(0, n_win // 2)
        def _(pair):
            for slot in range(2):
                win = 2 * pair + slot
                other = 1 - slot

                @pl.when(win >= 1)
                def _():
                    write(win - 1, other).wait()

                @pl.when(win + 1 < n_win)
                def _():
                    start_gather(win + 1, other)

                gather(slot).wait()
                write(win, slot).start()

        write(n_win - 1, 1).wait()

    return gather_rows(table, idx)


def _combine_kernel(alpha, g_ref, w_ref, x_ref, g2_ref, lng_ref, lnb_ref, *rest):
    o_ref = rest[-1]
    tm = x_ref.shape[1]
    w = w_ref[0]
    f = _load_rows_packed(g_ref.at[0], tm).astype(F32) * w[:, 0:1]
    for kk in range(1, TOP_K):
        f = f + _load_rows_packed(g_ref.at[kk], tm).astype(F32) * w[:, kk:kk + 1]
    o_ref[0] = _layer_norm(alpha * x_ref[0] + g2_ref[0] * f, lng_ref[...], lnb_ref[...])


def _combine(alpha, gathered, b0, nb, wts, x1, g2, lng, lnb, tm, prev=None):
    b, l, d = x1.shape
    nj = l // tm
    tok = pl.BlockSpec((1, tm, d), lambda i, j: (b0 + i, j, 0))
    full = pl.BlockSpec((1, d), lambda i, j: (0, 0))
    in_specs = [pl.BlockSpec((TOP_K, tm * PACKED_ROWS, LANES), lambda i, j: (0, i * nj + j, 0)),
                pl.BlockSpec((1, tm, ROUTER_PAD), lambda i, j: (b0 + i, j, 0)),
                tok, pl.BlockSpec((1, 1, d), lambda i, j: (b0 + i, 0, 0)), full, full]
    args = [gathered, wts, x1, g2, lng, lnb]
    aliases = {}
    if prev is not None:
        in_specs.append(pl.BlockSpec(memory_space=pl.ANY))
        args.append(prev)
        aliases = {len(args) - 1: 0}
    return pl.pallas_call(
        functools.partial(_combine_kernel, alpha),
        grid=(nb, nj),
        in_specs=in_specs,
        out_specs=tok,
        out_shape=jax.ShapeDtypeStruct((b, l, d), F32),
        input_output_aliases=aliases,
        compiler_params=_cparams("arbitrary", "arbitrary"),
        name="combine_ln",
    )(*args)


def _rope_tables(l):
    rows = l // GRID_W
    axis_dim = HEAD_DIM // 2
    inv_freq = ROPE_BASE ** (-jnp.arange(0, axis_dim, 2, dtype=F32) / axis_dim)
    row = jnp.repeat(jnp.arange(rows, dtype=F32), GRID_W)
    col = jnp.tile(jnp.arange(GRID_W, dtype=F32), rows)
    ang = jnp.stack([row[:, None] * inv_freq, col[:, None] * inv_freq], axis=1)
    cos, sin = jnp.cos(ang), jnp.sin(ang)
    cos64 = jnp.broadcast_to(cos[:, :, None, :], (l, 2, 2, HEAD_DIM // 4)).reshape(l, HEAD_DIM)
    sin64 = jnp.stack([-sin, sin], axis=2).reshape(l, HEAD_DIM)
    return jnp.tile(cos64, (1, LANES // HEAD_DIM)), jnp.tile(sin64, (1, LANES // HEAD_DIM))


def _route_tables(ri, counts, tm):
    t = ri.shape[0]
    top_e = ri[:, :TOP_K]
    rank = ri[:, TOP_K:2 * TOP_K]
    padded = (counts + tm - 1) // tm * tm
    pad_end = jnp.cumsum(padded)
    pad_start = pad_end - padded
    experts = jnp.arange(N_EXPERTS, dtype=jnp.int32)
    start = jnp.sum(jnp.where(top_e[:, :, None] == experts, pad_start, 0), axis=-1)
    dest = (start + rank).T.reshape(TOP_K * t)
    n_tiles = (t * TOP_K + N_EXPERTS * (tm - 1) + tm - 1) // tm
    tile_start = jnp.arange(n_tiles, dtype=jnp.int32) * tm
    block_e = jnp.minimum(jnp.sum((pad_end[None, :] <= tile_start[:, None]).astype(jnp.int32), axis=1), N_EXPERTS - 1)
    in_expert = jnp.where(block_e[:, None] == experts[None, :], (pad_start + counts)[None, :], 0).sum(axis=1)
    n_valid = jnp.clip(in_expert - tile_start, 0, tm)
    return dest, block_e.astype(jnp.int32), n_valid.astype(jnp.int32), n_tiles


def _layer_params(layer, d, w_in, conv_w, ret_decay_exp, ret_gn_g, q_norm_g, k_norm_g, w_out, ln_g, ln_b,
                  w_router, b_router):
    log_gamma = jnp.log1p(-jnp.exp2(-ret_decay_exp[layer].astype(F32)))
    lg_rows = jnp.repeat(log_gamma, HEAD_DIM, axis=1)
    wr_pad = jnp.zeros((d, ROUTER_PAD), F32).at[:, :N_EXPERTS].set(w_router[layer])
    wrh = wr_pad.astype(BF16)
    return dict(
        w_in=w_in[layer].astype(BF16), w_out=w_out[layer].astype(BF16),
        gq=jnp.tile(q_norm_g[layer], LANES // HEAD_DIM).reshape(1, LANES),
        gk=jnp.tile(k_norm_g[layer], LANES // HEAD_DIM).reshape(1, LANES),
        log_gamma=log_gamma, lgf_row=lg_rows[0:1], lgb_row=lg_rows[1:2], gn_row=ret_gn_g[layer].reshape(1, -1),
        cw=conv_w[layer].T.reshape(3, 1, -1),
        wr=jnp.concatenate([wrh, (wr_pad - wrh.astype(F32)).astype(BF16)], axis=1),
        br=jnp.zeros((1, ROUTER_PAD), F32).at[0, :N_EXPERTS].set(b_router[layer]),
        lng1=ln_g[layer, 0].reshape(1, d), lnb1=ln_b[layer, 0].reshape(1, d),
        lng2=ln_g[layer, 1].reshape(1, d), lnb2=ln_b[layer, 1].reshape(1, d))


def _layer(layer, last, alpha, x, ctx, m_lat, m_ctx, p, tables, experts):
    b, l, d = x.shape
    lc = ctx.shape[1]
    tm_in, tm_lat, tm_ctx, tm_moe, tq = 1024, 512, 256, 1024, 256
    assert l % tm_in == 0 and l % tm_lat == 0 and lc % tm_ctx == 0 and l % tq == 0
    cos_l, sin_l, cos_c, sin_c, ones_bd, zero_state = tables
    sh1, sc1, g1, sh2, sc2, g2 = (m_lat[:, :, i] for i in range(N_MOD))
    sh1c, sc1c, g1c, sh2c, sc2c, g2c = (m_ctx[:, :, i] for i in range(N_MOD))

    zc = _inproj(ctx, 1.0 + sc1c, sh1c, p["w_in"], cos_c, sin_c, p["gq"], p["gk"], ones_bd, tm_ctx)
    zl = _inproj(x, 1.0 + sc1, sh1, p["w_in"], cos_l, sin_l, p["gq"], p["gk"], ones_bd, tm_in)
    cu_c, cb_c, rq_c, rk_c, rv_c, rg_c, aq_c, ak_c, av_c = zc
    cu_l, cb_l, rq_l, rk_l, rv_l, rg_l, aq_l, ak_l, av_l = zl

    ret_args = (p["log_gamma"], p["lgf_row"], p["lgb_row"], p["gn_row"])
    ret_c, s_fwd, s_bwd = _retention(rq_c, rk_c, rv_c, rg_c, zero_state, zero_state, *ret_args)
    ret_l, _, _ = _retention(rq_l, rk_l, rv_l, rg_l, s_fwd, s_bwd, *ret_args)

    def kv_heads(a):
        return a.reshape(b, a.shape[1], -1, HEAD_DIM).transpose(0, 2, 1, 3)

    def v_heads(a):
        vt = a.reshape(b, a.shape[1], -1, HEAD_DIM).transpose(0, 2, 3, 1)
        ones = jnp.ones(vt.shape[:2] + (1, vt.shape[3]), BF16)
        pad = jnp.zeros(vt.shape[:2] + (ATT_V_ROWS - HEAD_DIM - 1, vt.shape[3]), BF16)
        return jnp.concatenate([vt, ones, pad], axis=2)

    k_all = kv_heads(jnp.concatenate([ak_c, ak_l], axis=1))
    v_all = v_heads(jnp.concatenate([av_c, av_l], axis=1))
    att_l = _attention(aq_l, k_all, v_all, tq)

    out_args = (p["w_out"], p["lng1"], p["lnb1"])
    rt_args = (p["wr"], p["br"])
    cnt0 = jnp.zeros((1, ROUTER_PAD), F32)
    n_c = 0 if last else b * lc
    n_tok = n_c + b * l
    h2_c = None
    if not last:
        att_c = _attention(aq_c, kv_heads(ak_c), v_heads(av_c), min(tq, lc))
        ctx1, h2_c, ri_c, rw_c, cnt0 = _outproj(alpha, cu_c, cb_c, p["cw"], ret_c, att_c, ctx, g1c, *out_args,
                                                1.0 + sc2c, sh2c, *rt_args, cnt0, tm_ctx, h2_tokens=n_tok)
    x1, h2, ri_l, rw_l, cnt = _outproj(alpha, cu_l, cb_l, p["cw"], ret_l, att_l, x, g1, *out_args,
                                       1.0 + sc2, sh2, *rt_args, cnt0, tm_lat,
                                       h2_tokens=n_tok, h2_offset=n_c, h2_prev=h2_c)
    if not last:
        ri = jnp.concatenate([ri_c.reshape(n_c, -1), ri_l.reshape(b * l, -1)], axis=0)
    else:
        ri = ri_l.reshape(b * l, -1)

    counts = cnt[0, :N_EXPERTS].astype(jnp.int32)
    dest, block_e, n_valid, n_tiles = _route_tables(ri, counts, tm_moe)
    xs = _sc_dispatch(h2.reshape(n_tok, PACKED_ROWS, LANES), dest, n_tiles * tm_moe)
    ys = _expert_ffn(layer, block_e, n_valid, xs.reshape(-1, LANES), *experts, tm_moe)
    ys = ys.reshape(-1, PACKED_ROWS, LANES)
    dest = dest.reshape(TOP_K, n_tok)

    def gather(lo, hi):
        rows = _sc_gather(ys, dest[:, lo:hi].reshape(-1))
        return rows.reshape(TOP_K, (hi - lo) * PACKED_ROWS, LANES)

    if not last:
        ctx = _combine(alpha, gather(0, n_c), 0, b, rw_c, ctx1, g2c, p["lng2"], p["lnb2"], tm_ctx)
    nb = b // COMBINE_CHUNKS if b % COMBINE_CHUNKS == 0 else b
    x_new = None
    for b0 in range(0, b, nb):
        g = gather(n_c + b0 * l, n_c + (b0 + nb) * l)
        x_new = _combine(alpha, g, b0, nb, rw_l, x1, g2, p["lng2"], p["lnb2"], tm_lat, prev=x_new)
    return x_new, ctx


def kernel(x, c, ctx, c_ctx, w_mod, b_mod, w_in, conv_w, ret_decay_exp, ret_gn_g, q_norm_g, k_norm_g, w_out,
           ln_g, ln_b, w_router, b_router, w_gate_up, b_gate_up, w_down, b_down):
    depth = w_mod.shape[0]
    alpha = (2.0 * depth) ** 0.25
    b, l, d = x.shape
    lc = ctx.shape[1]

    n_rows = (b + 1 + 7) // 8 * 8
    c_all = jnp.zeros((n_rows, d), F32).at[:b].set(c).at[b].set(c_ctx)
    mod = _modulation(c_all, w_mod, b_mod)

    cos_l, sin_l = _rope_tables(l)
    cos_c, sin_c = jnp.ones((lc, LANES), F32), jnp.zeros((lc, LANES), F32)
    lane_head = jnp.arange(LANES) // HEAD_DIM
    ones_bd = (lane_head[:, None] == lane_head[None, :]).astype(BF16)
    zero_state = jnp.zeros((b, RET_HEADS, HEAD_DIM, HEAD_DIM), F32)
    tables = (cos_l, sin_l, cos_c, sin_c, ones_bd, zero_state)
    experts = (w_gate_up, b_gate_up, w_down, b_down)

    for layer in range(depth):
        last = layer == depth - 1
        p = _layer_params(layer, d, w_in, conv_w, ret_decay_exp, ret_gn_g, q_norm_g, k_norm_g, w_out, ln_g, ln_b,
                          w_router, b_router)
        m_lat = mod[layer, :b].reshape(b, 1, N_MOD, d)
        m_ctx = jnp.broadcast_to(mod[layer, b].reshape(1, 1, N_MOD, d), (b, 1, N_MOD, d))
        x, ctx = _layer(layer, last, alpha, x, ctx, m_lat, m_ctx, p, tables, experts)
    return x
```

```python
import functools

import jax
import jax.numpy as jnp
from jax import lax
from jax.experimental import pallas as pl
from jax.experimental.pallas import tpu as pltpu
from jax.experimental.pallas import tpu_sc as plsc

F32 = jnp.float32
BF16 = jnp.bfloat16

HEAD_DIM = 64
GRID_W = 64
ROPE_BASE = 10000.0
N_EXPERTS = 32
TOP_K = 4
SWIGLU_ALPHA = 1.702
SWIGLU_LIMIT = 7.0
N_MOD = 6
EPS = 1e-6
IN_CONV, IN_RET, IN_ATT = (0, 768), (768, 1792), (1792, 2560)
IN_STREAMS = 2
RET_HEADS = 4
RET_CHUNK = 256
RET_BATCH = 8
ATT_Q_PER_KV = 4
ATT_KV_CHUNK = 256
ATT_Q_SCALE = HEAD_DIM ** -0.5 * 1.4426950408889634
ATT_STREAMS = 2
ATT_V_ROWS = 80
LANES = 128
ROUTER_PAD = 128
VMEM_LIMIT = 56 * 1024 * 1024
PACKED_ROWS = 4
FFN_BLOCK = 256
FFN_ROW_STEPS = 8
OUT_STREAMS = 2
COMBINE_CHUNKS = 4
SC_CORES = 2
SC_SUBCORES = 16
SC_WINDOW = 64


def _dot(a, b):
    return jnp.dot(a, b, preferred_element_type=F32)


def _split_bf16(x):
    hi = x.astype(BF16)
    lo = (x - hi.astype(F32)).astype(BF16)
    return hi, lo


def _cparams(*sem):
    return pltpu.CompilerParams(dimension_semantics=sem, vmem_limit_bytes=VMEM_LIMIT)


def _mod_kernel(c_ref, w_ref, b_ref, o_ref):
    c = c_ref[...]
    a = c * jax.nn.sigmoid(c)
    a_hi, a_lo = _split_bf16(a)
    w_hi, w_lo = _split_bf16(w_ref[0])
    o_ref[0] = _dot(a_hi, w_hi) + _dot(a_lo, w_hi) + _dot(a_hi, w_lo) + b_ref[0]


def _modulation(c_all, w_mod, b_mod):
    depth, d, n = w_mod.shape
    r = c_all.shape[0]
    tn = 1536
    return pl.pallas_call(
        _mod_kernel,
        grid=(depth, n // tn),
        in_specs=[
            pl.BlockSpec((r, d), lambda l, j: (0, 0)),
            pl.BlockSpec((1, d, tn), lambda l, j: (l, 0, j)),
            pl.BlockSpec((1, 1, tn), lambda l, j: (l, 0, j)),
        ],
        out_specs=pl.BlockSpec((1, r, tn), lambda l, j: (l, 0, j)),
        out_shape=jax.ShapeDtypeStruct((depth, r, n), F32),
        compiler_params=_cparams("arbitrary", "arbitrary"),
        name="modulation",
    )(c_all, w_mod, b_mod.reshape(depth, 1, n))


def _inproj_kernel(x_ref, sc_ref, sh_ref, w_ref, cos_ref, sin_ref, gq_ref, gk_ref, ones_ref,
                   cu_ref, cb_ref, rq_ref, rk_ref, rv_ref, rg_ref, aq_ref, ak_ref, av_ref):
    ones = ones_ref[...]
    gq = gq_ref[...]
    gk = gk_ref[...]
    tm = x_ref.shape[1]
    rb = tm // IN_STREAMS
    lane = lax.broadcasted_iota(jnp.int32, (rb, LANES), 1)
    first_half = (lane & 31) < 16

    def rms(xs, g):
        s_hi, s_lo = _split_bf16(xs * xs)
        ssq = _dot(s_hi, ones) + _dot(s_lo, ones)
        return xs * lax.rsqrt(ssq * (1.0 / HEAD_DIM) + EPS) * g

    for r0 in range(0, tm, rb):
        rows = slice(r0, r0 + rb)
        h = (x_ref[0, rows, :] * sc_ref[0] + sh_ref[0]).astype(BF16)
        cos = cos_ref[rows, :]
        sin = sin_ref[rows, :]

        def rope(xs):
            nxt = pltpu.roll(xs, LANES - 16, axis=1)
            prv = pltpu.roll(xs, 16, axis=1)
            return xs * cos + jnp.where(first_half, nxt, prv) * sin

        za = _dot(h, w_ref[:, IN_ATT[0]:IN_ATT[1]])
        for j in range(4):
            lo, hi = j * LANES, (j + 1) * LANES
            q = rope(rms(za[:, lo:hi], gq)) * ATT_Q_SCALE
            aq_ref[0, lo:hi, rows] = q.T.astype(BF16)
        ak_ref[0, rows, :] = rope(rms(za[:, 512:640], gk)).astype(BF16)
        av_ref[0, rows, :] = za[:, 640:768].astype(BF16)
        zr = _dot(h, w_ref[:, IN_RET[0]:IN_RET[1]])
        for j in range(2):
            lo, hi = j * LANES, (j + 1) * LANES
            rq_ref[0, lo:hi, rows] = rope(zr[:, lo:hi]).T
            rk_ref[0, rows, lo:hi] = rope(zr[:, 256 + lo:256 + hi] * (HEAD_DIM ** -0.5))
            rv_ref[0, lo:hi, rows] = zr[:, 512 + lo:512 + hi].T
            rg_ref[0, lo:hi, rows] = zr[:, 768 + lo:768 + hi].T
        zc = _dot(h, w_ref[:, IN_CONV[0]:IN_CONV[1]])
        cu_ref[0, rows, :] = zc[:, 512:768] * zc[:, 0:256]
        cb_ref[0, rows, :] = zc[:, 256:512]


def _inproj(x, sc, sh, w_bf, cos, sin, gq, gk, ones_bd, tm):
    b, l, d = x.shape
    n = w_bf.shape[1]
    tok = lambda w: pl.BlockSpec((1, tm, w), lambda i, j: (i, j, 0))
    vec = pl.BlockSpec((1, 1, d), lambda i, j: (i, 0, 0))
    full = lambda s: pl.BlockSpec(s, lambda i, j: (0,) * len(s))
    widths = (256, 256, 256, 256, 256, 256, 512, 128, 128)
    dtypes = (F32,) * 6 + (BF16,) * 3
    out_specs = [tok(w) for w in widths]
    out_shape = [jax.ShapeDtypeStruct((b, l, w), dt) for w, dt in zip(widths, dtypes)]
    for o in (2, 4, 5, 6):
        out_specs[o] = pl.BlockSpec((1, widths[o], tm), lambda i, j: (i, 0, j))
        out_shape[o] = jax.ShapeDtypeStruct((b, widths[o], l), dtypes[o])
    return pl.pallas_call(
        _inproj_kernel,
        grid=(b, l // tm),
        in_specs=[tok(d), vec, vec, full((d, n)),
                  pl.BlockSpec((tm, LANES), lambda i, j: (j, 0)),
                  pl.BlockSpec((tm, LANES), lambda i, j: (j, 0)),
                  full((1, LANES)), full((1, LANES)), full((LANES, LANES))],
        out_specs=out_specs,
        out_shape=out_shape,
        compiler_params=_cparams("arbitrary", "arbitrary"),
        name="inproj",
    )(x, sc, sh, w_bf, cos, sin, gq, gk, ones_bd)


def _attn_kernel(q_ref, k_ref, vt_ref, o_ref):
    tq = q_ref.shape[2]
    q4 = q_ref[0]
    qt = jnp.concatenate([q4[h * HEAD_DIM:(h + 1) * HEAD_DIM, :] for h in range(ATT_Q_PER_KV)], axis=1)
    lk = k_ref.shape[2]
    bounds = [(lo, min(lo + ATT_KV_CHUNK, lk)) for lo in range(0, lk, ATT_KV_CHUNK)]
    r = qt.shape[1] // ATT_STREAMS
    qs = [qt[:, i * r:(i + 1) * r] for i in range(ATT_STREAMS)]
    ms = [jnp.full((1, r), -jnp.inf, F32) for _ in qs]
    accs = [jnp.zeros((ATT_V_ROWS, r), F32) for _ in qs]
    nxt = [_dot(k_ref[0, 0, bounds[0][0]:bounds[0][1], :], q) for q in qs]
    for c, (lo, hi) in enumerate(bounds):
        for i, q in enumerate(qs):
            s = nxt[i]
            if c + 1 < len(bounds):
                nxt[i] = _dot(k_ref[0, 0, bounds[c + 1][0]:bounds[c + 1][1], :], q)
            m_new = jnp.maximum(ms[i], jnp.max(s, axis=0, keepdims=True))
            p = jnp.exp2(s - m_new)
            accs[i] = accs[i] * jnp.exp2(ms[i] - m_new) + _dot(vt_ref[0, 0, :, lo:hi], p.astype(BF16))
            ms[i] = m_new
    acc = jnp.concatenate(accs, axis=1)
    o = acc[:HEAD_DIM] / acc[HEAD_DIM:HEAD_DIM + 1]
    for h in range(ATT_Q_PER_KV):
        o_ref[0, h] = o[:, h * tq:(h + 1) * tq]


def _attention(q, k, vt, tq):
    b, wq, lq = q.shape
    hkv, lk = k.shape[1], k.shape[2]
    wg = ATT_Q_PER_KV * HEAD_DIM
    return pl.pallas_call(
        _attn_kernel,
        grid=(b, hkv, lq // tq),
        in_specs=[pl.BlockSpec((1, wg, tq), lambda i, g, j: (i, g, j)),
                  pl.BlockSpec((1, 1, lk, HEAD_DIM), lambda i, g, j: (i, g, 0, 0)),
                  pl.BlockSpec((1, 1, ATT_V_ROWS, lk), lambda i, g, j: (i, g, 0, 0))],
        out_specs=pl.BlockSpec((1, ATT_Q_PER_KV, HEAD_DIM, tq), lambda i, g, j: (i, g, 0, j)),
        out_shape=jax.ShapeDtypeStruct((b, wq // HEAD_DIM, HEAD_DIM, lq), F32),
        compiler_params=_cparams("arbitrary", "arbitrary", "arbitrary"),
        name="attention",
    )(q, k, vt)


def _ret_kernel(lg_ref, q_ref, k_ref, v_ref, g_ref, s0f_ref, s0b_ref, lgf_ref, lgb_ref, lgfc_ref, lgbc_ref, gn_ref,
                o_ref, sff_ref, sfb_ref, s_scr, sb_scr, dec_scr):
    i, p, j = pl.program_id(0), pl.program_id(1), pl.program_id(2)
    nc = pl.num_programs(2)
    c = k_ref.shape[1]
    w = k_ref.shape[2]
    pos = lax.broadcasted_iota(jnp.int32, (c, w), 0).astype(F32)
    qpos = lax.broadcasted_iota(jnp.int32, (w, c), 1).astype(F32)
    lgf = lgf_ref[...]
    lgb = lgb_ref[...]
    lgf_col = lgfc_ref[...]
    lgb_col = lgbc_ref[...]

    @pl.when((i == 0) & (p == 0) & (j == 0))
    def _():
        diff = (lax.broadcasted_iota(jnp.int32, (c, c), 1) - lax.broadcasted_iota(jnp.int32, (c, c), 0)).astype(F32)
        for h in range(RET_HEADS):
            dec_scr[h] = jnp.where(diff >= 0.0, jnp.exp(lg_ref[0, h] * jnp.maximum(diff, 0.0)),
                                   jnp.exp(lg_ref[1, h] * jnp.maximum(-diff, 0.0)))

    nb = k_ref.shape[0]

    def update_state(e, vt, kz, cdec):
        for h in range(RET_HEADS):
            lo, hi = h * HEAD_DIM, (h + 1) * HEAD_DIM
            s_scr[e, h] = cdec[:, lo:hi] * s_scr[e, h] + _dot(vt[lo:hi, :], kz[:, lo:hi])

    @pl.when(p == 0)
    def _():
        @pl.when(j == 0)
        def _():
            s_scr[...] = s0b_ref[...]

        sb_scr[nc - 1 - j] = s_scr[...]
        for e in range(nb):
            kz = (k_ref[e] * jnp.exp(lgb * pos)).astype(BF16)
            update_state(e, v_ref[e].astype(BF16), kz, jnp.exp(lgb * float(c)))

        @pl.when(j == nc - 1)
        def _():
            sfb_ref[...] = s_scr[...]

    @pl.when(p == 1)
    def _():
        @pl.when(j == 0)
        def _():
            s_scr[...] = s0f_ref[...]

        for e in range(nb):
            qt = q_ref[e]
            k = k_ref[e]
            qf = (qt * jnp.exp(lgf_col * (qpos + 1.0))).astype(BF16)
            qb = (qt * jnp.exp(lgb_col * (float(c) - qpos))).astype(BF16)
            kz = (k * jnp.exp(lgf * (float(c) - 1.0 - pos))).astype(BF16)
            qh = qt.astype(BF16)
            kh = k.astype(BF16)
            vt = v_ref[e].astype(BF16)
            gate = g_ref[e]
            for h in range(RET_HEADS):
                lo, hi = h * HEAD_DIM, (h + 1) * HEAD_DIM
                sc = _dot(kh[:, lo:hi], qh[lo:hi, :])
                y = (_dot(vt[lo:hi, :], (sc * dec_scr[h]).astype(BF16))
                     + _dot(s_scr[e, h].astype(BF16), qf[lo:hi, :])
                     + _dot(sb_scr[j, e, h].astype(BF16), qb[lo:hi, :]))
                mu = jnp.mean(y, axis=0, keepdims=True)
                yc = y - mu
                var = jnp.mean(yc * yc, axis=0, keepdims=True)
                gh = gate[lo:hi, :]
                o_ref[e, lo:hi, :] = yc * lax.rsqrt(var + EPS) * gn_ref[lo:hi, :] * (gh * jax.nn.sigmoid(gh))
            update_state(e, vt, kz, jnp.exp(lgf * float(c)))

        @pl.when(j == nc - 1)
        def _():
            sff_ref[...] = s_scr[...]


def _retention(qt, k, vt, gate_t, s0_f, s0_b, lg, lgf_row, lgb_row, gn_row):
    b, l, w = k.shape
    c = min(RET_CHUNK, l)
    nc = l // c
    nb = RET_BATCH if b % RET_BATCH == 0 else 1
    st = (nb, RET_HEADS, HEAD_DIM, HEAD_DIM)
    st_spec = pl.BlockSpec(st, lambda i, p, j: (i, 0, 0, 0))
    row = pl.BlockSpec((1, w), lambda i, p, j: (0, 0))
    col = pl.BlockSpec((w, 1), lambda i, p, j: (0, 0))
    k_both = pl.BlockSpec((nb, c, w), lambda i, p, j: (i, jnp.where(p == 0, nc - 1 - j, j), 0))
    t_both = pl.BlockSpec((nb, w, c), lambda i, p, j: (i, 0, jnp.where(p == 0, nc - 1 - j, j)))
    t_fwd = pl.BlockSpec((nb, w, c), lambda i, p, j: (i, 0, p * j))
    state_shape = jax.ShapeDtypeStruct((b,) + st[1:], F32)
    return pl.pallas_call(
        _ret_kernel,
        grid=(b // nb, 2, nc),
        in_specs=[pl.BlockSpec(memory_space=pltpu.SMEM), t_fwd, k_both, t_both, t_fwd,
                  st_spec, st_spec, row, row, col, col, col],
        out_specs=[t_fwd, st_spec, st_spec],
        out_shape=[jax.ShapeDtypeStruct((b, w, l), F32), state_shape, state_shape],
        scratch_shapes=[pltpu.VMEM(st, F32), pltpu.VMEM((nc,) + st, F32), pltpu.VMEM((RET_HEADS, c, c), F32)],
        compiler_params=_cparams("arbitrary", "arbitrary", "arbitrary"),
        name="retention",
    )(lg, qt, k, vt, gate_t, s0_f, s0_b, lgf_row, lgb_row, lgf_row.reshape(w, 1), lgb_row.reshape(w, 1),
      gn_row.reshape(w, 1))


def _layer_norm(x, g, b):
    mu = jnp.mean(x, axis=-1, keepdims=True)
    xc = x - mu
    var = jnp.mean(xc * xc, axis=-1, keepdims=True)
    return xc * lax.rsqrt(var + EPS) * g + b


def _store_rows_packed(ref, x):
    tm, d = x.shape
    bits = pltpu.bitcast(x.astype(BF16).astype(F32), jnp.uint32)
    word = (bits[:, :d // 2] >> 16) | bits[:, d // 2:]
    for s in range(PACKED_ROWS):
        ref[pl.ds(s, tm, stride=PACKED_ROWS), :] = word[:, s * LANES:(s + 1) * LANES]


def _load_rows_packed(ref, tm):
    word = jnp.concatenate([ref[pl.ds(s, tm, stride=PACKED_ROWS), :] for s in range(PACKED_ROWS)], axis=1)
    lo = pltpu.bitcast(word << 16, F32).astype(BF16)
    hi = pltpu.bitcast(word & jnp.uint32(0xFFFF0000), F32).astype(BF16)
    return jnp.concatenate([lo, hi], axis=1)


def _outproj_kernel(alpha, cu_ref, cup_ref, cun_ref, cb_ref, cw_ref, ret_ref, att_ref, x_ref, g1_ref,
                    w_ref, lng_ref, lnb_ref, sc_ref, sh_ref, wr_ref, br_ref, cnt0_ref, before_ref, *rest):
    x1_ref, h2_ref, ri_ref, rw_ref, cnt_ref, run_scr = rest[-6:]
    i = pl.program_id(0)
    j = pl.program_id(1)

    @pl.when((i == 0) & (j == 0))
    def _():
        run_scr[...] = cnt0_ref[...]

    t = cu_ref[0]
    tm = t.shape[0]
    prev_row = jnp.where(j == 0, 0.0, cup_ref[0][7:8, :])
    next_row = jnp.where(j == pl.num_programs(1) - 1, 0.0, cun_ref[0][0:1, :])
    row = lax.broadcasted_iota(jnp.int32, t.shape, 0)
    t_prev = jnp.where(row == 0, prev_row, pltpu.roll(t, 1, axis=0))
    t_next = jnp.where(row == tm - 1, next_row, pltpu.roll(t, tm - 1, axis=0))
    conv = cb_ref[0] * (t_prev * cw_ref[0] + t * cw_ref[1] + t_next * cw_ref[2])
    conv = conv.astype(BF16)
    wrh = wr_ref[:, :ROUTER_PAD]
    rb = tm // OUT_STREAMS
    lane = lax.broadcasted_iota(jnp.int32, (rb, ROUTER_PAD), 1)
    lane_f = lane.astype(F32)
    routed = []
    for r0 in range(0, tm, rb):
        att_t = att_ref[0, :, :, r0:r0 + rb].reshape(att_ref.shape[1] * HEAD_DIM, rb).astype(BF16)
        tn = (((0,), (0,)), ((), ()))
        y = (_dot(conv[r0:r0 + rb], w_ref[0:256, :])
             + lax.dot_general(ret_ref[0, :, r0:r0 + rb].astype(BF16), w_ref[256:512, :], tn,
                               preferred_element_type=F32)
             + lax.dot_general(att_t, w_ref[512:1024, :], tn, preferred_element_type=F32))
        x1 = _layer_norm(alpha * x_ref[0, r0:r0 + rb, :] + g1_ref[0] * y, lng_ref[...], lnb_ref[...])
        x1_ref[0, r0:r0 + rb, :] = x1
        h2 = x1 * sc_ref[0] + sh_ref[0]
        _store_rows_packed(h2_ref.at[pl.ds(r0 * PACKED_ROWS, rb * PACKED_ROWS)], h2)
        h_hi, h_lo = _split_bf16(h2)
        hw = _dot(h_hi, wr_ref[...])
        logits = hw[:, :ROUTER_PAD] + hw[:, ROUTER_PAD:] + _dot(h_lo, wrh) + br_ref[...]

        work = jnp.where(lane < N_EXPERTS, logits, -jnp.inf)
        vals, firsts, sels = [], [], []
        for _ in range(TOP_K):
            m = jnp.max(work, axis=-1, keepdims=True)
            first = jnp.min(jnp.where(work == m, lane_f, float(ROUTER_PAD)), axis=-1, keepdims=True)
            sel = lane_f == first
            vals.append(m)
            firsts.append(first)
            sels.append(sel)
            work = jnp.where(sel, -jnp.inf, work)
        exps = [jnp.exp(v - vals[0]) for v in vals]
        denom = exps[0]
        for e in exps[1:]:
            denom = denom + e
        cnt = jnp.zeros(logits.shape, F32)
        for sel in sels:
            cnt = cnt + jnp.where(sel, 1.0, 0.0)
        routed.append((firsts, sels, [e / denom for e in exps], cnt))

    cnt = jnp.concatenate([r[3] for r in routed], axis=0)
    base = _dot(before_ref[...], cnt.astype(BF16)) + run_scr[...]
    run_scr[...] = run_scr[...] + jnp.sum(cnt, axis=0, keepdims=True)
    cnt_ref[...] = run_scr[...]

    for blk, (firsts, sels, wts, _) in enumerate(routed):
        r0 = blk * rb
        ri = jnp.zeros((rb, ROUTER_PAD), F32)
        rw = jnp.zeros((rb, ROUTER_PAD), F32)
        for kk in range(TOP_K):
            rank = jnp.sum(jnp.where(sels[kk], base[r0:r0 + rb], 0.0), axis=-1, keepdims=True)
            ri = jnp.where(lane == kk, firsts[kk], ri)
            ri = jnp.where(lane == TOP_K + kk, rank, ri)
            rw = jnp.where(lane == kk, wts[kk], rw)
        ri_ref[0, r0:r0 + rb, :] = ri.astype(jnp.int32)
        rw_ref[0, r0:r0 + rb, :] = rw


def _outproj(alpha, cu, cb, cw, ret, att, x, g1, w_bf, lng, lnb, sc2, sh2, wr, br, cnt0, tm,
             h2_tokens=None, h2_offset=0, h2_prev=None):
    b, l, d = x.shape
    nj = l // tm
    h2_tokens = b * l if h2_tokens is None else h2_tokens
    assert h2_offset % tm == 0
    h2_blk = h2_offset // tm
    extra_specs, extra_args, aliases = [], [], {}
    if h2_prev is not None:
        extra_specs, extra_args, aliases = [pl.BlockSpec(memory_space=pl.ANY)], [h2_prev], {18: 1}
    tok = lambda w: pl.BlockSpec((1, tm, w), lambda i, j: (i, j, 0))
    vec = pl.BlockSpec((1, 1, d), lambda i, j: (i, 0, 0))
    full = lambda s: pl.BlockSpec(s, lambda i, j: (0,) * len(s))
    r8 = tm // 8
    nb8 = l // 8
    tri = jnp.arange(tm)
    before = (tri[None, :] < tri[:, None]).astype(BF16)
    return pl.pallas_call(
        functools.partial(_outproj_kernel, alpha),
        grid=(b, nj),
        in_specs=[tok(256),
                  pl.BlockSpec((1, 8, 256), lambda i, j: (i, jnp.maximum(j * r8 - 1, 0), 0)),
                  pl.BlockSpec((1, 8, 256), lambda i, j: (i, jnp.minimum((j + 1) * r8, nb8 - 1), 0)),
                  tok(256), full((3, 1, 256)),
                  pl.BlockSpec((1, ret.shape[1], tm), lambda i, j: (i, 0, j)),
                  pl.BlockSpec((1,) + att.shape[1:3] + (tm,), lambda i, j: (i, 0, 0, j)), tok(d), vec,
                  full((d, d)), full((1, d)), full((1, d)), vec, vec,
                  full((d, 2 * ROUTER_PAD)), full((1, ROUTER_PAD)), full((1, ROUTER_PAD)),
                  full((tm, tm))] + extra_specs,
        out_specs=[tok(d), pl.BlockSpec((tm * PACKED_ROWS, LANES), lambda i, j: (h2_blk + i * nj + j, 0)),
                   tok(ROUTER_PAD), tok(ROUTER_PAD), full((1, ROUTER_PAD))],
        out_shape=[jax.ShapeDtypeStruct((b, l, d), F32),
                   jax.ShapeDtypeStruct((h2_tokens * PACKED_ROWS, LANES), jnp.uint32),
                   jax.ShapeDtypeStruct((b, l, ROUTER_PAD), jnp.int32),
                   jax.ShapeDtypeStruct((b, l, ROUTER_PAD), F32),
                   jax.ShapeDtypeStruct((1, ROUTER_PAD), F32)],
        scratch_shapes=[pltpu.VMEM((1, ROUTER_PAD), F32)],
        input_output_aliases=aliases,
        compiler_params=_cparams("arbitrary", "arbitrary"),
        name="outproj_ln_router",
    )(cu, cu, cu, cb, cw, ret, att, x, g1, w_bf, lng, lnb, sc2, sh2, wr, br, cnt0, before, *extra_args)


def _ffn_kernel(be_ref, nv_ref, x_ref, wgu_ref, bgu_ref, wd_ref, bd_ref, o_ref, wgu_scr, wd_scr):
    j = pl.program_id(0)
    f = wd_ref.shape[2]
    tm = x_ref.shape[0] // PACKED_ROWS
    nv = nv_ref[j]

    @pl.when(nv > 0)
    def _():
        @pl.when((j == 0) | (be_ref[j] != be_ref[jnp.maximum(j - 1, 0)]))
        def _():
            wgu_scr[...] = wgu_ref[0, 0].astype(BF16)
            wd_scr[...] = wd_ref[0, 0].astype(BF16)

    def compute(rows):
        part = pl.ds(0, rows * PACKED_ROWS)
        x = _load_rows_packed(x_ref.at[part], rows)
        bgu = bgu_ref[0, 0]
        acts = []
        for lo in range(0, f, FFN_BLOCK):
            hi = lo + FFN_BLOCK
            gate = jnp.minimum(_dot(x, wgu_scr[:, lo:hi]) + bgu[:, lo:hi], SWIGLU_LIMIT)
            up = jnp.clip(_dot(x, wgu_scr[:, f + lo:f + hi]) + bgu[:, f + lo:f + hi], -SWIGLU_LIMIT, SWIGLU_LIMIT)
            acts.append(((up + 1.0) * (gate * jax.nn.sigmoid(SWIGLU_ALPHA * gate))).astype(BF16))
        act = jnp.concatenate(acts, axis=1)
        _store_rows_packed(o_ref.at[part], _dot(act, wd_scr[...]) + bd_ref[0, 0])

    step = tm // FFN_ROW_STEPS
    for k in range(1, FFN_ROW_STEPS + 1):
        @pl.when((nv > (k - 1) * step) & (nv <= k * step))
        def _():
            compute(k * step)


def _expert_ffn(layer, block_e, n_valid, xs, wgu, bgu, wd, bd, tm):
    n_rows = xs.shape[0] // PACKED_ROWS
    depth, ne, d, f2 = wgu.shape
    f = f2 // 2
    rows = pl.BlockSpec((tm * PACKED_ROWS, LANES), lambda j, be, nu: (j, 0))
    grid_spec = pltpu.PrefetchScalarGridSpec(
        num_scalar_prefetch=2,
        grid=(n_rows // tm,),
        in_specs=[rows,
                  pl.BlockSpec((1, 1, d, f2), lambda j, be, nu: (layer, be[j], 0, 0)),
                  pl.BlockSpec((1, 1, 1, f2), lambda j, be, nu: (layer, be[j], 0, 0)),
                  pl.BlockSpec((1, 1, f, d), lambda j, be, nu: (layer, be[j], 0, 0)),
                  pl.BlockSpec((1, 1, 1, d), lambda j, be, nu: (layer, be[j], 0, 0))],
        out_specs=rows,
        scratch_shapes=[pltpu.VMEM((d, f2), BF16), pltpu.VMEM((f, d), BF16)],
    )
    return pl.pallas_call(
        _ffn_kernel,
        grid_spec=grid_spec,
        out_shape=jax.ShapeDtypeStruct((n_rows * PACKED_ROWS, LANES), jnp.uint32),
        compiler_params=_cparams("arbitrary"),
        name="expert_ffn",
    )(block_e, n_valid, xs, wgu, bgu.reshape(depth, ne, 1, f2), wd, bd.reshape(depth, ne, 1, d))


def _sc_worker_base(per_worker):
    return (lax.axis_index("s") * SC_CORES + lax.axis_index("c")) * per_worker


def _sc_dispatch(rows, dest, n_out):
    t = rows.shape[0]
    kk = dest.shape[0] // t
    w = SC_WINDOW
    per_worker = t // (SC_CORES * SC_SUBCORES)
    assert per_worker * SC_CORES * SC_SUBCORES == t and per_worker % w == 0
    mesh = plsc.VectorSubcoreMesh(core_axis_name="c", subcore_axis_name="s")

    n_win = per_worker // w
    row_buf = pltpu.VMEM((w,) + rows.shape[1:], rows.dtype)

    @functools.partial(
        pl.kernel, mesh=mesh,
        out_type=jax.ShapeDtypeStruct((n_out,) + rows.shape[1:], rows.dtype),
        scratch_types=[pltpu.VMEM((w,), jnp.int32)] * (2 * kk) + [row_buf] * 2 + [pltpu.SemaphoreType.DMA] * 4)
    def scatter_rows(r_hbm, d_hbm, o_hbm, *scratch):
        idx_v = (scratch[:kk], scratch[kk:2 * kk])
        rows_v = scratch[2 * kk:2 * kk + 2]
        rsem = scratch[2 * kk + 2:2 * kk + 4]
        ssem = scratch[2 * kk + 4:2 * kk + 6]
        base = _sc_worker_base(per_worker)

        def read(win, slot):
            return pltpu.make_async_copy(r_hbm.at[pl.ds(base + win * w, w)], rows_v[slot], rsem[slot])

        def scatter(slot, s):
            return pltpu.make_async_copy(rows_v[slot], o_hbm.at[idx_v[slot][s]], ssem[slot])

        def start_read(win, slot):
            read(win, slot).start()
            for s in range(kk):
                pltpu.sync_copy(d_hbm.at[pl.ds(s * t + base + win * w, w)], idx_v[slot][s])

        start_read(0, 0)

        @pl.loop(0, (n_win + 1) // 2)
        def _(pair):
            for slot in range(2):
                win = 2 * pair + slot
                other = 1 - slot

                @pl.when(win < n_win)
                def _():
                    @pl.when(win >= 1)
                    def _():
                        for s in range(kk):
                            scatter(other, s).wait()

                    @pl.when(win + 1 < n_win)
                    def _():
                        start_read(win + 1, other)

                    read(win, slot).wait()
                    for s in range(kk):
                        scatter(slot, s).start()

        for s in range(kk):
            scatter((n_win - 1) % 2, s).wait()

    return scatter_rows(rows, dest)


def _sc_gather(table, idx):
    n = idx.shape[0]
    w = SC_WINDOW
    per_worker = n // (SC_CORES * SC_SUBCORES)
    assert per_worker * SC_CORES * SC_SUBCORES == n and per_worker % w == 0
    mesh = plsc.VectorSubcoreMesh(core_axis_name="c", subcore_axis_name="s")

    n_win = per_worker // w
    assert n_win % 2 == 0
    row_buf = pltpu.VMEM((w,) + table.shape[1:], table.dtype)

    @functools.partial(
        pl.kernel, mesh=mesh,
        out_type=jax.ShapeDtypeStruct((n,) + table.shape[1:], table.dtype),
        scratch_types=[pltpu.VMEM((w,), jnp.int32)] * 2 + [row_buf] * 2 + [pltpu.SemaphoreType.DMA] * 4)
    def gather_rows(t_hbm, i_hbm, o_hbm, idx0, idx1, rows0, rows1, gsem0, gsem1, wsem0, wsem1):
        idx_v, rows_v = (idx0, idx1), (rows0, rows1)
        gsem, wsem = (gsem0, gsem1), (wsem0, wsem1)
        base = _sc_worker_base(per_worker)

        def gather(slot):
            return pltpu.make_async_copy(t_hbm.at[idx_v[slot]], rows_v[slot], gsem[slot])

        def write(win, slot):
            return pltpu.make_async_copy(rows_v[slot], o_hbm.at[pl.ds(base + win * w, w)], wsem[slot])

        def start_gather(win, slot):
            pltpu.sync_copy(i_hbm.at[pl.ds(base + win * w, w)], idx_v[slot])
            gather(slot).start()

        start_gather(0, 0)

        @pl.loop(0, n_win // 2)
        def _(pair):
            for slot in range(2):
                win = 2 * pair + slot
                other = 1 - slot

                @pl.when(win >= 1)
                def _():
                    write(win - 1, other).wait()

                @pl.when(win + 1 < n_win)
                def _():
                    start_gather(win + 1, other)

                gather(slot).wait()
                write(win, slot).start()

        write(n_win - 1, 1).wait()

    return gather_rows(table, idx)


def _combine_kernel(alpha, g_ref, w_ref, x_ref, g2_ref, lng_ref, lnb_ref, *rest):
    o_ref = rest[-1]
    tm = x_ref.shape[1]
    w = w_ref[0]
    f = _load_rows_packed(g_ref.at[0], tm).astype(F32) * w[:, 0:1]
    for kk in range(1, TOP_K):
        f = f + _load_rows_packed(g_ref.at[kk], tm).astype(F32) * w[:, kk:kk + 1]
    o_ref[0] = _layer_norm(alpha * x_ref[0] + g2_ref[0] * f, lng_ref[...], lnb_ref[...])


def _combine(alpha, gathered, b0, nb, wts, x1, g2, lng, lnb, tm, prev=None):
    b, l, d = x1.shape
    nj = l // tm
    tok = pl.BlockSpec((1, tm, d), lambda i, j: (b0 + i, j, 0))
    full = pl.BlockSpec((1, d), lambda i, j: (0, 0))
    in_specs = [pl.BlockSpec((TOP_K, tm * PACKED_ROWS, LANES), lambda i, j: (0, i * nj + j, 0)),
                pl.BlockSpec((1, tm, ROUTER_PAD), lambda i, j: (b0 + i, j, 0)),
                tok, pl.BlockSpec((1, 1, d), lambda i, j: (b0 + i, 0, 0)), full, full]
    args = [gathered, wts, x1, g2, lng, lnb]
    aliases = {}
    if prev is not None:
        in_specs.append(pl.BlockSpec(memory_space=pl.ANY))
        args.append(prev)
        aliases = {len(args) - 1: 0}
    return pl.pallas_call(
        functools.partial(_combine_kernel, alpha),
        grid=(nb, nj),
        in_specs=in_specs,
        out_specs=tok,
        out_shape=jax.ShapeDtypeStruct((b, l, d), F32),
        input_output_aliases=aliases,
        compiler_params=_cparams("arbitrary", "arbitrary"),
        name="combine_ln",
    )(*args)


def _rope_tables(l):
    rows = l // GRID_W
    axis_dim = HEAD_DIM // 2
    inv_freq = ROPE_BASE ** (-jnp.arange(0, axis_dim, 2, dtype=F32) / axis_dim)
    row = jnp.repeat(jnp.arange(rows, dtype=F32), GRID_W)
    col = jnp.tile(jnp.arange(GRID_W, dtype=F32), rows)
    ang = jnp.stack([row[:, None] * inv_freq, col[:, None] * inv_freq], axis=1)
    cos, sin = jnp.cos(ang), jnp.sin(ang)
    cos64 = jnp.broadcast_to(cos[:, :, None, :], (l, 2, 2, HEAD_DIM // 4)).reshape(l, HEAD_DIM)
    sin64 = jnp.stack([-sin, sin], axis=2).reshape(l, HEAD_DIM)
    return jnp.tile(cos64, (1, LANES // HEAD_DIM)), jnp.tile(sin64, (1, LANES // HEAD_DIM))


def _route_tables(ri, counts, tm):
    t = ri.shape[0]
    top_e = ri[:, :TOP_K]
    rank = ri[:, TOP_K:2 * TOP_K]
    padded = (counts + tm - 1) // tm * tm
    pad_end = jnp.cumsum(padded)
    pad_start = pad_end - padded
    experts = jnp.arange(N_EXPERTS, dtype=jnp.int32)
    start = jnp.sum(jnp.where(top_e[:, :, None] == experts, pad_start, 0), axis=-1)
    dest = (start + rank).T.reshape(TOP_K * t)
    n_tiles = (t * TOP_K + N_EXPERTS * (tm - 1) + tm - 1) // tm
    tile_start = jnp.arange(n_tiles, dtype=jnp.int32) * tm
    block_e = jnp.minimum(jnp.sum((pad_end[None, :] <= tile_start[:, None]).astype(jnp.int32), axis=1), N_EXPERTS - 1)
    in_expert = jnp.where(block_e[:, None] == experts[None, :], (pad_start + counts)[None, :], 0).sum(axis=1)
    n_valid = jnp.clip(in_expert - tile_start, 0, tm)
    return dest, block_e.astype(jnp.int32), n_valid.astype(jnp.int32), n_tiles


def _layer_params(layer, d, w_in, conv_w, ret_decay_exp, ret_gn_g, q_norm_g, k_norm_g, w_out, ln_g, ln_b,
                  w_router, b_router):
    log_gamma = jnp.log1p(-jnp.exp2(-ret_decay_exp[layer].astype(F32)))
    lg_rows = jnp.repeat(log_gamma, HEAD_DIM, axis=1)
    wr_pad = jnp.zeros((d, ROUTER_PAD), F32).at[:, :N_EXPERTS].set(w_router[layer])
    wrh = wr_pad.astype(BF16)
    return dict(
        w_in=w_in[layer].astype(BF16), w_out=w_out[layer].astype(BF16),
        gq=jnp.tile(q_norm_g[layer], LANES // HEAD_DIM).reshape(1, LANES),
        gk=jnp.tile(k_norm_g[layer], LANES // HEAD_DIM).reshape(1, LANES),
        log_gamma=log_gamma, lgf_row=lg_rows[0:1], lgb_row=lg_rows[1:2], gn_row=ret_gn_g[layer].reshape(1, -1),
        cw=conv_w[layer].T.reshape(3, 1, -1),
        wr=jnp.concatenate([wrh, (wr_pad - wrh.astype(F32)).astype(BF16)], axis=1),
        br=jnp.zeros((1, ROUTER_PAD), F32).at[0, :N_EXPERTS].set(b_router[layer]),
        lng1=ln_g[layer, 0].reshape(1, d), lnb1=ln_b[layer, 0].reshape(1, d),
        lng2=ln_g[layer, 1].reshape(1, d), lnb2=ln_b[layer, 1].reshape(1, d))


def _layer(layer, last, alpha, x, ctx, m_lat, m_ctx, p, tables, experts):
    b, l, d = x.shape
    lc = ctx.shape[1]
    tm_in, tm_lat, tm_ctx, tm_moe, tq = 1024, 1024, 256, 1024, 256
    assert l % tm_in == 0 and l % tm_lat == 0 and lc % tm_ctx == 0 and l % tq == 0
    cos_l, sin_l, cos_c, sin_c, ones_bd, zero_state = tables
    sh1, sc1, g1, sh2, sc2, g2 = (m_lat[:, :, i] for i in range(N_MOD))
    sh1c, sc1c, g1c, sh2c, sc2c, g2c = (m_ctx[:, :, i] for i in range(N_MOD))

    zc = _inproj(ctx, 1.0 + sc1c, sh1c, p["w_in"], cos_c, sin_c, p["gq"], p["gk"], ones_bd, tm_ctx)
    zl = _inproj(x, 1.0 + sc1, sh1, p["w_in"], cos_l, sin_l, p["gq"], p["gk"], ones_bd, tm_in)
    cu_c, cb_c, rq_c, rk_c, rv_c, rg_c, aq_c, ak_c, av_c = zc
    cu_l, cb_l, rq_l, rk_l, rv_l, rg_l, aq_l, ak_l, av_l = zl

    ret_args = (p["log_gamma"], p["lgf_row"], p["lgb_row"], p["gn_row"])
    ret_c, s_fwd, s_bwd = _retention(rq_c, rk_c, rv_c, rg_c, zero_state, zero_state, *ret_args)
    ret_l, _, _ = _retention(rq_l, rk_l, rv_l, rg_l, s_fwd, s_bwd, *ret_args)

    def kv_heads(a):
        return a.reshape(b, a.shape[1], -1, HEAD_DIM).transpose(0, 2, 1, 3)

    def v_heads(a):
        vt = a.reshape(b, a.shape[1], -1, HEAD_DIM).transpose(0, 2, 3, 1)
        ones = jnp.ones(vt.shape[:2] + (1, vt.shape[3]), BF16)
        pad = jnp.zeros(vt.shape[:2] + (ATT_V_ROWS - HEAD_DIM - 1, vt.shape[3]), BF16)
        return jnp.concatenate([vt, ones, pad], axis=2)

    k_all = kv_heads(jnp.concatenate([ak_c, ak_l], axis=1))
    v_all = v_heads(jnp.concatenate([av_c, av_l], axis=1))
    att_l = _attention(aq_l, k_all, v_all, tq)

    out_args = (p["w_out"], p["lng1"], p["lnb1"])
    rt_args = (p["wr"], p["br"])
    cnt0 = jnp.zeros((1, ROUTER_PAD), F32)
    n_c = 0 if last else b * lc
    n_tok = n_c + b * l
    h2_c = None
    if not last:
        att_c = _attention(aq_c, kv_heads(ak_c), v_heads(av_c), min(tq, lc))
        ctx1, h2_c, ri_c, rw_c, cnt0 = _outproj(alpha, cu_c, cb_c, p["cw"], ret_c, att_c, ctx, g1c, *out_args,
                                                1.0 + sc2c, sh2c, *rt_args, cnt0, tm_ctx, h2_tokens=n_tok)
    x1, h2, ri_l, rw_l, cnt = _outproj(alpha, cu_l, cb_l, p["cw"], ret_l, att_l, x, g1, *out_args,
                                       1.0 + sc2, sh2, *rt_args, cnt0, tm_lat,
                                       h2_tokens=n_tok, h2_offset=n_c, h2_prev=h2_c)
    if not last:
        ri = jnp.concatenate([ri_c.reshape(n_c, -1), ri_l.reshape(b * l, -1)], axis=0)
    else:
        ri = ri_l.reshape(b * l, -1)

    counts = cnt[0, :N_EXPERTS].astype(jnp.int32)
    dest, block_e, n_valid, n_tiles = _route_tables(ri, counts, tm_moe)
    xs = _sc_dispatch(h2.reshape(n_tok, PACKED_ROWS, LANES), dest, n_tiles * tm_moe)
    ys = _expert_ffn(layer, block_e, n_valid, xs.reshape(-1, LANES), *experts, tm_moe)
    ys = ys.reshape(-1, PACKED_ROWS, LANES)
    dest = dest.reshape(TOP_K, n_tok)

    def gather(lo, hi):
        rows = _sc_gather(ys, dest[:, lo:hi].reshape(-1))
        return rows.reshape(TOP_K, (hi - lo) * PACKED_ROWS, LANES)

    if not last:
        ctx = _combine(alpha, gather(0, n_c), 0, b, rw_c, ctx1, g2c, p["lng2"], p["lnb2"], tm_ctx)
    nb = b // COMBINE_CHUNKS if b % COMBINE_CHUNKS == 0 else b
    x_new = None
    for b0 in range(0, b, nb):
        g = gather(n_c + b0 * l, n_c + (b0 + nb) * l)
        x_new = _combine(alpha, g, b0, nb, rw_l, x1, g2, p["lng2"], p["lnb2"], tm_lat, prev=x_new)
    return x_new, ctx


def kernel(x, c, ctx, c_ctx, w_mod, b_mod, w_in, conv_w, ret_decay_exp, ret_gn_g, q_norm_g, k_norm_g, w_out,
           ln_g, ln_b, w_router, b_router, w_gate_up, b_gate_up, w_down, b_down):
    depth = w_mod.shape[0]
    alpha = (2.0 * depth) ** 0.25
    b, l, d = x.shape
    lc = ctx.shape[1]

    n_rows = (b + 1 + 7) // 8 * 8
    c_all = jnp.zeros((n_rows, d), F32).at[:b].set(c).at[b].set(c_ctx)
    mod = _modulation(c_all, w_mod, b_mod)

    cos_l, sin_l = _rope_tables(l)
    cos_c, sin_c = jnp.ones((lc, LANES), F32), jnp.zeros((lc, LANES), F32)
    lane_head = jnp.arange(LANES) // HEAD_DIM
    ones_bd = (lane_head[:, None] == lane_head[None, :]).astype(BF16)
    zero_state = jnp.zeros((b, RET_HEADS, HEAD_DIM, HEAD_DIM), F32)
    tables = (cos_l, sin_l, cos_c, sin_c, ones_bd, zero_state)
    experts = (w_gate_up, b_gate_up, w_down, b_down)

    for layer in range(depth):
        last = layer == depth - 1
        p = _layer_params(layer, d, w_in, conv_w, ret_decay_exp, ret_gn_g, q_norm_g, k_norm_g, w_out, ln_g, ln_b,
                          w_router, b_router)
        m_lat = mod[layer, :b].reshape(b, 1, N_MOD, d)
        m_ctx = jnp.broadcast_to(mod[layer, b].reshape(1, 1, N_MOD, d), (b, 1, N_MOD, d))
        x, ctx = _layer(layer, last, alpha, x, ctx, m_lat, m_ctx, p, tables, experts)
    return x
```
